```python
import jax, jax.numpy as jnp
from jax import lax
import numpy as np

D_MODEL = 1024
BATCH = 8
SEQ = 8192
DEPTH = 1

CHUNK = 64
N_META = 16
Q_BLOCK = 128
HEAD_DIM = 64
N_HEADS_FOX = 8
N_HEADS_SB = 8
WIDTH_FOX = N_HEADS_FOX * HEAD_DIM
WIDTH_SB = N_HEADS_SB * HEAD_DIM
D_FF = 2816
CONV_WIDTH = 3
EPS = 1e-6

SPLIT_SIZES = (WIDTH_FOX, WIDTH_FOX, WIDTH_FOX, N_HEADS_FOX,
               WIDTH_SB, WIDTH_SB, WIDTH_SB, D_MODEL, D_MODEL)
PROJ_WIDTH = sum(SPLIT_SIZES)
SPLIT_POINTS = tuple(sum(SPLIT_SIZES[:i + 1]) for i in range(len(SPLIT_SIZES) - 1))

kernel_name = "hybrid_fox_stickbreak_convffn_block"


def rms_norm(x, g):
    xf = x.astype(jnp.float32)
    y = xf * lax.rsqrt(jnp.mean(xf * xf, axis=-1, keepdims=True) + EPS)
    return (y * g.astype(jnp.float32)).astype(x.dtype)


def _split_heads(t, n_heads):
    b, l, _ = t.shape
    return t.reshape(b, l, n_heads, HEAD_DIM).transpose(0, 2, 1, 3)


def _merge_heads(t):
    b, h, l, d = t.shape
    return t.transpose(0, 2, 1, 3).reshape(b, l, h * d)


def forgetting_attention(q, k, v, log_f):
    seq_len = q.shape[2]
    scale = HEAD_DIM ** -0.5
    c = jnp.cumsum(log_f, axis=-1)
    outs = []
    for start in range(0, seq_len, Q_BLOCK):
        end = min(start + Q_BLOCK, seq_len)
        s = jnp.einsum('bhqd,bhkd->bhqk', q[:, :, start:end], k[:, :, :end],
                       preferred_element_type=jnp.float32) * scale
        s = s + c[:, :, start:end, None] - c[:, :, None, :end]
        t_pos = jnp.arange(start, end)[:, None]
        s_pos = jnp.arange(end)[None, :]
        s = jnp.where(s_pos <= t_pos, s, -jnp.inf)
        p = jax.nn.softmax(s, axis=-1)
        outs.append(jnp.einsum('bhqk,bhkd->bhqd', p.astype(v.dtype), v[:, :, :end]))
    return jnp.concatenate(outs, axis=2)


def stick_breaking_attention(q, k, v):
    seq_len = q.shape[2]
    scale = HEAD_DIM ** -0.5
    outs = []
    for start in range(0, seq_len, Q_BLOCK):
        end = min(start + Q_BLOCK, seq_len)
        z = jnp.einsum('bhqd,bhkd->bhqk', q[:, :, start:end], k[:, :, :end],
                       preferred_element_type=jnp.float32) * scale
        t_pos = jnp.arange(start, end)[:, None]
        s_pos = jnp.arange(end)[None, :]
        causal = s_pos < t_pos
        log_keep = jnp.where(causal, jax.nn.log_sigmoid(-z), 0.0)
        later = lax.cumsum(log_keep, axis=3, reverse=True) - log_keep
        a = jnp.where(causal, jnp.exp(jax.nn.log_sigmoid(z) + later), 0.0)
        outs.append(jnp.einsum('bhqk,bhkd->bhqd', a.astype(v.dtype), v[:, :, :end]))
    return jnp.concatenate(outs, axis=2)


def causal_depthwise_conv(u, w, b):
    seq_len = u.shape[1]
    up = jnp.pad(u, ((0, 0), (CONV_WIDTH - 1, 0), (0, 0)))
    out = b.astype(u.dtype)
    for i in range(CONV_WIDTH):
        out = out + w[i] * up[:, i:i + seq_len]
    return out


def _fwd_setup_inputs(seed: int = 0) -> dict:
    key = jax.random.key(seed)
    ks = jax.random.split(key, 13)
    x = jax.random.normal(ks[0], (BATCH, SEQ, D_MODEL), jnp.float32)
    meta_tokens = jax.random.normal(ks[1], (N_META, D_MODEL), jnp.float32)
    norm_gains = 1.0 + 0.05 * jax.random.normal(ks[2], (DEPTH, 4, D_MODEL), jnp.float32)
    w_in = jax.random.normal(ks[3], (DEPTH, D_MODEL, PROJ_WIDTH), jnp.float32) * D_MODEL ** -0.5
    b_forget = jax.random.uniform(ks[4], (DEPTH, N_HEADS_FOX), jnp.float32, minval=1.0, maxval=6.0)
    w_o_fox = jax.random.normal(ks[5], (DEPTH, WIDTH_FOX, D_MODEL), jnp.float32) * WIDTH_FOX ** -0.5
    w_o_sb = jax.random.normal(ks[6], (DEPTH, WIDTH_SB, D_MODEL), jnp.float32) * WIDTH_SB ** -0.5
    w_out = jax.random.normal(ks[7], (DEPTH, D_MODEL, D_MODEL), jnp.float32) * D_MODEL ** -0.5
    w_up = jax.random.normal(ks[8], (DEPTH, D_MODEL, 2 * D_FF), jnp.float32) * D_MODEL ** -0.5
    conv_w = jax.random.normal(ks[9], (DEPTH, CONV_WIDTH, 2 * D_FF), jnp.float32) * CONV_WIDTH ** -0.5
    conv_b = 0.02 * jax.random.normal(ks[10], (DEPTH, 2 * D_FF), jnp.float32)
    w_down = jax.random.normal(ks[11], (DEPTH, D_FF, D_MODEL), jnp.float32) * D_FF ** -0.5
    return {"x": x, "meta_tokens": meta_tokens, "norm_gains": norm_gains, "w_in": w_in,
            "b_forget": b_forget, "w_o_fox": w_o_fox, "w_o_sb": w_o_sb, "w_out": w_out,
            "w_up": w_up, "conv_w": conv_w, "conv_b": conv_b, "w_down": w_down}


def _fwd_reference(x, meta_tokens, norm_gains, w_in, b_forget, w_o_fox, w_o_sb, w_out,
              w_up, conv_w, conv_b, w_down):
    batch = x.shape[0]
    meta = jnp.broadcast_to(meta_tokens[None].astype(x.dtype), (batch, N_META, D_MODEL))
    h = jnp.concatenate([meta, x], axis=1)
    for layer in range(DEPTH):
        xn = rms_norm(h, norm_gains[layer, 0])
        proj = xn @ w_in[layer]
        q_a, k_a, v_a, f_a, q_b, k_b, v_b, g_a, g_b = jnp.split(proj, SPLIT_POINTS, axis=-1)
        log_f = jax.nn.log_sigmoid((f_a + b_forget[layer]).astype(jnp.float32))
        o_a = forgetting_attention(_split_heads(q_a, N_HEADS_FOX), _split_heads(k_a, N_HEADS_FOX),
                                   _split_heads(v_a, N_HEADS_FOX), log_f.transpose(0, 2, 1))
        o_b = stick_breaking_attention(_split_heads(q_b, N_HEADS_SB), _split_heads(k_b, N_HEADS_SB),
                                       _split_heads(v_b, N_HEADS_SB))
        y_a = _merge_heads(o_a) @ w_o_fox[layer]
        y_b = _merge_heads(o_b) @ w_o_sb[layer]
        mixed = (jax.nn.sigmoid(g_a) * y_a + jax.nn.sigmoid(g_b) * y_b) @ w_out[layer]
        h = h + rms_norm(mixed, norm_gains[layer, 1])
        xn = rms_norm(h, norm_gains[layer, 2])
        u = causal_depthwise_conv(xn @ w_up[layer], conv_w[layer], conv_b[layer])
        u_gate, u_val = jnp.split(u, 2, axis=-1)
        ffn = (jax.nn.gelu(u_gate, approximate=True) * u_val) @ w_down[layer]
        h = h + rms_norm(ffn, norm_gains[layer, 3])
    return h[:, N_META:]


import jax as _jax
import jax.numpy as _jnp

TWIN_FORMAT = 'train_step'
FWD_PARAMS = ['x', 'meta_tokens', 'norm_gains', 'w_in', 'b_forget', 'w_o_fox', 'w_o_sb', 'w_out', 'w_up', 'conv_w', 'conv_b', 'w_down']
TWIN_WEIGHTS = ['meta_tokens', 'norm_gains', 'w_in', 'b_forget', 'w_o_fox', 'w_o_sb', 'w_out', 'w_up', 'conv_w', 'conv_b', 'w_down']
TWIN_DIFF_INPUT = 'x'
TWIN_INPUTS = ['x', 'meta_tokens', 'norm_gains', 'w_in', 'b_forget', 'w_o_fox', 'w_o_sb', 'w_out', 'w_up', 'conv_w', 'conv_b', 'w_down', 'loss_target', 'm_meta_tokens', 'm_norm_gains', 'm_w_in', 'm_b_forget', 'm_w_o_fox', 'm_w_o_sb', 'm_w_out', 'm_w_up', 'm_conv_w', 'm_conv_b', 'm_w_down', 'v_meta_tokens', 'v_norm_gains', 'v_w_in', 'v_b_forget', 'v_w_o_fox', 'v_w_o_sb', 'v_w_out', 'v_w_up', 'v_conv_w', 'v_conv_b', 'v_w_down']
TWIN_OUTPUTS = ['loss', 'grad_x', 'grad_meta_tokens', 'grad_norm_gains', 'grad_w_in', 'grad_b_forget', 'grad_w_o_fox', 'grad_w_o_sb', 'grad_w_out', 'grad_w_up', 'grad_conv_w', 'grad_conv_b', 'grad_w_down', 'delta_meta_tokens', 'delta_norm_gains', 'delta_w_in', 'delta_b_forget', 'delta_w_o_fox', 'delta_w_o_sb', 'delta_w_out', 'delta_w_up', 'delta_conv_w', 'delta_conv_b', 'delta_w_down', 'new_m_meta_tokens', 'new_m_norm_gains', 'new_m_w_in', 'new_m_b_forget', 'new_m_w_o_fox', 'new_m_w_o_sb', 'new_m_w_out', 'new_m_w_up', 'new_m_conv_w', 'new_m_conv_b', 'new_m_w_down', 'new_v_meta_tokens', 'new_v_norm_gains', 'new_v_w_in', 'new_v_b_forget', 'new_v_w_o_fox', 'new_v_w_o_sb', 'new_v_w_out', 'new_v_w_up', 'new_v_conv_w', 'new_v_conv_b', 'new_v_w_down']
TWIN_LEAF_KINDS = {'loss': 'loss', 'grad_x': 'grad_x', 'grad_meta_tokens': 'grad_w', 'grad_norm_gains': 'grad_w', 'grad_w_in': 'grad_w', 'grad_b_forget': 'grad_w', 'grad_w_o_fox': 'grad_w', 'grad_w_o_sb': 'grad_w', 'grad_w_out': 'grad_w', 'grad_w_up': 'grad_w', 'grad_conv_w': 'grad_w', 'grad_conv_b': 'grad_w', 'grad_w_down': 'grad_w', 'delta_meta_tokens': 'delta_w', 'delta_norm_gains': 'delta_w', 'delta_w_in': 'delta_w', 'delta_b_forget': 'delta_w', 'delta_w_o_fox': 'delta_w', 'delta_w_o_sb': 'delta_w', 'delta_w_out': 'delta_w', 'delta_w_up': 'delta_w', 'delta_conv_w': 'delta_w', 'delta_conv_b': 'delta_w', 'delta_w_down': 'delta_w', 'new_m_meta_tokens': 'new_m', 'new_m_norm_gains': 'new_m', 'new_m_w_in': 'new_m', 'new_m_b_forget': 'new_m', 'new_m_w_o_fox': 'new_m', 'new_m_w_o_sb': 'new_m', 'new_m_w_out': 'new_m', 'new_m_w_up': 'new_m', 'new_m_conv_w': 'new_m', 'new_m_conv_b': 'new_m', 'new_m_w_down': 'new_m', 'new_v_meta_tokens': 'new_v', 'new_v_norm_gains': 'new_v', 'new_v_w_in': 'new_v', 'new_v_b_forget': 'new_v', 'new_v_w_o_fox': 'new_v', 'new_v_w_o_sb': 'new_v', 'new_v_w_out': 'new_v', 'new_v_w_up': 'new_v', 'new_v_conv_w': 'new_v', 'new_v_conv_b': 'new_v', 'new_v_w_down': 'new_v'}


def _forward(args):
    return _fwd_reference(*[args[k] for k in FWD_PARAMS])


def _output_shape():
    def fwd():
        inp = _fwd_setup_inputs(0)
        return _fwd_reference(*[inp[k] for k in FWD_PARAMS])
    out = _jax.eval_shape(fwd)
    return out.shape, out.dtype

N_MICROBATCH = 1
ADAM_LR = 0.001
ADAM_B1 = 0.9
ADAM_B2 = 0.999
ADAM_EPS = 1e-08
ADAM_WD = 0.01
ADAM_STEP = 10
PER_EXAMPLE_BATCH_AXIS = {'x': 0, 'loss_target': 0}
SHARED_INPUTS = []
_WEIGHT_DTYPES = {'meta_tokens': _jnp.float32, 'norm_gains': _jnp.float32, 'w_in': _jnp.float32, 'b_forget': _jnp.float32, 'w_o_fox': _jnp.float32, 'w_o_sb': _jnp.float32, 'w_out': _jnp.float32, 'w_up': _jnp.float32, 'conv_w': _jnp.float32, 'conv_b': _jnp.float32, 'w_down': _jnp.float32}
MOMENT_SCALE = {'meta_tokens': 2.923970e-02, 'norm_gains': 4.519806e+01, 'w_in': 3.983480e-01, 'b_forget': 4.929045e+00, 'w_o_fox': 3.792884e-01, 'w_o_sb': 6.873987e-01, 'w_out': 8.602961e-01, 'w_up': 3.650062e-01, 'conv_w': 4.345758e-01, 'conv_b': 7.541447e-01, 'w_down': 7.720383e-01}


def _to_microbatches(a, axis):
    t = _jnp.moveaxis(a, axis, 0)
    t = t.reshape((N_MICROBATCH, t.shape[0] // N_MICROBATCH) + t.shape[1:])
    return _jnp.moveaxis(t, 1, axis + 1)


def setup_inputs(seed: int = 0) -> dict:
    inp = _fwd_setup_inputs(seed)
    key = _jax.random.fold_in(_jax.random.key(seed), 7919)
    shape, _ = _output_shape()
    out = dict(inp)
    out["loss_target"] = _jax.random.normal(_jax.random.fold_in(key, 0), shape, _jnp.float32)
    for i, name in enumerate(TWIN_WEIGHTS):
        w = inp[name].astype(_jnp.float32)
        if MOMENT_SCALE is None:
            s = _jnp.sqrt(_jnp.mean(_jnp.square(w)) + 1e-30)
        else:
            s = MOMENT_SCALE[name]
        km, kv = _jax.random.split(_jax.random.fold_in(key, i + 1))
        out[name] = w
        out["m_" + name] = s * _jax.random.normal(km, w.shape, _jnp.float32)
        out["v_" + name] = (s * s) * _jax.random.uniform(kv, w.shape, _jnp.float32, 0.5, 1.5)
    if N_MICROBATCH > 1:
        for name, axis in PER_EXAMPLE_BATCH_AXIS.items():
            out[name] = _to_microbatches(out[name], axis)
    return {'x': out['x'], 'meta_tokens': out['meta_tokens'], 'norm_gains': out['norm_gains'], 'w_in': out['w_in'], 'b_forget': out['b_forget'], 'w_o_fox': out['w_o_fox'], 'w_o_sb': out['w_o_sb'], 'w_out': out['w_out'], 'w_up': out['w_up'], 'conv_w': out['conv_w'], 'conv_b': out['conv_b'], 'w_down': out['w_down'], 'loss_target': out['loss_target'], 'm_meta_tokens': out['m_meta_tokens'], 'm_norm_gains': out['m_norm_gains'], 'm_w_in': out['m_w_in'], 'm_b_forget': out['m_b_forget'], 'm_w_o_fox': out['m_w_o_fox'], 'm_w_o_sb': out['m_w_o_sb'], 'm_w_out': out['m_w_out'], 'm_w_up': out['m_w_up'], 'm_conv_w': out['m_conv_w'], 'm_conv_b': out['m_conv_b'], 'm_w_down': out['m_w_down'], 'v_meta_tokens': out['v_meta_tokens'], 'v_norm_gains': out['v_norm_gains'], 'v_w_in': out['v_w_in'], 'v_b_forget': out['v_b_forget'], 'v_w_o_fox': out['v_w_o_fox'], 'v_w_o_sb': out['v_w_o_sb'], 'v_w_out': out['v_w_out'], 'v_w_up': out['v_w_up'], 'v_conv_w': out['v_conv_w'], 'v_conv_b': out['v_conv_b'], 'v_w_down': out['v_w_down']}


def _loss(weights, diff, rest, loss_target):
    with _jax.named_scope("forward"):
        args = {**rest, TWIN_DIFF_INPUT: diff, **{k: w.astype(_WEIGHT_DTYPES[k]) for k, w in weights.items()}}
        y = _forward(args)
    with _jax.named_scope("loss_head"):
        err = _jnp.square(y.astype(_jnp.float32) - loss_target)
        return 0.5 * _jnp.sum(_jnp.mean(err, axis=-1)) if err.ndim else 0.5 * err


def _adamw(w, g, m, v):
    m = ADAM_B1 * m + (1.0 - ADAM_B1) * g
    v = ADAM_B2 * v + (1.0 - ADAM_B2) * _jnp.square(g)
    m_hat = m / (1.0 - ADAM_B1 ** ADAM_STEP)
    v_hat = v / (1.0 - ADAM_B2 ** ADAM_STEP)
    delta = -ADAM_LR * (m_hat / (_jnp.sqrt(v_hat) + ADAM_EPS) + ADAM_WD * w)
    return delta, m, v


def reference(x, meta_tokens, norm_gains, w_in, b_forget, w_o_fox, w_o_sb, w_out, w_up, conv_w, conv_b, w_down, loss_target, m_meta_tokens, m_norm_gains, m_w_in, m_b_forget, m_w_o_fox, m_w_o_sb, m_w_out, m_w_up, m_conv_w, m_conv_b, m_w_down, v_meta_tokens, v_norm_gains, v_w_in, v_b_forget, v_w_o_fox, v_w_o_sb, v_w_out, v_w_up, v_conv_w, v_conv_b, v_w_down):
    given = dict(x=x, meta_tokens=meta_tokens, norm_gains=norm_gains, w_in=w_in, b_forget=b_forget, w_o_fox=w_o_fox, w_o_sb=w_o_sb, w_out=w_out, w_up=w_up, conv_w=conv_w, conv_b=conv_b, w_down=w_down, loss_target=loss_target, m_meta_tokens=m_meta_tokens, m_norm_gains=m_norm_gains, m_w_in=m_w_in, m_b_forget=m_b_forget, m_w_o_fox=m_w_o_fox, m_w_o_sb=m_w_o_sb, m_w_out=m_w_out, m_w_up=m_w_up, m_conv_w=m_conv_w, m_conv_b=m_conv_b, m_w_down=m_w_down, v_meta_tokens=v_meta_tokens, v_norm_gains=v_norm_gains, v_w_in=v_w_in, v_b_forget=v_b_forget, v_w_o_fox=v_w_o_fox, v_w_o_sb=v_w_o_sb, v_w_out=v_w_out, v_w_up=v_w_up, v_conv_w=v_conv_w, v_conv_b=v_conv_b, v_w_down=v_w_down)
    weights = {n: given[n] for n in TWIN_WEIGHTS}
    shared = {n: given[n] for n in SHARED_INPUTS}
    per_example = {n: given[n] for n in ['x']}
    grad_fn = _jax.value_and_grad(_loss, argnums=(0, 1))

    def one_microbatch(ex, loss_target):
        ex = dict(ex)
        diff = ex.pop(TWIN_DIFF_INPUT)
        return grad_fn(weights, diff, {**shared, **ex}, loss_target)

    if N_MICROBATCH == 1:
        loss, (grad_w, grad_x) = one_microbatch(per_example, given["loss_target"])
    else:
        def body(carry, xs):
            loss_sum, grad_sum = carry
            l_k, (gw_k, gx_k) = one_microbatch(xs[0], xs[1])
            with _jax.named_scope("update"):
                return (loss_sum + l_k, _jax.tree.map(_jnp.add, grad_sum, gw_k)), gx_k

        init = (_jnp.zeros((), _jnp.float32), _jax.tree.map(_jnp.zeros_like, weights))
        (loss, grad_w), grad_x = _jax.lax.scan(body, init, (per_example, given["loss_target"]))
    with _jax.named_scope("update"):
        delta_w, new_m, new_v = {}, {}, {}
        for n in TWIN_WEIGHTS:
            delta_w[n], new_m[n], new_v[n] = _adamw(weights[n], grad_w[n], given["m_" + n], given["v_" + n])
    return (loss, grad_x, *[grad_w[n] for n in TWIN_WEIGHTS], *[delta_w[n] for n in TWIN_WEIGHTS],
            *[new_m[n] for n in TWIN_WEIGHTS], *[new_v[n] for n in TWIN_WEIGHTS])
```

```python
import functools
import math

import jax
import jax.numpy as jnp
from jax import lax
from jax.experimental import pallas as pl
from jax.experimental.pallas import tpu as pltpu

F32 = jnp.float32
BF16 = jnp.bfloat16

D_MODEL = 1024
N_META = 16
HEAD_DIM = 64
N_PAIRS = 4
W_ATT = 512
D_FF = 2816
EPS = 1e-6
NEG = -1e30
TQ = 256
PAD = TQ - N_META
TCONV = 128
HALO = 16
LANES = 128
VMEM_LIMIT = 56 * 1024 * 1024

ADAM_LR = 0.001
ADAM_B1 = 0.9
ADAM_B2 = 0.999
ADAM_EPS = 1e-08
ADAM_WD = 0.01
ADAM_STEP = 10

MESH = pl.DeviceIdType.MESH

PK_BIG_ROWS = 3920
PK_SMALL_ROWS = 24
PK_ROWS = 3968
PK_TILE = 128


def _cparams(sem, **kw):
    return pltpu.CompilerParams(dimension_semantics=sem, vmem_limit_bytes=VMEM_LIMIT, **kw)


def _row_tile(lp):
    return 768 if lp % 768 == 0 else 256


def _rms(x):
    return lax.rsqrt(jnp.mean(x * x, axis=-1, keepdims=True) + EPS)


def _log_sigmoid(x):
    return jnp.minimum(x, 0.0) - jnp.log(1.0 + jnp.exp(-jnp.abs(x)))


def _split2(x):
    hi = x.astype(BF16)
    lo = (x - hi.astype(F32)).astype(BF16)
    return hi, lo


def _dot(a, b):
    return jnp.dot(a, b, preferred_element_type=F32)


def _dot_nt(a, b):
    return lax.dot_general(a, b, (((1,), (1,)), ((), ())), preferred_element_type=F32)


def _dot_tn(a, b):
    return lax.dot_general(a, b, (((0,), (0,)), ((), ())), preferred_element_type=F32)


def _rms_mm(h, g, w, name):
    lp, dm = h.shape
    n = w.shape[1]
    tr, tn = _row_tile(lp), 512

    def body(h_ref, g_ref, w_ref, out_ref, xn_ref):
        @pl.when(pl.program_id(1) == 0)
        def _():
            x = h_ref[...]
            xn_ref[...] = (x * _rms(x) * g_ref[...]).astype(BF16)
        out_ref[...] = _dot(xn_ref[...], w_ref[...]).astype(BF16)

    return pl.pallas_call(
        body, name=name, grid=(lp // tr, n // tn),
        in_specs=[pl.BlockSpec((tr, dm), lambda i, j: (i, 0)),
                  pl.BlockSpec((1, dm), lambda i, j: (0, 0)),
                  pl.BlockSpec((dm, tn), lambda i, j: (0, j))],
        out_specs=[pl.BlockSpec((tr, tn), lambda i, j: (i, j)),
                   pl.BlockSpec((tr, dm), lambda i, j: (i, 0))],
        out_shape=[jax.ShapeDtypeStruct((lp, n), BF16), jax.ShapeDtypeStruct((lp, dm), BF16)],
        compiler_params=_cparams(("parallel", "arbitrary")),
    )(h, g, w)


def _mm_rmsbwd(dy, wt, h, g, dh_in, name):
    lp, kd = dy.shape
    dm = wt.shape[1]
    tr, tk = 384, 512
    nk = kd // tk

    def body(dy_ref, wt_ref, h_ref, g_ref, dhin_ref, dh_ref, dg_ref, acc_ref):
        i, k = pl.program_id(0), pl.program_id(1)

        @pl.when(k == 0)
        def _():
            acc_ref[...] = jnp.zeros_like(acc_ref)

        acc_ref[...] += _dot(dy_ref[...], wt_ref[...])

        @pl.when(k == nk - 1)
        def _():
            dxn = acc_ref[...]
            x = h_ref[...]
            r = _rms(x)
            yhat = x * r
            part = jnp.sum(dxn * yhat, axis=0, keepdims=True)
            dyh = dxn * g_ref[...]
            dx = r * (dyh - yhat * jnp.mean(dyh * yhat, axis=-1, keepdims=True))
            dh_ref[...] = dhin_ref[...] + dx

            @pl.when(i == 0)
            def _():
                dg_ref[...] = part

            @pl.when(i > 0)
            def _():
                dg_ref[...] += part

    return pl.pallas_call(
        body, name=name, grid=(lp // tr, nk),
        in_specs=[pl.BlockSpec((tr, tk), lambda i, k: (i, k)),
                  pl.BlockSpec((tk, dm), lambda i, k: (k, 0)),
                  pl.BlockSpec((tr, dm), lambda i, k: (i, 0)),
                  pl.BlockSpec((1, dm), lambda i, k: (0, 0)),
                  pl.BlockSpec((tr, dm), lambda i, k: (i, 0))],
        out_specs=[pl.BlockSpec((tr, dm), lambda i, k: (i, 0)),
                   pl.BlockSpec((1, dm), lambda i, k: (0, 0))],
        out_shape=[jax.ShapeDtypeStruct((lp, dm), F32), jax.ShapeDtypeStruct((1, dm), F32)],
        scratch_shapes=[pltpu.VMEM((tr, dm), F32)],
        compiler_params=_cparams(("arbitrary", "arbitrary")),
    )(dy, wt, h, g, dh_in)


def _mm_tn(x, dy, name):
    lp, kd = x.shape
    n = dy.shape[1]
    tl = _row_tile(lp)
    tk = 512 if kd % 512 == 0 else 256
    tn = 512
    nl = lp // tl

    def body(x_ref, dy_ref, o_ref):
        @pl.when(pl.program_id(2) == 0)
        def _():
            o_ref[...] = jnp.zeros_like(o_ref)
        o_ref[...] += _dot_tn(x_ref[...], dy_ref[...])

    return pl.pallas_call(
        body, name=name, grid=(kd // tk, n // tn, nl),
        in_specs=[pl.BlockSpec((tl, tk), lambda a, b, l: (l, a)),
                  pl.BlockSpec((tl, tn), lambda a, b, l: (l, b))],
        out_specs=pl.BlockSpec((tk, tn), lambda a, b, l: (a, b)),
        out_shape=jax.ShapeDtypeStruct((kd, n), F32),
        compiler_params=_cparams(("parallel", "parallel", "arbitrary")),
    )(x, dy)


def _logf(xn, wf, bf, name):
    lp, dm = xn.shape
    tr = _row_tile(lp)

    def body(xn_ref, wf_ref, b_ref, o_ref):
        f = _dot(xn_ref[...], wf_ref[...]) + b_ref[...]
        row = pl.program_id(0) * tr + lax.broadcasted_iota(jnp.int32, f.shape, 0)
        lane = lax.broadcasted_iota(jnp.int32, f.shape, 1)
        o_ref[...] = jnp.where((row >= PAD) & (lane < 8), _log_sigmoid(f), 0.0)

    return pl.pallas_call(
        body, name=name, grid=(lp // tr,),
        in_specs=[pl.BlockSpec((tr, dm), lambda i: (i, 0)),
                  pl.BlockSpec((dm, LANES), lambda i: (0, 0)),
                  pl.BlockSpec((1, LANES), lambda i: (0, 0))],
        out_specs=pl.BlockSpec((tr, LANES), lambda i: (i, 0)),
        out_shape=jax.ShapeDtypeStruct((lp, LANES), F32),
        compiler_params=_cparams(("parallel",)),
    )(xn, wf, bf)


def _tri(n, rel):
    r = lax.broadcasted_iota(jnp.int32, (n, n), 0)
    c = lax.broadcasted_iota(jnp.int32, (n, n), 1)
    return rel(r, c).astype(BF16)


def _cumsum_rows(x, name):
    lp = x.shape[0]
    nb = lp // TQ
    tl = _tri(TQ, lambda r, c: c <= r)

    def body(x_ref, t_ref, o_ref):
        def step(b, carry):
            rows = pl.ds(pl.multiple_of(b * TQ, TQ), TQ)
            xb = x_ref[rows, :]
            hi = xb.astype(BF16)
            r1 = xb - hi.astype(F32)
            mid = r1.astype(BF16)
            lo = (r1 - mid.astype(F32)).astype(BF16)
            t = t_ref[...]
            o_ref[rows, :] = carry + (_dot(t, hi) + _dot(t, mid) + _dot(t, lo))
            return carry + jnp.sum(xb, axis=0, keepdims=True)
        lax.fori_loop(0, nb, step, jnp.zeros((1, LANES), F32))

    return pl.pallas_call(
        body, name=name,
        in_specs=[pl.BlockSpec(memory_space=pltpu.VMEM)] * 2,
        out_specs=pl.BlockSpec(memory_space=pltpu.VMEM),
        out_shape=jax.ShapeDtypeStruct((lp, LANES), F32),
        compiler_params=pltpu.CompilerParams(vmem_limit_bytes=VMEM_LIMIT),
    )(x, tl)


def _dlogf(dc, logf, name):
    lp = dc.shape[0]
    nb = lp // TQ
    tu = _tri(TQ, lambda r, c: c >= r)

    def body(x_ref, lf_ref, t_ref, df_ref, db_ref):
        def step(bb, carry):
            run, db = carry
            b = nb - 1 - bb
            rows = pl.ds(pl.multiple_of(b * TQ, TQ), TQ)
            xb = x_ref[rows, :]
            hi = xb.astype(BF16)
            r1 = xb - hi.astype(F32)
            mid = r1.astype(BF16)
            lo = (r1 - mid.astype(F32)).astype(BF16)
            t = t_ref[...]
            dlf = run + (_dot(t, hi) + _dot(t, mid) + _dot(t, lo))
            df = dlf * (1.0 - jnp.exp(lf_ref[rows, :]))
            df_ref[rows, :] = df
            return run + jnp.sum(xb, axis=0, keepdims=True), db + jnp.sum(df, axis=0, keepdims=True)
        z = jnp.zeros((1, LANES), F32)
        _, db = lax.fori_loop(0, nb, step, (z, z))
        db_ref[...] = db

    return pl.pallas_call(
        body, name=name,
        in_specs=[pl.BlockSpec(memory_space=pltpu.VMEM)] * 3,
        out_specs=[pl.BlockSpec(memory_space=pltpu.VMEM)] * 2,
        out_shape=[jax.ShapeDtypeStruct((lp, LANES), F32), jax.ShapeDtypeStruct((1, LANES), F32)],
        compiler_params=pltpu.CompilerParams(vmem_limit_bytes=VMEM_LIMIT),
    )(dc, logf, tu)


def _merge_fwd(o_a, o_b, proj, h0, w_oa, w_ob, w_out, g1, name):
    lp, dm = h0.shape
    tr = TQ
    ga_blk = (6 * W_ATT) // dm

    def body(oa_ref, ob_ref, ga_ref, gb_ref, h0_ref, woa_ref, wob_ref, wout_ref, g1_ref,
             ya_ref, yb_ref, m_ref, mixed_ref, h1_ref):
        ya = _dot(oa_ref[...], woa_ref[...])
        yb = _dot(ob_ref[...], wob_ref[...])
        m = jax.nn.sigmoid(ga_ref[...].astype(F32)) * ya + jax.nn.sigmoid(gb_ref[...].astype(F32)) * yb
        mb = m.astype(BF16)
        mixed = _dot(mb, wout_ref[...])
        ya_ref[...] = ya.astype(BF16)
        yb_ref[...] = yb.astype(BF16)
        m_ref[...] = mb
        mixed_ref[...] = mixed
        h1_ref[...] = h0_ref[...] + mixed * _rms(mixed) * g1_ref[...]

    row = lambda w: pl.BlockSpec((tr, w), lambda i: (i, 0))
    full = lambda a: pl.BlockSpec(a.shape, lambda i: (0, 0))
    return pl.pallas_call(
        body, name=name, grid=(lp // tr,),
        in_specs=[row(W_ATT), row(W_ATT),
                  pl.BlockSpec((tr, dm), lambda i: (i, ga_blk)),
                  pl.BlockSpec((tr, dm), lambda i: (i, ga_blk + 1)),
                  row(dm), full(w_oa), full(w_ob), full(w_out), full(g1)],
        out_specs=[row(dm)] * 5,
        out_shape=[jax.ShapeDtypeStruct((lp, dm), BF16)] * 3 + [jax.ShapeDtypeStruct((lp, dm), F32)] * 2,
        compiler_params=_cparams(("parallel",)),
    )(o_a, o_b, proj, proj, h0, w_oa, w_ob, w_out, g1)


def _merge_bwd(dh1, mixed, g1, w_out_t, proj, y_a, y_b, w_oa_t, w_ob_t, name):
    lp, dm = dh1.shape
    tr = TQ
    ga_blk = (6 * W_ATT) // dm

    def body(dh_ref, mx_ref, g1_ref, wout_ref, ga_ref, gb_ref, ya_ref, yb_ref, woa_ref, wob_ref,
             dmx_ref, dya_ref, dyb_ref, dga_ref, dgb_ref, doa_ref, dob_ref, dg1_ref):
        i = pl.program_id(0)
        dn = dh_ref[...]
        x = mx_ref[...]
        r = _rms(x)
        yhat = x * r
        part = jnp.sum(dn * yhat, axis=0, keepdims=True)
        dyh = dn * g1_ref[...]
        dmx = (r * (dyh - yhat * jnp.mean(dyh * yhat, axis=-1, keepdims=True))).astype(BF16)
        dmx_ref[...] = dmx
        dm_ = _dot(dmx, wout_ref[...])
        sa = jax.nn.sigmoid(ga_ref[...].astype(F32))
        sb = jax.nn.sigmoid(gb_ref[...].astype(F32))
        dya = (dm_ * sa).astype(BF16)
        dyb = (dm_ * sb).astype(BF16)
        dya_ref[...] = dya
        dyb_ref[...] = dyb
        dga_ref[...] = (dm_ * ya_ref[...].astype(F32) * sa * (1.0 - sa)).astype(BF16)
        dgb_ref[...] = (dm_ * yb_ref[...].astype(F32) * sb * (1.0 - sb)).astype(BF16)
        doa_ref[...] = _dot(dya, woa_ref[...]).astype(BF16)
        dob_ref[...] = _dot(dyb, wob_ref[...]).astype(BF16)

        @pl.when(i == 0)
        def _():
            dg1_ref[...] = part

        @pl.when(i > 0)
        def _():
            dg1_ref[...] += part

    row = lambda w: pl.BlockSpec((tr, w), lambda i: (i, 0))
    full = lambda a: pl.BlockSpec(a.shape, lambda i: (0, 0))
    return pl.pallas_call(
        body, name=name, grid=(lp // tr,),
        in_specs=[row(dm), row(dm), full(g1), full(w_out_t),
                  pl.BlockSpec((tr, dm), lambda i: (i, ga_blk)),
                  pl.BlockSpec((tr, dm), lambda i: (i, ga_blk + 1)),
                  row(dm), row(dm), full(w_oa_t), full(w_ob_t)],
        out_specs=[row(dm)] * 5 + [row(W_ATT)] * 2 + [pl.BlockSpec((1, dm), lambda i: (0, 0))],
        out_shape=[jax.ShapeDtypeStruct((lp, dm), BF16)] * 5 + [jax.ShapeDtypeStruct((lp, W_ATT), BF16)] * 2
        + [jax.ShapeDtypeStruct((1, dm), F32)],
        compiler_params=_cparams(("arbitrary",)),
    )(dh1, mixed, g1, w_out_t, proj, proj, y_a, y_b, w_oa_t, w_ob_t)


_GELU_C = math.sqrt(2.0 / math.pi)
_GELU_A = 0.044715


def _gelu(x):
    t = jnp.tanh(_GELU_C * (x + _GELU_A * x * x * x))
    return 0.5 * x * (1.0 + t), t


CW = 256


def _taps(cur_ref, prev_ref, first, c0):
    cur = cur_ref[:, c0:c0 + CW].astype(F32)
    p1 = jnp.where(first, 0.0, prev_ref[HALO - 1:HALO, c0:c0 + CW].astype(F32))
    p2 = jnp.where(first, 0.0, prev_ref[HALO - 2:HALO - 1, c0:c0 + CW].astype(F32))
    row = lax.broadcasted_iota(jnp.int32, cur.shape, 0)
    x1 = jnp.where(row == 0, p1, pltpu.roll(cur, 1, 0))
    x2 = jnp.where(row == 0, p2, jnp.where(row == 1, p1, pltpu.roll(cur, 2, 0)))
    return cur, x1, x2


def _conv_at(cur_ref, prev_ref, w_ref, b_ref, first, c0):
    cur, x1, x2 = _taps(cur_ref, prev_ref, first, c0)
    cols = slice(c0, c0 + CW)
    u = b_ref[:, cols] + w_ref[0:1, cols] * x2 + w_ref[1:2, cols] * x1 + w_ref[2:3, cols] * cur
    return u, (x2, x1, cur)


def _up_specs(tr, width):
    per = tr // HALO
    return [pl.BlockSpec((tr, width), lambda i: (i, 0)),
            pl.BlockSpec((HALO, width), lambda i: (jnp.maximum(i * per - 1, 0), 0))]


def _convgate_fwd(up, conv_w, conv_b, name):
    lp, c2 = up.shape
    tr = TCONV

    def body(cur_ref, prev_ref, w_ref, b_ref, a_ref):
        first = pl.program_id(0) == 0
        for c0 in range(0, D_FF, CW):
            ug, _ = _conv_at(cur_ref, prev_ref, w_ref, b_ref, first, c0)
            uv, _ = _conv_at(cur_ref, prev_ref, w_ref, b_ref, first, D_FF + c0)
            gel, _ = _gelu(ug)
            a_ref[:, c0:c0 + CW] = (gel * uv).astype(BF16)

    return pl.pallas_call(
        body, name=name, grid=(lp // tr,),
        in_specs=_up_specs(tr, c2) + [pl.BlockSpec((3, c2), lambda i: (0, 0)),
                                      pl.BlockSpec((1, c2), lambda i: (0, 0))],
        out_specs=pl.BlockSpec((tr, D_FF), lambda i: (i, 0)),
        out_shape=jax.ShapeDtypeStruct((lp, D_FF), BF16),
        compiler_params=_cparams(("parallel",)),
    )(up, up, conv_w, conv_b)


def _convgate_bwd(up, da, conv_w, conv_b, name):
    lp, c2 = up.shape
    tr = TCONV

    def body(cur_ref, prev_ref, da_ref, w_ref, b_ref, du_ref, dw_ref, db_ref):
        i = pl.program_id(0)
        first = i == 0

        @pl.when(first)
        def _():
            dw_ref[...] = jnp.zeros_like(dw_ref)
            db_ref[...] = jnp.zeros_like(db_ref)

        for c0 in range(0, D_FF, CW):
            ug, taps_g = _conv_at(cur_ref, prev_ref, w_ref, b_ref, first, c0)
            uv, taps_v = _conv_at(cur_ref, prev_ref, w_ref, b_ref, first, D_FF + c0)
            gel, t = _gelu(ug)
            dgel = 0.5 * (1.0 + t) + 0.5 * ug * (1.0 - t * t) * _GELU_C * (1.0 + 3.0 * _GELU_A * ug * ug)
            da_ = da_ref[:, c0:c0 + CW].astype(F32)
            for base, du, taps in ((c0, da_ * uv * dgel, taps_g), (D_FF + c0, da_ * gel, taps_v)):
                cols = slice(base, base + CW)
                du_ref[:, cols] = du.astype(BF16)
                for tap in range(3):
                    dw_ref[tap:tap + 1, cols] += jnp.sum(du * taps[tap], axis=0, keepdims=True)
                db_ref[:, cols] += jnp.sum(du, axis=0, keepdims=True)

    return pl.pallas_call(
        body, name=name, grid=(lp // tr,),
        in_specs=_up_specs(tr, c2) + [pl.BlockSpec((tr, D_FF), lambda i: (i, 0)),
                                      pl.BlockSpec((3, c2), lambda i: (0, 0)),
                                      pl.BlockSpec((1, c2), lambda i: (0, 0))],
        out_specs=[pl.BlockSpec((tr, c2), lambda i: (i, 0)),
                   pl.BlockSpec((3, c2), lambda i: (0, 0)),
                   pl.BlockSpec((1, c2), lambda i: (0, 0))],
        out_shape=[jax.ShapeDtypeStruct((lp, c2), BF16), jax.ShapeDtypeStruct((3, c2), F32),
                   jax.ShapeDtypeStruct((1, c2), F32)],
        compiler_params=_cparams(("arbitrary",)),
    )(up, up, da, conv_w, conv_b)


def _conv_transpose(du, conv_w, name):
    lp, c2 = du.shape
    tr = TCONV
    per = tr // HALO
    n_halo = lp // HALO
    nt = lp // tr

    def body(cur_ref, nxt_ref, w_ref, o_ref):
        last = pl.program_id(0) == nt - 1
        for c0 in range(0, c2, CW):
            cols = slice(c0, c0 + CW)
            cur = cur_ref[:, cols].astype(F32)
            n0 = jnp.where(last, 0.0, nxt_ref[0:1, cols].astype(F32))
            n1 = jnp.where(last, 0.0, nxt_ref[1:2, cols].astype(F32))
            row = lax.broadcasted_iota(jnp.int32, cur.shape, 0)
            y1 = jnp.where(row == tr - 1, n0, pltpu.roll(cur, tr - 1, 0))
            y2 = jnp.where(row == tr - 1, n1, jnp.where(row == tr - 2, n0, pltpu.roll(cur, tr - 2, 0)))
            o_ref[:, cols] = (w_ref[2:3, cols] * cur + w_ref[1:2, cols] * y1 + w_ref[0:1, cols] * y2).astype(BF16)

    return pl.pallas_call(
        body, name=name, grid=(nt,),
        in_specs=[pl.BlockSpec((tr, c2), lambda i: (i, 0)),
                  pl.BlockSpec((HALO, c2), lambda i: (jnp.minimum((i + 1) * per, n_halo - 1), 0)),
                  pl.BlockSpec((3, c2), lambda i: (0, 0))],
        out_specs=pl.BlockSpec((tr, c2), lambda i: (i, 0)),
        out_shape=jax.ShapeDtypeStruct((lp, c2), BF16),
        compiler_params=_cparams(("parallel",)),
    )(du, du, conv_w)


def _down_loss(a, w_down, h1, g3, target, name):
    lp, dm = h1.shape
    tr = TQ

    def body(a_ref, w_ref, h1_ref, g_ref, t_ref, ffn_ref, dy_ref, ss_ref):
        i = pl.program_id(0)
        ffn = _dot(a_ref[...], w_ref[...])
        ffn_ref[...] = ffn
        h2 = h1_ref[...] + ffn * _rms(ffn) * g_ref[...]
        d = jnp.where(i > 0, h2 - t_ref[...], 0.0)
        dy_ref[...] = d * (1.0 / dm)
        part = jnp.sum(jnp.sum(d * d, axis=0, keepdims=True), axis=1, keepdims=True)

        @pl.when(i == 0)
        def _():
            ss_ref[...] = jnp.zeros_like(ss_ref)

        ss_ref[...] += part

    return pl.pallas_call(
        body, name=name, grid=(lp // tr,),
        in_specs=[pl.BlockSpec((tr, D_FF), lambda i: (i, 0)),
                  pl.BlockSpec(w_down.shape, lambda i: (0, 0)),
                  pl.BlockSpec((tr, dm), lambda i: (i, 0)),
                  pl.BlockSpec((1, dm), lambda i: (0, 0)),
                  pl.BlockSpec((tr, dm), lambda i: (jnp.maximum(i - 1, 0), 0))],
        out_specs=[pl.BlockSpec((tr, dm), lambda i: (i, 0)),
                   pl.BlockSpec((tr, dm), lambda i: (i, 0)),
                   pl.BlockSpec((8, LANES), lambda i: (0, 0))],
        out_shape=[jax.ShapeDtypeStruct((lp, dm), F32), jax.ShapeDtypeStruct((lp, dm), F32),
                   jax.ShapeDtypeStruct((8, LANES), F32)],
        compiler_params=_cparams(("arbitrary",)),
    )(a, w_down, h1, g3, target)


def _down_bwd(dy, ffn, g3, w_down_t, name):
    lp, dm = dy.shape
    tr = TQ

    def body(dy_ref, f_ref, g_ref, w_ref, dffn_ref, da_ref, dg_ref):
        i = pl.program_id(0)
        dn = dy_ref[...]
        x = f_ref[...]
        r = _rms(x)
        yhat = x * r
        part = jnp.sum(dn * yhat, axis=0, keepdims=True)
        dyh = dn * g_ref[...]
        dffn = (r * (dyh - yhat * jnp.mean(dyh * yhat, axis=-1, keepdims=True))).astype(BF16)
        dffn_ref[...] = dffn
        da_ref[...] = _dot(dffn, w_ref[...]).astype(BF16)

        @pl.when(i == 0)
        def _():
            dg_ref[...] = part

        @pl.when(i > 0)
        def _():
            dg_ref[...] += part

    return pl.pallas_call(
        body, name=name, grid=(lp // tr,),
        in_specs=[pl.BlockSpec((tr, dm), lambda i: (i, 0)),
                  pl.BlockSpec((tr, dm), lambda i: (i, 0)),
                  pl.BlockSpec((1, dm), lambda i: (0, 0)),
                  pl.BlockSpec(w_down_t.shape, lambda i: (0, 0))],
        out_specs=[pl.BlockSpec((tr, dm), lambda i: (i, 0)),
                   pl.BlockSpec((tr, D_FF), lambda i: (i, 0)),
                   pl.BlockSpec((1, dm), lambda i: (0, 0))],
        out_shape=[jax.ShapeDtypeStruct((lp, dm), BF16), jax.ShapeDtypeStruct((lp, D_FF), BF16),
                   jax.ShapeDtypeStruct((1, dm), F32)],
        compiler_params=_cparams(("arbitrary",)),
    )(dy, ffn, g3, w_down_t)


def _pair_specs(lp, base):
    return [pl.BlockSpec((TQ, LANES), lambda p, i: (i, base + p)),
            pl.BlockSpec((lp, LANES), lambda p, i: (0, base + N_PAIRS + p)),
            pl.BlockSpec((lp, LANES), lambda p, i: (0, base + 2 * N_PAIRS + p))]


def _col_spec():
    return pl.BlockSpec((None, 2, TQ, 1), lambda p, i: (p, 0, i, 0))


def _rowvec_spec(nb):
    return pl.BlockSpec((None, 2, nb, 1, TQ), lambda p, i: (p, 0, 0, 0, 0))


def _tile_spec():
    return pl.BlockSpec((TQ, LANES), lambda p, i: (i, p))


def _head_masks():
    lane = lax.broadcasted_iota(jnp.int32, (TQ, LANES), 1)
    return lane < HEAD_DIM, lane >= HEAD_DIM


def _positions(i):
    qpos = i * TQ + lax.broadcasted_iota(jnp.int32, (TQ, TQ), 0)
    kloc = lax.broadcasted_iota(jnp.int32, (TQ, TQ), 1)
    return qpos, kloc


def _ktile(ref, j):
    return ref[pl.ds(pl.multiple_of(j * TQ, TQ), TQ), :]


def _fox_fwd(proj, ccol, crow, name):
    lp = proj.shape[0]
    nb = lp // TQ

    def body(q_ref, k_ref, v_ref, cc_ref, cr_ref, o_ref, lse_ref):
        i = pl.program_id(1)
        q = q_ref[...]
        qpos, kloc = _positions(i)
        outs = []
        for h, hm in enumerate(_head_masks()):
            qh = jnp.where(hm, q, jnp.zeros_like(q)) * 0.125
            cc = cc_ref[h]

            def step(j, carry, qh=qh, cc=cc, h=h):
                m, l, acc = carry
                s = _dot_nt(qh, _ktile(k_ref, j)) + cc - cr_ref[h, j]
                kpos = j * TQ + kloc
                s = jnp.where((kpos <= qpos) & (kpos >= PAD), s, NEG)
                m_new = jnp.maximum(m, jnp.max(s, axis=-1, keepdims=True))
                alpha = jnp.exp(m - m_new)
                p = jnp.exp(s - m_new)
                l = alpha * l + jnp.sum(p, axis=-1, keepdims=True)
                acc = alpha * acc + _dot(p.astype(BF16), _ktile(v_ref, j))
                return m_new, l, acc

            m, l, acc = lax.fori_loop(
                0, i + 1, step,
                (jnp.full((TQ, 1), NEG, F32), jnp.zeros((TQ, 1), F32), jnp.zeros((TQ, LANES), F32)))
            outs.append(acc / l)
            lse_ref[h] = m + jnp.log(l)
        o_ref[...] = jnp.where(_head_masks()[0], outs[0], outs[1]).astype(BF16)

    return pl.pallas_call(
        body, name=name, grid=(N_PAIRS, nb),
        in_specs=_pair_specs(lp, 0) + [_col_spec(), _rowvec_spec(nb)],
        out_specs=[_tile_spec(), _col_spec()],
        out_shape=[jax.ShapeDtypeStruct((lp, W_ATT), BF16),
                   jax.ShapeDtypeStruct((N_PAIRS, 2, lp, 1), F32)],
        compiler_params=_cparams(("parallel", "arbitrary")),
    )(proj, proj, proj, ccol, crow)


def _fox_bwd(proj, do, o, lse, ccol, crow, name):
    lp = proj.shape[0]
    nb = lp // TQ

    def body(q_ref, k_ref, v_ref, do_ref, o_ref, lse_ref, cc_ref, cr_ref,
             dq_ref, dk_ref, dv_ref, dcs_ref, dct_ref, dk_acc, dv_acc):
        i = pl.program_id(1)

        @pl.when(i == 0)
        def _():
            dk_acc[...] = jnp.zeros_like(dk_acc)
            dv_acc[...] = jnp.zeros_like(dv_acc)
            dcs_ref[...] = jnp.zeros_like(dcs_ref)

        q = q_ref[...]
        do_ = do_ref[...]
        prod = do_.astype(F32) * o_ref[...].astype(F32)
        qpos, kloc = _positions(i)
        dqs = []
        for h, hm in enumerate(_head_masks()):
            qh = jnp.where(hm, q, jnp.zeros_like(q)) * 0.125
            doh = jnp.where(hm, do_, jnp.zeros_like(do_))
            delta = jnp.sum(jnp.where(hm, prod, 0.0), axis=-1, keepdims=True)
            cc = cc_ref[h]
            lse_h = lse_ref[h]

            def step(j, carry, qh=qh, doh=doh, delta=delta, cc=cc, lse_h=lse_h, h=h):
                dq, dct = carry
                rows = pl.ds(pl.multiple_of(j * TQ, TQ), TQ)
                k = k_ref[rows, :]
                s = _dot_nt(qh, k) + cc - cr_ref[h, j]
                kpos = j * TQ + kloc
                p = jnp.where((kpos <= qpos) & (kpos >= PAD), jnp.exp(s - lse_h), 0.0)
                dp = _dot_nt(doh, v_ref[rows, :])
                ds = p * (dp - delta)
                dsb = ds.astype(BF16)
                dk_acc[rows, :] += _dot_tn(dsb, qh)
                dv_acc[rows, :] += _dot_tn(p.astype(BF16), doh)
                dcs_ref[h, j] -= jnp.sum(ds, axis=0, keepdims=True)
                return dq + _dot(dsb, k), dct + jnp.sum(ds, axis=-1, keepdims=True)

            dq, dct = lax.fori_loop(0, i + 1, step,
                                    (jnp.zeros((TQ, LANES), F32), jnp.zeros((TQ, 1), F32)))
            dqs.append(dq)
            dct_ref[h] = dct
        dq_ref[...] = (jnp.where(_head_masks()[0], dqs[0], dqs[1]) * 0.125).astype(BF16)

        @pl.when(i == nb - 1)
        def _():
            dk_ref[...] = dk_acc[...].astype(BF16)
            dv_ref[...] = dv_acc[...].astype(BF16)

    whole = pl.BlockSpec((lp, LANES), lambda p, i: (0, p))
    return pl.pallas_call(
        body, name=name, grid=(N_PAIRS, nb),
        in_specs=_pair_specs(lp, 0) + [_tile_spec(), _tile_spec(), _col_spec(), _col_spec(), _rowvec_spec(nb)],
        out_specs=[_tile_spec(), whole, whole, _rowvec_spec(nb), _col_spec()],
        out_shape=[jax.ShapeDtypeStruct((lp, W_ATT), BF16)] * 3
        + [jax.ShapeDtypeStruct((N_PAIRS, 2, nb, 1, TQ), F32), jax.ShapeDtypeStruct((N_PAIRS, 2, lp, 1), F32)],
        scratch_shapes=[pltpu.VMEM((lp, LANES), F32)] * 2,
        compiler_params=_cparams(("parallel", "arbitrary")),
    )(proj, proj, proj, do, o, lse, ccol, crow)


def _sb_fwd(proj, name):
    lp = proj.shape[0]
    nb = lp // TQ
    tsuf = _tri(TQ, lambda r, c: r > c)

    def body(q_ref, k_ref, v_ref, t_ref, o_ref, lt_ref):
        i = pl.program_id(1)
        q = q_ref[...]
        qpos, kloc = _positions(i)
        outs = []
        for h, hm in enumerate(_head_masks()):
            qh = jnp.where(hm, q, jnp.zeros_like(q)) * 0.125

            def step(jj, carry, qh=qh):
                run, acc = carry
                j = i - jj
                z = _dot_nt(qh, _ktile(k_ref, j))
                kpos = j * TQ + kloc
                valid = (kpos < qpos) & (kpos >= PAD)
                lk = jnp.where(valid, _log_sigmoid(-z), 0.0)
                hi, lo = _split2(lk)
                t = t_ref[...]
                later = run + (_dot(hi, t) + _dot(lo, t))
                a = jnp.where(valid, jnp.exp(z + lk + later), 0.0)
                acc = acc + _dot(a.astype(BF16), _ktile(v_ref, j))
                return run + jnp.sum(lk, axis=-1, keepdims=True), acc

            run, acc = lax.fori_loop(0, i + 1, step,
                                     (jnp.zeros((TQ, 1), F32), jnp.zeros((TQ, LANES), F32)))
            outs.append(acc)
            lt_ref[h] = run
        o_ref[...] = jnp.where(_head_masks()[0], outs[0], outs[1]).astype(BF16)

    base = 3 * N_PAIRS
    return pl.pallas_call(
        body, name=name, grid=(N_PAIRS, nb),
        in_specs=_pair_specs(lp, base) + [pl.BlockSpec((TQ, TQ), lambda p, i: (0, 0))],
        out_specs=[_tile_spec(), _col_spec()],
        out_shape=[jax.ShapeDtypeStruct((lp, W_ATT), BF16),
                   jax.ShapeDtypeStruct((N_PAIRS, 2, lp, 1), F32)],
        compiler_params=_cparams(("parallel", "arbitrary")),
    )(proj, proj, proj, tsuf)


def _sb_bwd(proj, do, ltot, name):
    lp = proj.shape[0]
    nb = lp // TQ
    tincl = _tri(TQ, lambda r, c: r <= c)
    texcl = _tri(TQ, lambda r, c: r < c)

    def body(q_ref, k_ref, v_ref, do_ref, lt_ref, ti_ref, te_ref,
             dq_ref, dk_ref, dv_ref, dk_acc, dv_acc):
        i = pl.program_id(1)

        @pl.when(i == 0)
        def _():
            dk_acc[...] = jnp.zeros_like(dk_acc)
            dv_acc[...] = jnp.zeros_like(dv_acc)

        q = q_ref[...]
        do_ = do_ref[...]
        qpos, kloc = _positions(i)
        dqs = []
        for h, hm in enumerate(_head_masks()):
            qh = jnp.where(hm, q, jnp.zeros_like(q)) * 0.125
            doh = jnp.where(hm, do_, jnp.zeros_like(do_))
            lt = lt_ref[h]

            def step(j, carry, qh=qh, doh=doh, lt=lt):
                dq, pc, gc = carry
                rows = pl.ds(pl.multiple_of(j * TQ, TQ), TQ)
                k = k_ref[rows, :]
                z = _dot_nt(qh, k)
                kpos = j * TQ + kloc
                valid = (kpos < qpos) & (kpos >= PAD)
                lk = jnp.where(valid, _log_sigmoid(-z), 0.0)
                hi, lo = _split2(lk)
                ti = ti_ref[...]
                later = lt - (pc + (_dot(hi, ti) + _dot(lo, ti)))
                a = jnp.where(valid, jnp.exp(z + lk + later), 0.0)
                g = a * _dot_nt(doh, v_ref[rows, :])
                ghi, glo = _split2(g)
                te = te_ref[...]
                gbefore = gc + (_dot(ghi, te) + _dot(glo, te))
                keep = jnp.exp(lk)
                dz = jnp.where(valid, g * keep - (1.0 - keep) * gbefore, 0.0)
                dzb = dz.astype(BF16)
                dk_acc[rows, :] += _dot_tn(dzb, qh)
                dv_acc[rows, :] += _dot_tn(a.astype(BF16), doh)
                return (dq + _dot(dzb, k), pc + jnp.sum(lk, axis=-1, keepdims=True),
                        gc + jnp.sum(g, axis=-1, keepdims=True))

            zc = jnp.zeros((TQ, 1), F32)
            dq, _, _ = lax.fori_loop(0, i + 1, step, (jnp.zeros((TQ, LANES), F32), zc, zc))
            dqs.append(dq)
        dq_ref[...] = (jnp.where(_head_masks()[0], dqs[0], dqs[1]) * 0.125).astype(BF16)

        @pl.when(i == nb - 1)
        def _():
            dk_ref[...] = dk_acc[...].astype(BF16)
            dv_ref[...] = dv_acc[...].astype(BF16)

    base = 3 * N_PAIRS
    whole = pl.BlockSpec((lp, LANES), lambda p, i: (0, p))
    tri = pl.BlockSpec((TQ, TQ), lambda p, i: (0, 0))
    return pl.pallas_call(
        body, name=name, grid=(N_PAIRS, nb),
        in_specs=_pair_specs(lp, base) + [_tile_spec(), _col_spec(), tri, tri],
        out_specs=[_tile_spec(), whole, whole],
        out_shape=[jax.ShapeDtypeStruct((lp, W_ATT), BF16)] * 3,
        scratch_shapes=[pltpu.VMEM((lp, LANES), F32)] * 2,
        compiler_params=_cparams(("parallel", "arbitrary")),
    )(proj, proj, proj, do, ltot, tincl, texcl)


def _local_step(x, target, meta, gains, w_in, b_forget, w_o_fox, w_o_sb, w_out, w_up, conv_w, conv_b, w_down):
    seq, dm = x.shape
    lp = PAD + N_META + seq
    nb = lp // TQ
    s = [W_ATT, W_ATT, W_ATT, 8, W_ATT, W_ATT, W_ATT, dm, dm]
    off = [sum(s[:i]) for i in range(len(s) + 1)]
    cols = lambda i: w_in[:, off[i]:off[i + 1]]
    w1 = jnp.concatenate([cols(0), cols(1), cols(2), cols(4), cols(5), cols(6), cols(7), cols(8)], axis=1)
    wf = jnp.pad(cols(3), ((0, 0), (0, LANES - 8)))
    n1 = w1.shape[1]
    ncat = n1 + 512
    w_cat_t = jnp.concatenate([w1, wf, jnp.zeros((dm, ncat - n1 - LANES), BF16)], axis=1).T
    bf = jnp.pad(b_forget.reshape(1, 8), ((0, 0), (0, LANES - 8)))
    g = [gains[i].reshape(1, dm) for i in range(4)]
    cb = conv_b.reshape(1, -1)

    h0 = jnp.concatenate([jnp.zeros((PAD, dm), F32), meta, x], axis=0)

    proj, xn1 = _rms_mm(h0, g[0], w1, "in_proj")
    logf = _logf(xn1, wf, bf, "log_forget")
    c = _cumsum_rows(logf, "forget_cumsum")
    c8 = c[:, :8].T.reshape(N_PAIRS, 2, lp)
    ccol = c8.reshape(N_PAIRS, 2, lp, 1)
    crow = c8.reshape(N_PAIRS, 2, nb, 1, TQ)
    o_a, lse = _fox_fwd(proj, ccol, crow, "fox_fwd")
    o_b, ltot = _sb_fwd(proj, "sb_fwd")
    y_a, y_b, m, mixed, h1 = _merge_fwd(o_a, o_b, proj, h0, w_o_fox, w_o_sb, w_out, g[1], "merge_fwd")
    up, xn3 = _rms_mm(h1, g[2], w_up, "up_proj")
    a = _convgate_fwd(up, conv_w, cb, "convgate_fwd")
    ffn, dy, ss = _down_loss(a, w_down, h1, g[3], target, "down_loss")

    dffn, da, dg3 = _down_bwd(dy, ffn, g[3], w_down.T, "down_bwd")
    d_w_down = _mm_tn(a, dffn, "dw_down")
    du, d_conv_w, d_conv_b = _convgate_bwd(up, da, conv_w, cb, "convgate_bwd")
    dup = _conv_transpose(du, conv_w, "conv_transpose")
    d_w_up = _mm_tn(xn3, dup, "dw_up")
    dh1, dg2 = _mm_rmsbwd(dup, w_up.T, h1, g[2], dy, "up_bwd")
    dmx, dya, dyb, dga, dgb, do_a, do_b, dg1 = _merge_bwd(
        dh1, mixed, g[1], w_out.T, proj, y_a, y_b, w_o_fox.T, w_o_sb.T, "merge_bwd")
    d_w_out = _mm_tn(m, dmx, "dw_out")
    d_w_o_fox = _mm_tn(o_a, dya, "dw_o_fox")
    d_w_o_sb = _mm_tn(o_b, dyb, "dw_o_sb")
    dq_a, dk_a, dv_a, dcs, dct = _fox_bwd(proj, do_a, o_a, lse, ccol, crow, "fox_bwd")
    dq_b, dk_b, dv_b = _sb_bwd(proj, do_b, ltot, "sb_bwd")
    dc = (dct.reshape(8, lp) + dcs.reshape(8, lp)).T
    df, db = _dlogf(jnp.pad(dc, ((0, 0), (0, LANES - 8))), logf, "forget_bwd")
    dcat = jnp.concatenate([dq_a, dk_a, dv_a, dq_b, dk_b, dv_b, dga, dgb, df.astype(BF16),
                            jnp.zeros((lp, ncat - n1 - LANES), BF16)], axis=1)
    d_w_cat = _mm_tn(xn1, dcat, "dw_in")
    dh0, dg0 = _mm_rmsbwd(dcat, w_cat_t, h0, g[0], dh1, "in_bwd")

    wc = lambda k: d_w_cat[:, k * W_ATT:(k + 1) * W_ATT]
    d_w_in = jnp.concatenate([wc(0), wc(1), wc(2), d_w_cat[:, n1:n1 + 8], wc(3), wc(4), wc(5),
                              d_w_cat[:, 6 * W_ATT:n1]], axis=1)
    d_gains = jnp.concatenate([dg0, dg1, dg2, dg3], axis=0)
    grads = (dh0[PAD:PAD + N_META], d_gains, d_w_in, db[0, :8], d_w_o_fox, d_w_o_sb, d_w_out,
             d_w_up, d_conv_w, d_conv_b[0], d_w_down)
    return ss[0, 0], dh0[PAD + N_META:], grads


def _rows(a, n_rows):
    flat = a.reshape(-1)
    return jnp.pad(flat, (0, n_rows * D_MODEL - flat.shape[0])).reshape(n_rows, D_MODEL)


_SMALL = (("meta", 4), ("gains", 1), ("conv_w", 5), ("b_forget", 1), ("conv_b", 6))


def _pack(meta, gains, w_in, b_forget, w_o_fox, w_o_sb, w_out, w_up, conv_w, conv_b, w_down):
    big = [a.reshape(-1, D_MODEL) for a in (w_in, w_o_fox, w_o_sb, w_out, w_up, w_down)]
    n_big = sum(a.shape[0] for a in big)
    small = [_rows(a, n) for a, (_, n) in zip((meta, gains, conv_w, b_forget, conv_b), _SMALL)]
    n_small = sum(n for _, n in _SMALL)
    z = lambda n: jnp.zeros((n, D_MODEL), big[0].dtype)
    return jnp.concatenate(big + [z(PK_BIG_ROWS - n_big)] + small
                           + [z(PK_ROWS - PK_BIG_ROWS - n_small)], axis=0)


def _unpack(p):
    def take(r0, shape):
        n = math.prod(shape)
        nr = -(-n // D_MODEL)
        return p[r0:r0 + nr].reshape(-1)[:n].reshape(shape), r0 + nr
    w_in, r = take(0, (1, 1024, 1282))
    w_o_fox, r = take(r, (1, 512, 256))
    w_o_sb, r = take(r, (1, 512, 256))
    w_out, r = take(r, (1, 256, 1024))
    w_up, r = take(r, (1, 1024, 1408))
    w_down, r = take(r, (1, 704, 1024))
    r = PK_BIG_ROWS
    meta, r = take(r, (16, 256))
    gains, r = take(r, (1, 4, 256))
    conv_w, r = take(r, (1, 3, 1408))
    b_forget, r = take(r, (1, 8))
    conv_b, r = take(r, (1, 5632))
    return meta, gains, w_in, b_forget, w_o_fox, w_o_sb, w_out, w_up, conv_w, conv_b, w_down


def _chip_peers():
    x, y, c = lax.axis_index("x"), lax.axis_index("y"), lax.axis_index("c")
    return [(x, 1 - y, c), (1 - x, y, c), (1 - x, 1 - y, c)]


def _all_gather_chips(arrays, name):
    n = len(arrays)

    def body(*refs):
        ins, outs = refs[:n], refs[n:2 * n]
        send_sems, recv_sems, local_sems = refs[2 * n:]
        x, y = lax.axis_index("x"), lax.axis_index("y")
        me = 2 * x + y
        peers = _chip_peers()
        copies = []
        for a in range(n):
            mine = pltpu.make_async_copy(ins[a], outs[a].at[me], local_sems.at[a])
            mine.start()
            copies.append(mine)
        remote = []
        for a in range(n):
            for j, peer in enumerate(peers):
                cp = pltpu.make_async_remote_copy(
                    src_ref=ins[a], dst_ref=outs[a].at[me],
                    send_sem=send_sems.at[3 * a + j], recv_sem=recv_sems.at[3 * a + j],
                    device_id=peer, device_id_type=MESH)
                cp.start()
                remote.append(cp)
        for cp in remote:
            cp.wait()
        for cp in copies:
            cp.wait()

    any_spec = pl.BlockSpec(memory_space=pl.ANY)
    return pl.pallas_call(
        body, name=name,
        in_specs=[any_spec] * n, out_specs=[any_spec] * n,
        out_shape=[jax.ShapeDtypeStruct((4,) + a.shape, a.dtype) for a in arrays],
        scratch_shapes=[pltpu.SemaphoreType.DMA((3 * n,)), pltpu.SemaphoreType.DMA((3 * n,)),
                        pltpu.SemaphoreType.DMA((n,))],
    )(*arrays)


def _scatter_chips(chunks, name):
    _, rows, cols = chunks.shape

    def body(in_ref, out_ref, send_sems, recv_sems):
        x, y = lax.axis_index("x"), lax.axis_index("y")
        targets = [2 * x + (1 - y), 2 * (1 - x) + y, 2 * (1 - x) + (1 - y)]
        remote = []
        for j, peer in enumerate(_chip_peers()):
            cp = pltpu.make_async_remote_copy(
                src_ref=in_ref.at[targets[j]], dst_ref=out_ref.at[j],
                send_sem=send_sems.at[j], recv_sem=recv_sems.at[j],
                device_id=peer, device_id_type=MESH)
            cp.start()
            remote.append(cp)
        for cp in remote:
            cp.wait()

    any_spec = pl.BlockSpec(memory_space=pl.ANY)
    return pl.pallas_call(
        body, name=name, in_specs=[any_spec], out_specs=any_spec,
        out_shape=jax.ShapeDtypeStruct((3, rows, cols), chunks.dtype),
        scratch_shapes=[pltpu.SemaphoreType.DMA((3,)), pltpu.SemaphoreType.DMA((3,))],
    )(chunks)


def _swap_cores(a, name):
    def body(in_ref, out_ref, send_sem, recv_sem):
        x, y, c = lax.axis_index("x"), lax.axis_index("y"), lax.axis_index("c")
        cp = pltpu.make_async_remote_copy(
            src_ref=in_ref, dst_ref=out_ref, send_sem=send_sem, recv_sem=recv_sem,
            device_id=(x, y, 1 - c), device_id_type=MESH)
        cp.start()
        cp.wait()

    any_spec = pl.BlockSpec(memory_space=pl.ANY)
    return pl.pallas_call(
        body, name=name, in_specs=[any_spec], out_specs=any_spec,
        out_shape=jax.ShapeDtypeStruct(a.shape, a.dtype),
        scratch_shapes=[pltpu.SemaphoreType.DMA, pltpu.SemaphoreType.DMA],
    )(a)


def _chip_sum(chunks, recv, name):
    _, rows, cols = chunks.shape

    def body(own_ref, r_ref, o_ref):
        o_ref[...] = (own_ref[...] + r_ref[0]) + (r_ref[1] + r_ref[2])

    me = 2 * lax.axis_index("x") + lax.axis_index("y")
    own = lax.dynamic_index_in_dim(chunks, me, axis=0, keepdims=False)
    return pl.pallas_call(
        body, name=name, grid=(rows // PK_TILE,),
        in_specs=[pl.BlockSpec((PK_TILE, cols), lambda i: (i, 0)),
                  pl.BlockSpec((3, PK_TILE, cols), lambda i: (0, i, 0))],
        out_specs=pl.BlockSpec((PK_TILE, cols), lambda i: (i, 0)),
        out_shape=jax.ShapeDtypeStruct((rows, cols), F32),
        compiler_params=_cparams(("parallel",)),
    )(own, recv)


def _adamw(w, m, v, g_mine, g_other, name):
    rows, cols = w.shape
    c1 = 1.0 - ADAM_B1 ** ADAM_STEP
    c2 = 1.0 - ADAM_B2 ** ADAM_STEP

    def body(w_ref, m_ref, v_ref, ga_ref, gb_ref, g_ref, d_ref, nm_ref, nv_ref):
        g = ga_ref[...] + gb_ref[...]
        nm = ADAM_B1 * m_ref[...] + (1.0 - ADAM_B1) * g
        nv = ADAM_B2 * v_ref[...] + (1.0 - ADAM_B2) * (g * g)
        g_ref[...] = g
        nm_ref[...] = nm
        nv_ref[...] = nv
        d_ref[...] = -ADAM_LR * ((nm / c1) / (jnp.sqrt(nv / c2) + ADAM_EPS) + ADAM_WD * w_ref[...])

    spec = pl.BlockSpec((PK_TILE, cols), lambda i: (i, 0))
    return pl.pallas_call(
        body, name=name, grid=(rows // PK_TILE,),
        in_specs=[spec] * 5, out_specs=[spec] * 4,
        out_shape=[jax.ShapeDtypeStruct((rows, cols), F32)] * 4,
        compiler_params=_cparams(("parallel",)),
    )(w, m, v, g_mine, g_other)


def _full_weights(big, small):
    def gather(src, r0, shape, axis):
        n = math.prod(shape)
        nr = -(-n // D_MODEL)
        parts = [src[k, r0:r0 + nr].reshape(-1)[:n].reshape(shape) for k in range(4)]
        return jnp.concatenate(parts, axis=axis), r0 + nr
    w_in, r = gather(big, 0, (1024, 1282), 1)
    w_o_fox, r = gather(big, r, (512, 256), 1)
    w_o_sb, r = gather(big, r, (512, 256), 1)
    w_out, r = gather(big, r, (256, 1024), 0)
    w_up, r = gather(big, r, (1024, 1408), 1)
    w_down, r = gather(big, r, (704, 1024), 0)
    meta, r = gather(small, 0, (16, 256), 1)
    gains, r = gather(small, r, (4, 256), 1)
    conv_w, r = gather(small, r, (3, 1408), 1)
    return meta, gains, w_in, w_o_fox, w_o_sb, w_out, w_up, conv_w, w_down


def _chunks_for_chips(grads):
    d_meta, d_gains, d_w_in, d_b, d_w_o_fox, d_w_o_sb, d_w_out, d_w_up, d_conv_w, d_conv_b, d_w_down = grads
    out = []
    for k in range(4):
        col = lambda a, w: a[:, k * w:(k + 1) * w]
        row = lambda a, w: a[k * w:(k + 1) * w]
        out.append(_pack(col(d_meta, 256), col(d_gains, 256), col(d_w_in, 1282), d_b, col(d_w_o_fox, 256),
                         col(d_w_o_sb, 256), row(d_w_out, 256), col(d_w_up, 1408), col(d_conv_w, 1408),
                         d_conv_b, row(d_w_down, 704)))
    return jnp.stack(out, axis=0)


def kernel(x, meta_tokens, norm_gains, w_in, b_forget, w_o_fox, w_o_sb, w_out, w_up, conv_w, conv_b, w_down, loss_target, m_meta_tokens, m_norm_gains, m_w_in, m_b_forget, m_w_o_fox, m_w_o_sb, m_w_out, m_w_up, m_conv_w, m_conv_b, m_w_down, v_meta_tokens, v_norm_gains, v_w_in, v_b_forget, v_w_o_fox, v_w_o_sb, v_w_out, v_w_up, v_conv_w, v_conv_b, v_w_down):
    shard = lambda mt, ng, wi, bf, wof, wos, wo, wu, cw, cb, wd: _pack(
        mt, ng[0], wi[0], bf[0], wof[0], wos[0], wo[0], wu[0], cw[0], cb[0], wd[0])
    wp = shard(meta_tokens, norm_gains, w_in, b_forget, w_o_fox, w_o_sb, w_out, w_up, conv_w, conv_b, w_down)
    mp = shard(m_meta_tokens, m_norm_gains, m_w_in, m_b_forget, m_w_o_fox, m_w_o_sb, m_w_out, m_w_up,
               m_conv_w, m_conv_b, m_w_down)
    vp = shard(v_meta_tokens, v_norm_gains, v_w_in, v_b_forget, v_w_o_fox, v_w_o_sb, v_w_out, v_w_up,
               v_conv_w, v_conv_b, v_w_down)

    big, small = _all_gather_chips(
        [wp[:PK_BIG_ROWS].astype(BF16), wp[PK_BIG_ROWS:PK_BIG_ROWS + PK_SMALL_ROWS]], "gather_weights")
    meta, gains, f_w_in, f_w_o_fox, f_w_o_sb, f_w_out, f_w_up, f_conv_w, f_w_down = _full_weights(big, small)

    ss, dx, grads = _local_step(x[0], loss_target[0], meta, gains, f_w_in, b_forget[0], f_w_o_fox, f_w_o_sb,
                                f_w_out, f_w_up, f_conv_w, conv_b[0], f_w_down)
    loss = lax.psum(0.5 * ss / D_MODEL, ("x", "y", "c"))

    chunks = _chunks_for_chips(grads)
    recv = _scatter_chips(chunks, "scatter_grads")
    mine = _chip_sum(chunks, recv, "chip_sum")
    other = _swap_cores(mine, "swap_cores")
    g, delta, new_m, new_v = _adamw(wp, mp, vp, mine, other, "adamw")
    return (loss, dx[None], *_unpack(g), *_unpack(delta), *_unpack(new_m), *_unpack(new_v))
```

```python
import functools
import math

import jax
import jax.numpy as jnp
from jax import lax
from jax.experimental import pallas as pl
from jax.experimental.pallas import tpu as pltpu

F32 = jnp.float32
BF16 = jnp.bfloat16

D_MODEL = 1024
N_META = 16
HEAD_DIM = 64
N_PAIRS = 4
W_ATT = 512
D_FF = 2816
EPS = 1e-6
NEG = -1e30
TQ = 256
PAD = TQ - N_META
TCONV = 128
HALO = 16
LANES = 128
VMEM_LIMIT = 56 * 1024 * 1024

ADAM_LR = 0.001
ADAM_B1 = 0.9
ADAM_B2 = 0.999
ADAM_EPS = 1e-08
ADAM_WD = 0.01
ADAM_STEP = 10

MESH = pl.DeviceIdType.MESH

PK_BIG_ROWS = 3920
PK_SMALL_ROWS = 24
PK_ROWS = 3968
PK_TILE = 128


def _cparams(sem, **kw):
    return pltpu.CompilerParams(dimension_semantics=sem, vmem_limit_bytes=VMEM_LIMIT, **kw)


def _row_tile(lp):
    return 768 if lp % 768 == 0 else 256


def _rms(x):
    return lax.rsqrt(jnp.mean(x * x, axis=-1, keepdims=True) + EPS)


def _log_sigmoid(x):
    return jnp.minimum(x, 0.0) - jnp.log(1.0 + jnp.exp(-jnp.abs(x)))


def _split2(x):
    hi = x.astype(BF16)
    lo = (x - hi.astype(F32)).astype(BF16)
    return hi, lo


def _dot(a, b):
    return jnp.dot(a, b, preferred_element_type=F32)


def _dot_nt(a, b):
    return lax.dot_general(a, b, (((1,), (1,)), ((), ())), preferred_element_type=F32)


def _dot_tn(a, b):
    return lax.dot_general(a, b, (((0,), (0,)), ((), ())), preferred_element_type=F32)


def _rms_mm(h, g, w, name):
    lp, dm = h.shape
    n = w.shape[1]
    tr, tn = _row_tile(lp), 512

    def body(h_ref, g_ref, w_ref, out_ref, xn_ref):
        @pl.when(pl.program_id(1) == 0)
        def _():
            x = h_ref[...]
            xn_ref[...] = (x * _rms(x) * g_ref[...]).astype(BF16)
        out_ref[...] = _dot(xn_ref[...], w_ref[...]).astype(BF16)

    return pl.pallas_call(
        body, name=name, grid=(lp // tr, n // tn),
        in_specs=[pl.BlockSpec((tr, dm), lambda i, j: (i, 0)),
                  pl.BlockSpec((1, dm), lambda i, j: (0, 0)),
                  pl.BlockSpec((dm, tn), lambda i, j: (0, j))],
        out_specs=[pl.BlockSpec((tr, tn), lambda i, j: (i, j)),
                   pl.BlockSpec((tr, dm), lambda i, j: (i, 0))],
        out_shape=[jax.ShapeDtypeStruct((lp, n), BF16), jax.ShapeDtypeStruct((lp, dm), BF16)],
        compiler_params=_cparams(("parallel", "arbitrary")),
    )(h, g, w)


def _mm_rmsbwd(dy, wt, h, g, dh_in, name):
    lp, kd = dy.shape
    dm = wt.shape[1]
    tr, tk = 384, 512
    nk = kd // tk

    def body(dy_ref, wt_ref, h_ref, g_ref, dhin_ref, dh_ref, dg_ref, acc_ref):
        i, k = pl.program_id(0), pl.program_id(1)

        @pl.when(k == 0)
        def _():
            acc_ref[...] = jnp.zeros_like(acc_ref)

        acc_ref[...] += _dot(dy_ref[...], wt_ref[...])

        @pl.when(k == nk - 1)
        def _():
            dxn = acc_ref[...]
            x = h_ref[...]
            r = _rms(x)
            yhat = x * r
            part = jnp.sum(dxn * yhat, axis=0, keepdims=True)
            dyh = dxn * g_ref[...]
            dx = r * (dyh - yhat * jnp.mean(dyh * yhat, axis=-1, keepdims=True))
            dh_ref[...] = dhin_ref[...] + dx

            @pl.when(i == 0)
            def _():
                dg_ref[...] = part

            @pl.when(i > 0)
            def _():
                dg_ref[...] += part

    return pl.pallas_call(
        body, name=name, grid=(lp // tr, nk),
        in_specs=[pl.BlockSpec((tr, tk), lambda i, k: (i, k)),
                  pl.BlockSpec((tk, dm), lambda i, k: (k, 0)),
                  pl.BlockSpec((tr, dm), lambda i, k: (i, 0)),
                  pl.BlockSpec((1, dm), lambda i, k: (0, 0)),
                  pl.BlockSpec((tr, dm), lambda i, k: (i, 0))],
        out_specs=[pl.BlockSpec((tr, dm), lambda i, k: (i, 0)),
                   pl.BlockSpec((1, dm), lambda i, k: (0, 0))],
        out_shape=[jax.ShapeDtypeStruct((lp, dm), F32), jax.ShapeDtypeStruct((1, dm), F32)],
        scratch_shapes=[pltpu.VMEM((tr, dm), F32)],
        compiler_params=_cparams(("arbitrary", "arbitrary")),
    )(dy, wt, h, g, dh_in)


def _mm_tn(x, dy, name):
    lp, kd = x.shape
    n = dy.shape[1]
    tl = _row_tile(lp)
    tk = 512 if kd % 512 == 0 else 256
    tn = 512
    nl = lp // tl

    def body(x_ref, dy_ref, o_ref):
        @pl.when(pl.program_id(2) == 0)
        def _():
            o_ref[...] = jnp.zeros_like(o_ref)
        o_ref[...] += _dot_tn(x_ref[...], dy_ref[...])

    return pl.pallas_call(
        body, name=name, grid=(kd // tk, n // tn, nl),
        in_specs=[pl.BlockSpec((tl, tk), lambda a, b, l: (l, a)),
                  pl.BlockSpec((tl, tn), lambda a, b, l: (l, b))],
        out_specs=pl.BlockSpec((tk, tn), lambda a, b, l: (a, b)),
        out_shape=jax.ShapeDtypeStruct((kd, n), F32),
        compiler_params=_cparams(("parallel", "parallel", "arbitrary")),
    )(x, dy)


def _logf(xn, wf, bf, name):
    lp, dm = xn.shape
    tr = _row_tile(lp)

    def body(xn_ref, wf_ref, b_ref, o_ref):
        f = _dot(xn_ref[...], wf_ref[...]) + b_ref[...]
        row = pl.program_id(0) * tr + lax.broadcasted_iota(jnp.int32, f.shape, 0)
        lane = lax.broadcasted_iota(jnp.int32, f.shape, 1)
        o_ref[...] = jnp.where((row >= PAD) & (lane < 8), _log_sigmoid(f), 0.0)

    return pl.pallas_call(
        body, name=name, grid=(lp // tr,),
        in_specs=[pl.BlockSpec((tr, dm), lambda i: (i, 0)),
                  pl.BlockSpec((dm, LANES), lambda i: (0, 0)),
                  pl.BlockSpec((1, LANES), lambda i: (0, 0))],
        out_specs=pl.BlockSpec((tr, LANES), lambda i: (i, 0)),
        out_shape=jax.ShapeDtypeStruct((lp, LANES), F32),
        compiler_params=_cparams(("parallel",)),
    )(xn, wf, bf)


def _tri(n, rel):
    r = lax.broadcasted_iota(jnp.int32, (n, n), 0)
    c = lax.broadcasted_iota(jnp.int32, (n, n), 1)
    return rel(r, c).astype(BF16)


def _cumsum_rows(x, name):
    lp = x.shape[0]
    nb = lp // TQ
    tl = _tri(TQ, lambda r, c: c <= r)

    def body(x_ref, t_ref, o_ref):
        def step(b, carry):
            rows = pl.ds(pl.multiple_of(b * TQ, TQ), TQ)
            xb = x_ref[rows, :]
            hi = xb.astype(BF16)
            r1 = xb - hi.astype(F32)
            mid = r1.astype(BF16)
            lo = (r1 - mid.astype(F32)).astype(BF16)
            t = t_ref[...]
            o_ref[rows, :] = carry + (_dot(t, hi) + _dot(t, mid) + _dot(t, lo))
            return carry + jnp.sum(xb, axis=0, keepdims=True)
        lax.fori_loop(0, nb, step, jnp.zeros((1, LANES), F32))

    return pl.pallas_call(
        body, name=name,
        in_specs=[pl.BlockSpec(memory_space=pltpu.VMEM)] * 2,
        out_specs=pl.BlockSpec(memory_space=pltpu.VMEM),
        out_shape=jax.ShapeDtypeStruct((lp, LANES), F32),
        compiler_params=pltpu.CompilerParams(vmem_limit_bytes=VMEM_LIMIT),
    )(x, tl)


def _dlogf(dc, logf, name):
    lp = dc.shape[0]
    nb = lp // TQ
    tu = _tri(TQ, lambda r, c: c >= r)

    def body(x_ref, lf_ref, t_ref, df_ref, db_ref):
        def step(bb, carry):
            run, db = carry
            b = nb - 1 - bb
            rows = pl.ds(pl.multiple_of(b * TQ, TQ), TQ)
            xb = x_ref[rows, :]
            hi = xb.astype(BF16)
            r1 = xb - hi.astype(F32)
            mid = r1.astype(BF16)
            lo = (r1 - mid.astype(F32)).astype(BF16)
            t = t_ref[...]
            dlf = run + (_dot(t, hi) + _dot(t, mid) + _dot(t, lo))
            df = dlf * (1.0 - jnp.exp(lf_ref[rows, :]))
            df_ref[rows, :] = df
            return run + jnp.sum(xb, axis=0, keepdims=True), db + jnp.sum(df, axis=0, keepdims=True)
        z = jnp.zeros((1, LANES), F32)
        _, db = lax.fori_loop(0, nb, step, (z, z))
        db_ref[...] = db

    return pl.pallas_call(
        body, name=name,
        in_specs=[pl.BlockSpec(memory_space=pltpu.VMEM)] * 3,
        out_specs=[pl.BlockSpec(memory_space=pltpu.VMEM)] * 2,
        out_shape=[jax.ShapeDtypeStruct((lp, LANES), F32), jax.ShapeDtypeStruct((1, LANES), F32)],
        compiler_params=pltpu.CompilerParams(vmem_limit_bytes=VMEM_LIMIT),
    )(dc, logf, tu)


def _merge_fwd(o_a, o_b, proj, h0, w_oa, w_ob, w_out, g1, name):
    lp, dm = h0.shape
    tr = TQ
    ga_blk = (6 * W_ATT) // dm

    def body(oa_ref, ob_ref, ga_ref, gb_ref, h0_ref, woa_ref, wob_ref, wout_ref, g1_ref,
             ya_ref, yb_ref, m_ref, mixed_ref, h1_ref):
        ya = _dot(oa_ref[...], woa_ref[...])
        yb = _dot(ob_ref[...], wob_ref[...])
        m = jax.nn.sigmoid(ga_ref[...].astype(F32)) * ya + jax.nn.sigmoid(gb_ref[...].astype(F32)) * yb
        mb = m.astype(BF16)
        mixed = _dot(mb, wout_ref[...])
        ya_ref[...] = ya.astype(BF16)
        yb_ref[...] = yb.astype(BF16)
        m_ref[...] = mb
        mixed_ref[...] = mixed
        h1_ref[...] = h0_ref[...] + mixed * _rms(mixed) * g1_ref[...]

    row = lambda w: pl.BlockSpec((tr, w), lambda i: (i, 0))
    full = lambda a: pl.BlockSpec(a.shape, lambda i: (0, 0))
    return pl.pallas_call(
        body, name=name, grid=(lp // tr,),
        in_specs=[row(W_ATT), row(W_ATT),
                  pl.BlockSpec((tr, dm), lambda i: (i, ga_blk)),
                  pl.BlockSpec((tr, dm), lambda i: (i, ga_blk + 1)),
                  row(dm), full(w_oa), full(w_ob), full(w_out), full(g1)],
        out_specs=[row(dm)] * 5,
        out_shape=[jax.ShapeDtypeStruct((lp, dm), BF16)] * 3 + [jax.ShapeDtypeStruct((lp, dm), F32)] * 2,
        compiler_params=_cparams(("parallel",)),
    )(o_a, o_b, proj, proj, h0, w_oa, w_ob, w_out, g1)


def _merge_bwd(dh1, mixed, g1, w_out_t, proj, y_a, y_b, w_oa_t, w_ob_t, name):
    lp, dm = dh1.shape
    tr = TQ
    ga_blk = (6 * W_ATT) // dm

    def body(dh_ref, mx_ref, g1_ref, wout_ref, ga_ref, gb_ref, ya_ref, yb_ref, woa_ref, wob_ref,
             dmx_ref, dya_ref, dyb_ref, dga_ref, dgb_ref, doa_ref, dob_ref, dg1_ref):
        i = pl.program_id(0)
        dn = dh_ref[...]
        x = mx_ref[...]
        r = _rms(x)
        yhat = x * r
        part = jnp.sum(dn * yhat, axis=0, keepdims=True)
        dyh = dn * g1_ref[...]
        dmx = (r * (dyh - yhat * jnp.mean(dyh * yhat, axis=-1, keepdims=True))).astype(BF16)
        dmx_ref[...] = dmx
        dm_ = _dot(dmx, wout_ref[...])
        sa = jax.nn.sigmoid(ga_ref[...].astype(F32))
        sb = jax.nn.sigmoid(gb_ref[...].astype(F32))
        dya = (dm_ * sa).astype(BF16)
        dyb = (dm_ * sb).astype(BF16)
        dya_ref[...] = dya
        dyb_ref[...] = dyb
        dga_ref[...] = (dm_ * ya_ref[...].astype(F32) * sa * (1.0 - sa)).astype(BF16)
        dgb_ref[...] = (dm_ * yb_ref[...].astype(F32) * sb * (1.0 - sb)).astype(BF16)
        doa_ref[...] = _dot(dya, woa_ref[...]).astype(BF16)
        dob_ref[...] = _dot(dyb, wob_ref[...]).astype(BF16)

        @pl.when(i == 0)
        def _():
            dg1_ref[...] = part

        @pl.when(i > 0)
        def _():
            dg1_ref[...] += part

    row = lambda w: pl.BlockSpec((tr, w), lambda i: (i, 0))
    full = lambda a: pl.BlockSpec(a.shape, lambda i: (0, 0))
    return pl.pallas_call(
        body, name=name, grid=(lp // tr,),
        in_specs=[row(dm), row(dm), full(g1), full(w_out_t),
                  pl.BlockSpec((tr, dm), lambda i: (i, ga_blk)),
                  pl.BlockSpec((tr, dm), lambda i: (i, ga_blk + 1)),
                  row(dm), row(dm), full(w_oa_t), full(w_ob_t)],
        out_specs=[row(dm)] * 5 + [row(W_ATT)] * 2 + [pl.BlockSpec((1, dm), lambda i: (0, 0))],
        out_shape=[jax.ShapeDtypeStruct((lp, dm), BF16)] * 5 + [jax.ShapeDtypeStruct((lp, W_ATT), BF16)] * 2
        + [jax.ShapeDtypeStruct((1, dm), F32)],
        compiler_params=_cparams(("arbitrary",)),
    )(dh1, mixed, g1, w_out_t, proj, proj, y_a, y_b, w_oa_t, w_ob_t)


_GELU_C = math.sqrt(2.0 / math.pi)
_GELU_A = 0.044715


def _gelu(x):
    t = jnp.tanh(_GELU_C * (x + _GELU_A * x * x * x))
    return 0.5 * x * (1.0 + t), t


CW = 256


def _taps(cur_ref, prev_ref, first, c0):
    cur = cur_ref[:, c0:c0 + CW].astype(F32)
    p1 = jnp.where(first, 0.0, prev_ref[HALO - 1:HALO, c0:c0 + CW].astype(F32))
    p2 = jnp.where(first, 0.0, prev_ref[HALO - 2:HALO - 1, c0:c0 + CW].astype(F32))
    row = lax.broadcasted_iota(jnp.int32, cur.shape, 0)
    x1 = jnp.where(row == 0, p1, pltpu.roll(cur, 1, 0))
    x2 = jnp.where(row == 0, p2, jnp.where(row == 1, p1, pltpu.roll(cur, 2, 0)))
    return cur, x1, x2


def _conv_at(cur_ref, prev_ref, w_ref, b_ref, first, c0):
    cur, x1, x2 = _taps(cur_ref, prev_ref, first, c0)
    cols = slice(c0, c0 + CW)
    u = b_ref[:, cols] + w_ref[0:1, cols] * x2 + w_ref[1:2, cols] * x1 + w_ref[2:3, cols] * cur
    return u, (x2, x1, cur)


def _up_specs(tr, width):
    per = tr // HALO
    return [pl.BlockSpec((tr, width), lambda i: (i, 0)),
            pl.BlockSpec((HALO, width), lambda i: (jnp.maximum(i * per - 1, 0), 0))]


def _convgate_fwd(up, conv_w, conv_b, name):
    lp, c2 = up.shape
    tr = TCONV

    def body(cur_ref, prev_ref, w_ref, b_ref, a_ref):
        first = pl.program_id(0) == 0
        for c0 in range(0, D_FF, CW):
            ug, _ = _conv_at(cur_ref, prev_ref, w_ref, b_ref, first, c0)
            uv, _ = _conv_at(cur_ref, prev_ref, w_ref, b_ref, first, D_FF + c0)
            gel, _ = _gelu(ug)
            a_ref[:, c0:c0 + CW] = (gel * uv).astype(BF16)

    return pl.pallas_call(
        body, name=name, grid=(lp // tr,),
        in_specs=_up_specs(tr, c2) + [pl.BlockSpec((3, c2), lambda i: (0, 0)),
                                      pl.BlockSpec((1, c2), lambda i: (0, 0))],
        out_specs=pl.BlockSpec((tr, D_FF), lambda i: (i, 0)),
        out_shape=jax.ShapeDtypeStruct((lp, D_FF), BF16),
        compiler_params=_cparams(("parallel",)),
    )(up, up, conv_w, conv_b)


def _convgate_bwd(up, da, conv_w, conv_b, name):
    lp, c2 = up.shape
    tr = TCONV

    def body(cur_ref, prev_ref, da_ref, w_ref, b_ref, du_ref, dw_ref, db_ref):
        i = pl.program_id(0)
        first = i == 0

        @pl.when(first)
        def _():
            dw_ref[...] = jnp.zeros_like(dw_ref)
            db_ref[...] = jnp.zeros_like(db_ref)

        for c0 in range(0, D_FF, CW):
            ug, taps_g = _conv_at(cur_ref, prev_ref, w_ref, b_ref, first, c0)
            uv, taps_v = _conv_at(cur_ref, prev_ref, w_ref, b_ref, first, D_FF + c0)
            gel, t = _gelu(ug)
            dgel = 0.5 * (1.0 + t) + 0.5 * ug * (1.0 - t * t) * _GELU_C * (1.0 + 3.0 * _GELU_A * ug * ug)
            da_ = da_ref[:, c0:c0 + CW].astype(F32)
            for base, du, taps in ((c0, da_ * uv * dgel, taps_g), (D_FF + c0, da_ * gel, taps_v)):
                cols = slice(base, base + CW)
                du_ref[:, cols] = du.astype(BF16)
                for tap in range(3):
                    dw_ref[tap:tap + 1, cols] += jnp.sum(du * taps[tap], axis=0, keepdims=True)
                db_ref[:, cols] += jnp.sum(du, axis=0, keepdims=True)

    return pl.pallas_call(
        body, name=name, grid=(lp // tr,),
        in_specs=_up_specs(tr, c2) + [pl.BlockSpec((tr, D_FF), lambda i: (i, 0)),
                                      pl.BlockSpec((3, c2), lambda i: (0, 0)),
                                      pl.BlockSpec((1, c2), lambda i: (0, 0))],
        out_specs=[pl.BlockSpec((tr, c2), lambda i: (i, 0)),
                   pl.BlockSpec((3, c2), lambda i: (0, 0)),
                   pl.BlockSpec((1, c2), lambda i: (0, 0))],
        out_shape=[jax.ShapeDtypeStruct((lp, c2), BF16), jax.ShapeDtypeStruct((3, c2), F32),
                   jax.ShapeDtypeStruct((1, c2), F32)],
        compiler_params=_cparams(("arbitrary",)),
    )(up, up, da, conv_w, conv_b)


def _conv_transpose(du, conv_w, name):
    lp, c2 = du.shape
    tr = TCONV
    per = tr // HALO
    n_halo = lp // HALO
    nt = lp // tr

    def body(cur_ref, nxt_ref, w_ref, o_ref):
        last = pl.program_id(0) == nt - 1
        for c0 in range(0, c2, CW):
            cols = slice(c0, c0 + CW)
            cur = cur_ref[:, cols].astype(F32)
            n0 = jnp.where(last, 0.0, nxt_ref[0:1, cols].astype(F32))
            n1 = jnp.where(last, 0.0, nxt_ref[1:2, cols].astype(F32))
            row = lax.broadcasted_iota(jnp.int32, cur.shape, 0)
            y1 = jnp.where(row == tr - 1, n0, pltpu.roll(cur, tr - 1, 0))
            y2 = jnp.where(row == tr - 1, n1, jnp.where(row == tr - 2, n0, pltpu.roll(cur, tr - 2, 0)))
            o_ref[:, cols] = (w_ref[2:3, cols] * cur + w_ref[1:2, cols] * y1 + w_ref[0:1, cols] * y2).astype(BF16)

    return pl.pallas_call(
        body, name=name, grid=(nt,),
        in_specs=[pl.BlockSpec((tr, c2), lambda i: (i, 0)),
                  pl.BlockSpec((HALO, c2), lambda i: (jnp.minimum((i + 1) * per, n_halo - 1), 0)),
                  pl.BlockSpec((3, c2), lambda i: (0, 0))],
        out_specs=pl.BlockSpec((tr, c2), lambda i: (i, 0)),
        out_shape=jax.ShapeDtypeStruct((lp, c2), BF16),
        compiler_params=_cparams(("parallel",)),
    )(du, du, conv_w)


def _down_loss(a, w_down, h1, g3, target, name):
    lp, dm = h1.shape
    tr = TQ

    def body(a_ref, w_ref, h1_ref, g_ref, t_ref, ffn_ref, dy_ref, ss_ref):
        i = pl.program_id(0)
        ffn = _dot(a_ref[...], w_ref[...])
        ffn_ref[...] = ffn
        h2 = h1_ref[...] + ffn * _rms(ffn) * g_ref[...]
        d = jnp.where(i > 0, h2 - t_ref[...], 0.0)
        dy_ref[...] = d * (1.0 / dm)
        part = jnp.sum(jnp.sum(d * d, axis=0, keepdims=True), axis=1, keepdims=True)

        @pl.when(i == 0)
        def _():
            ss_ref[...] = jnp.zeros_like(ss_ref)

        ss_ref[...] += part

    return pl.pallas_call(
        body, name=name, grid=(lp // tr,),
        in_specs=[pl.BlockSpec((tr, D_FF), lambda i: (i, 0)),
                  pl.BlockSpec(w_down.shape, lambda i: (0, 0)),
                  pl.BlockSpec((tr, dm), lambda i: (i, 0)),
                  pl.BlockSpec((1, dm), lambda i: (0, 0)),
                  pl.BlockSpec((tr, dm), lambda i: (jnp.maximum(i - 1, 0), 0))],
        out_specs=[pl.BlockSpec((tr, dm), lambda i: (i, 0)),
                   pl.BlockSpec((tr, dm), lambda i: (i, 0)),
                   pl.BlockSpec((8, LANES), lambda i: (0, 0))],
        out_shape=[jax.ShapeDtypeStruct((lp, dm), F32), jax.ShapeDtypeStruct((lp, dm), F32),
                   jax.ShapeDtypeStruct((8, LANES), F32)],
        compiler_params=_cparams(("arbitrary",)),
    )(a, w_down, h1, g3, target)


def _down_bwd(dy, ffn, g3, w_down_t, name):
    lp, dm = dy.shape
    tr = TQ

    def body(dy_ref, f_ref, g_ref, w_ref, dffn_ref, da_ref, dg_ref):
        i = pl.program_id(0)
        dn = dy_ref[...]
        x = f_ref[...]
        r = _rms(x)
        yhat = x * r
        part = jnp.sum(dn * yhat, axis=0, keepdims=True)
        dyh = dn * g_ref[...]
        dffn = (r * (dyh - yhat * jnp.mean(dyh * yhat, axis=-1, keepdims=True))).astype(BF16)
        dffn_ref[...] = dffn
        da_ref[...] = _dot(dffn, w_ref[...]).astype(BF16)

        @pl.when(i == 0)
        def _():
            dg_ref[...] = part

        @pl.when(i > 0)
        def _():
            dg_ref[...] += part

    return pl.pallas_call(
        body, name=name, grid=(lp // tr,),
        in_specs=[pl.BlockSpec((tr, dm), lambda i: (i, 0)),
                  pl.BlockSpec((tr, dm), lambda i: (i, 0)),
                  pl.BlockSpec((1, dm), lambda i: (0, 0)),
                  pl.BlockSpec(w_down_t.shape, lambda i: (0, 0))],
        out_specs=[pl.BlockSpec((tr, dm), lambda i: (i, 0)),
                   pl.BlockSpec((tr, D_FF), lambda i: (i, 0)),
                   pl.BlockSpec((1, dm), lambda i: (0, 0))],
        out_shape=[jax.ShapeDtypeStruct((lp, dm), BF16), jax.ShapeDtypeStruct((lp, D_FF), BF16),
                   jax.ShapeDtypeStruct((1, dm), F32)],
        compiler_params=_cparams(("arbitrary",)),
    )(dy, ffn, g3, w_down_t)


def _pair_specs(lp, base):
    return [pl.BlockSpec((TQ, LANES), lambda p, i: (i, base + p)),
            pl.BlockSpec((lp, LANES), lambda p, i: (0, base + N_PAIRS + p)),
            pl.BlockSpec((lp, LANES), lambda p, i: (0, base + 2 * N_PAIRS + p))]


def _col_spec():
    return pl.BlockSpec((None, 2, TQ, 1), lambda p, i: (p, 0, i, 0))


def _rowvec_spec(nb):
    return pl.BlockSpec((None, 2, nb, 1, TQ), lambda p, i: (p, 0, 0, 0, 0))


def _tile_spec():
    return pl.BlockSpec((TQ, LANES), lambda p, i: (i, p))


RC = 64
T2 = 2 * TQ


def _stack_heads(x, scale=None):
    lane = lax.broadcasted_iota(jnp.int32, x.shape, 1)
    zero = jnp.zeros_like(x)
    x2 = jnp.concatenate([jnp.where(lane < HEAD_DIM, x, zero), jnp.where(lane >= HEAD_DIM, x, zero)], axis=0)
    return x2 if scale is None else x2 * scale


def _unstack_heads(x2):
    lane = lax.broadcasted_iota(jnp.int32, (TQ, LANES), 1)
    return jnp.where(lane < HEAD_DIM, x2[:TQ], x2[TQ:])


def _stack_cols(ref):
    return jnp.concatenate([ref[0], ref[1]], axis=0)


def _chunk_valid(i, j, r, strict):
    qpos = i * TQ + (r % TQ) + lax.broadcasted_iota(jnp.int32, (RC, TQ), 0)
    kpos = j * TQ + lax.broadcasted_iota(jnp.int32, (RC, TQ), 1)
    causal = (kpos < qpos) if strict else (kpos <= qpos)
    return causal & (kpos >= PAD)


def _walk_tiles(i, step, reverse):
    first, last = (i, 0) if reverse else (0, i)
    step(first, True)
    if reverse:
        lax.fori_loop(0, i - 1, lambda t, c: (step(i - 1 - t, False), c)[1], 0)
    else:
        lax.fori_loop(1, i, lambda j, c: (step(j, False), c)[1], 0)

    @pl.when(i > 0)
    def _():
        step(last, True)


def _krows(j):
    return pl.ds(pl.multiple_of(j * TQ, TQ), TQ)


def _fox_fwd(proj, crow, name):
    lp = proj.shape[0]
    nb = lp // TQ

    def body(q_ref, k_ref, v_ref, cr_ref, o_ref, lse_ref, m_ref, l_ref, acc_ref, p_ref):
        i = pl.program_id(1)
        q2 = _stack_heads(q_ref[...], 0.125)
        m_ref[...] = jnp.full(m_ref.shape, NEG, F32)
        l_ref[...] = jnp.zeros_like(l_ref)
        acc_ref[...] = jnp.zeros_like(acc_ref)

        def step(j, masked):
            rows_j = _krows(j)
            s = _dot_nt(q2, k_ref[rows_j, :])
            for r in range(0, T2, RC):
                rows = slice(r, r + RC)
                s_c = s[rows] - cr_ref[r // TQ, j]
                if masked:
                    s_c = jnp.where(_chunk_valid(i, j, r, False), s_c, NEG)
                m_old = m_ref[rows]
                m_new = jnp.maximum(m_old, jnp.max(s_c, axis=-1, keepdims=True))
                alpha = jnp.exp(m_old - m_new)
                p = jnp.exp(s_c - m_new)
                l_ref[rows] = alpha * l_ref[rows] + jnp.sum(p, axis=-1, keepdims=True)
                m_ref[rows] = m_new
                acc_ref[rows] = alpha * acc_ref[rows]
                p_ref[rows] = p.astype(BF16)
            acc_ref[...] += _dot(p_ref[...], v_ref[rows_j, :])

        _walk_tiles(i, step, reverse=False)
        l = l_ref[...]
        lse = m_ref[...] + jnp.log(l)
        lse_ref[0] = lse[:TQ]
        lse_ref[1] = lse[TQ:]
        o_ref[...] = _unstack_heads(acc_ref[...] / l).astype(BF16)

    return pl.pallas_call(
        body, name=name, grid=(N_PAIRS, nb),
        in_specs=_pair_specs(lp, 0) + [_rowvec_spec(nb)],
        out_specs=[_tile_spec(), _col_spec()],
        out_shape=[jax.ShapeDtypeStruct((lp, W_ATT), BF16),
                   jax.ShapeDtypeStruct((N_PAIRS, 2, lp, 1), F32)],
        scratch_shapes=[pltpu.VMEM((T2, 1), F32), pltpu.VMEM((T2, 1), F32), pltpu.VMEM((T2, LANES), F32),
                        pltpu.VMEM((T2, TQ), BF16)],
        compiler_params=_cparams(("parallel", "arbitrary")),
    )(proj, proj, proj, crow)


def _fox_bwd(proj, do, o, lse, crow, name):
    lp = proj.shape[0]
    nb = lp // TQ

    def body(q_ref, k_ref, v_ref, do_ref, o_ref, lse_ref, cr_ref,
             dq_ref, dk_ref, dv_ref, dcs_ref, dct_ref,
             dk_acc, dv_acc, dq_acc, dct_acc, p_ref, ds_ref):
        i = pl.program_id(1)

        @pl.when(i == 0)
        def _():
            dk_acc[...] = jnp.zeros_like(dk_acc)
            dv_acc[...] = jnp.zeros_like(dv_acc)
            dcs_ref[...] = jnp.zeros_like(dcs_ref)

        dq_acc[...] = jnp.zeros_like(dq_acc)
        dct_acc[...] = jnp.zeros_like(dct_acc)
        do_ = do_ref[...]
        q2 = _stack_heads(q_ref[...], 0.125)
        do2 = _stack_heads(do_)
        prod = do_.astype(F32) * o_ref[...].astype(F32)
        lane = lax.broadcasted_iota(jnp.int32, prod.shape, 1)
        delta2 = jnp.concatenate(
            [jnp.sum(jnp.where(lane < HEAD_DIM, prod, 0.0), axis=-1, keepdims=True),
             jnp.sum(jnp.where(lane >= HEAD_DIM, prod, 0.0), axis=-1, keepdims=True)], axis=0)
        lse2 = _stack_cols(lse_ref)

        def step(j, masked):
            rows_j = _krows(j)
            k, v = k_ref[rows_j, :], v_ref[rows_j, :]
            s = _dot_nt(q2, k)
            dp = _dot_nt(do2, v)
            cs = [jnp.zeros((1, TQ), F32), jnp.zeros((1, TQ), F32)]
            for r in range(0, T2, RC):
                rows = slice(r, r + RC)
                h = r // TQ
                p = jnp.exp(s[rows] - cr_ref[h, j] - lse2[rows])
                if masked:
                    p = jnp.where(_chunk_valid(i, j, r, False), p, 0.0)
                ds = p * (dp[rows] - delta2[rows])
                p_ref[rows] = p.astype(BF16)
                ds_ref[rows] = ds.astype(BF16)
                dct_acc[rows] += jnp.sum(ds, axis=-1, keepdims=True)
                cs[h] = cs[h] + jnp.sum(ds, axis=0, keepdims=True)
            dcs_ref[0, j] -= cs[0]
            dcs_ref[1, j] -= cs[1]
            dsb = ds_ref[...]
            dq_acc[...] += _dot(dsb, k)
            dk_acc[rows_j, :] += _dot_tn(dsb, q2)
            dv_acc[rows_j, :] += _dot_tn(p_ref[...], do2)

        _walk_tiles(i, step, reverse=False)
        dct = dct_acc[...]
        dct_ref[0] = dct[:TQ]
        dct_ref[1] = dct[TQ:]
        dq_ref[...] = (_unstack_heads(dq_acc[...]) * 0.125).astype(BF16)

        @pl.when(i == nb - 1)
        def _():
            dk_ref[...] = dk_acc[...].astype(BF16)
            dv_ref[...] = dv_acc[...].astype(BF16)

    whole = pl.BlockSpec((lp, LANES), lambda p, i: (0, p))
    return pl.pallas_call(
        body, name=name, grid=(N_PAIRS, nb),
        in_specs=_pair_specs(lp, 0) + [_tile_spec(), _tile_spec(), _col_spec(), _rowvec_spec(nb)],
        out_specs=[_tile_spec(), whole, whole, _rowvec_spec(nb), _col_spec()],
        out_shape=[jax.ShapeDtypeStruct((lp, W_ATT), BF16)] * 3
        + [jax.ShapeDtypeStruct((N_PAIRS, 2, nb, 1, TQ), F32), jax.ShapeDtypeStruct((N_PAIRS, 2, lp, 1), F32)],
        scratch_shapes=[pltpu.VMEM((lp, LANES), F32), pltpu.VMEM((lp, LANES), F32),
                        pltpu.VMEM((T2, LANES), F32), pltpu.VMEM((T2, 1), F32),
                        pltpu.VMEM((T2, TQ), BF16), pltpu.VMEM((T2, TQ), BF16)],
        compiler_params=_cparams(("parallel", "arbitrary")),
    )(proj, proj, proj, do, o, lse, crow)


def _sb_fwd(proj, name):
    lp = proj.shape[0]
    nb = lp // TQ
    tsuf = _tri(TQ, lambda r, c: r > c)

    def body(q_ref, k_ref, v_ref, t_ref, o_ref, lt_ref, run_ref, acc_ref, zl_ref, hi_ref, lo_ref, a_ref):
        i = pl.program_id(1)
        q2 = _stack_heads(q_ref[...], 0.125)
        run_ref[...] = jnp.zeros_like(run_ref)
        acc_ref[...] = jnp.zeros_like(acc_ref)

        def step(j, masked):
            rows_j = _krows(j)
            z = _dot_nt(q2, k_ref[rows_j, :])
            for r in range(0, T2, RC):
                rows = slice(r, r + RC)
                z_c = z[rows]
                lk = _log_sigmoid(-z_c)
                if masked:
                    lk = jnp.where(_chunk_valid(i, j, r, True), lk, 0.0)
                hi, lo = _split2(lk)
                hi_ref[rows] = hi
                lo_ref[rows] = lo
                zl_ref[rows] = z_c + lk + run_ref[rows]
                run_ref[rows] += jnp.sum(lk, axis=-1, keepdims=True)
            t = t_ref[...]
            later = _dot(hi_ref[...], t) + _dot(lo_ref[...], t)
            for r in range(0, T2, RC):
                rows = slice(r, r + RC)
                a = jnp.exp(zl_ref[rows] + later[rows])
                if masked:
                    a = jnp.where(_chunk_valid(i, j, r, True), a, 0.0)
                a_ref[rows] = a.astype(BF16)
            acc_ref[...] += _dot(a_ref[...], v_ref[rows_j, :])

        _walk_tiles(i, step, reverse=True)
        run = run_ref[...]
        lt_ref[0] = run[:TQ]
        lt_ref[1] = run[TQ:]
        o_ref[...] = _unstack_heads(acc_ref[...]).astype(BF16)

    base = 3 * N_PAIRS
    return pl.pallas_call(
        body, name=name, grid=(N_PAIRS, nb),
        in_specs=_pair_specs(lp, base) + [pl.BlockSpec((TQ, TQ), lambda p, i: (0, 0))],
        out_specs=[_tile_spec(), _col_spec()],
        out_shape=[jax.ShapeDtypeStruct((lp, W_ATT), BF16),
                   jax.ShapeDtypeStruct((N_PAIRS, 2, lp, 1), F32)],
        scratch_shapes=[pltpu.VMEM((T2, 1), F32), pltpu.VMEM((T2, LANES), F32), pltpu.VMEM((T2, TQ), F32),
                        pltpu.VMEM((T2, TQ), BF16), pltpu.VMEM((T2, TQ), BF16), pltpu.VMEM((T2, TQ), BF16)],
        compiler_params=_cparams(("parallel", "arbitrary")),
    )(proj, proj, proj, tsuf)


def _sb_bwd(proj, do, ltot, name):
    lp = proj.shape[0]
    nb = lp // TQ
    tincl = _tri(TQ, lambda r, c: r <= c)
    texcl = _tri(TQ, lambda r, c: r < c)

    def body(q_ref, k_ref, v_ref, do_ref, lt_ref, ti_ref, te_ref, dq_ref, dk_ref, dv_ref,
             dk_acc, dv_acc, dq_acc, pc_ref, gc_ref, zl_ref, keep_ref, g_ref, hi_ref, lo_ref, a_ref, dz_ref):
        i = pl.program_id(1)

        @pl.when(i == 0)
        def _():
            dk_acc[...] = jnp.zeros_like(dk_acc)
            dv_acc[...] = jnp.zeros_like(dv_acc)

        dq_acc[...] = jnp.zeros_like(dq_acc)
        gc_ref[...] = jnp.zeros_like(gc_ref)
        pc_ref[...] = _stack_cols(lt_ref)
        q2 = _stack_heads(q_ref[...], 0.125)
        do2 = _stack_heads(do_ref[...])

        def step(j, masked):
            rows_j = _krows(j)
            k, v = k_ref[rows_j, :], v_ref[rows_j, :]
            z = _dot_nt(q2, k)
            da = _dot_nt(do2, v)
            for r in range(0, T2, RC):
                rows = slice(r, r + RC)
                z_c = z[rows]
                lk = _log_sigmoid(-z_c)
                if masked:
                    lk = jnp.where(_chunk_valid(i, j, r, True), lk, 0.0)
                hi, lo = _split2(lk)
                hi_ref[rows] = hi
                lo_ref[rows] = lo
                keep_ref[rows] = jnp.exp(lk)
                zl_ref[rows] = z_c + lk + pc_ref[rows]
                pc_ref[rows] -= jnp.sum(lk, axis=-1, keepdims=True)
            ti = ti_ref[...]
            upto = _dot(hi_ref[...], ti) + _dot(lo_ref[...], ti)
            for r in range(0, T2, RC):
                rows = slice(r, r + RC)
                a = jnp.exp(zl_ref[rows] - upto[rows])
                if masked:
                    a = jnp.where(_chunk_valid(i, j, r, True), a, 0.0)
                g = a * da[rows]
                a_ref[rows] = a.astype(BF16)
                g_ref[rows] = g
                hi_ref[rows] = g.astype(BF16)
            before = _dot(hi_ref[...], te_ref[...])
            for r in range(0, T2, RC):
                rows = slice(r, r + RC)
                g = g_ref[rows]
                keep = keep_ref[rows]
                dz = g * keep - (1.0 - keep) * (gc_ref[rows] + before[rows])
                if masked:
                    dz = jnp.where(_chunk_valid(i, j, r, True), dz, 0.0)
                dz_ref[rows] = dz.astype(BF16)
                gc_ref[rows] += jnp.sum(g, axis=-1, keepdims=True)
            dzb = dz_ref[...]
            dq_acc[...] += _dot(dzb, k)
            dk_acc[rows_j, :] += _dot_tn(dzb, q2)
            dv_acc[rows_j, :] += _dot_tn(a_ref[...], do2)

        _walk_tiles(i, step, reverse=False)
        dq_ref[...] = (_unstack_heads(dq_acc[...]) * 0.125).astype(BF16)

        @pl.when(i == nb - 1)
        def _():
            dk_ref[...] = dk_acc[...].astype(BF16)
            dv_ref[...] = dv_acc[...].astype(BF16)

    base = 3 * N_PAIRS
    whole = pl.BlockSpec((lp, LANES), lambda p, i: (0, p))
    tri = pl.BlockSpec((TQ, TQ), lambda p, i: (0, 0))
    wide = lambda dt: pltpu.VMEM((T2, TQ), dt)
    return pl.pallas_call(
        body, name=name, grid=(N_PAIRS, nb),
        in_specs=_pair_specs(lp, base) + [_tile_spec(), _col_spec(), tri, tri],
        out_specs=[_tile_spec(), whole, whole],
        out_shape=[jax.ShapeDtypeStruct((lp, W_ATT), BF16)] * 3,
        scratch_shapes=[pltpu.VMEM((lp, LANES), F32), pltpu.VMEM((lp, LANES), F32),
                        pltpu.VMEM((T2, LANES), F32), pltpu.VMEM((T2, 1), F32), pltpu.VMEM((T2, 1), F32),
                        wide(F32), wide(F32), wide(F32), wide(BF16), wide(BF16), wide(BF16), wide(BF16)],
        compiler_params=_cparams(("parallel", "arbitrary")),
    )(proj, proj, proj, do, ltot, tincl, texcl)


def _local_step(x, target, meta, gains, w_in, b_forget, w_o_fox, w_o_sb, w_out, w_up, conv_w, conv_b, w_down):
    seq, dm = x.shape
    lp = PAD + N_META + seq
    nb = lp // TQ
    s = [W_ATT, W_ATT, W_ATT, 8, W_ATT, W_ATT, W_ATT, dm, dm]
    off = [sum(s[:i]) for i in range(len(s) + 1)]
    cols = lambda i: w_in[:, off[i]:off[i + 1]]
    w1 = jnp.concatenate([cols(0), cols(1), cols(2), cols(4), cols(5), cols(6), cols(7), cols(8)], axis=1)
    wf = jnp.pad(cols(3), ((0, 0), (0, LANES - 8)))
    n1 = w1.shape[1]
    ncat = n1 + 512
    w_cat_t = jnp.concatenate([w1, wf, jnp.zeros((dm, ncat - n1 - LANES), BF16)], axis=1).T
    bf = jnp.pad(b_forget.reshape(1, 8), ((0, 0), (0, LANES - 8)))
    g = [gains[i].reshape(1, dm) for i in range(4)]
    cb = conv_b.reshape(1, -1)

    h0 = jnp.concatenate([jnp.zeros((PAD, dm), F32), meta, x], axis=0)

    proj, xn1 = _rms_mm(h0, g[0], w1, "in_proj")
    logf = _logf(xn1, wf, bf, "log_forget")
    c = _cumsum_rows(logf, "forget_cumsum")
    crow = c[:, :8].T.reshape(N_PAIRS, 2, nb, 1, TQ)
    o_a, lse = _fox_fwd(proj, crow, "fox_fwd")
    o_b, ltot = _sb_fwd(proj, "sb_fwd")
    y_a, y_b, m, mixed, h1 = _merge_fwd(o_a, o_b, proj, h0, w_o_fox, w_o_sb, w_out, g[1], "merge_fwd")
    up, xn3 = _rms_mm(h1, g[2], w_up, "up_proj")
    a = _convgate_fwd(up, conv_w, cb, "convgate_fwd")
    ffn, dy, ss = _down_loss(a, w_down, h1, g[3], target, "down_loss")

    dffn, da, dg3 = _down_bwd(dy, ffn, g[3], w_down.T, "down_bwd")
    d_w_down = _mm_tn(a, dffn, "dw_down")
    du, d_conv_w, d_conv_b = _convgate_bwd(up, da, conv_w, cb, "convgate_bwd")
    dup = _conv_transpose(du, conv_w, "conv_transpose")
    d_w_up = _mm_tn(xn3, dup, "dw_up")
    dh1, dg2 = _mm_rmsbwd(dup, w_up.T, h1, g[2], dy, "up_bwd")
    dmx, dya, dyb, dga, dgb, do_a, do_b, dg1 = _merge_bwd(
        dh1, mixed, g[1], w_out.T, proj, y_a, y_b, w_o_fox.T, w_o_sb.T, "merge_bwd")
    d_w_out = _mm_tn(m, dmx, "dw_out")
    d_w_o_fox = _mm_tn(o_a, dya, "dw_o_fox")
    d_w_o_sb = _mm_tn(o_b, dyb, "dw_o_sb")
    dq_a, dk_a, dv_a, dcs, dct = _fox_bwd(proj, do_a, o_a, lse, crow, "fox_bwd")
    dq_b, dk_b, dv_b = _sb_bwd(proj, do_b, ltot, "sb_bwd")
    dc = (dct.reshape(8, lp) + dcs.reshape(8, lp)).T
    df, db = _dlogf(jnp.pad(dc, ((0, 0), (0, LANES - 8))), logf, "forget_bwd")
    dcat = jnp.concatenate([dq_a, dk_a, dv_a, dq_b, dk_b, dv_b, dga, dgb, df.astype(BF16),
                            jnp.zeros((lp, ncat - n1 - LANES), BF16)], axis=1)
    d_w_cat = _mm_tn(xn1, dcat, "dw_in")
    dh0, dg0 = _mm_rmsbwd(dcat, w_cat_t, h0, g[0], dh1, "in_bwd")

    wc = lambda k: d_w_cat[:, k * W_ATT:(k + 1) * W_ATT]
    d_w_in = jnp.concatenate([wc(0), wc(1), wc(2), d_w_cat[:, n1:n1 + 8], wc(3), wc(4), wc(5),
                              d_w_cat[:, 6 * W_ATT:n1]], axis=1)
    d_gains = jnp.concatenate([dg0, dg1, dg2, dg3], axis=0)
    grads = (dh0[PAD:PAD + N_META], d_gains, d_w_in, db[0, :8], d_w_o_fox, d_w_o_sb, d_w_out,
             d_w_up, d_conv_w, d_conv_b[0], d_w_down)
    return ss[0, 0], dh0[PAD + N_META:], grads


def _rows(a, n_rows):
    flat = a.reshape(-1)
    return jnp.pad(flat, (0, n_rows * D_MODEL - flat.shape[0])).reshape(n_rows, D_MODEL)


_SMALL = (("meta", 4), ("gains", 1), ("conv_w", 5), ("b_forget", 1), ("conv_b", 6))


def _pack(meta, gains, w_in, b_forget, w_o_fox, w_o_sb, w_out, w_up, conv_w, conv_b, w_down):
    big = [a.reshape(-1, D_MODEL) for a in (w_in, w_o_fox, w_o_sb, w_out, w_up, w_down)]
    n_big = sum(a.shape[0] for a in big)
    small = [_rows(a, n) for a, (_, n) in zip((meta, gains, conv_w, b_forget, conv_b), _SMALL)]
    n_small = sum(n for _, n in _SMALL)
    z = lambda n: jnp.zeros((n, D_MODEL), big[0].dtype)
    return jnp.concatenate(big + [z(PK_BIG_ROWS - n_big)] + small
                           + [z(PK_ROWS - PK_BIG_ROWS - n_small)], axis=0)


def _unpack(p):
    def take(r0, shape):
        n = math.prod(shape)
        nr = -(-n // D_MODEL)
        return p[r0:r0 + nr].reshape(-1)[:n].reshape(shape), r0 + nr
    w_in, r = take(0, (1, 1024, 1282))
    w_o_fox, r = take(r, (1, 512, 256))
    w_o_sb, r = take(r, (1, 512, 256))
    w_out, r = take(r, (1, 256, 1024))
    w_up, r = take(r, (1, 1024, 1408))
    w_down, r = take(r, (1, 704, 1024))
    r = PK_BIG_ROWS
    meta, r = take(r, (16, 256))
    gains, r = take(r, (1, 4, 256))
    conv_w, r = take(r, (1, 3, 1408))
    b_forget, r = take(r, (1, 8))
    conv_b, r = take(r, (1, 5632))
    return meta, gains, w_in, b_forget, w_o_fox, w_o_sb, w_out, w_up, conv_w, conv_b, w_down


def _chip_peers():
    x, y, c = lax.axis_index("x"), lax.axis_index("y"), lax.axis_index("c")
    return [(x, 1 - y, c), (1 - x, y, c), (1 - x, 1 - y, c)]


def _all_gather_chips(arrays, name):
    n = len(arrays)

    def body(*refs):
        ins, outs = refs[:n], refs[n:2 * n]
        send_sems, recv_sems, local_sems = refs[2 * n:]
        x, y = lax.axis_index("x"), lax.axis_index("y")
        me = 2 * x + y
        peers = _chip_peers()
        copies = []
        for a in range(n):
            mine = pltpu.make_async_copy(ins[a], outs[a].at[me], local_sems.at[a])
            mine.start()
            copies.append(mine)
        remote = []
        for a in range(n):
            for j, peer in enumerate(peers):
                cp = pltpu.make_async_remote_copy(
                    src_ref=ins[a], dst_ref=outs[a].at[me],
                    send_sem=send_sems.at[3 * a + j], recv_sem=recv_sems.at[3 * a + j],
                    device_id=peer, device_id_type=MESH)
                cp.start()
                remote.append(cp)
        for cp in remote:
            cp.wait()
        for cp in copies:
            cp.wait()

    any_spec = pl.BlockSpec(memory_space=pl.ANY)
    return pl.pallas_call(
        body, name=name,
        in_specs=[any_spec] * n, out_specs=[any_spec] * n,
        out_shape=[jax.ShapeDtypeStruct((4,) + a.shape, a.dtype) for a in arrays],
        scratch_shapes=[pltpu.SemaphoreType.DMA((3 * n,)), pltpu.SemaphoreType.DMA((3 * n,)),
                        pltpu.SemaphoreType.DMA((n,))],
    )(*arrays)


def _scatter_chips(chunks, name):
    _, rows, cols = chunks.shape

    def body(in_ref, out_ref, send_sems, recv_sems):
        x, y = lax.axis_index("x"), lax.axis_index("y")
        targets = [2 * x + (1 - y), 2 * (1 - x) + y, 2 * (1 - x) + (1 - y)]
        remote = []
        for j, peer in enumerate(_chip_peers()):
            cp = pltpu.make_async_remote_copy(
                src_ref=in_ref.at[targets[j]], dst_ref=out_ref.at[j],
                send_sem=send_sems.at[j], recv_sem=recv_sems.at[j],
                device_id=peer, device_id_type=MESH)
            cp.start()
            remote.append(cp)
        for cp in remote:
            cp.wait()

    any_spec = pl.BlockSpec(memory_space=pl.ANY)
    return pl.pallas_call(
        body, name=name, in_specs=[any_spec], out_specs=any_spec,
        out_shape=jax.ShapeDtypeStruct((3, rows, cols), chunks.dtype),
        scratch_shapes=[pltpu.SemaphoreType.DMA((3,)), pltpu.SemaphoreType.DMA((3,))],
    )(chunks)


def _swap_cores(a, name):
    def body(in_ref, out_ref, send_sem, recv_sem):
        x, y, c = lax.axis_index("x"), lax.axis_index("y"), lax.axis_index("c")
        cp = pltpu.make_async_remote_copy(
            src_ref=in_ref, dst_ref=out_ref, send_sem=send_sem, recv_sem=recv_sem,
            device_id=(x, y, 1 - c), device_id_type=MESH)
        cp.start()
        cp.wait()

    any_spec = pl.BlockSpec(memory_space=pl.ANY)
    return pl.pallas_call(
        body, name=name, in_specs=[any_spec], out_specs=any_spec,
        out_shape=jax.ShapeDtypeStruct(a.shape, a.dtype),
        scratch_shapes=[pltpu.SemaphoreType.DMA, pltpu.SemaphoreType.DMA],
    )(a)


def _chip_sum(chunks, recv, name):
    _, rows, cols = chunks.shape

    def body(own_ref, r_ref, o_ref):
        o_ref[...] = (own_ref[...] + r_ref[0]) + (r_ref[1] + r_ref[2])

    me = 2 * lax.axis_index("x") + lax.axis_index("y")
    own = lax.dynamic_index_in_dim(chunks, me, axis=0, keepdims=False)
    return pl.pallas_call(
        body, name=name, grid=(rows // PK_TILE,),
        in_specs=[pl.BlockSpec((PK_TILE, cols), lambda i: (i, 0)),
                  pl.BlockSpec((3, PK_TILE, cols), lambda i: (0, i, 0))],
        out_specs=pl.BlockSpec((PK_TILE, cols), lambda i: (i, 0)),
        out_shape=jax.ShapeDtypeStruct((rows, cols), F32),
        compiler_params=_cparams(("parallel",)),
    )(own, recv)


def _adamw(w, m, v, g_mine, g_other, name):
    rows, cols = w.shape
    c1 = 1.0 - ADAM_B1 ** ADAM_STEP
    c2 = 1.0 - ADAM_B2 ** ADAM_STEP

    def body(w_ref, m_ref, v_ref, ga_ref, gb_ref, g_ref, d_ref, nm_ref, nv_ref):
        g = ga_ref[...] + gb_ref[...]
        nm = ADAM_B1 * m_ref[...] + (1.0 - ADAM_B1) * g
        nv = ADAM_B2 * v_ref[...] + (1.0 - ADAM_B2) * (g * g)
        g_ref[...] = g
        nm_ref[...] = nm
        nv_ref[...] = nv
        d_ref[...] = -ADAM_LR * ((nm / c1) / (jnp.sqrt(nv / c2) + ADAM_EPS) + ADAM_WD * w_ref[...])

    spec = pl.BlockSpec((PK_TILE, cols), lambda i: (i, 0))
    return pl.pallas_call(
        body, name=name, grid=(rows // PK_TILE,),
        in_specs=[spec] * 5, out_specs=[spec] * 4,
        out_shape=[jax.ShapeDtypeStruct((rows, cols), F32)] * 4,
        compiler_params=_cparams(("parallel",)),
    )(w, m, v, g_mine, g_other)


def _full_weights(big, small):
    def gather(src, r0, shape, axis):
        n = math.prod(shape)
        nr = -(-n // D_MODEL)
        parts = [src[k, r0:r0 + nr].reshape(-1)[:n].reshape(shape) for k in range(4)]
        return jnp.concatenate(parts, axis=axis), r0 + nr
    w_in, r = gather(big, 0, (1024, 1282), 1)
    w_o_fox, r = gather(big, r, (512, 256), 1)
    w_o_sb, r = gather(big, r, (512, 256), 1)
    w_out, r = gather(big, r, (256, 1024), 0)
    w_up, r = gather(big, r, (1024, 1408), 1)
    w_down, r = gather(big, r, (704, 1024), 0)
    meta, r = gather(small, 0, (16, 256), 1)
    gains, r = gather(small, r, (4, 256), 1)
    conv_w, r = gather(small, r, (3, 1408), 1)
    return meta, gains, w_in, w_o_fox, w_o_sb, w_out, w_up, conv_w, w_down


def _chunks_for_chips(grads):
    d_meta, d_gains, d_w_in, d_b, d_w_o_fox, d_w_o_sb, d_w_out, d_w_up, d_conv_w, d_conv_b, d_w_down = grads
    out = []
    for k in range(4):
        col = lambda a, w: a[:, k * w:(k + 1) * w]
        row = lambda a, w: a[k * w:(k + 1) * w]
        out.append(_pack(col(d_meta, 256), col(d_gains, 256), col(d_w_in, 1282), d_b, col(d_w_o_fox, 256),
                         col(d_w_o_sb, 256), row(d_w_out, 256), col(d_w_up, 1408), col(d_conv_w, 1408),
                         d_conv_b, row(d_w_down, 704)))
    return jnp.stack(out, axis=0)


def kernel(x, meta_tokens, norm_gains, w_in, b_forget, w_o_fox, w_o_sb, w_out, w_up, conv_w, conv_b, w_down, loss_target, m_meta_tokens, m_norm_gains, m_w_in, m_b_forget, m_w_o_fox, m_w_o_sb, m_w_out, m_w_up, m_conv_w, m_conv_b, m_w_down, v_meta_tokens, v_norm_gains, v_w_in, v_b_forget, v_w_o_fox, v_w_o_sb, v_w_out, v_w_up, v_conv_w, v_conv_b, v_w_down):
    shard = lambda mt, ng, wi, bf, wof, wos, wo, wu, cw, cb, wd: _pack(
        mt, ng[0], wi[0], bf[0], wof[0], wos[0], wo[0], wu[0], cw[0], cb[0], wd[0])
    wp = shard(meta_tokens, norm_gains, w_in, b_forget, w_o_fox, w_o_sb, w_out, w_up, conv_w, conv_b, w_down)
    mp = shard(m_meta_tokens, m_norm_gains, m_w_in, m_b_forget, m_w_o_fox, m_w_o_sb, m_w_out, m_w_up,
               m_conv_w, m_conv_b, m_w_down)
    vp = shard(v_meta_tokens, v_norm_gains, v_w_in, v_b_forget, v_w_o_fox, v_w_o_sb, v_w_out, v_w_up,
               v_conv_w, v_conv_b, v_w_down)

    big, small = _all_gather_chips(
        [wp[:PK_BIG_ROWS].astype(BF16), wp[PK_BIG_ROWS:PK_BIG_ROWS + PK_SMALL_ROWS]], "gather_weights")
    meta, gains, f_w_in, f_w_o_fox, f_w_o_sb, f_w_out, f_w_up, f_conv_w, f_w_down = _full_weights(big, small)

    ss, dx, grads = _local_step(x[0], loss_target[0], meta, gains, f_w_in, b_forget[0], f_w_o_fox, f_w_o_sb,
                                f_w_out, f_w_up, f_conv_w, conv_b[0], f_w_down)
    loss = lax.psum(0.5 * ss / D_MODEL, ("x", "y", "c"))

    chunks = _chunks_for_chips(grads)
    recv = _scatter_chips(chunks, "scatter_grads")
    mine = _chip_sum(chunks, recv, "chip_sum")
    other = _swap_cores(mine, "swap_cores")
    g, delta, new_m, new_v = _adamw(wp, mp, vp, mine, other, "adamw")
    return (loss, dx[None], *_unpack(g), *_unpack(delta), *_unpack(new_m), *_unpack(new_v))
```

```python
import functools
import math

import jax
import jax.numpy as jnp
from jax import lax
from jax.experimental import pallas as pl
from jax.experimental.pallas import tpu as pltpu

F32 = jnp.float32
BF16 = jnp.bfloat16

D_MODEL = 1024
N_META = 16
HEAD_DIM = 64
N_PAIRS = 4
W_ATT = 512
D_FF = 2816
EPS = 1e-6
NEG = -1e30
TQ = 256
PAD = TQ - N_META
TCONV = 128
HALO = 16
LANES = 128
VMEM_LIMIT = 56 * 1024 * 1024

ADAM_LR = 0.001
ADAM_B1 = 0.9
ADAM_B2 = 0.999
ADAM_EPS = 1e-08
ADAM_WD = 0.01
ADAM_STEP = 10

MESH = pl.DeviceIdType.MESH

PK_BIG_ROWS = 3920
PK_SMALL_ROWS = 24
PK_ROWS = 3968
PK_TILE = 128


def _cparams(sem, **kw):
    return pltpu.CompilerParams(dimension_semantics=sem, vmem_limit_bytes=VMEM_LIMIT, **kw)


def _row_tile(lp):
    return 768 if lp % 768 == 0 else 256


def _rms(x):
    return lax.rsqrt(jnp.mean(x * x, axis=-1, keepdims=True) + EPS)


def _log_sigmoid(x):
    return jnp.minimum(x, 0.0) - jnp.log(1.0 + jnp.exp(-jnp.abs(x)))


def _split2(x):
    hi = x.astype(BF16)
    lo = (x - hi.astype(F32)).astype(BF16)
    return hi, lo


def _dot(a, b):
    return jnp.dot(a, b, preferred_element_type=F32)


def _dot_nt(a, b):
    return lax.dot_general(a, b, (((1,), (1,)), ((), ())), preferred_element_type=F32)


def _dot_tn(a, b):
    return lax.dot_general(a, b, (((0,), (0,)), ((), ())), preferred_element_type=F32)


def _rms_mm(h, g, w, name):
    lp, dm = h.shape
    n = w.shape[1]
    tr, tn = _row_tile(lp), 512

    def body(h_ref, g_ref, w_ref, out_ref, xn_ref):
        @pl.when(pl.program_id(1) == 0)
        def _():
            x = h_ref[...]
            xn_ref[...] = (x * _rms(x) * g_ref[...]).astype(BF16)
        out_ref[...] = _dot(xn_ref[...], w_ref[...]).astype(BF16)

    return pl.pallas_call(
        body, name=name, grid=(lp // tr, n // tn),
        in_specs=[pl.BlockSpec((tr, dm), lambda i, j: (i, 0)),
                  pl.BlockSpec((1, dm), lambda i, j: (0, 0)),
                  pl.BlockSpec((dm, tn), lambda i, j: (0, j))],
        out_specs=[pl.BlockSpec((tr, tn), lambda i, j: (i, j)),
                   pl.BlockSpec((tr, dm), lambda i, j: (i, 0))],
        out_shape=[jax.ShapeDtypeStruct((lp, n), BF16), jax.ShapeDtypeStruct((lp, dm), BF16)],
        compiler_params=_cparams(("parallel", "arbitrary")),
    )(h, g, w)


def _mm_rmsbwd(dy, wt, h, g, dh_in, name):
    lp, kd = dy.shape
    dm = wt.shape[1]
    tr, tk = 384, 512
    nk = kd // tk

    def body(dy_ref, wt_ref, h_ref, g_ref, dhin_ref, dh_ref, dg_ref, acc_ref):
        i, k = pl.program_id(0), pl.program_id(1)

        @pl.when(k == 0)
        def _():
            acc_ref[...] = jnp.zeros_like(acc_ref)

        acc_ref[...] += _dot(dy_ref[...], wt_ref[...])

        @pl.when(k == nk - 1)
        def _():
            dxn = acc_ref[...]
            x = h_ref[...]
            r = _rms(x)
            yhat = x * r
            part = jnp.sum(dxn * yhat, axis=0, keepdims=True)
            dyh = dxn * g_ref[...]
            dx = r * (dyh - yhat * jnp.mean(dyh * yhat, axis=-1, keepdims=True))
            dh_ref[...] = dhin_ref[...] + dx

            @pl.when(i == 0)
            def _():
                dg_ref[...] = part

            @pl.when(i > 0)
            def _():
                dg_ref[...] += part

    return pl.pallas_call(
        body, name=name, grid=(lp // tr, nk),
        in_specs=[pl.BlockSpec((tr, tk), lambda i, k: (i, k)),
                  pl.BlockSpec((tk, dm), lambda i, k: (k, 0)),
                  pl.BlockSpec((tr, dm), lambda i, k: (i, 0)),
                  pl.BlockSpec((1, dm), lambda i, k: (0, 0)),
                  pl.BlockSpec((tr, dm), lambda i, k: (i, 0))],
        out_specs=[pl.BlockSpec((tr, dm), lambda i, k: (i, 0)),
                   pl.BlockSpec((1, dm), lambda i, k: (0, 0))],
        out_shape=[jax.ShapeDtypeStruct((lp, dm), F32), jax.ShapeDtypeStruct((1, dm), F32)],
        scratch_shapes=[pltpu.VMEM((tr, dm), F32)],
        compiler_params=_cparams(("arbitrary", "arbitrary")),
    )(dy, wt, h, g, dh_in)


def _mm_tn(x, dy, name):
    lp, kd = x.shape
    n = dy.shape[1]
    tl = _row_tile(lp)
    tk = 512 if kd % 512 == 0 else 256
    tn = 512
    nl = lp // tl

    def body(x_ref, dy_ref, o_ref):
        @pl.when(pl.program_id(2) == 0)
        def _():
            o_ref[...] = jnp.zeros_like(o_ref)
        o_ref[...] += _dot_tn(x_ref[...], dy_ref[...])

    return pl.pallas_call(
        body, name=name, grid=(kd // tk, n // tn, nl),
        in_specs=[pl.BlockSpec((tl, tk), lambda a, b, l: (l, a)),
                  pl.BlockSpec((tl, tn), lambda a, b, l: (l, b))],
        out_specs=pl.BlockSpec((tk, tn), lambda a, b, l: (a, b)),
        out_shape=jax.ShapeDtypeStruct((kd, n), F32),
        compiler_params=_cparams(("parallel", "parallel", "arbitrary")),
    )(x, dy)


def _logf(xn, wf, bf, name):
    lp, dm = xn.shape
    tr = _row_tile(lp)

    def body(xn_ref, wf_ref, b_ref, o_ref):
        f = _dot(xn_ref[...], wf_ref[...]) + b_ref[...]
        row = pl.program_id(0) * tr + lax.broadcasted_iota(jnp.int32, f.shape, 0)
        lane = lax.broadcasted_iota(jnp.int32, f.shape, 1)
        o_ref[...] = jnp.where((row >= PAD) & (lane < 8), _log_sigmoid(f), 0.0)

    return pl.pallas_call(
        body, name=name, grid=(lp // tr,),
        in_specs=[pl.BlockSpec((tr, dm), lambda i: (i, 0)),
                  pl.BlockSpec((dm, LANES), lambda i: (0, 0)),
                  pl.BlockSpec((1, LANES), lambda i: (0, 0))],
        out_specs=pl.BlockSpec((tr, LANES), lambda i: (i, 0)),
        out_shape=jax.ShapeDtypeStruct((lp, LANES), F32),
        compiler_params=_cparams(("parallel",)),
    )(xn, wf, bf)


def _tri(n, rel):
    r = lax.broadcasted_iota(jnp.int32, (n, n), 0)
    c = lax.broadcasted_iota(jnp.int32, (n, n), 1)
    return rel(r, c).astype(BF16)


def _cumsum_rows(x, name):
    lp = x.shape[0]
    nb = lp // TQ
    tl = _tri(TQ, lambda r, c: c <= r)

    def body(x_ref, t_ref, o_ref):
        def step(b, carry):
            rows = pl.ds(pl.multiple_of(b * TQ, TQ), TQ)
            xb = x_ref[rows, :]
            hi = xb.astype(BF16)
            r1 = xb - hi.astype(F32)
            mid = r1.astype(BF16)
            lo = (r1 - mid.astype(F32)).astype(BF16)
            t = t_ref[...]
            o_ref[rows, :] = carry + (_dot(t, hi) + _dot(t, mid) + _dot(t, lo))
            return carry + jnp.sum(xb, axis=0, keepdims=True)
        lax.fori_loop(0, nb, step, jnp.zeros((1, LANES), F32))

    return pl.pallas_call(
        body, name=name,
        in_specs=[pl.BlockSpec(memory_space=pltpu.VMEM)] * 2,
        out_specs=pl.BlockSpec(memory_space=pltpu.VMEM),
        out_shape=jax.ShapeDtypeStruct((lp, LANES), F32),
        compiler_params=pltpu.CompilerParams(vmem_limit_bytes=VMEM_LIMIT),
    )(x, tl)


def _dlogf(dc, logf, name):
    lp = dc.shape[0]
    nb = lp // TQ
    tu = _tri(TQ, lambda r, c: c >= r)

    def body(x_ref, lf_ref, t_ref, df_ref, db_ref):
        def step(bb, carry):
            run, db = carry
            b = nb - 1 - bb
            rows = pl.ds(pl.multiple_of(b * TQ, TQ), TQ)
            xb = x_ref[rows, :]
            hi = xb.astype(BF16)
            r1 = xb - hi.astype(F32)
            mid = r1.astype(BF16)
            lo = (r1 - mid.astype(F32)).astype(BF16)
            t = t_ref[...]
            dlf = run + (_dot(t, hi) + _dot(t, mid) + _dot(t, lo))
            df = dlf * (1.0 - jnp.exp(lf_ref[rows, :]))
            df_ref[rows, :] = df
            return run + jnp.sum(xb, axis=0, keepdims=True), db + jnp.sum(df, axis=0, keepdims=True)
        z = jnp.zeros((1, LANES), F32)
        _, db = lax.fori_loop(0, nb, step, (z, z))
        db_ref[...] = db

    return pl.pallas_call(
        body, name=name,
        in_specs=[pl.BlockSpec(memory_space=pltpu.VMEM)] * 3,
        out_specs=[pl.BlockSpec(memory_space=pltpu.VMEM)] * 2,
        out_shape=[jax.ShapeDtypeStruct((lp, LANES), F32), jax.ShapeDtypeStruct((1, LANES), F32)],
        compiler_params=pltpu.CompilerParams(vmem_limit_bytes=VMEM_LIMIT),
    )(dc, logf, tu)


def _merge_fwd(o_a, o_b, proj, h0, w_oa, w_ob, w_out, g1, name):
    lp, dm = h0.shape
    tr = TQ
    ga_blk = (6 * W_ATT) // dm

    def body(oa_ref, ob_ref, ga_ref, gb_ref, h0_ref, woa_ref, wob_ref, wout_ref, g1_ref,
             ya_ref, yb_ref, m_ref, mixed_ref, h1_ref):
        ya = _dot(oa_ref[...], woa_ref[...])
        yb = _dot(ob_ref[...], wob_ref[...])
        m = jax.nn.sigmoid(ga_ref[...].astype(F32)) * ya + jax.nn.sigmoid(gb_ref[...].astype(F32)) * yb
        mb = m.astype(BF16)
        mixed = _dot(mb, wout_ref[...])
        ya_ref[...] = ya.astype(BF16)
        yb_ref[...] = yb.astype(BF16)
        m_ref[...] = mb
        mixed_ref[...] = mixed
        h1_ref[...] = h0_ref[...] + mixed * _rms(mixed) * g1_ref[...]

    row = lambda w: pl.BlockSpec((tr, w), lambda i: (i, 0))
    full = lambda a: pl.BlockSpec(a.shape, lambda i: (0, 0))
    return pl.pallas_call(
        body, name=name, grid=(lp // tr,),
        in_specs=[row(W_ATT), row(W_ATT),
                  pl.BlockSpec((tr, dm), lambda i: (i, ga_blk)),
                  pl.BlockSpec((tr, dm), lambda i: (i, ga_blk + 1)),
                  row(dm), full(w_oa), full(w_ob), full(w_out), full(g1)],
        out_specs=[row(dm)] * 5,
        out_shape=[jax.ShapeDtypeStruct((lp, dm), BF16)] * 3 + [jax.ShapeDtypeStruct((lp, dm), F32)] * 2,
        compiler_params=_cparams(("parallel",)),
    )(o_a, o_b, proj, proj, h0, w_oa, w_ob, w_out, g1)


def _merge_bwd(dh1, mixed, g1, w_out_t, proj, y_a, y_b, w_oa_t, w_ob_t, name):
    lp, dm = dh1.shape
    tr = TQ
    ga_blk = (6 * W_ATT) // dm

    def body(dh_ref, mx_ref, g1_ref, wout_ref, ga_ref, gb_ref, ya_ref, yb_ref, woa_ref, wob_ref,
             dmx_ref, dya_ref, dyb_ref, dga_ref, dgb_ref, doa_ref, dob_ref, dg1_ref):
        i = pl.program_id(0)
        dn = dh_ref[...]
        x = mx_ref[...]
        r = _rms(x)
        yhat = x * r
        part = jnp.sum(dn * yhat, axis=0, keepdims=True)
        dyh = dn * g1_ref[...]
        dmx = (r * (dyh - yhat * jnp.mean(dyh * yhat, axis=-1, keepdims=True))).astype(BF16)
        dmx_ref[...] = dmx
        dm_ = _dot(dmx, wout_ref[...])
        sa = jax.nn.sigmoid(ga_ref[...].astype(F32))
        sb = jax.nn.sigmoid(gb_ref[...].astype(F32))
        dya = (dm_ * sa).astype(BF16)
        dyb = (dm_ * sb).astype(BF16)
        dya_ref[...] = dya
        dyb_ref[...] = dyb
        dga_ref[...] = (dm_ * ya_ref[...].astype(F32) * sa * (1.0 - sa)).astype(BF16)
        dgb_ref[...] = (dm_ * yb_ref[...].astype(F32) * sb * (1.0 - sb)).astype(BF16)
        doa_ref[...] = _dot(dya, woa_ref[...]).astype(BF16)
        dob_ref[...] = _dot(dyb, wob_ref[...]).astype(BF16)

        @pl.when(i == 0)
        def _():
            dg1_ref[...] = part

        @pl.when(i > 0)
        def _():
            dg1_ref[...] += part

    row = lambda w: pl.BlockSpec((tr, w), lambda i: (i, 0))
    full = lambda a: pl.BlockSpec(a.shape, lambda i: (0, 0))
    return pl.pallas_call(
        body, name=name, grid=(lp // tr,),
        in_specs=[row(dm), row(dm), full(g1), full(w_out_t),
                  pl.BlockSpec((tr, dm), lambda i: (i, ga_blk)),
                  pl.BlockSpec((tr, dm), lambda i: (i, ga_blk + 1)),
                  row(dm), row(dm), full(w_oa_t), full(w_ob_t)],
        out_specs=[row(dm)] * 5 + [row(W_ATT)] * 2 + [pl.BlockSpec((1, dm), lambda i: (0, 0))],
        out_shape=[jax.ShapeDtypeStruct((lp, dm), BF16)] * 5 + [jax.ShapeDtypeStruct((lp, W_ATT), BF16)] * 2
        + [jax.ShapeDtypeStruct((1, dm), F32)],
        compiler_params=_cparams(("arbitrary",)),
    )(dh1, mixed, g1, w_out_t, proj, proj, y_a, y_b, w_oa_t, w_ob_t)


_GELU_C = math.sqrt(2.0 / math.pi)
_GELU_A = 0.044715


def _gelu(x):
    t = jnp.tanh(_GELU_C * (x + _GELU_A * x * x * x))
    return 0.5 * x * (1.0 + t), t


CW = 256


def _taps(cur_ref, prev_ref, first, c0):
    cur = cur_ref[:, c0:c0 + CW].astype(F32)
    p1 = jnp.where(first, 0.0, prev_ref[HALO - 1:HALO, c0:c0 + CW].astype(F32))
    p2 = jnp.where(first, 0.0, prev_ref[HALO - 2:HALO - 1, c0:c0 + CW].astype(F32))
    row = lax.broadcasted_iota(jnp.int32, cur.shape, 0)
    x1 = jnp.where(row == 0, p1, pltpu.roll(cur, 1, 0))
    x2 = jnp.where(row == 0, p2, jnp.where(row == 1, p1, pltpu.roll(cur, 2, 0)))
    return cur, x1, x2


def _conv_at(cur_ref, prev_ref, w_ref, b_ref, first, c0):
    cur, x1, x2 = _taps(cur_ref, prev_ref, first, c0)
    cols = slice(c0, c0 + CW)
    u = b_ref[:, cols] + w_ref[0:1, cols] * x2 + w_ref[1:2, cols] * x1 + w_ref[2:3, cols] * cur
    return u, (x2, x1, cur)


def _up_specs(tr, width):
    per = tr // HALO
    return [pl.BlockSpec((tr, width), lambda i: (i, 0)),
            pl.BlockSpec((HALO, width), lambda i: (jnp.maximum(i * per - 1, 0), 0))]


def _convgate_fwd(up, conv_w, conv_b, name):
    lp, c2 = up.shape
    tr = TCONV

    def body(cur_ref, prev_ref, w_ref, b_ref, a_ref):
        first = pl.program_id(0) == 0
        for c0 in range(0, D_FF, CW):
            ug, _ = _conv_at(cur_ref, prev_ref, w_ref, b_ref, first, c0)
            uv, _ = _conv_at(cur_ref, prev_ref, w_ref, b_ref, first, D_FF + c0)
            gel, _ = _gelu(ug)
            a_ref[:, c0:c0 + CW] = (gel * uv).astype(BF16)

    return pl.pallas_call(
        body, name=name, grid=(lp // tr,),
        in_specs=_up_specs(tr, c2) + [pl.BlockSpec((3, c2), lambda i: (0, 0)),
                                      pl.BlockSpec((1, c2), lambda i: (0, 0))],
        out_specs=pl.BlockSpec((tr, D_FF), lambda i: (i, 0)),
        out_shape=jax.ShapeDtypeStruct((lp, D_FF), BF16),
        compiler_params=_cparams(("parallel",)),
    )(up, up, conv_w, conv_b)


def _convgate_bwd(up, da, conv_w, conv_b, name):
    lp, c2 = up.shape
    tr = TCONV

    def body(cur_ref, prev_ref, da_ref, w_ref, b_ref, du_ref, dw_ref, db_ref):
        i = pl.program_id(0)
        first = i == 0

        @pl.when(first)
        def _():
            dw_ref[...] = jnp.zeros_like(dw_ref)
            db_ref[...] = jnp.zeros_like(db_ref)

        for c0 in range(0, D_FF, CW):
            ug, taps_g = _conv_at(cur_ref, prev_ref, w_ref, b_ref, first, c0)
            uv, taps_v = _conv_at(cur_ref, prev_ref, w_ref, b_ref, first, D_FF + c0)
            gel, t = _gelu(ug)
            dgel = 0.5 * (1.0 + t) + 0.5 * ug * (1.0 - t * t) * _GELU_C * (1.0 + 3.0 * _GELU_A * ug * ug)
            da_ = da_ref[:, c0:c0 + CW].astype(F32)
            for base, du, taps in ((c0, da_ * uv * dgel, taps_g), (D_FF + c0, da_ * gel, taps_v)):
                cols = slice(base, base + CW)
                du_ref[:, cols] = du.astype(BF16)
                for tap in range(3):
                    dw_ref[tap:tap + 1, cols] += jnp.sum(du * taps[tap], axis=0, keepdims=True)
                db_ref[:, cols] += jnp.sum(du, axis=0, keepdims=True)

    return pl.pallas_call(
        body, name=name, grid=(lp // tr,),
        in_specs=_up_specs(tr, c2) + [pl.BlockSpec((tr, D_FF), lambda i: (i, 0)),
                                      pl.BlockSpec((3, c2), lambda i: (0, 0)),
                                      pl.BlockSpec((1, c2), lambda i: (0, 0))],
        out_specs=[pl.BlockSpec((tr, c2), lambda i: (i, 0)),
                   pl.BlockSpec((3, c2), lambda i: (0, 0)),
                   pl.BlockSpec((1, c2), lambda i: (0, 0))],
        out_shape=[jax.ShapeDtypeStruct((lp, c2), BF16), jax.ShapeDtypeStruct((3, c2), F32),
                   jax.ShapeDtypeStruct((1, c2), F32)],
        compiler_params=_cparams(("arbitrary",)),
    )(up, up, da, conv_w, conv_b)


def _conv_transpose(du, conv_w, name):
    lp, c2 = du.shape
    tr = TCONV
    per = tr // HALO
    n_halo = lp // HALO
    nt = lp // tr

    def body(cur_ref, nxt_ref, w_ref, o_ref):
        last = pl.program_id(0) == nt - 1
        for c0 in range(0, c2, CW):
            cols = slice(c0, c0 + CW)
            cur = cur_ref[:, cols].astype(F32)
            n0 = jnp.where(last, 0.0, nxt_ref[0:1, cols].astype(F32))
            n1 = jnp.where(last, 0.0, nxt_ref[1:2, cols].astype(F32))
            row = lax.broadcasted_iota(jnp.int32, cur.shape, 0)
            y1 = jnp.where(row == tr - 1, n0, pltpu.roll(cur, tr - 1, 0))
            y2 = jnp.where(row == tr - 1, n1, jnp.where(row == tr - 2, n0, pltpu.roll(cur, tr - 2, 0)))
            o_ref[:, cols] = (w_ref[2:3, cols] * cur + w_ref[1:2, cols] * y1 + w_ref[0:1, cols] * y2).astype(BF16)

    return pl.pallas_call(
        body, name=name, grid=(nt,),
        in_specs=[pl.BlockSpec((tr, c2), lambda i: (i, 0)),
                  pl.BlockSpec((HALO, c2), lambda i: (jnp.minimum((i + 1) * per, n_halo - 1), 0)),
                  pl.BlockSpec((3, c2), lambda i: (0, 0))],
        out_specs=pl.BlockSpec((tr, c2), lambda i: (i, 0)),
        out_shape=jax.ShapeDtypeStruct((lp, c2), BF16),
        compiler_params=_cparams(("parallel",)),
    )(du, du, conv_w)


def _down_loss(a, w_down, h1, g3, target, name):
    lp, dm = h1.shape
    tr = TQ

    def body(a_ref, w_ref, h1_ref, g_ref, t_ref, ffn_ref, dy_ref, ss_ref):
        i = pl.program_id(0)
        ffn = _dot(a_ref[...], w_ref[...])
        ffn_ref[...] = ffn
        h2 = h1_ref[...] + ffn * _rms(ffn) * g_ref[...]
        d = jnp.where(i > 0, h2 - t_ref[...], 0.0)
        dy_ref[...] = d * (1.0 / dm)
        part = jnp.sum(jnp.sum(d * d, axis=0, keepdims=True), axis=1, keepdims=True)

        @pl.when(i == 0)
        def _():
            ss_ref[...] = jnp.zeros_like(ss_ref)

        ss_ref[...] += part

    return pl.pallas_call(
        body, name=name, grid=(lp // tr,),
        in_specs=[pl.BlockSpec((tr, D_FF), lambda i: (i, 0)),
                  pl.BlockSpec(w_down.shape, lambda i: (0, 0)),
                  pl.BlockSpec((tr, dm), lambda i: (i, 0)),
                  pl.BlockSpec((1, dm), lambda i: (0, 0)),
                  pl.BlockSpec((tr, dm), lambda i: (jnp.maximum(i - 1, 0), 0))],
        out_specs=[pl.BlockSpec((tr, dm), lambda i: (i, 0)),
                   pl.BlockSpec((tr, dm), lambda i: (i, 0)),
                   pl.BlockSpec((8, LANES), lambda i: (0, 0))],
        out_shape=[jax.ShapeDtypeStruct((lp, dm), F32), jax.ShapeDtypeStruct((lp, dm), F32),
                   jax.ShapeDtypeStruct((8, LANES), F32)],
        compiler_params=_cparams(("arbitrary",)),
    )(a, w_down, h1, g3, target)


def _down_bwd(dy, ffn, g3, w_down_t, name):
    lp, dm = dy.shape
    tr = TQ

    def body(dy_ref, f_ref, g_ref, w_ref, dffn_ref, da_ref, dg_ref):
        i = pl.program_id(0)
        dn = dy_ref[...]
        x = f_ref[...]
        r = _rms(x)
        yhat = x * r
        part = jnp.sum(dn * yhat, axis=0, keepdims=True)
        dyh = dn * g_ref[...]
        dffn = (r * (dyh - yhat * jnp.mean(dyh * yhat, axis=-1, keepdims=True))).astype(BF16)
        dffn_ref[...] = dffn
        da_ref[...] = _dot(dffn, w_ref[...]).astype(BF16)

        @pl.when(i == 0)
        def _():
            dg_ref[...] = part

        @pl.when(i > 0)
        def _():
            dg_ref[...] += part

    return pl.pallas_call(
        body, name=name, grid=(lp // tr,),
        in_specs=[pl.BlockSpec((tr, dm), lambda i: (i, 0)),
                  pl.BlockSpec((tr, dm), lambda i: (i, 0)),
                  pl.BlockSpec((1, dm), lambda i: (0, 0)),
                  pl.BlockSpec(w_down_t.shape, lambda i: (0, 0))],
        out_specs=[pl.BlockSpec((tr, dm), lambda i: (i, 0)),
                   pl.BlockSpec((tr, D_FF), lambda i: (i, 0)),
                   pl.BlockSpec((1, dm), lambda i: (0, 0))],
        out_shape=[jax.ShapeDtypeStruct((lp, dm), BF16), jax.ShapeDtypeStruct((lp, D_FF), BF16),
                   jax.ShapeDtypeStruct((1, dm), F32)],
        compiler_params=_cparams(("arbitrary",)),
    )(dy, ffn, g3, w_down_t)


def _pair_specs(lp, base):
    return [pl.BlockSpec((TQ, LANES), lambda p, i: (i, base + p)),
            pl.BlockSpec((lp, LANES), lambda p, i: (0, base + N_PAIRS + p)),
            pl.BlockSpec((lp, LANES), lambda p, i: (0, base + 2 * N_PAIRS + p))]


def _col_spec():
    return pl.BlockSpec((None, 2, TQ, 1), lambda p, i: (p, 0, i, 0))


def _rowvec_spec(nb):
    return pl.BlockSpec((None, 2, nb, 1, TQ), lambda p, i: (p, 0, 0, 0, 0))


def _tile_spec():
    return pl.BlockSpec((TQ, LANES), lambda p, i: (i, p))


RC = 64
T2 = 2 * TQ


def _stack_heads(x, scale=None):
    lane = lax.broadcasted_iota(jnp.int32, x.shape, 1)
    zero = jnp.zeros_like(x)
    x2 = jnp.concatenate([jnp.where(lane < HEAD_DIM, x, zero), jnp.where(lane >= HEAD_DIM, x, zero)], axis=0)
    return x2 if scale is None else x2 * scale


def _unstack_heads(x2):
    lane = lax.broadcasted_iota(jnp.int32, (TQ, LANES), 1)
    return jnp.where(lane < HEAD_DIM, x2[:TQ], x2[TQ:])


def _stack_cols(ref):
    return jnp.concatenate([ref[0], ref[1]], axis=0)


def _chunk_valid(i, j, r, strict):
    qpos = i * TQ + (r % TQ) + lax.broadcasted_iota(jnp.int32, (RC, TQ), 0)
    kpos = j * TQ + lax.broadcasted_iota(jnp.int32, (RC, TQ), 1)
    causal = (kpos < qpos) if strict else (kpos <= qpos)
    return causal & (kpos >= PAD)


def _walk_tiles(i, step, reverse):
    first, last = (i, 0) if reverse else (0, i)
    step(first, True)
    if reverse:
        lax.fori_loop(0, i - 1, lambda t, c: (step(i - 1 - t, False), c)[1], 0)
    else:
        lax.fori_loop(1, i, lambda j, c: (step(j, False), c)[1], 0)

    @pl.when(i > 0)
    def _():
        step(last, True)


_HALF = (slice(0, TQ), slice(TQ, T2))


def _walk_tiles_lead(i, lead, step, reverse):
    first, last = (i, 0) if reverse else (0, i)
    lead(first, 0)
    lead(first, 1)
    step(first, jnp.maximum(i - 1, 0) if reverse else jnp.minimum(1, i), True)

    def between(t, c):
        j = i - 1 - t if reverse else t + 1
        step(j, j - 1 if reverse else j + 1, False)
        return c

    lax.fori_loop(0, i - 1, between, 0)

    @pl.when(i > 0)
    def _():
        step(last, None, True)


def _krows(j):
    return pl.ds(pl.multiple_of(j * TQ, TQ), TQ)


def _fox_fwd(proj, crow, name):
    lp = proj.shape[0]
    nb = lp // TQ

    def body(q_ref, k_ref, v_ref, cr_ref, o_ref, lse_ref, m_ref, acc_ref, p_ref):
        i = pl.program_id(1)
        q2 = _stack_heads(q_ref[...], 0.125)
        m_ref[...] = jnp.full(m_ref.shape, NEG, F32)
        acc_ref[...] = jnp.zeros_like(acc_ref)
        lane = lax.broadcasted_iota(jnp.int32, (TQ, LANES), 1)

        def step(j, masked):
            rows_j = _krows(j)
            s = _dot_nt(q2, k_ref[rows_j, :])
            v = v_ref[rows_j, :]
            one = jnp.ones_like(v)
            v_heads = (jnp.where(lane < HEAD_DIM, v, one), jnp.where(lane >= HEAD_DIM, v, one))
            for r in range(0, T2, RC):
                rows = slice(r, r + RC)
                s_c = s[rows] - cr_ref[r // TQ, j]
                if masked:
                    s_c = jnp.where(_chunk_valid(i, j, r, False), s_c, NEG)
                s0, s1 = s_c[:, :LANES], s_c[:, LANES:]
                m_old = m_ref[rows]
                m_new = jnp.maximum(m_old, jnp.max(jnp.maximum(s0, s1), axis=-1, keepdims=True))
                m_ref[rows] = m_new
                acc_ref[rows] = jnp.exp(m_old - m_new) * acc_ref[rows]
                p_ref[rows, :LANES] = jnp.exp(s0 - m_new).astype(BF16)
                p_ref[rows, LANES:] = jnp.exp(s1 - m_new).astype(BF16)
            for h in range(2):
                acc_ref[_HALF[h]] += _dot(p_ref[_HALF[h]], v_heads[h])

        _walk_tiles(i, step, reverse=False)
        acc = acc_ref[...]
        m = m_ref[...]
        outs = []
        for h in range(2):
            a_h = acc[_HALF[h]]
            l = a_h[:, HEAD_DIM:HEAD_DIM + 1] if h == 0 else a_h[:, 0:1]
            lse_ref[h] = m[_HALF[h]][:, 0:1] + jnp.log(l)
            outs.append(a_h / l)
        o_ref[...] = jnp.where(lane < HEAD_DIM, outs[0], outs[1]).astype(BF16)

    return pl.pallas_call(
        body, name=name, grid=(N_PAIRS, nb),
        in_specs=_pair_specs(lp, 0) + [_rowvec_spec(nb)],
        out_specs=[_tile_spec(), _col_spec()],
        out_shape=[jax.ShapeDtypeStruct((lp, W_ATT), BF16),
                   jax.ShapeDtypeStruct((N_PAIRS, 2, lp, 1), F32)],
        scratch_shapes=[pltpu.VMEM((T2, LANES), F32), pltpu.VMEM((T2, LANES), F32), pltpu.VMEM((T2, TQ), BF16)],
        compiler_params=_cparams(("parallel", "arbitrary")),
    )(proj, proj, proj, crow)


def _fox_bwd(proj, do, o, lse, crow, name):
    lp = proj.shape[0]
    nb = lp // TQ

    def body(q_ref, k_ref, v_ref, do_ref, o_ref, lse_ref, cr_ref,
             dq_ref, dk_ref, dv_ref, dcs_ref, dct_ref,
             dk_acc, dv_acc, dq_acc, dct_acc, p_ref, ds_ref, s_ref, dp_ref):
        i = pl.program_id(1)

        @pl.when(i == 0)
        def _():
            dk_acc[...] = jnp.zeros_like(dk_acc)
            dv_acc[...] = jnp.zeros_like(dv_acc)
            dcs_ref[...] = jnp.zeros_like(dcs_ref)

        dq_acc[...] = jnp.zeros_like(dq_acc)
        dct_acc[...] = jnp.zeros_like(dct_acc)
        do_ = do_ref[...]
        q2 = _stack_heads(q_ref[...], 0.125)
        do2 = _stack_heads(do_)
        prod = do_.astype(F32) * o_ref[...].astype(F32)
        lane = lax.broadcasted_iota(jnp.int32, prod.shape, 1)
        delta2 = jnp.concatenate(
            [jnp.sum(jnp.where(lane < HEAD_DIM, prod, 0.0), axis=-1, keepdims=True),
             jnp.sum(jnp.where(lane >= HEAD_DIM, prod, 0.0), axis=-1, keepdims=True)], axis=0)
        lse2 = _stack_cols(lse_ref)

        def lead(j, h):
            rows_j = _krows(j)
            s_ref[_HALF[h]] = _dot_nt(q2[_HALF[h]], k_ref[rows_j, :])
            dp_ref[_HALF[h]] = _dot_nt(do2[_HALF[h]], v_ref[rows_j, :])

        def step(j, nxt, masked):
            rows_j = _krows(j)
            k = k_ref[rows_j, :]
            for h in range(2):
                cs = jnp.zeros((1, TQ), F32)
                for r in range(h * TQ, (h + 1) * TQ, RC):
                    rows = slice(r, r + RC)
                    p = jnp.exp(s_ref[rows] - cr_ref[h, j] - lse2[rows])
                    if masked:
                        p = jnp.where(_chunk_valid(i, j, r, False), p, 0.0)
                    ds = p * (dp_ref[rows] - delta2[rows])
                    p_ref[rows] = p.astype(BF16)
                    ds_ref[rows] = ds.astype(BF16)
                    dct_acc[rows] += jnp.sum(ds, axis=-1, keepdims=True)
                    cs = cs + jnp.sum(ds, axis=0, keepdims=True)
                dcs_ref[h, j] -= cs
                if nxt is not None:
                    lead(nxt, h)
                dsb = ds_ref[_HALF[h]]
                dq_acc[_HALF[h]] += _dot(dsb, k)
                dk_acc[rows_j, :] += _dot_tn(dsb, q2[_HALF[h]])
                dv_acc[rows_j, :] += _dot_tn(p_ref[_HALF[h]], do2[_HALF[h]])

        _walk_tiles_lead(i, lead, step, reverse=False)
        dct = dct_acc[...]
        dct_ref[0] = dct[:TQ]
        dct_ref[1] = dct[TQ:]
        dq_ref[...] = (_unstack_heads(dq_acc[...]) * 0.125).astype(BF16)

        @pl.when(i == nb - 1)
        def _():
            dk_ref[...] = dk_acc[...].astype(BF16)
            dv_ref[...] = dv_acc[...].astype(BF16)

    whole = pl.BlockSpec((lp, LANES), lambda p, i: (0, p))
    return pl.pallas_call(
        body, name=name, grid=(N_PAIRS, nb),
        in_specs=_pair_specs(lp, 0) + [_tile_spec(), _tile_spec(), _col_spec(), _rowvec_spec(nb)],
        out_specs=[_tile_spec(), whole, whole, _rowvec_spec(nb), _col_spec()],
        out_shape=[jax.ShapeDtypeStruct((lp, W_ATT), BF16)] * 3
        + [jax.ShapeDtypeStruct((N_PAIRS, 2, nb, 1, TQ), F32), jax.ShapeDtypeStruct((N_PAIRS, 2, lp, 1), F32)],
        scratch_shapes=[pltpu.VMEM((lp, LANES), F32), pltpu.VMEM((lp, LANES), F32),
                        pltpu.VMEM((T2, LANES), F32), pltpu.VMEM((T2, 1), F32),
                        pltpu.VMEM((T2, TQ), BF16), pltpu.VMEM((T2, TQ), BF16),
                        pltpu.VMEM((T2, TQ), F32), pltpu.VMEM((T2, TQ), F32)],
        compiler_params=_cparams(("parallel", "arbitrary")),
    )(proj, proj, proj, do, o, lse, crow)


def _sb_fwd(proj, name):
    lp = proj.shape[0]
    nb = lp // TQ
    tsuf = _tri(TQ, lambda r, c: r > c)

    def body(q_ref, k_ref, v_ref, t_ref, o_ref, lt_ref, run_ref, acc_ref, zl_ref, hl_ref, a_ref, z_ref):
        i = pl.program_id(1)
        q2 = _stack_heads(q_ref[...], 0.125)
        run_ref[...] = jnp.zeros_like(run_ref)
        acc_ref[...] = jnp.zeros_like(acc_ref)

        def lead(j, h):
            z_ref[_HALF[h]] = _dot_nt(q2[_HALF[h]], k_ref[_krows(j), :])

        def step(j, nxt, masked):
            t = t_ref[...]
            v = v_ref[_krows(j), :]
            later = []
            for h in range(2):
                for r in range(h * TQ, (h + 1) * TQ, RC):
                    rows = slice(r, r + RC)
                    z_c = z_ref[rows]
                    lk = _log_sigmoid(-z_c)
                    if masked:
                        lk = jnp.where(_chunk_valid(i, j, r, True), lk, 0.0)
                    hi, lo = _split2(lk)
                    hl_ref[rows, :TQ] = hi
                    hl_ref[rows, TQ:] = lo
                    zl_ref[rows] = z_c + lk + run_ref[rows]
                    run_ref[rows] += jnp.sum(lk, axis=-1, keepdims=True)
                if nxt is not None:
                    lead(nxt, h)
                later.append(_dot(hl_ref[_HALF[h]], t))
            for h in range(2):
                for r in range(0, TQ, RC):
                    rows = slice(h * TQ + r, h * TQ + r + RC)
                    a = jnp.exp(zl_ref[rows] + later[h][r:r + RC])
                    if masked:
                        a = jnp.where(_chunk_valid(i, j, h * TQ + r, True), a, 0.0)
                    a_ref[rows] = a.astype(BF16)
                acc_ref[_HALF[h]] += _dot(a_ref[_HALF[h]], v)

        _walk_tiles_lead(i, lead, step, reverse=True)
        run = run_ref[...]
        lt_ref[0] = run[:TQ]
        lt_ref[1] = run[TQ:]
        o_ref[...] = _unstack_heads(acc_ref[...]).astype(BF16)

    base = 3 * N_PAIRS
    return pl.pallas_call(
        body, name=name, grid=(N_PAIRS, nb),
        in_specs=_pair_specs(lp, base) + [pl.BlockSpec((2 * TQ, TQ), lambda p, i: (0, 0))],
        out_specs=[_tile_spec(), _col_spec()],
        out_shape=[jax.ShapeDtypeStruct((lp, W_ATT), BF16),
                   jax.ShapeDtypeStruct((N_PAIRS, 2, lp, 1), F32)],
        scratch_shapes=[pltpu.VMEM((T2, 1), F32), pltpu.VMEM((T2, LANES), F32), pltpu.VMEM((T2, TQ), F32),
                        pltpu.VMEM((T2, 2 * TQ), BF16), pltpu.VMEM((T2, TQ), BF16), pltpu.VMEM((T2, TQ), F32)],
        compiler_params=_cparams(("parallel", "arbitrary")),
    )(proj, proj, proj, jnp.concatenate([tsuf, tsuf], axis=0))


def _sb_bwd(proj, do, ltot, name):
    lp = proj.shape[0]
    nb = lp // TQ
    tincl = _tri(TQ, lambda r, c: r <= c)
    texcl = _tri(TQ, lambda r, c: r < c)

    def body(q_ref, k_ref, v_ref, do_ref, lt_ref, ti_ref, te_ref, dq_ref, dk_ref, dv_ref,
             dk_acc, dv_acc, dq_acc, pc_ref, gc_ref, zl_ref, keep_ref, g_ref, z_ref, da_ref,
             hl_ref, gb_ref, a_ref, dz_ref):
        i = pl.program_id(1)

        @pl.when(i == 0)
        def _():
            dk_acc[...] = jnp.zeros_like(dk_acc)
            dv_acc[...] = jnp.zeros_like(dv_acc)

        dq_acc[...] = jnp.zeros_like(dq_acc)
        gc_ref[...] = jnp.zeros_like(gc_ref)
        pc_ref[...] = _stack_cols(lt_ref)
        q2 = _stack_heads(q_ref[...], 0.125)
        do2 = _stack_heads(do_ref[...])

        def lead(j, h):
            rows_j = _krows(j)
            z_ref[_HALF[h]] = _dot_nt(q2[_HALF[h]], k_ref[rows_j, :])
            da_ref[_HALF[h]] = _dot_nt(do2[_HALF[h]], v_ref[rows_j, :])

        def step(j, nxt, masked):
            rows_j = _krows(j)
            k = k_ref[rows_j, :]
            ti = ti_ref[...]
            te = te_ref[...]
            upto, before = [], []
            for h in range(2):
                for r in range(h * TQ, (h + 1) * TQ, RC):
                    rows = slice(r, r + RC)
                    z_c = z_ref[rows]
                    lk = _log_sigmoid(-z_c)
                    if masked:
                        lk = jnp.where(_chunk_valid(i, j, r, True), lk, 0.0)
                    hi, lo = _split2(lk)
                    hl_ref[rows, :TQ] = hi
                    hl_ref[rows, TQ:] = lo
                    keep_ref[rows] = jnp.exp(lk)
                    zl_ref[rows] = z_c + lk + pc_ref[rows]
                    pc_ref[rows] -= jnp.sum(lk, axis=-1, keepdims=True)
                upto.append(_dot(hl_ref[_HALF[h]], ti))
            for h in range(2):
                for r in range(0, TQ, RC):
                    rows = slice(h * TQ + r, h * TQ + r + RC)
                    a = jnp.exp(zl_ref[rows] - upto[h][r:r + RC])
                    if masked:
                        a = jnp.where(_chunk_valid(i, j, h * TQ + r, True), a, 0.0)
                    g = a * da_ref[rows]
                    a_ref[rows] = a.astype(BF16)
                    g_ref[rows] = g
                    gb_ref[rows] = g.astype(BF16)
                if nxt is not None:
                    lead(nxt, h)
                before.append(_dot(gb_ref[_HALF[h]], te))
            for h in range(2):
                for r in range(0, TQ, RC):
                    rows = slice(h * TQ + r, h * TQ + r + RC)
                    g = g_ref[rows]
                    keep = keep_ref[rows]
                    dz = g * keep - (1.0 - keep) * (gc_ref[rows] + before[h][r:r + RC])
                    if masked:
                        dz = jnp.where(_chunk_valid(i, j, h * TQ + r, True), dz, 0.0)
                    dz_ref[rows] = dz.astype(BF16)
                    gc_ref[rows] += jnp.sum(g, axis=-1, keepdims=True)
                dzb = dz_ref[_HALF[h]]
                dq_acc[_HALF[h]] += _dot(dzb, k)
                dk_acc[rows_j, :] += _dot_tn(dzb, q2[_HALF[h]])
                dv_acc[rows_j, :] += _dot_tn(a_ref[_HALF[h]], do2[_HALF[h]])

        _walk_tiles_lead(i, lead, step, reverse=False)
        dq_ref[...] = (_unstack_heads(dq_acc[...]) * 0.125).astype(BF16)

        @pl.when(i == nb - 1)
        def _():
            dk_ref[...] = dk_acc[...].astype(BF16)
            dv_ref[...] = dv_acc[...].astype(BF16)

    base = 3 * N_PAIRS
    whole = pl.BlockSpec((lp, LANES), lambda p, i: (0, p))
    tri = lambda rows: pl.BlockSpec((rows, TQ), lambda p, i: (0, 0))
    wide = lambda dt: pltpu.VMEM((T2, TQ), dt)
    return pl.pallas_call(
        body, name=name, grid=(N_PAIRS, nb),
        in_specs=_pair_specs(lp, base) + [_tile_spec(), _col_spec(), tri(2 * TQ), tri(TQ)],
        out_specs=[_tile_spec(), whole, whole],
        out_shape=[jax.ShapeDtypeStruct((lp, W_ATT), BF16)] * 3,
        scratch_shapes=[pltpu.VMEM((lp, LANES), F32), pltpu.VMEM((lp, LANES), F32),
                        pltpu.VMEM((T2, LANES), F32), pltpu.VMEM((T2, 1), F32), pltpu.VMEM((T2, 1), F32),
                        wide(F32), wide(F32), wide(F32), wide(F32), wide(F32),
                        pltpu.VMEM((T2, 2 * TQ), BF16), wide(BF16), wide(BF16), wide(BF16)],
        compiler_params=_cparams(("parallel", "arbitrary")),
    )(proj, proj, proj, do, ltot, jnp.concatenate([tincl, tincl], axis=0), texcl)


def _local_step(x, target, meta, gains, w_in, b_forget, w_o_fox, w_o_sb, w_out, w_up, conv_w, conv_b, w_down):
    seq, dm = x.shape
    lp = PAD + N_META + seq
    nb = lp // TQ
    s = [W_ATT, W_ATT, W_ATT, 8, W_ATT, W_ATT, W_ATT, dm, dm]
    off = [sum(s[:i]) for i in range(len(s) + 1)]
    cols = lambda i: w_in[:, off[i]:off[i + 1]]
    w1 = jnp.concatenate([cols(0), cols(1), cols(2), cols(4), cols(5), cols(6), cols(7), cols(8)], axis=1)
    wf = jnp.pad(cols(3), ((0, 0), (0, LANES - 8)))
    n1 = w1.shape[1]
    ncat = n1 + 512
    w_cat_t = jnp.concatenate([w1, wf, jnp.zeros((dm, ncat - n1 - LANES), BF16)], axis=1).T
    bf = jnp.pad(b_forget.reshape(1, 8), ((0, 0), (0, LANES - 8)))
    g = [gains[i].reshape(1, dm) for i in range(4)]
    cb = conv_b.reshape(1, -1)

    h0 = jnp.concatenate([jnp.zeros((PAD, dm), F32), meta, x], axis=0)

    proj, xn1 = _rms_mm(h0, g[0], w1, "in_proj")
    logf = _logf(xn1, wf, bf, "log_forget")
    c = _cumsum_rows(logf, "forget_cumsum")
    crow = c[:, :8].T.reshape(N_PAIRS, 2, nb, 1, TQ)
    o_a, lse = _fox_fwd(proj, crow, "fox_fwd")
    o_b, ltot = _sb_fwd(proj, "sb_fwd")
    y_a, y_b, m, mixed, h1 = _merge_fwd(o_a, o_b, proj, h0, w_o_fox, w_o_sb, w_out, g[1], "merge_fwd")
    up, xn3 = _rms_mm(h1, g[2], w_up, "up_proj")
    a = _convgate_fwd(up, conv_w, cb, "convgate_fwd")
    ffn, dy, ss = _down_loss(a, w_down, h1, g[3], target, "down_loss")

    dffn, da, dg3 = _down_bwd(dy, ffn, g[3], w_down.T, "down_bwd")
    d_w_down = _mm_tn(a, dffn, "dw_down")
    du, d_conv_w, d_conv_b = _convgate_bwd(up, da, conv_w, cb, "convgate_bwd")
    dup = _conv_transpose(du, conv_w, "conv_transpose")
    d_w_up = _mm_tn(xn3, dup, "dw_up")
    dh1, dg2 = _mm_rmsbwd(dup, w_up.T, h1, g[2], dy, "up_bwd")
    dmx, dya, dyb, dga, dgb, do_a, do_b, dg1 = _merge_bwd(
        dh1, mixed, g[1], w_out.T, proj, y_a, y_b, w_o_fox.T, w_o_sb.T, "merge_bwd")
    d_w_out = _mm_tn(m, dmx, "dw_out")
    d_w_o_fox = _mm_tn(o_a, dya, "dw_o_fox")
    d_w_o_sb = _mm_tn(o_b, dyb, "dw_o_sb")
    dq_a, dk_a, dv_a, dcs, dct = _fox_bwd(proj, do_a, o_a, lse, crow, "fox_bwd")
    dq_b, dk_b, dv_b = _sb_bwd(proj, do_b, ltot, "sb_bwd")
    dc = (dct.reshape(8, lp) + dcs.reshape(8, lp)).T
    df, db = _dlogf(jnp.pad(dc, ((0, 0), (0, LANES - 8))), logf, "forget_bwd")
    dcat = jnp.concatenate([dq_a, dk_a, dv_a, dq_b, dk_b, dv_b, dga, dgb, df.astype(BF16),
                            jnp.zeros((lp, ncat - n1 - LANES), BF16)], axis=1)
    d_w_cat = _mm_tn(xn1, dcat, "dw_in")
    dh0, dg0 = _mm_rmsbwd(dcat, w_cat_t, h0, g[0], dh1, "in_bwd")

    wc = lambda k: d_w_cat[:, k * W_ATT:(k + 1) * W_ATT]
    d_w_in = jnp.concatenate([wc(0), wc(1), wc(2), d_w_cat[:, n1:n1 + 8], wc(3), wc(4), wc(5),
                              d_w_cat[:, 6 * W_ATT:n1]], axis=1)
    d_gains = jnp.concatenate([dg0, dg1, dg2, dg3], axis=0)
    grads = (dh0[PAD:PAD + N_META], d_gains, d_w_in, db[0, :8], d_w_o_fox, d_w_o_sb, d_w_out,
             d_w_up, d_conv_w, d_conv_b[0], d_w_down)
    return ss[0, 0], dh0[PAD + N_META:], grads


def _rows(a, n_rows):
    flat = a.reshape(-1)
    return jnp.pad(flat, (0, n_rows * D_MODEL - flat.shape[0])).reshape(n_rows, D_MODEL)


_SMALL = (("meta", 4), ("gains", 1), ("conv_w", 5), ("b_forget", 1), ("conv_b", 6))


def _pack(meta, gains, w_in, b_forget, w_o_fox, w_o_sb, w_out, w_up, conv_w, conv_b, w_down):
    big = [a.reshape(-1, D_MODEL) for a in (w_in, w_o_fox, w_o_sb, w_out, w_up, w_down)]
    n_big = sum(a.shape[0] for a in big)
    small = [_rows(a, n) for a, (_, n) in zip((meta, gains, conv_w, b_forget, conv_b), _SMALL)]
    n_small = sum(n for _, n in _SMALL)
    z = lambda n: jnp.zeros((n, D_MODEL), big[0].dtype)
    return jnp.concatenate(big + [z(PK_BIG_ROWS - n_big)] + small
                           + [z(PK_ROWS - PK_BIG_ROWS - n_small)], axis=0)


def _unpack(p):
    def take(r0, shape):
        n = math.prod(shape)
        nr = -(-n // D_MODEL)
        return p[r0:r0 + nr].reshape(-1)[:n].reshape(shape), r0 + nr
    w_in, r = take(0, (1, 1024, 1282))
    w_o_fox, r = take(r, (1, 512, 256))
    w_o_sb, r = take(r, (1, 512, 256))
    w_out, r = take(r, (1, 256, 1024))
    w_up, r = take(r, (1, 1024, 1408))
    w_down, r = take(r, (1, 704, 1024))
    r = PK_BIG_ROWS
    meta, r = take(r, (16, 256))
    gains, r = take(r, (1, 4, 256))
    conv_w, r = take(r, (1, 3, 1408))
    b_forget, r = take(r, (1, 8))
    conv_b, r = take(r, (1, 5632))
    return meta, gains, w_in, b_forget, w_o_fox, w_o_sb, w_out, w_up, conv_w, conv_b, w_down


def _chip_peers():
    x, y, c = lax.axis_index("x"), lax.axis_index("y"), lax.axis_index("c")
    return [(x, 1 - y, c), (1 - x, y, c), (1 - x, 1 - y, c)]


def _all_gather_chips(arrays, name):
    n = len(arrays)

    def body(*refs):
        ins, outs = refs[:n], refs[n:2 * n]
        send_sems, recv_sems, local_sems = refs[2 * n:]
        x, y = lax.axis_index("x"), lax.axis_index("y")
        me = 2 * x + y
        peers = _chip_peers()
        copies = []
        for a in range(n):
            mine = pltpu.make_async_copy(ins[a], outs[a].at[me], local_sems.at[a])
            mine.start()
            copies.append(mine)
        remote = []
        for a in range(n):
            for j, peer in enumerate(peers):
                cp = pltpu.make_async_remote_copy(
                    src_ref=ins[a], dst_ref=outs[a].at[me],
                    send_sem=send_sems.at[3 * a + j], recv_sem=recv_sems.at[3 * a + j],
                    device_id=peer, device_id_type=MESH)
                cp.start()
                remote.append(cp)
        for cp in remote:
            cp.wait()
        for cp in copies:
            cp.wait()

    any_spec = pl.BlockSpec(memory_space=pl.ANY)
    return pl.pallas_call(
        body, name=name,
        in_specs=[any_spec] * n, out_specs=[any_spec] * n,
        out_shape=[jax.ShapeDtypeStruct((4,) + a.shape, a.dtype) for a in arrays],
        scratch_shapes=[pltpu.SemaphoreType.DMA((3 * n,)), pltpu.SemaphoreType.DMA((3 * n,)),
                        pltpu.SemaphoreType.DMA((n,))],
    )(*arrays)


def _scatter_chips(chunks, name):
    _, rows, cols = chunks.shape

    def body(in_ref, out_ref, send_sems, recv_sems):
        x, y = lax.axis_index("x"), lax.axis_index("y")
        targets = [2 * x + (1 - y), 2 * (1 - x) + y, 2 * (1 - x) + (1 - y)]
        remote = []
        for j, peer in enumerate(_chip_peers()):
            cp = pltpu.make_async_remote_copy(
                src_ref=in_ref.at[targets[j]], dst_ref=out_ref.at[j],
                send_sem=send_sems.at[j], recv_sem=recv_sems.at[j],
                device_id=peer, device_id_type=MESH)
            cp.start()
            remote.append(cp)
        for cp in remote:
            cp.wait()

    any_spec = pl.BlockSpec(memory_space=pl.ANY)
    return pl.pallas_call(
        body, name=name, in_specs=[any_spec], out_specs=any_spec,
        out_shape=jax.ShapeDtypeStruct((3, rows, cols), chunks.dtype),
        scratch_shapes=[pltpu.SemaphoreType.DMA((3,)), pltpu.SemaphoreType.DMA((3,))],
    )(chunks)


def _swap_cores(a, name):
    def body(in_ref, out_ref, send_sem, recv_sem):
        x, y, c = lax.axis_index("x"), lax.axis_index("y"), lax.axis_index("c")
        cp = pltpu.make_async_remote_copy(
            src_ref=in_ref, dst_ref=out_ref, send_sem=send_sem, recv_sem=recv_sem,
            device_id=(x, y, 1 - c), device_id_type=MESH)
        cp.start()
        cp.wait()

    any_spec = pl.BlockSpec(memory_space=pl.ANY)
    return pl.pallas_call(
        body, name=name, in_specs=[any_spec], out_specs=any_spec,
        out_shape=jax.ShapeDtypeStruct(a.shape, a.dtype),
        scratch_shapes=[pltpu.SemaphoreType.DMA, pltpu.SemaphoreType.DMA],
    )(a)


def _chip_sum(chunks, recv, name):
    _, rows, cols = chunks.shape

    def body(own_ref, r_ref, o_ref):
        o_ref[...] = (own_ref[...] + r_ref[0]) + (r_ref[1] + r_ref[2])

    me = 2 * lax.axis_index("x") + lax.axis_index("y")
    own = lax.dynamic_index_in_dim(chunks, me, axis=0, keepdims=False)
    return pl.pallas_call(
        body, name=name, grid=(rows // PK_TILE,),
        in_specs=[pl.BlockSpec((PK_TILE, cols), lambda i: (i, 0)),
                  pl.BlockSpec((3, PK_TILE, cols), lambda i: (0, i, 0))],
        out_specs=pl.BlockSpec((PK_TILE, cols), lambda i: (i, 0)),
        out_shape=jax.ShapeDtypeStruct((rows, cols), F32),
        compiler_params=_cparams(("parallel",)),
    )(own, recv)


def _adamw(w, m, v, g_mine, g_other, name):
    rows, cols = w.shape
    c1 = 1.0 - ADAM_B1 ** ADAM_STEP
    c2 = 1.0 - ADAM_B2 ** ADAM_STEP

    def body(w_ref, m_ref, v_ref, ga_ref, gb_ref, g_ref, d_ref, nm_ref, nv_ref):
        g = ga_ref[...] + gb_ref[...]
        nm = ADAM_B1 * m_ref[...] + (1.0 - ADAM_B1) * g
        nv = ADAM_B2 * v_ref[...] + (1.0 - ADAM_B2) * (g * g)
        g_ref[...] = g
        nm_ref[...] = nm
        nv_ref[...] = nv
        d_ref[...] = -ADAM_LR * ((nm / c1) / (jnp.sqrt(nv / c2) + ADAM_EPS) + ADAM_WD * w_ref[...])

    spec = pl.BlockSpec((PK_TILE, cols), lambda i: (i, 0))
    return pl.pallas_call(
        body, name=name, grid=(rows // PK_TILE,),
        in_specs=[spec] * 5, out_specs=[spec] * 4,
        out_shape=[jax.ShapeDtypeStruct((rows, cols), F32)] * 4,
        compiler_params=_cparams(("parallel",)),
    )(w, m, v, g_mine, g_other)


def _full_weights(big, small):
    def gather(src, r0, shape, axis):
        n = math.prod(shape)
        nr = -(-n // D_MODEL)
        parts = [src[k, r0:r0 + nr].reshape(-1)[:n].reshape(shape) for k in range(4)]
        return jnp.concatenate(parts, axis=axis), r0 + nr
    w_in, r = gather(big, 0, (1024, 1282), 1)
    w_o_fox, r = gather(big, r, (512, 256), 1)
    w_o_sb, r = gather(big, r, (512, 256), 1)
    w_out, r = gather(big, r, (256, 1024), 0)
    w_up, r = gather(big, r, (1024, 1408), 1)
    w_down, r = gather(big, r, (704, 1024), 0)
    meta, r = gather(small, 0, (16, 256), 1)
    gains, r = gather(small, r, (4, 256), 1)
    conv_w, r = gather(small, r, (3, 1408), 1)
    return meta, gains, w_in, w_o_fox, w_o_sb, w_out, w_up, conv_w, w_down


def _chunks_for_chips(grads):
    d_meta, d_gains, d_w_in, d_b, d_w_o_fox, d_w_o_sb, d_w_out, d_w_up, d_conv_w, d_conv_b, d_w_down = grads
    out = []
    for k in range(4):
        col = lambda a, w: a[:, k * w:(k + 1) * w]
        row = lambda a, w: a[k * w:(k + 1) * w]
        out.append(_pack(col(d_meta, 256), col(d_gains, 256), col(d_w_in, 1282), d_b, col(d_w_o_fox, 256),
                         col(d_w_o_sb, 256), row(d_w_out, 256), col(d_w_up, 1408), col(d_conv_w, 1408),
                         d_conv_b, row(d_w_down, 704)))
    return jnp.stack(out, axis=0)


def kernel(x, meta_tokens, norm_gains, w_in, b_forget, w_o_fox, w_o_sb, w_out, w_up, conv_w, conv_b, w_down, loss_target, m_meta_tokens, m_norm_gains, m_w_in, m_b_forget, m_w_o_fox, m_w_o_sb, m_w_out, m_w_up, m_conv_w, m_conv_b, m_w_down, v_meta_tokens, v_norm_gains, v_w_in, v_b_forget, v_w_o_fox, v_w_o_sb, v_w_out, v_w_up, v_conv_w, v_conv_b, v_w_down):
    shard = lambda mt, ng, wi, bf, wof, wos, wo, wu, cw, cb, wd: _pack(
        mt, ng[0], wi[0], bf[0], wof[0], wos[0], wo[0], wu[0], cw[0], cb[0], wd[0])
    wp = shard(meta_tokens, norm_gains, w_in, b_forget, w_o_fox, w_o_sb, w_out, w_up, conv_w, conv_b, w_down)
    mp = shard(m_meta_tokens, m_norm_gains, m_w_in, m_b_forget, m_w_o_fox, m_w_o_sb, m_w_out, m_w_up,
               m_conv_w, m_conv_b, m_w_down)
    vp = shard(v_meta_tokens, v_norm_gains, v_w_in, v_b_forget, v_w_o_fox, v_w_o_sb, v_w_out, v_w_up,
               v_conv_w, v_conv_b, v_w_down)

    big, small = _all_gather_chips(
        [wp[:PK_BIG_ROWS].astype(BF16), wp[PK_BIG_ROWS:PK_BIG_ROWS + PK_SMALL_ROWS]], "gather_weights")
    meta, gains, f_w_in, f_w_o_fox, f_w_o_sb, f_w_out, f_w_up, f_conv_w, f_w_down = _full_weights(big, small)

    ss, dx, grads = _local_step(x[0], loss_target[0], meta, gains, f_w_in, b_forget[0], f_w_o_fox, f_w_o_sb,
                                f_w_out, f_w_up, f_conv_w, conv_b[0], f_w_down)
    loss = lax.psum(0.5 * ss / D_MODEL, ("x", "y", "c"))

    chunks = _chunks_for_chips(grads)
    recv = _scatter_chips(chunks, "scatter_grads")
    mine = _chip_sum(chunks, recv, "chip_sum")
    other = _swap_cores(mine, "swap_cores")
    g, delta, new_m, new_v = _adamw(wp, mp, vp, mine, other, "adamw")
    return (loss, dx[None], *_unpack(g), *_unpack(delta), *_unpack(new_m), *_unpack(new_v))
```

```python
import functools
import math

import jax
import jax.numpy as jnp
from jax import lax
from jax.experimental import pallas as pl
from jax.experimental.pallas import tpu as pltpu

F32 = jnp.float32
BF16 = jnp.bfloat16

D_MODEL = 1024
N_META = 16
HEAD_DIM = 64
N_PAIRS = 4
W_ATT = 512
D_FF = 2816
EPS = 1e-6
NEG = -1e30
TQ = 256
PAD = TQ - N_META
TCONV = 128
HALO = 16
LANES = 128
VMEM_LIMIT = 56 * 1024 * 1024

ADAM_LR = 0.001
ADAM_B1 = 0.9
ADAM_B2 = 0.999
ADAM_EPS = 1e-08
ADAM_WD = 0.01
ADAM_STEP = 10

MESH = pl.DeviceIdType.MESH

PK_BIG_ROWS = 3936
PK_SMALL_ROWS = 24
PK_ROWS = 4096
PK_TILE = 128


def _cparams(sem, **kw):
    return pltpu.CompilerParams(dimension_semantics=sem, vmem_limit_bytes=VMEM_LIMIT, **kw)


def _row_tile(lp):
    return 768 if lp % 768 == 0 else 256


def _wide_tile(n):
    return next(t for t in (1408, 1280, 1024, 512, 256) if n % t == 0)


def _rms(x):
    return lax.rsqrt(jnp.mean(x * x, axis=-1, keepdims=True) + EPS)


def _log_sigmoid(x):
    return jnp.minimum(x, 0.0) - jnp.log(1.0 + jnp.exp(-jnp.abs(x)))


def _split2(x):
    hi = x.astype(BF16)
    lo = (x - hi.astype(F32)).astype(BF16)
    return hi, lo


def _dot(a, b):
    return jnp.dot(a, b, preferred_element_type=F32)


def _dot_nt(a, b):
    return lax.dot_general(a, b, (((1,), (1,)), ((), ())), preferred_element_type=F32)


def _dot_tn(a, b):
    return lax.dot_general(a, b, (((0,), (0,)), ((), ())), preferred_element_type=F32)


def _rms_mm(h, g, w, name):
    lp, dm = h.shape
    n = w.shape[1]
    tr, tn = _row_tile(lp), _wide_tile(n)

    def body(h_ref, g_ref, w_ref, out_ref, xn_ref):
        @pl.when(pl.program_id(1) == 0)
        def _():
            x = h_ref[...]
            xn_ref[...] = (x * _rms(x) * g_ref[...]).astype(BF16)
        out_ref[...] = _dot(xn_ref[...], w_ref[...]).astype(BF16)

    return pl.pallas_call(
        body, name=name, grid=(lp // tr, n // tn),
        in_specs=[pl.BlockSpec((tr, dm), lambda i, j: (i, 0)),
                  pl.BlockSpec((1, dm), lambda i, j: (0, 0)),
                  pl.BlockSpec((dm, tn), lambda i, j: (0, j))],
        out_specs=[pl.BlockSpec((tr, tn), lambda i, j: (i, j)),
                   pl.BlockSpec((tr, dm), lambda i, j: (i, 0))],
        out_shape=[jax.ShapeDtypeStruct((lp, n), BF16), jax.ShapeDtypeStruct((lp, dm), BF16)],
        compiler_params=_cparams(("parallel", "arbitrary")),
    )(h, g, w)


def _mm_rmsbwd(dy, w, h, g, dh_in, name):
    lp, kd = dy.shape
    dm = w.shape[0]
    tr, tk = 384, _wide_tile(kd)
    nk = kd // tk

    def body(dy_ref, w_ref, h_ref, g_ref, dhin_ref, dh_ref, dg_ref, acc_ref):
        i, k = pl.program_id(0), pl.program_id(1)

        @pl.when(k == 0)
        def _():
            acc_ref[...] = jnp.zeros_like(acc_ref)

        acc_ref[...] += _dot_nt(dy_ref[...], w_ref[...])

        @pl.when(k == nk - 1)
        def _():
            dxn = acc_ref[...]
            x = h_ref[...]
            r = _rms(x)
            yhat = x * r
            part = jnp.sum(dxn * yhat, axis=0, keepdims=True)
            dyh = dxn * g_ref[...]
            dx = r * (dyh - yhat * jnp.mean(dyh * yhat, axis=-1, keepdims=True))
            dh_ref[...] = dhin_ref[...] + dx

            @pl.when(i == 0)
            def _():
                dg_ref[...] = part

            @pl.when(i > 0)
            def _():
                dg_ref[...] += part

    return pl.pallas_call(
        body, name=name, grid=(lp // tr, nk),
        in_specs=[pl.BlockSpec((tr, tk), lambda i, k: (i, k)),
                  pl.BlockSpec((dm, tk), lambda i, k: (0, k)),
                  pl.BlockSpec((tr, dm), lambda i, k: (i, 0)),
                  pl.BlockSpec((1, dm), lambda i, k: (0, 0)),
                  pl.BlockSpec((tr, dm), lambda i, k: (i, 0))],
        out_specs=[pl.BlockSpec((tr, dm), lambda i, k: (i, 0)),
                   pl.BlockSpec((1, dm), lambda i, k: (0, 0))],
        out_shape=[jax.ShapeDtypeStruct((lp, dm), F32), jax.ShapeDtypeStruct((1, dm), F32)],
        scratch_shapes=[pltpu.VMEM((tr, dm), F32)],
        compiler_params=_cparams(("arbitrary", "arbitrary")),
    )(dy, w, h, g, dh_in)


def _mm_tn(x, dy, name):
    lp, kd = x.shape
    n = dy.shape[1]
    tl = _row_tile(lp)
    tk = _wide_tile(kd)
    tn = _wide_tile(n)
    nl = lp // tl

    def body(x_ref, dy_ref, o_ref):
        @pl.when(pl.program_id(2) == 0)
        def _():
            o_ref[...] = jnp.zeros_like(o_ref)
        o_ref[...] += _dot_tn(x_ref[...], dy_ref[...])

    return pl.pallas_call(
        body, name=name, grid=(kd // tk, n // tn, nl),
        in_specs=[pl.BlockSpec((tl, tk), lambda a, b, l: (l, a)),
                  pl.BlockSpec((tl, tn), lambda a, b, l: (l, b))],
        out_specs=pl.BlockSpec((tk, tn), lambda a, b, l: (a, b)),
        out_shape=jax.ShapeDtypeStruct((kd, n), F32),
        compiler_params=_cparams(("parallel", "parallel", "arbitrary")),
    )(x, dy)


def _logf(xn, wf, bf, name):
    lp, dm = xn.shape
    tr = _row_tile(lp)

    def body(xn_ref, wf_ref, b_ref, o_ref):
        f = _dot(xn_ref[...], wf_ref[...]) + b_ref[...]
        row = pl.program_id(0) * tr + lax.broadcasted_iota(jnp.int32, f.shape, 0)
        lane = lax.broadcasted_iota(jnp.int32, f.shape, 1)
        o_ref[...] = jnp.where((row >= PAD) & (lane < 8), _log_sigmoid(f), 0.0)

    return pl.pallas_call(
        body, name=name, grid=(lp // tr,),
        in_specs=[pl.BlockSpec((tr, dm), lambda i: (i, 0)),
                  pl.BlockSpec((dm, LANES), lambda i: (0, 0)),
                  pl.BlockSpec((1, LANES), lambda i: (0, 0))],
        out_specs=pl.BlockSpec((tr, LANES), lambda i: (i, 0)),
        out_shape=jax.ShapeDtypeStruct((lp, LANES), F32),
        compiler_params=_cparams(("parallel",)),
    )(xn, wf, bf)


def _tri(n, rel):
    r = lax.broadcasted_iota(jnp.int32, (n, n), 0)
    c = lax.broadcasted_iota(jnp.int32, (n, n), 1)
    return rel(r, c).astype(BF16)


def _cumsum_rows(x, name):
    lp = x.shape[0]
    nb = lp // TQ
    tl = _tri(TQ, lambda r, c: c <= r)

    def body(x_ref, t_ref, o_ref):
        def step(b, carry):
            rows = pl.ds(pl.multiple_of(b * TQ, TQ), TQ)
            xb = x_ref[rows, :]
            hi = xb.astype(BF16)
            r1 = xb - hi.astype(F32)
            mid = r1.astype(BF16)
            lo = (r1 - mid.astype(F32)).astype(BF16)
            t = t_ref[...]
            o_ref[rows, :] = carry + (_dot(t, hi) + _dot(t, mid) + _dot(t, lo))
            return carry + jnp.sum(xb, axis=0, keepdims=True)
        lax.fori_loop(0, nb, step, jnp.zeros((1, LANES), F32))

    return pl.pallas_call(
        body, name=name,
        in_specs=[pl.BlockSpec(memory_space=pltpu.VMEM)] * 2,
        out_specs=pl.BlockSpec(memory_space=pltpu.VMEM),
        out_shape=jax.ShapeDtypeStruct((lp, LANES), F32),
        compiler_params=pltpu.CompilerParams(vmem_limit_bytes=VMEM_LIMIT),
    )(x, tl)


def _dlogf(dc, logf, name):
    lp = dc.shape[0]
    nb = lp // TQ
    tu = _tri(TQ, lambda r, c: c >= r)

    def body(x_ref, lf_ref, t_ref, df_ref, db_ref):
        def step(bb, carry):
            run, db = carry
            b = nb - 1 - bb
            rows = pl.ds(pl.multiple_of(b * TQ, TQ), TQ)
            xb = x_ref[rows, :]
            hi = xb.astype(BF16)
            r1 = xb - hi.astype(F32)
            mid = r1.astype(BF16)
            lo = (r1 - mid.astype(F32)).astype(BF16)
            t = t_ref[...]
            dlf = run + (_dot(t, hi) + _dot(t, mid) + _dot(t, lo))
            df = dlf * (1.0 - jnp.exp(lf_ref[rows, :]))
            df_ref[rows, :] = df
            return run + jnp.sum(xb, axis=0, keepdims=True), db + jnp.sum(df, axis=0, keepdims=True)
        z = jnp.zeros((1, LANES), F32)
        _, db = lax.fori_loop(0, nb, step, (z, z))
        db_ref[...] = db

    return pl.pallas_call(
        body, name=name,
        in_specs=[pl.BlockSpec(memory_space=pltpu.VMEM)] * 3,
        out_specs=[pl.BlockSpec(memory_space=pltpu.VMEM)] * 2,
        out_shape=[jax.ShapeDtypeStruct((lp, LANES), F32), jax.ShapeDtypeStruct((1, LANES), F32)],
        compiler_params=pltpu.CompilerParams(vmem_limit_bytes=VMEM_LIMIT),
    )(dc, logf, tu)


def _merge_fwd(o_a, o_b, proj, h0, w_oa, w_ob, w_out, g1, name):
    lp, dm = h0.shape
    tr = TQ
    ga_blk = (6 * W_ATT) // dm

    def body(oa_ref, ob_ref, ga_ref, gb_ref, h0_ref, woa_ref, wob_ref, wout_ref, g1_ref,
             ya_ref, yb_ref, m_ref, mixed_ref, h1_ref):
        ya = _dot(oa_ref[...], woa_ref[...])
        yb = _dot(ob_ref[...], wob_ref[...])
        m = jax.nn.sigmoid(ga_ref[...].astype(F32)) * ya + jax.nn.sigmoid(gb_ref[...].astype(F32)) * yb
        mb = m.astype(BF16)
        mixed = _dot(mb, wout_ref[...])
        ya_ref[...] = ya.astype(BF16)
        yb_ref[...] = yb.astype(BF16)
        m_ref[...] = mb
        mixed_ref[...] = mixed
        h1_ref[...] = h0_ref[...] + mixed * _rms(mixed) * g1_ref[...]

    row = lambda w: pl.BlockSpec((tr, w), lambda i: (i, 0))
    full = lambda a: pl.BlockSpec(a.shape, lambda i: (0, 0))
    return pl.pallas_call(
        body, name=name, grid=(lp // tr,),
        in_specs=[row(W_ATT), row(W_ATT),
                  pl.BlockSpec((tr, dm), lambda i: (i, ga_blk)),
                  pl.BlockSpec((tr, dm), lambda i: (i, ga_blk + 1)),
                  row(dm), full(w_oa), full(w_ob), full(w_out), full(g1)],
        out_specs=[row(dm)] * 5,
        out_shape=[jax.ShapeDtypeStruct((lp, dm), BF16)] * 3 + [jax.ShapeDtypeStruct((lp, dm), F32)] * 2,
        compiler_params=_cparams(("parallel",)),
    )(o_a, o_b, proj, proj, h0, w_oa, w_ob, w_out, g1)


def _merge_bwd(dh1, mixed, g1, w_out, proj, y_a, y_b, w_oa, w_ob, name):
    lp, dm = dh1.shape
    tr = TQ
    ga_blk = (6 * W_ATT) // dm

    def body(dh_ref, mx_ref, g1_ref, wout_ref, ga_ref, gb_ref, ya_ref, yb_ref, woa_ref, wob_ref,
             dmx_ref, dya_ref, dyb_ref, dga_ref, dgb_ref, doa_ref, dob_ref, dg1_ref):
        i = pl.program_id(0)
        dn = dh_ref[...]
        x = mx_ref[...]
        r = _rms(x)
        yhat = x * r
        part = jnp.sum(dn * yhat, axis=0, keepdims=True)
        dyh = dn * g1_ref[...]
        dmx = (r * (dyh - yhat * jnp.mean(dyh * yhat, axis=-1, keepdims=True))).astype(BF16)
        dmx_ref[...] = dmx
        dm_ = _dot_nt(dmx, wout_ref[...])
        sa = jax.nn.sigmoid(ga_ref[...].astype(F32))
        sb = jax.nn.sigmoid(gb_ref[...].astype(F32))
        dya = (dm_ * sa).astype(BF16)
        dyb = (dm_ * sb).astype(BF16)
        dya_ref[...] = dya
        dyb_ref[...] = dyb
        dga_ref[...] = (dm_ * ya_ref[...].astype(F32) * sa * (1.0 - sa)).astype(BF16)
        dgb_ref[...] = (dm_ * yb_ref[...].astype(F32) * sb * (1.0 - sb)).astype(BF16)
        doa_ref[...] = _dot_nt(dya, woa_ref[...]).astype(BF16)
        dob_ref[...] = _dot_nt(dyb, wob_ref[...]).astype(BF16)

        @pl.when(i == 0)
        def _():
            dg1_ref[...] = part

        @pl.when(i > 0)
        def _():
            dg1_ref[...] += part

    row = lambda w: pl.BlockSpec((tr, w), lambda i: (i, 0))
    full = lambda a: pl.BlockSpec(a.shape, lambda i: (0, 0))
    return pl.pallas_call(
        body, name=name, grid=(lp // tr,),
        in_specs=[row(dm), row(dm), full(g1), full(w_out),
                  pl.BlockSpec((tr, dm), lambda i: (i, ga_blk)),
                  pl.BlockSpec((tr, dm), lambda i: (i, ga_blk + 1)),
                  row(dm), row(dm), full(w_oa), full(w_ob)],
        out_specs=[row(dm)] * 5 + [row(W_ATT)] * 2 + [pl.BlockSpec((1, dm), lambda i: (0, 0))],
        out_shape=[jax.ShapeDtypeStruct((lp, dm), BF16)] * 5 + [jax.ShapeDtypeStruct((lp, W_ATT), BF16)] * 2
        + [jax.ShapeDtypeStruct((1, dm), F32)],
        compiler_params=_cparams(("arbitrary",)),
    )(dh1, mixed, g1, w_out, proj, proj, y_a, y_b, w_oa, w_ob)


_GELU_C = math.sqrt(2.0 / math.pi)
_GELU_A = 0.044715


def _gelu(x):
    t = jnp.tanh(_GELU_C * (x + _GELU_A * x * x * x))
    return 0.5 * x * (1.0 + t), t


CW = 256


def _taps(cur_ref, prev_ref, first, c0):
    cur = cur_ref[:, c0:c0 + CW].astype(F32)
    p1 = jnp.where(first, 0.0, prev_ref[HALO - 1:HALO, c0:c0 + CW].astype(F32))
    p2 = jnp.where(first, 0.0, prev_ref[HALO - 2:HALO - 1, c0:c0 + CW].astype(F32))
    row = lax.broadcasted_iota(jnp.int32, cur.shape, 0)
    x1 = jnp.where(row == 0, p1, pltpu.roll(cur, 1, 0))
    x2 = jnp.where(row == 0, p2, jnp.where(row == 1, p1, pltpu.roll(cur, 2, 0)))
    return cur, x1, x2


def _conv_at(cur_ref, prev_ref, w_ref, b_ref, first, c0):
    cur, x1, x2 = _taps(cur_ref, prev_ref, first, c0)
    cols = slice(c0, c0 + CW)
    u = b_ref[:, cols] + w_ref[0:1, cols] * x2 + w_ref[1:2, cols] * x1 + w_ref[2:3, cols] * cur
    return u, (x2, x1, cur)


def _up_specs(tr, width):
    per = tr // HALO
    return [pl.BlockSpec((tr, width), lambda i: (i, 0)),
            pl.BlockSpec((HALO, width), lambda i: (jnp.maximum(i * per - 1, 0), 0))]


def _convgate_fwd(up, conv_w, conv_b, name):
    lp, c2 = up.shape
    tr = TCONV

    def body(cur_ref, prev_ref, w_ref, b_ref, a_ref):
        first = pl.program_id(0) == 0
        for c0 in range(0, D_FF, CW):
            ug, _ = _conv_at(cur_ref, prev_ref, w_ref, b_ref, first, c0)
            uv, _ = _conv_at(cur_ref, prev_ref, w_ref, b_ref, first, D_FF + c0)
            gel, _ = _gelu(ug)
            a_ref[:, c0:c0 + CW] = (gel * uv).astype(BF16)

    return pl.pallas_call(
        body, name=name, grid=(lp // tr,),
        in_specs=_up_specs(tr, c2) + [pl.BlockSpec((3, c2), lambda i: (0, 0)),
                                      pl.BlockSpec((1, c2), lambda i: (0, 0))],
        out_specs=pl.BlockSpec((tr, D_FF), lambda i: (i, 0)),
        out_shape=jax.ShapeDtypeStruct((lp, D_FF), BF16),
        compiler_params=_cparams(("parallel",)),
    )(up, up, conv_w, conv_b)


def _convgate_bwd(up, da, conv_w, conv_b, name):
    lp, c2 = up.shape
    tr = TCONV

    def body(cur_ref, prev_ref, da_ref, w_ref, b_ref, du_ref, dw_ref, db_ref):
        i = pl.program_id(0)
        first = i == 0

        @pl.when(first)
        def _():
            dw_ref[...] = jnp.zeros_like(dw_ref)
            db_ref[...] = jnp.zeros_like(db_ref)

        for c0 in range(0, D_FF, CW):
            ug, taps_g = _conv_at(cur_ref, prev_ref, w_ref, b_ref, first, c0)
            uv, taps_v = _conv_at(cur_ref, prev_ref, w_ref, b_ref, first, D_FF + c0)
            gel, t = _gelu(ug)
            dgel = 0.5 * (1.0 + t) + 0.5 * ug * (1.0 - t * t) * _GELU_C * (1.0 + 3.0 * _GELU_A * ug * ug)
            da_ = da_ref[:, c0:c0 + CW].astype(F32)
            for base, du, taps in ((c0, da_ * uv * dgel, taps_g), (D_FF + c0, da_ * gel, taps_v)):
                cols = slice(base, base + CW)
                du_ref[:, cols] = du.astype(BF16)
                for tap in range(3):
                    dw_ref[tap:tap + 1, cols] += jnp.sum(du * taps[tap], axis=0, keepdims=True)
                db_ref[:, cols] += jnp.sum(du, axis=0, keepdims=True)

    return pl.pallas_call(
        body, name=name, grid=(lp // tr,),
        in_specs=_up_specs(tr, c2) + [pl.BlockSpec((tr, D_FF), lambda i: (i, 0)),
                                      pl.BlockSpec((3, c2), lambda i: (0, 0)),
                                      pl.BlockSpec((1, c2), lambda i: (0, 0))],
        out_specs=[pl.BlockSpec((tr, c2), lambda i: (i, 0)),
                   pl.BlockSpec((3, c2), lambda i: (0, 0)),
                   pl.BlockSpec((1, c2), lambda i: (0, 0))],
        out_shape=[jax.ShapeDtypeStruct((lp, c2), BF16), jax.ShapeDtypeStruct((3, c2), F32),
                   jax.ShapeDtypeStruct((1, c2), F32)],
        compiler_params=_cparams(("arbitrary",)),
    )(up, up, da, conv_w, conv_b)


def _conv_transpose(du, conv_w, name):
    lp, c2 = du.shape
    tr = TCONV
    per = tr // HALO
    n_halo = lp // HALO
    nt = lp // tr

    def body(cur_ref, nxt_ref, w_ref, o_ref):
        last = pl.program_id(0) == nt - 1
        for c0 in range(0, c2, CW):
            cols = slice(c0, c0 + CW)
            cur = cur_ref[:, cols].astype(F32)
            n0 = jnp.where(last, 0.0, nxt_ref[0:1, cols].astype(F32))
            n1 = jnp.where(last, 0.0, nxt_ref[1:2, cols].astype(F32))
            row = lax.broadcasted_iota(jnp.int32, cur.shape, 0)
            y1 = jnp.where(row == tr - 1, n0, pltpu.roll(cur, tr - 1, 0))
            y2 = jnp.where(row == tr - 1, n1, jnp.where(row == tr - 2, n0, pltpu.roll(cur, tr - 2, 0)))
            o_ref[:, cols] = (w_ref[2:3, cols] * cur + w_ref[1:2, cols] * y1 + w_ref[0:1, cols] * y2).astype(BF16)

    return pl.pallas_call(
        body, name=name, grid=(nt,),
        in_specs=[pl.BlockSpec((tr, c2), lambda i: (i, 0)),
                  pl.BlockSpec((HALO, c2), lambda i: (jnp.minimum((i + 1) * per, n_halo - 1), 0)),
                  pl.BlockSpec((3, c2), lambda i: (0, 0))],
        out_specs=pl.BlockSpec((tr, c2), lambda i: (i, 0)),
        out_shape=jax.ShapeDtypeStruct((lp, c2), BF16),
        compiler_params=_cparams(("parallel",)),
    )(du, du, conv_w)


def _down_loss(a, w_down, h1, g3, target, name):
    lp, dm = h1.shape
    tr = TQ

    def body(a_ref, w_ref, h1_ref, g_ref, t_ref, ffn_ref, dy_ref, ss_ref):
        i = pl.program_id(0)
        ffn = _dot(a_ref[...], w_ref[...])
        ffn_ref[...] = ffn
        h2 = h1_ref[...] + ffn * _rms(ffn) * g_ref[...]
        d = jnp.where(i > 0, h2 - t_ref[...], 0.0)
        dy_ref[...] = d * (1.0 / dm)
        part = jnp.sum(jnp.sum(d * d, axis=0, keepdims=True), axis=1, keepdims=True)

        @pl.when(i == 0)
        def _():
            ss_ref[...] = jnp.zeros_like(ss_ref)

        ss_ref[...] += part

    return pl.pallas_call(
        body, name=name, grid=(lp // tr,),
        in_specs=[pl.BlockSpec((tr, D_FF), lambda i: (i, 0)),
                  pl.BlockSpec(w_down.shape, lambda i: (0, 0)),
                  pl.BlockSpec((tr, dm), lambda i: (i, 0)),
                  pl.BlockSpec((1, dm), lambda i: (0, 0)),
                  pl.BlockSpec((tr, dm), lambda i: (jnp.maximum(i - 1, 0), 0))],
        out_specs=[pl.BlockSpec((tr, dm), lambda i: (i, 0)),
                   pl.BlockSpec((tr, dm), lambda i: (i, 0)),
                   pl.BlockSpec((8, LANES), lambda i: (0, 0))],
        out_shape=[jax.ShapeDtypeStruct((lp, dm), F32), jax.ShapeDtypeStruct((lp, dm), F32),
                   jax.ShapeDtypeStruct((8, LANES), F32)],
        compiler_params=_cparams(("arbitrary",)),
    )(a, w_down, h1, g3, target)


def _down_bwd(dy, ffn, g3, w_down, name):
    lp, dm = dy.shape
    tr = TQ

    def body(dy_ref, f_ref, g_ref, w_ref, dffn_ref, da_ref, dg_ref):
        i = pl.program_id(0)
        dn = dy_ref[...]
        x = f_ref[...]
        r = _rms(x)
        yhat = x * r
        part = jnp.sum(dn * yhat, axis=0, keepdims=True)
        dyh = dn * g_ref[...]
        dffn = (r * (dyh - yhat * jnp.mean(dyh * yhat, axis=-1, keepdims=True))).astype(BF16)
        dffn_ref[...] = dffn
        da_ref[...] = _dot_nt(dffn, w_ref[...]).astype(BF16)

        @pl.when(i == 0)
        def _():
            dg_ref[...] = part

        @pl.when(i > 0)
        def _():
            dg_ref[...] += part

    return pl.pallas_call(
        body, name=name, grid=(lp // tr,),
        in_specs=[pl.BlockSpec((tr, dm), lambda i: (i, 0)),
                  pl.BlockSpec((tr, dm), lambda i: (i, 0)),
                  pl.BlockSpec((1, dm), lambda i: (0, 0)),
                  pl.BlockSpec(w_down.shape, lambda i: (0, 0))],
        out_specs=[pl.BlockSpec((tr, dm), lambda i: (i, 0)),
                   pl.BlockSpec((tr, D_FF), lambda i: (i, 0)),
                   pl.BlockSpec((1, dm), lambda i: (0, 0))],
        out_shape=[jax.ShapeDtypeStruct((lp, dm), BF16), jax.ShapeDtypeStruct((lp, D_FF), BF16),
                   jax.ShapeDtypeStruct((1, dm), F32)],
        compiler_params=_cparams(("arbitrary",)),
    )(dy, ffn, g3, w_down)


def _pair_specs(lp, base):
    return [pl.BlockSpec((TQ, LANES), lambda p, i: (i, base + p)),
            pl.BlockSpec((lp, LANES), lambda p, i: (0, base + N_PAIRS + p)),
            pl.BlockSpec((lp, LANES), lambda p, i: (0, base + 2 * N_PAIRS + p))]


def _col_spec():
    return pl.BlockSpec((None, 2, TQ, 1), lambda p, i: (p, 0, i, 0))


def _rowvec_spec(nb):
    return pl.BlockSpec((None, 2, nb, 1, TQ), lambda p, i: (p, 0, 0, 0, 0))


def _tile_spec():
    return pl.BlockSpec((TQ, LANES), lambda p, i: (i, p))


RC = 64
T2 = 2 * TQ


def _stack_heads(x, scale=None):
    lane = lax.broadcasted_iota(jnp.int32, x.shape, 1)
    zero = jnp.zeros_like(x)
    x2 = jnp.concatenate([jnp.where(lane < HEAD_DIM, x, zero), jnp.where(lane >= HEAD_DIM, x, zero)], axis=0)
    return x2 if scale is None else x2 * scale


def _unstack_heads(x2):
    lane = lax.broadcasted_iota(jnp.int32, (TQ, LANES), 1)
    return jnp.where(lane < HEAD_DIM, x2[:TQ], x2[TQ:])


def _stack_cols(ref):
    return jnp.concatenate([ref[0], ref[1]], axis=0)


def _chunk_valid(i, j, r, strict):
    qpos = i * TQ + (r % TQ) + lax.broadcasted_iota(jnp.int32, (RC, TQ), 0)
    kpos = j * TQ + lax.broadcasted_iota(jnp.int32, (RC, TQ), 1)
    causal = (kpos < qpos) if strict else (kpos <= qpos)
    return causal & (kpos >= PAD)


def _walk_tiles(i, step, reverse):
    first, last = (i, 0) if reverse else (0, i)
    step(first, True)
    if reverse:
        lax.fori_loop(0, i - 1, lambda t, c: (step(i - 1 - t, False), c)[1], 0)
    else:
        lax.fori_loop(1, i, lambda j, c: (step(j, False), c)[1], 0)

    @pl.when(i > 0)
    def _():
        step(last, True)


_HALF = (slice(0, TQ), slice(TQ, T2))


def _walk_tiles_lead(i, lead, step, reverse):
    first, last = (i, 0) if reverse else (0, i)
    lead(first, 0)
    lead(first, 1)
    step(first, jnp.maximum(i - 1, 0) if reverse else jnp.minimum(1, i), True)

    def between(t, c):
        j = i - 1 - t if reverse else t + 1
        step(j, j - 1 if reverse else j + 1, False)
        return c

    lax.fori_loop(0, i - 1, between, 0)

    @pl.when(i > 0)
    def _():
        step(last, None, True)


def _krows(j):
    return pl.ds(pl.multiple_of(j * TQ, TQ), TQ)


def _fox_fwd(proj, crow, name):
    lp = proj.shape[0]
    nb = lp // TQ

    def body(q_ref, k_ref, v_ref, cr_ref, o_ref, lse_ref, m_ref, acc_ref, p_ref):
        i = pl.program_id(1)
        q2 = _stack_heads(q_ref[...], 0.125)
        m_ref[...] = jnp.full(m_ref.shape, NEG, F32)
        acc_ref[...] = jnp.zeros_like(acc_ref)
        lane = lax.broadcasted_iota(jnp.int32, (TQ, LANES), 1)

        def step(j, masked):
            rows_j = _krows(j)
            s = _dot_nt(q2, k_ref[rows_j, :])
            v = v_ref[rows_j, :]
            one = jnp.ones_like(v)
            v_heads = (jnp.where(lane < HEAD_DIM, v, one), jnp.where(lane >= HEAD_DIM, v, one))
            for r in range(0, T2, RC):
                rows = slice(r, r + RC)
                s_c = s[rows] - cr_ref[r // TQ, j]
                if masked:
                    s_c = jnp.where(_chunk_valid(i, j, r, False), s_c, NEG)
                s0, s1 = s_c[:, :LANES], s_c[:, LANES:]
                m_old = m_ref[rows]
                m_new = jnp.maximum(m_old, jnp.max(jnp.maximum(s0, s1), axis=-1, keepdims=True))
                m_ref[rows] = m_new
                acc_ref[rows] = jnp.exp(m_old - m_new) * acc_ref[rows]
                p_ref[rows, :LANES] = jnp.exp(s0 - m_new).astype(BF16)
                p_ref[rows, LANES:] = jnp.exp(s1 - m_new).astype(BF16)
            for h in range(2):
                acc_ref[_HALF[h]] += _dot(p_ref[_HALF[h]], v_heads[h])

        _walk_tiles(i, step, reverse=False)
        acc = acc_ref[...]
        m = m_ref[...]
        outs = []
        for h in range(2):
            a_h = acc[_HALF[h]]
            l = a_h[:, HEAD_DIM:HEAD_DIM + 1] if h == 0 else a_h[:, 0:1]
            lse_ref[h] = m[_HALF[h]][:, 0:1] + jnp.log(l)
            outs.append(a_h / l)
        o_ref[...] = jnp.where(lane < HEAD_DIM, outs[0], outs[1]).astype(BF16)

    return pl.pallas_call(
        body, name=name, grid=(N_PAIRS, nb),
        in_specs=_pair_specs(lp, 0) + [_rowvec_spec(nb)],
        out_specs=[_tile_spec(), _col_spec()],
        out_shape=[jax.ShapeDtypeStruct((lp, W_ATT), BF16),
                   jax.ShapeDtypeStruct((N_PAIRS, 2, lp, 1), F32)],
        scratch_shapes=[pltpu.VMEM((T2, LANES), F32), pltpu.VMEM((T2, LANES), F32), pltpu.VMEM((T2, TQ), BF16)],
        compiler_params=_cparams(("parallel", "arbitrary")),
    )(proj, proj, proj, crow)


def _fox_bwd(proj, do, o, lse, crow, name):
    lp = proj.shape[0]
    nb = lp // TQ

    def body(q_ref, k_ref, v_ref, do_ref, o_ref, lse_ref, cr_ref,
             dq_ref, dk_ref, dv_ref, dcs_ref, dct_ref,
             dk_acc, dv_acc, dq_acc, dct_acc, p_ref, ds_ref, s_ref, dp_ref):
        i = pl.program_id(1)

        @pl.when(i == 0)
        def _():
            dk_acc[...] = jnp.zeros_like(dk_acc)
            dv_acc[...] = jnp.zeros_like(dv_acc)
            dcs_ref[...] = jnp.zeros_like(dcs_ref)

        dq_acc[...] = jnp.zeros_like(dq_acc)
        dct_acc[...] = jnp.zeros_like(dct_acc)
        do_ = do_ref[...]
        q2 = _stack_heads(q_ref[...], 0.125)
        do2 = _stack_heads(do_)
        prod = do_.astype(F32) * o_ref[...].astype(F32)
        lane = lax.broadcasted_iota(jnp.int32, prod.shape, 1)
        delta2 = jnp.concatenate(
            [jnp.sum(jnp.where(lane < HEAD_DIM, prod, 0.0), axis=-1, keepdims=True),
             jnp.sum(jnp.where(lane >= HEAD_DIM, prod, 0.0), axis=-1, keepdims=True)], axis=0)
        lse2 = _stack_cols(lse_ref)

        def lead(j, h):
            rows_j = _krows(j)
            s_ref[_HALF[h]] = _dot_nt(q2[_HALF[h]], k_ref[rows_j, :])
            dp_ref[_HALF[h]] = _dot_nt(do2[_HALF[h]], v_ref[rows_j, :])

        def step(j, nxt, masked):
            rows_j = _krows(j)
            k = k_ref[rows_j, :]
            for h in range(2):
                cs = jnp.zeros((1, TQ), F32)
                for r in range(h * TQ, (h + 1) * TQ, RC):
                    rows = slice(r, r + RC)
                    p = jnp.exp(s_ref[rows] - cr_ref[h, j] - lse2[rows])
                    if masked:
                        p = jnp.where(_chunk_valid(i, j, r, False), p, 0.0)
                    ds = p * (dp_ref[rows] - delta2[rows])
                    p_ref[rows] = p.astype(BF16)
                    ds_ref[rows] = ds.astype(BF16)
                    dct_acc[rows] += jnp.sum(ds, axis=-1, keepdims=True)
                    cs = cs + jnp.sum(ds, axis=0, keepdims=True)
                dcs_ref[h, j] -= cs
                if nxt is not None:
                    lead(nxt, h)
                dsb = ds_ref[_HALF[h]]
                dq_acc[_HALF[h]] += _dot(dsb, k)
                dk_acc[rows_j, :] += _dot_tn(dsb, q2[_HALF[h]])
                dv_acc[rows_j, :] += _dot_tn(p_ref[_HALF[h]], do2[_HALF[h]])

        _walk_tiles_lead(i, lead, step, reverse=False)
        dct = dct_acc[...]
        dct_ref[0] = dct[:TQ]
        dct_ref[1] = dct[TQ:]
        dq_ref[...] = (_unstack_heads(dq_acc[...]) * 0.125).astype(BF16)

        @pl.when(i == nb - 1)
        def _():
            dk_ref[...] = dk_acc[...].astype(BF16)
            dv_ref[...] = dv_acc[...].astype(BF16)

    whole = pl.BlockSpec((lp, LANES), lambda p, i: (0, p))
    return pl.pallas_call(
        body, name=name, grid=(N_PAIRS, nb),
        in_specs=_pair_specs(lp, 0) + [_tile_spec(), _tile_spec(), _col_spec(), _rowvec_spec(nb)],
        out_specs=[_tile_spec(), whole, whole, _rowvec_spec(nb), _col_spec()],
        out_shape=[jax.ShapeDtypeStruct((lp, W_ATT), BF16)] * 3
        + [jax.ShapeDtypeStruct((N_PAIRS, 2, nb, 1, TQ), F32), jax.ShapeDtypeStruct((N_PAIRS, 2, lp, 1), F32)],
        scratch_shapes=[pltpu.VMEM((lp, LANES), F32), pltpu.VMEM((lp, LANES), F32),
                        pltpu.VMEM((T2, LANES), F32), pltpu.VMEM((T2, 1), F32),
                        pltpu.VMEM((T2, TQ), BF16), pltpu.VMEM((T2, TQ), BF16),
                        pltpu.VMEM((T2, TQ), F32), pltpu.VMEM((T2, TQ), F32)],
        compiler_params=_cparams(("parallel", "arbitrary")),
    )(proj, proj, proj, do, o, lse, crow)


def _sb_fwd(proj, name):
    lp = proj.shape[0]
    nb = lp // TQ
    tsuf = _tri(TQ, lambda r, c: r > c)

    def body(q_ref, k_ref, v_ref, t_ref, o_ref, lt_ref, run_ref, acc_ref, zl_ref, hl_ref, a_ref, z_ref):
        i = pl.program_id(1)
        q2 = _stack_heads(q_ref[...], 0.125)
        run_ref[...] = jnp.zeros_like(run_ref)
        acc_ref[...] = jnp.zeros_like(acc_ref)

        def lead(j, h):
            z_ref[_HALF[h]] = _dot_nt(q2[_HALF[h]], k_ref[_krows(j), :])

        def step(j, nxt, masked):
            t = t_ref[...]
            v = v_ref[_krows(j), :]
            later = []
            for h in range(2):
                for r in range(h * TQ, (h + 1) * TQ, RC):
                    rows = slice(r, r + RC)
                    z_c = z_ref[rows]
                    lk = _log_sigmoid(-z_c)
                    if masked:
                        lk = jnp.where(_chunk_valid(i, j, r, True), lk, 0.0)
                    hi, lo = _split2(lk)
                    hl_ref[rows, :TQ] = hi
                    hl_ref[rows, TQ:] = lo
                    zl_ref[rows] = z_c + lk + run_ref[rows]
                    run_ref[rows] += jnp.sum(lk, axis=-1, keepdims=True)
                if nxt is not None:
                    lead(nxt, h)
                later.append(_dot(hl_ref[_HALF[h]], t))
            for h in range(2):
                for r in range(0, TQ, RC):
                    rows = slice(h * TQ + r, h * TQ + r + RC)
                    a = jnp.exp(zl_ref[rows] + later[h][r:r + RC])
                    if masked:
                        a = jnp.where(_chunk_valid(i, j, h * TQ + r, True), a, 0.0)
                    a_ref[rows] = a.astype(BF16)
                acc_ref[_HALF[h]] += _dot(a_ref[_HALF[h]], v)

        _walk_tiles_lead(i, lead, step, reverse=True)
        run = run_ref[...]
        lt_ref[0] = run[:TQ]
        lt_ref[1] = run[TQ:]
        o_ref[...] = _unstack_heads(acc_ref[...]).astype(BF16)

    base = 3 * N_PAIRS
    return pl.pallas_call(
        body, name=name, grid=(N_PAIRS, nb),
        in_specs=_pair_specs(lp, base) + [pl.BlockSpec((2 * TQ, TQ), lambda p, i: (0, 0))],
        out_specs=[_tile_spec(), _col_spec()],
        out_shape=[jax.ShapeDtypeStruct((lp, W_ATT), BF16),
                   jax.ShapeDtypeStruct((N_PAIRS, 2, lp, 1), F32)],
        scratch_shapes=[pltpu.VMEM((T2, 1), F32), pltpu.VMEM((T2, LANES), F32), pltpu.VMEM((T2, TQ), F32),
                        pltpu.VMEM((T2, 2 * TQ), BF16), pltpu.VMEM((T2, TQ), BF16), pltpu.VMEM((T2, TQ), F32)],
        compiler_params=_cparams(("parallel", "arbitrary")),
    )(proj, proj, proj, jnp.concatenate([tsuf, tsuf], axis=0))


def _sb_bwd(proj, do, ltot, name):
    lp = proj.shape[0]
    nb = lp // TQ
    tincl = _tri(TQ, lambda r, c: r <= c)
    texcl = _tri(TQ, lambda r, c: r < c)

    def body(q_ref, k_ref, v_ref, do_ref, lt_ref, ti_ref, te_ref, dq_ref, dk_ref, dv_ref,
             dk_acc, dv_acc, dq_acc, pc_ref, gc_ref, zl_ref, keep_ref, g_ref, z_ref, da_ref,
             hl_ref, gb_ref, a_ref, dz_ref):
        i = pl.program_id(1)

        @pl.when(i == 0)
        def _():
            dk_acc[...] = jnp.zeros_like(dk_acc)
            dv_acc[...] = jnp.zeros_like(dv_acc)

        dq_acc[...] = jnp.zeros_like(dq_acc)
        gc_ref[...] = jnp.zeros_like(gc_ref)
        pc_ref[...] = _stack_cols(lt_ref)
        q2 = _stack_heads(q_ref[...], 0.125)
        do2 = _stack_heads(do_ref[...])

        def lead(j, h):
            rows_j = _krows(j)
            z_ref[_HALF[h]] = _dot_nt(q2[_HALF[h]], k_ref[rows_j, :])
            da_ref[_HALF[h]] = _dot_nt(do2[_HALF[h]], v_ref[rows_j, :])

        def step(j, nxt, masked):
            rows_j = _krows(j)
            k = k_ref[rows_j, :]
            ti = ti_ref[...]
            te = te_ref[...]
            upto, before = [], []
            for h in range(2):
                for r in range(h * TQ, (h + 1) * TQ, RC):
                    rows = slice(r, r + RC)
                    z_c = z_ref[rows]
                    lk = _log_sigmoid(-z_c)
                    if masked:
                        lk = jnp.where(_chunk_valid(i, j, r, True), lk, 0.0)
                    hi, lo = _split2(lk)
                    hl_ref[rows, :TQ] = hi
                    hl_ref[rows, TQ:] = lo
                    keep_ref[rows] = jnp.exp(lk)
                    zl_ref[rows] = z_c + lk + pc_ref[rows]
                    pc_ref[rows] -= jnp.sum(lk, axis=-1, keepdims=True)
                upto.append(_dot(hl_ref[_HALF[h]], ti))
            for h in range(2):
                for r in range(0, TQ, RC):
                    rows = slice(h * TQ + r, h * TQ + r + RC)
                    a = jnp.exp(zl_ref[rows] - upto[h][r:r + RC])
                    if masked:
                        a = jnp.where(_chunk_valid(i, j, h * TQ + r, True), a, 0.0)
                    g = a * da_ref[rows]
                    a_ref[rows] = a.astype(BF16)
                    g_ref[rows] = g
                    gb_ref[rows] = g.astype(BF16)
                if nxt is not None:
                    lead(nxt, h)
                before.append(_dot(gb_ref[_HALF[h]], te))
            for h in range(2):
                for r in range(0, TQ, RC):
                    rows = slice(h * TQ + r, h * TQ + r + RC)
                    g = g_ref[rows]
                    keep = keep_ref[rows]
                    dz = g * keep - (1.0 - keep) * (gc_ref[rows] + before[h][r:r + RC])
                    if masked:
                        dz = jnp.where(_chunk_valid(i, j, h * TQ + r, True), dz, 0.0)
                    dz_ref[rows] = dz.astype(BF16)
                    gc_ref[rows] += jnp.sum(g, axis=-1, keepdims=True)
                dzb = dz_ref[_HALF[h]]
                dq_acc[_HALF[h]] += _dot(dzb, k)
                dk_acc[rows_j, :] += _dot_tn(dzb, q2[_HALF[h]])
                dv_acc[rows_j, :] += _dot_tn(a_ref[_HALF[h]], do2[_HALF[h]])

        _walk_tiles_lead(i, lead, step, reverse=False)
        dq_ref[...] = (_unstack_heads(dq_acc[...]) * 0.125).astype(BF16)

        @pl.when(i == nb - 1)
        def _():
            dk_ref[...] = dk_acc[...].astype(BF16)
            dv_ref[...] = dv_acc[...].astype(BF16)

    base = 3 * N_PAIRS
    whole = pl.BlockSpec((lp, LANES), lambda p, i: (0, p))
    tri = lambda rows: pl.BlockSpec((rows, TQ), lambda p, i: (0, 0))
    wide = lambda dt: pltpu.VMEM((T2, TQ), dt)
    return pl.pallas_call(
        body, name=name, grid=(N_PAIRS, nb),
        in_specs=_pair_specs(lp, base) + [_tile_spec(), _col_spec(), tri(2 * TQ), tri(TQ)],
        out_specs=[_tile_spec(), whole, whole],
        out_shape=[jax.ShapeDtypeStruct((lp, W_ATT), BF16)] * 3,
        scratch_shapes=[pltpu.VMEM((lp, LANES), F32), pltpu.VMEM((lp, LANES), F32),
                        pltpu.VMEM((T2, LANES), F32), pltpu.VMEM((T2, 1), F32), pltpu.VMEM((T2, 1), F32),
                        wide(F32), wide(F32), wide(F32), wide(F32), wide(F32),
                        pltpu.VMEM((T2, 2 * TQ), BF16), wide(BF16), wide(BF16), wide(BF16)],
        compiler_params=_cparams(("parallel", "arbitrary")),
    )(proj, proj, proj, do, ltot, jnp.concatenate([tincl, tincl], axis=0), texcl)


def _local_step(x, target, meta, gains, w_in, b_forget, w_o_fox, w_o_sb, w_out, w_up, conv_w, conv_b, w_down):
    seq, dm = x.shape
    lp = PAD + N_META + seq
    nb = lp // TQ
    s = [W_ATT, W_ATT, W_ATT, 8, W_ATT, W_ATT, W_ATT, dm, dm]
    off = [sum(s[:i]) for i in range(len(s) + 1)]
    cols = lambda i: w_in[:, off[i]:off[i + 1]]
    w1 = jnp.concatenate([cols(0), cols(1), cols(2), cols(4), cols(5), cols(6), cols(7), cols(8)], axis=1)
    wf = jnp.pad(cols(3), ((0, 0), (0, LANES - 8)))
    n1 = w1.shape[1]
    ncat = n1 + 512
    w_cat = jnp.concatenate([w1, wf, jnp.zeros((dm, ncat - n1 - LANES), BF16)], axis=1)
    bf = jnp.pad(b_forget.reshape(1, 8), ((0, 0), (0, LANES - 8)))
    g = [gains[i].reshape(1, dm) for i in range(4)]
    cb = conv_b.reshape(1, -1)

    h0 = jnp.concatenate([jnp.zeros((PAD, dm), F32), meta, x], axis=0)

    proj, xn1 = _rms_mm(h0, g[0], w1, "in_proj")
    logf = _logf(xn1, wf, bf, "log_forget")
    c = _cumsum_rows(logf, "forget_cumsum")
    crow = c[:, :8].T.reshape(N_PAIRS, 2, nb, 1, TQ)
    o_a, lse = _fox_fwd(proj, crow, "fox_fwd")
    o_b, ltot = _sb_fwd(proj, "sb_fwd")
    y_a, y_b, m, mixed, h1 = _merge_fwd(o_a, o_b, proj, h0, w_o_fox, w_o_sb, w_out, g[1], "merge_fwd")
    up, xn3 = _rms_mm(h1, g[2], w_up, "up_proj")
    a = _convgate_fwd(up, conv_w, cb, "convgate_fwd")
    ffn, dy, ss = _down_loss(a, w_down, h1, g[3], target, "down_loss")

    dffn, da, dg3 = _down_bwd(dy, ffn, g[3], w_down, "down_bwd")
    d_w_down = _mm_tn(a, dffn, "dw_down")
    du, d_conv_w, d_conv_b = _convgate_bwd(up, da, conv_w, cb, "convgate_bwd")
    dup = _conv_transpose(du, conv_w, "conv_transpose")
    d_w_up = _mm_tn(xn3, dup, "dw_up")
    dh1, dg2 = _mm_rmsbwd(dup, w_up, h1, g[2], dy, "up_bwd")
    dmx, dya, dyb, dga, dgb, do_a, do_b, dg1 = _merge_bwd(
        dh1, mixed, g[1], w_out, proj, y_a, y_b, w_o_fox, w_o_sb, "merge_bwd")
    d_w_out = _mm_tn(m, dmx, "dw_out")
    d_w_o_fox = _mm_tn(o_a, dya, "dw_o_fox")
    d_w_o_sb = _mm_tn(o_b, dyb, "dw_o_sb")
    dq_a, dk_a, dv_a, dcs, dct = _fox_bwd(proj, do_a, o_a, lse, crow, "fox_bwd")
    dq_b, dk_b, dv_b = _sb_bwd(proj, do_b, ltot, "sb_bwd")
    dc = (dct.reshape(8, lp) + dcs.reshape(8, lp)).T
    df, db = _dlogf(jnp.pad(dc, ((0, 0), (0, LANES - 8))), logf, "forget_bwd")
    dcat = jnp.concatenate([dq_a, dk_a, dv_a, dq_b, dk_b, dv_b, dga, dgb, df.astype(BF16),
                            jnp.zeros((lp, ncat - n1 - LANES), BF16)], axis=1)
    d_w_cat = _mm_tn(xn1, dcat, "dw_in")
    dh0, dg0 = _mm_rmsbwd(dcat, w_cat, h0, g[0], dh1, "in_bwd")

    wc = lambda k: d_w_cat[:, k * W_ATT:(k + 1) * W_ATT]
    d_w_in = jnp.concatenate([wc(0), wc(1), wc(2), d_w_cat[:, n1:n1 + 8], wc(3), wc(4), wc(5),
                              d_w_cat[:, 6 * W_ATT:n1]], axis=1)
    d_gains = jnp.concatenate([dg0, dg1, dg2, dg3], axis=0)
    grads = (dh0[PAD:PAD + N_META], d_gains, d_w_in, db[0, :8], d_w_o_fox, d_w_o_sb, d_w_out,
             d_w_up, d_conv_w, d_conv_b[0], d_w_down)
    return ss[0, 0], dh0[PAD + N_META:], grads


def _rows(a, n_rows):
    flat = a.reshape(-1)
    return jnp.pad(flat, (0, n_rows * D_MODEL - flat.shape[0])).reshape(n_rows, D_MODEL)


_SMALL = (("meta", 4), ("gains", 1), ("conv_w", 5), ("b_forget", 1), ("conv_b", 6))


def _pack(meta, gains, w_in, b_forget, w_o_fox, w_o_sb, w_out, w_up, conv_w, conv_b, w_down):
    big = [a.reshape(-1, D_MODEL) for a in (w_in, w_o_fox, w_o_sb, w_out, w_up, w_down)]
    n_big = sum(a.shape[0] for a in big)
    small = [_rows(a, n) for a, (_, n) in zip((meta, gains, conv_w, b_forget, conv_b), _SMALL)]
    n_small = sum(n for _, n in _SMALL)
    z = lambda n: jnp.zeros((n, D_MODEL), big[0].dtype)
    return jnp.concatenate(big + [z(PK_BIG_ROWS - n_big)] + small
                           + [z(PK_ROWS - PK_BIG_ROWS - n_small)], axis=0)


def _unpack(p):
    def take(r0, shape):
        n = math.prod(shape)
        nr = -(-n // D_MODEL)
        return p[r0:r0 + nr].reshape(-1)[:n].reshape(shape), r0 + nr
    w_in, r = take(0, (1, 1024, 1282))
    w_o_fox, r = take(r, (1, 512, 256))
    w_o_sb, r = take(r, (1, 512, 256))
    w_out, r = take(r, (1, 256, 1024))
    w_up, r = take(r, (1, 1024, 1408))
    w_down, r = take(r, (1, 704, 1024))
    r = PK_BIG_ROWS
    meta, r = take(r, (16, 256))
    gains, r = take(r, (1, 4, 256))
    conv_w, r = take(r, (1, 3, 1408))
    b_forget, r = take(r, (1, 8))
    conv_b, r = take(r, (1, 5632))
    return meta, gains, w_in, b_forget, w_o_fox, w_o_sb, w_out, w_up, conv_w, conv_b, w_down


def _chip_peers():
    x, y, c = lax.axis_index("x"), lax.axis_index("y"), lax.axis_index("c")
    return [(x, 1 - y, c), (1 - x, y, c), (1 - x, 1 - y, c)]


def _all_gather_chips(arrays, name):
    n = len(arrays)

    def body(*refs):
        ins, outs = refs[:n], refs[n:2 * n]
        send_sems, recv_sems, local_sems = refs[2 * n:]
        x, y = lax.axis_index("x"), lax.axis_index("y")
        me = 2 * x + y
        peers = _chip_peers()
        copies = []
        for a in range(n):
            mine = pltpu.make_async_copy(ins[a], outs[a].at[me], local_sems.at[a])
            mine.start()
            copies.append(mine)
        remote = []
        for a in range(n):
            for j, peer in enumerate(peers):
                cp = pltpu.make_async_remote_copy(
                    src_ref=ins[a], dst_ref=outs[a].at[me],
                    send_sem=send_sems.at[3 * a + j], recv_sem=recv_sems.at[3 * a + j],
                    device_id=peer, device_id_type=MESH)
                cp.start()
                remote.append(cp)
        for cp in remote:
            cp.wait()
        for cp in copies:
            cp.wait()

    any_spec = pl.BlockSpec(memory_space=pl.ANY)
    return pl.pallas_call(
        body, name=name,
        in_specs=[any_spec] * n, out_specs=[any_spec] * n,
        out_shape=[jax.ShapeDtypeStruct((4,) + a.shape, a.dtype) for a in arrays],
        scratch_shapes=[pltpu.SemaphoreType.DMA((3 * n,)), pltpu.SemaphoreType.DMA((3 * n,)),
                        pltpu.SemaphoreType.DMA((n,))],
    )(*arrays)


def _scatter_chips(chunks, name):
    _, rows, cols = chunks.shape

    def body(in_ref, out_ref, send_sems, recv_sems):
        x, y = lax.axis_index("x"), lax.axis_index("y")
        targets = [2 * x + (1 - y), 2 * (1 - x) + y, 2 * (1 - x) + (1 - y)]
        remote = []
        for j, peer in enumerate(_chip_peers()):
            cp = pltpu.make_async_remote_copy(
                src_ref=in_ref.at[targets[j]], dst_ref=out_ref.at[j],
                send_sem=send_sems.at[j], recv_sem=recv_sems.at[j],
                device_id=peer, device_id_type=MESH)
            cp.start()
            remote.append(cp)
        for cp in remote:
            cp.wait()

    any_spec = pl.BlockSpec(memory_space=pl.ANY)
    return pl.pallas_call(
        body, name=name, in_specs=[any_spec], out_specs=any_spec,
        out_shape=jax.ShapeDtypeStruct((3, rows, cols), chunks.dtype),
        scratch_shapes=[pltpu.SemaphoreType.DMA((3,)), pltpu.SemaphoreType.DMA((3,))],
    )(chunks)


def _swap_half_rows(chunks, name):
    n, rows, cols = chunks.shape
    r2 = rows // 2

    def body(in_ref, out_ref, send_sem, recv_sem):
        x, y, c = lax.axis_index("x"), lax.axis_index("y"), lax.axis_index("c")
        cp = pltpu.make_async_remote_copy(
            src_ref=in_ref.at[:, pl.ds((1 - c) * r2, r2), :], dst_ref=out_ref,
            send_sem=send_sem, recv_sem=recv_sem, device_id=(x, y, 1 - c), device_id_type=MESH)
        cp.start()
        cp.wait()

    any_spec = pl.BlockSpec(memory_space=pl.ANY)
    return pl.pallas_call(
        body, name=name, in_specs=[any_spec], out_specs=any_spec,
        out_shape=jax.ShapeDtypeStruct((n, r2, cols), chunks.dtype),
        scratch_shapes=[pltpu.SemaphoreType.DMA, pltpu.SemaphoreType.DMA],
    )(chunks)


def _gather_halves(half, name):
    lead, (r2, cols) = half.shape[:-2], half.shape[-2:]

    def body(in_ref, out_ref, send_sem, recv_sem, local_sem):
        x, y, c = lax.axis_index("x"), lax.axis_index("y"), lax.axis_index("c")
        place = out_ref.at[(slice(None),) * len(lead) + (pl.ds(c * r2, r2), slice(None))]
        local = pltpu.make_async_copy(in_ref, place, local_sem)
        local.start()
        cp = pltpu.make_async_remote_copy(
            src_ref=in_ref, dst_ref=place, send_sem=send_sem, recv_sem=recv_sem,
            device_id=(x, y, 1 - c), device_id_type=MESH)
        cp.start()
        cp.wait()
        local.wait()

    any_spec = pl.BlockSpec(memory_space=pl.ANY)
    return pl.pallas_call(
        body, name=name, in_specs=[any_spec], out_specs=any_spec,
        out_shape=jax.ShapeDtypeStruct(lead + (2 * r2, cols), half.dtype),
        scratch_shapes=[pltpu.SemaphoreType.DMA, pltpu.SemaphoreType.DMA, pltpu.SemaphoreType.DMA],
    )(half)


def _add(a, b, name):
    n, rows, cols = a.shape

    def body(a_ref, b_ref, o_ref):
        o_ref[...] = a_ref[...] + b_ref[...]

    spec = pl.BlockSpec((None, PK_TILE, cols), lambda k, i: (k, i, 0))
    return pl.pallas_call(
        body, name=name, grid=(n, rows // PK_TILE),
        in_specs=[spec, spec], out_specs=spec,
        out_shape=jax.ShapeDtypeStruct(a.shape, F32),
        compiler_params=_cparams(("parallel", "parallel")),
    )(a, b)


def _chip_sum(chunks, recv, name):
    _, rows, cols = chunks.shape

    def body(own_ref, r_ref, o_ref):
        o_ref[...] = (own_ref[...] + r_ref[0]) + (r_ref[1] + r_ref[2])

    me = 2 * lax.axis_index("x") + lax.axis_index("y")
    own = lax.dynamic_index_in_dim(chunks, me, axis=0, keepdims=False)
    return pl.pallas_call(
        body, name=name, grid=(rows // PK_TILE,),
        in_specs=[pl.BlockSpec((PK_TILE, cols), lambda i: (i, 0)),
                  pl.BlockSpec((3, PK_TILE, cols), lambda i: (0, i, 0))],
        out_specs=pl.BlockSpec((PK_TILE, cols), lambda i: (i, 0)),
        out_shape=jax.ShapeDtypeStruct((rows, cols), F32),
        compiler_params=_cparams(("parallel",)),
    )(own, recv)


def _adamw(w, m, v, g, name):
    rows, cols = w.shape
    c1 = 1.0 - ADAM_B1 ** ADAM_STEP
    c2 = 1.0 - ADAM_B2 ** ADAM_STEP

    def body(w_ref, m_ref, v_ref, g_ref, d_ref, nm_ref, nv_ref):
        g = g_ref[...]
        nm = ADAM_B1 * m_ref[...] + (1.0 - ADAM_B1) * g
        nv = ADAM_B2 * v_ref[...] + (1.0 - ADAM_B2) * (g * g)
        nm_ref[...] = nm
        nv_ref[...] = nv
        d_ref[...] = -ADAM_LR * ((nm / c1) / (jnp.sqrt(nv / c2) + ADAM_EPS) + ADAM_WD * w_ref[...])

    spec = pl.BlockSpec((PK_TILE, cols), lambda i: (i, 0))
    return pl.pallas_call(
        body, name=name, grid=(rows // PK_TILE,),
        in_specs=[spec] * 4, out_specs=[spec] * 3,
        out_shape=[jax.ShapeDtypeStruct((rows, cols), F32)] * 3,
        compiler_params=_cparams(("parallel",)),
    )(w, m, v, g)


def _full_weights(big, small):
    def gather(src, r0, shape, axis):
        n = math.prod(shape)
        nr = -(-n // D_MODEL)
        parts = [src[k, r0:r0 + nr].reshape(-1)[:n].reshape(shape) for k in range(4)]
        return jnp.concatenate(parts, axis=axis), r0 + nr
    w_in, r = gather(big, 0, (1024, 1282), 1)
    w_o_fox, r = gather(big, r, (512, 256), 1)
    w_o_sb, r = gather(big, r, (512, 256), 1)
    w_out, r = gather(big, r, (256, 1024), 0)
    w_up, r = gather(big, r, (1024, 1408), 1)
    w_down, r = gather(big, r, (704, 1024), 0)
    meta, r = gather(small, 0, (16, 256), 1)
    gains, r = gather(small, r, (4, 256), 1)
    conv_w, r = gather(small, r, (3, 1408), 1)
    return meta, gains, w_in, w_o_fox, w_o_sb, w_out, w_up, conv_w, w_down


def _chunks_for_chips(grads):
    d_meta, d_gains, d_w_in, d_b, d_w_o_fox, d_w_o_sb, d_w_out, d_w_up, d_conv_w, d_conv_b, d_w_down = grads
    out = []
    for k in range(4):
        col = lambda a, w: a[:, k * w:(k + 1) * w]
        row = lambda a, w: a[k * w:(k + 1) * w]
        out.append(_pack(col(d_meta, 256), col(d_gains, 256), col(d_w_in, 1282), d_b, col(d_w_o_fox, 256),
                         col(d_w_o_sb, 256), row(d_w_out, 256), col(d_w_up, 1408), col(d_conv_w, 1408),
                         d_conv_b, row(d_w_down, 704)))
    return jnp.stack(out, axis=0)


def kernel(x, meta_tokens, norm_gains, w_in, b_forget, w_o_fox, w_o_sb, w_out, w_up, conv_w, conv_b, w_down, loss_target, m_meta_tokens, m_norm_gains, m_w_in, m_b_forget, m_w_o_fox, m_w_o_sb, m_w_out, m_w_up, m_conv_w, m_conv_b, m_w_down, v_meta_tokens, v_norm_gains, v_w_in, v_b_forget, v_w_o_fox, v_w_o_sb, v_w_out, v_w_up, v_conv_w, v_conv_b, v_w_down):
    shard = lambda mt, ng, wi, bf, wof, wos, wo, wu, cw, cb, wd: _pack(
        mt, ng[0], wi[0], bf[0], wof[0], wos[0], wo[0], wu[0], cw[0], cb[0], wd[0])
    wp = shard(meta_tokens, norm_gains, w_in, b_forget, w_o_fox, w_o_sb, w_out, w_up, conv_w, conv_b, w_down)
    mp = shard(m_meta_tokens, m_norm_gains, m_w_in, m_b_forget, m_w_o_fox, m_w_o_sb, m_w_out, m_w_up,
               m_conv_w, m_conv_b, m_w_down)
    vp = shard(v_meta_tokens, v_norm_gains, v_w_in, v_b_forget, v_w_o_fox, v_w_o_sb, v_w_out, v_w_up,
               v_conv_w, v_conv_b, v_w_down)

    hb = PK_BIG_ROWS // 2
    big_half = lax.dynamic_slice_in_dim(wp[:PK_BIG_ROWS].astype(BF16), lax.axis_index("c") * hb, hb, axis=0)
    big_half, small = _all_gather_chips(
        [big_half, wp[PK_BIG_ROWS:PK_BIG_ROWS + PK_SMALL_ROWS]], "gather_weights")
    big = _gather_halves(big_half, "gather_weight_halves")
    meta, gains, f_w_in, f_w_o_fox, f_w_o_sb, f_w_out, f_w_up, f_conv_w, f_w_down = _full_weights(big, small)

    ss, dx, grads = _local_step(x[0], loss_target[0], meta, gains, f_w_in, b_forget[0], f_w_o_fox, f_w_o_sb,
                                f_w_out, f_w_up, f_conv_w, conv_b[0], f_w_down)
    loss = lax.psum(0.5 * ss / D_MODEL, ("x", "y", "c"))

    chunks = _chunks_for_chips(grads)
    r2 = PK_ROWS // 2
    from_sibling = _swap_half_rows(chunks, "swap_halves")
    own = lax.dynamic_slice_in_dim(chunks, lax.axis_index("c") * r2, r2, axis=1)
    core_sum = _add(own, from_sibling, "core_sum")
    recv = _scatter_chips(core_sum, "scatter_grads")
    g = _gather_halves(_chip_sum(core_sum, recv, "chip_sum"), "gather_halves")
    delta, new_m, new_v = _adamw(wp, mp, vp, g, "adamw")
    return (loss, dx[None], *_unpack(g), *_unpack(delta), *_unpack(new_m), *_unpack(new_v))
```

```python
import functools
import math

import jax
import jax.numpy as jnp
from jax import lax
from jax.experimental import pallas as pl
from jax.experimental.pallas import tpu as pltpu

F32 = jnp.float32
BF16 = jnp.bfloat16

D_MODEL = 1024
N_META = 16
HEAD_DIM = 64
N_PAIRS = 4
W_ATT = 512
D_FF = 2816
EPS = 1e-6
NEG = -1e30
TQ = 256
PAD = TQ - N_META
TCONV = 128
HALO = 16
LANES = 128
VMEM_LIMIT = 56 * 1024 * 1024

ADAM_LR = 0.001
ADAM_B1 = 0.9
ADAM_B2 = 0.999
ADAM_EPS = 1e-08
ADAM_WD = 0.01
ADAM_STEP = 10

MESH = pl.DeviceIdType.MESH

PK_BIG_ROWS = 3936
PK_SMALL_ROWS = 24
PK_ROWS = 4096
PK_TILE = 128


def _cparams(sem, **kw):
    return pltpu.CompilerParams(dimension_semantics=sem, vmem_limit_bytes=VMEM_LIMIT, **kw)


def _row_tile(lp):
    return 768 if lp % 768 == 0 else 256


def _wide_tile(n):
    return next(t for t in (1408, 1280, 1024, 512, 256) if n % t == 0)


def _rms(x):
    return lax.rsqrt(jnp.mean(x * x, axis=-1, keepdims=True) + EPS)


def _log_sigmoid(x):
    return jnp.minimum(x, 0.0) - jnp.log(1.0 + jnp.exp(-jnp.abs(x)))


_LOG2E = 1.4426950408889634
_LN2 = 0.6931471805599453


def _log_keep(z):
    t = jnp.exp2(jnp.abs(z) * (-_LOG2E))
    return jnp.log2(1.0 + t) * (-_LN2) - jnp.maximum(z, 0.0)


def _split2(x):
    hi = x.astype(BF16)
    lo = (x - hi.astype(F32)).astype(BF16)
    return hi, lo


def _dot(a, b):
    return jnp.dot(a, b, preferred_element_type=F32)


def _dot_nt(a, b):
    return lax.dot_general(a, b, (((1,), (1,)), ((), ())), preferred_element_type=F32)


def _dot_tn(a, b):
    return lax.dot_general(a, b, (((0,), (0,)), ((), ())), preferred_element_type=F32)


def _rms_mm(h, g, w, name):
    lp, dm = h.shape
    n = w.shape[1]
    tr, tn = _row_tile(lp), _wide_tile(n)

    def body(h_ref, g_ref, w_ref, out_ref, xn_ref):
        @pl.when(pl.program_id(1) == 0)
        def _():
            x = h_ref[...]
            xn_ref[...] = (x * _rms(x) * g_ref[...]).astype(BF16)
        out_ref[...] = _dot(xn_ref[...], w_ref[...]).astype(BF16)

    return pl.pallas_call(
        body, name=name, grid=(lp // tr, n // tn),
        in_specs=[pl.BlockSpec((tr, dm), lambda i, j: (i, 0)),
                  pl.BlockSpec((1, dm), lambda i, j: (0, 0)),
                  pl.BlockSpec((dm, tn), lambda i, j: (0, j))],
        out_specs=[pl.BlockSpec((tr, tn), lambda i, j: (i, j)),
                   pl.BlockSpec((tr, dm), lambda i, j: (i, 0))],
        out_shape=[jax.ShapeDtypeStruct((lp, n), BF16), jax.ShapeDtypeStruct((lp, dm), BF16)],
        compiler_params=_cparams(("parallel", "arbitrary")),
    )(h, g, w)


def _mm_rmsbwd(dy, w, h, g, dh_in, name):
    lp, kd = dy.shape
    dm = w.shape[0]
    tr, tk = 384, _wide_tile(kd)
    nk = kd // tk

    def body(dy_ref, w_ref, h_ref, g_ref, dhin_ref, dh_ref, dg_ref, acc_ref):
        i, k = pl.program_id(0), pl.program_id(1)

        @pl.when(k == 0)
        def _():
            acc_ref[...] = jnp.zeros_like(acc_ref)

        acc_ref[...] += _dot_nt(dy_ref[...], w_ref[...])

        @pl.when(k == nk - 1)
        def _():
            dxn = acc_ref[...]
            x = h_ref[...]
            r = _rms(x)
            yhat = x * r
            part = jnp.sum(dxn * yhat, axis=0, keepdims=True)
            dyh = dxn * g_ref[...]
            dx = r * (dyh - yhat * jnp.mean(dyh * yhat, axis=-1, keepdims=True))
            dh_ref[...] = dhin_ref[...] + dx

            @pl.when(i == 0)
            def _():
                dg_ref[...] = part

            @pl.when(i > 0)
            def _():
                dg_ref[...] += part

    return pl.pallas_call(
        body, name=name, grid=(lp // tr, nk),
        in_specs=[pl.BlockSpec((tr, tk), lambda i, k: (i, k)),
                  pl.BlockSpec((dm, tk), lambda i, k: (0, k)),
                  pl.BlockSpec((tr, dm), lambda i, k: (i, 0)),
                  pl.BlockSpec((1, dm), lambda i, k: (0, 0)),
                  pl.BlockSpec((tr, dm), lambda i, k: (i, 0))],
        out_specs=[pl.BlockSpec((tr, dm), lambda i, k: (i, 0)),
                   pl.BlockSpec((1, dm), lambda i, k: (0, 0))],
        out_shape=[jax.ShapeDtypeStruct((lp, dm), F32), jax.ShapeDtypeStruct((1, dm), F32)],
        scratch_shapes=[pltpu.VMEM((tr, dm), F32)],
        compiler_params=_cparams(("arbitrary", "arbitrary")),
    )(dy, w, h, g, dh_in)


def _mm_tn(x, dy, name):
    lp, kd = x.shape
    n = dy.shape[1]
    tl = _row_tile(lp)
    tk = _wide_tile(kd)
    tn = _wide_tile(n)
    nl = lp // tl

    def body(x_ref, dy_ref, o_ref):
        @pl.when(pl.program_id(2) == 0)
        def _():
            o_ref[...] = jnp.zeros_like(o_ref)
        o_ref[...] += _dot_tn(x_ref[...], dy_ref[...])

    return pl.pallas_call(
        body, name=name, grid=(kd // tk, n // tn, nl),
        in_specs=[pl.BlockSpec((tl, tk), lambda a, b, l: (l, a)),
                  pl.BlockSpec((tl, tn), lambda a, b, l: (l, b))],
        out_specs=pl.BlockSpec((tk, tn), lambda a, b, l: (a, b)),
        out_shape=jax.ShapeDtypeStruct((kd, n), F32),
        compiler_params=_cparams(("parallel", "parallel", "arbitrary")),
    )(x, dy)


def _logf(xn, wf, bf, name):
    lp, dm = xn.shape
    tr = _row_tile(lp)

    def body(xn_ref, wf_ref, b_ref, o_ref):
        f = _dot(xn_ref[...], wf_ref[...]) + b_ref[...]
        row = pl.program_id(0) * tr + lax.broadcasted_iota(jnp.int32, f.shape, 0)
        lane = lax.broadcasted_iota(jnp.int32, f.shape, 1)
        o_ref[...] = jnp.where((row >= PAD) & (lane < 8), _log_sigmoid(f), 0.0)

    return pl.pallas_call(
        body, name=name, grid=(lp // tr,),
        in_specs=[pl.BlockSpec((tr, dm), lambda i: (i, 0)),
                  pl.BlockSpec((dm, LANES), lambda i: (0, 0)),
                  pl.BlockSpec((1, LANES), lambda i: (0, 0))],
        out_specs=pl.BlockSpec((tr, LANES), lambda i: (i, 0)),
        out_shape=jax.ShapeDtypeStruct((lp, LANES), F32),
        compiler_params=_cparams(("parallel",)),
    )(xn, wf, bf)


def _tri(n, rel):
    r = lax.broadcasted_iota(jnp.int32, (n, n), 0)
    c = lax.broadcasted_iota(jnp.int32, (n, n), 1)
    return rel(r, c).astype(BF16)


def _cumsum_rows(x, name):
    lp = x.shape[0]
    nb = lp // TQ
    tl = _tri(TQ, lambda r, c: c <= r)

    def body(x_ref, t_ref, o_ref):
        def step(b, carry):
            rows = pl.ds(pl.multiple_of(b * TQ, TQ), TQ)
            xb = x_ref[rows, :]
            hi = xb.astype(BF16)
            r1 = xb - hi.astype(F32)
            mid = r1.astype(BF16)
            lo = (r1 - mid.astype(F32)).astype(BF16)
            t = t_ref[...]
            o_ref[rows, :] = carry + (_dot(t, hi) + _dot(t, mid) + _dot(t, lo))
            return carry + jnp.sum(xb, axis=0, keepdims=True)
        lax.fori_loop(0, nb, step, jnp.zeros((1, LANES), F32))

    return pl.pallas_call(
        body, name=name,
        in_specs=[pl.BlockSpec(memory_space=pltpu.VMEM)] * 2,
        out_specs=pl.BlockSpec(memory_space=pltpu.VMEM),
        out_shape=jax.ShapeDtypeStruct((lp, LANES), F32),
        compiler_params=pltpu.CompilerParams(vmem_limit_bytes=VMEM_LIMIT),
    )(x, tl)


def _dlogf(dc, logf, name):
    lp = dc.shape[0]
    nb = lp // TQ
    tu = _tri(TQ, lambda r, c: c >= r)

    def body(x_ref, lf_ref, t_ref, df_ref, db_ref):
        def step(bb, carry):
            run, db = carry
            b = nb - 1 - bb
            rows = pl.ds(pl.multiple_of(b * TQ, TQ), TQ)
            xb = x_ref[rows, :]
            hi = xb.astype(BF16)
            r1 = xb - hi.astype(F32)
            mid = r1.astype(BF16)
            lo = (r1 - mid.astype(F32)).astype(BF16)
            t = t_ref[...]
            dlf = run + (_dot(t, hi) + _dot(t, mid) + _dot(t, lo))
            df = dlf * (1.0 - jnp.exp(lf_ref[rows, :]))
            df_ref[rows, :] = df
            return run + jnp.sum(xb, axis=0, keepdims=True), db + jnp.sum(df, axis=0, keepdims=True)
        z = jnp.zeros((1, LANES), F32)
        _, db = lax.fori_loop(0, nb, step, (z, z))
        db_ref[...] = db

    return pl.pallas_call(
        body, name=name,
        in_specs=[pl.BlockSpec(memory_space=pltpu.VMEM)] * 3,
        out_specs=[pl.BlockSpec(memory_space=pltpu.VMEM)] * 2,
        out_shape=[jax.ShapeDtypeStruct((lp, LANES), F32), jax.ShapeDtypeStruct((1, LANES), F32)],
        compiler_params=pltpu.CompilerParams(vmem_limit_bytes=VMEM_LIMIT),
    )(dc, logf, tu)


def _merge_fwd(o_a, o_b, proj, h0, w_oa, w_ob, w_out, g1, name):
    lp, dm = h0.shape
    tr = TQ
    ga_blk = (6 * W_ATT) // dm

    def body(oa_ref, ob_ref, ga_ref, gb_ref, h0_ref, woa_ref, wob_ref, wout_ref, g1_ref,
             ya_ref, yb_ref, m_ref, mixed_ref, h1_ref):
        ya = _dot(oa_ref[...], woa_ref[...])
        yb = _dot(ob_ref[...], wob_ref[...])
        m = jax.nn.sigmoid(ga_ref[...].astype(F32)) * ya + jax.nn.sigmoid(gb_ref[...].astype(F32)) * yb
        mb = m.astype(BF16)
        mixed = _dot(mb, wout_ref[...])
        ya_ref[...] = ya.astype(BF16)
        yb_ref[...] = yb.astype(BF16)
        m_ref[...] = mb
        mixed_ref[...] = mixed
        h1_ref[...] = h0_ref[...] + mixed * _rms(mixed) * g1_ref[...]

    row = lambda w: pl.BlockSpec((tr, w), lambda i: (i, 0))
    full = lambda a: pl.BlockSpec(a.shape, lambda i: (0, 0))
    return pl.pallas_call(
        body, name=name, grid=(lp // tr,),
        in_specs=[row(W_ATT), row(W_ATT),
                  pl.BlockSpec((tr, dm), lambda i: (i, ga_blk)),
                  pl.BlockSpec((tr, dm), lambda i: (i, ga_blk + 1)),
                  row(dm), full(w_oa), full(w_ob), full(w_out), full(g1)],
        out_specs=[row(dm)] * 5,
        out_shape=[jax.ShapeDtypeStruct((lp, dm), BF16)] * 3 + [jax.ShapeDtypeStruct((lp, dm), F32)] * 2,
        compiler_params=_cparams(("parallel",)),
    )(o_a, o_b, proj, proj, h0, w_oa, w_ob, w_out, g1)


def _merge_bwd(dh1, mixed, g1, w_out, proj, y_a, y_b, w_oa, w_ob, name):
    lp, dm = dh1.shape
    tr = TQ
    ga_blk = (6 * W_ATT) // dm

    def body(dh_ref, mx_ref, g1_ref, wout_ref, ga_ref, gb_ref, ya_ref, yb_ref, woa_ref, wob_ref,
             dmx_ref, dya_ref, dyb_ref, dga_ref, dgb_ref, doa_ref, dob_ref, dg1_ref):
        i = pl.program_id(0)
        dn = dh_ref[...]
        x = mx_ref[...]
        r = _rms(x)
        yhat = x * r
        part = jnp.sum(dn * yhat, axis=0, keepdims=True)
        dyh = dn * g1_ref[...]
        dmx = (r * (dyh - yhat * jnp.mean(dyh * yhat, axis=-1, keepdims=True))).astype(BF16)
        dmx_ref[...] = dmx
        dm_ = _dot_nt(dmx, wout_ref[...])
        sa = jax.nn.sigmoid(ga_ref[...].astype(F32))
        sb = jax.nn.sigmoid(gb_ref[...].astype(F32))
        dya = (dm_ * sa).astype(BF16)
        dyb = (dm_ * sb).astype(BF16)
        dya_ref[...] = dya
        dyb_ref[...] = dyb
        dga_ref[...] = (dm_ * ya_ref[...].astype(F32) * sa * (1.0 - sa)).astype(BF16)
        dgb_ref[...] = (dm_ * yb_ref[...].astype(F32) * sb * (1.0 - sb)).astype(BF16)
        doa_ref[...] = _dot_nt(dya, woa_ref[...]).astype(BF16)
        dob_ref[...] = _dot_nt(dyb, wob_ref[...]).astype(BF16)

        @pl.when(i == 0)
        def _():
            dg1_ref[...] = part

        @pl.when(i > 0)
        def _():
            dg1_ref[...] += part

    row = lambda w: pl.BlockSpec((tr, w), lambda i: (i, 0))
    full = lambda a: pl.BlockSpec(a.shape, lambda i: (0, 0))
    return pl.pallas_call(
        body, name=name, grid=(lp // tr,),
        in_specs=[row(dm), row(dm), full(g1), full(w_out),
                  pl.BlockSpec((tr, dm), lambda i: (i, ga_blk)),
                  pl.BlockSpec((tr, dm), lambda i: (i, ga_blk + 1)),
                  row(dm), row(dm), full(w_oa), full(w_ob)],
        out_specs=[row(dm)] * 5 + [row(W_ATT)] * 2 + [pl.BlockSpec((1, dm), lambda i: (0, 0))],
        out_shape=[jax.ShapeDtypeStruct((lp, dm), BF16)] * 5 + [jax.ShapeDtypeStruct((lp, W_ATT), BF16)] * 2
        + [jax.ShapeDtypeStruct((1, dm), F32)],
        compiler_params=_cparams(("arbitrary",)),
    )(dh1, mixed, g1, w_out, proj, proj, y_a, y_b, w_oa, w_ob)


_GELU_C = math.sqrt(2.0 / math.pi)
_GELU_A = 0.044715


def _gelu(x):
    t = jnp.tanh(_GELU_C * (x + _GELU_A * x * x * x))
    return 0.5 * x * (1.0 + t), t


CW = 256


def _taps(cur_ref, prev_ref, first, c0):
    cur = cur_ref[:, c0:c0 + CW].astype(F32)
    p1 = jnp.where(first, 0.0, prev_ref[HALO - 1:HALO, c0:c0 + CW].astype(F32))
    p2 = jnp.where(first, 0.0, prev_ref[HALO - 2:HALO - 1, c0:c0 + CW].astype(F32))
    row = lax.broadcasted_iota(jnp.int32, cur.shape, 0)
    x1 = jnp.where(row == 0, p1, pltpu.roll(cur, 1, 0))
    x2 = jnp.where(row == 0, p2, jnp.where(row == 1, p1, pltpu.roll(cur, 2, 0)))
    return cur, x1, x2


def _conv_at(cur_ref, prev_ref, w_ref, b_ref, first, c0):
    cur, x1, x2 = _taps(cur_ref, prev_ref, first, c0)
    cols = slice(c0, c0 + CW)
    u = b_ref[:, cols] + w_ref[0:1, cols] * x2 + w_ref[1:2, cols] * x1 + w_ref[2:3, cols] * cur
    return u, (x2, x1, cur)


def _up_specs(tr, width):
    per = tr // HALO
    return [pl.BlockSpec((tr, width), lambda i: (i, 0)),
            pl.BlockSpec((HALO, width), lambda i: (jnp.maximum(i * per - 1, 0), 0))]


def _convgate_fwd(up, conv_w, conv_b, name):
    lp, c2 = up.shape
    tr = TCONV

    def body(cur_ref, prev_ref, w_ref, b_ref, a_ref):
        first = pl.program_id(0) == 0
        for c0 in range(0, D_FF, CW):
            ug, _ = _conv_at(cur_ref, prev_ref, w_ref, b_ref, first, c0)
            uv, _ = _conv_at(cur_ref, prev_ref, w_ref, b_ref, first, D_FF + c0)
            gel, _ = _gelu(ug)
            a_ref[:, c0:c0 + CW] = (gel * uv).astype(BF16)

    return pl.pallas_call(
        body, name=name, grid=(lp // tr,),
        in_specs=_up_specs(tr, c2) + [pl.BlockSpec((3, c2), lambda i: (0, 0)),
                                      pl.BlockSpec((1, c2), lambda i: (0, 0))],
        out_specs=pl.BlockSpec((tr, D_FF), lambda i: (i, 0)),
        out_shape=jax.ShapeDtypeStruct((lp, D_FF), BF16),
        compiler_params=_cparams(("parallel",)),
    )(up, up, conv_w, conv_b)


def _convgate_bwd(up, da, conv_w, conv_b, name):
    lp, c2 = up.shape
    tr = TCONV

    def body(cur_ref, prev_ref, da_ref, w_ref, b_ref, du_ref, dw_ref, db_ref):
        i = pl.program_id(0)
        first = i == 0

        @pl.when(first)
        def _():
            dw_ref[...] = jnp.zeros_like(dw_ref)
            db_ref[...] = jnp.zeros_like(db_ref)

        for c0 in range(0, D_FF, CW):
            ug, taps_g = _conv_at(cur_ref, prev_ref, w_ref, b_ref, first, c0)
            uv, taps_v = _conv_at(cur_ref, prev_ref, w_ref, b_ref, first, D_FF + c0)
            gel, t = _gelu(ug)
            dgel = 0.5 * (1.0 + t) + 0.5 * ug * (1.0 - t * t) * _GELU_C * (1.0 + 3.0 * _GELU_A * ug * ug)
            da_ = da_ref[:, c0:c0 + CW].astype(F32)
            for base, du, taps in ((c0, da_ * uv * dgel, taps_g), (D_FF + c0, da_ * gel, taps_v)):
                cols = slice(base, base + CW)
                du_ref[:, cols] = du.astype(BF16)
                for tap in range(3):
                    dw_ref[tap:tap + 1, cols] += jnp.sum(du * taps[tap], axis=0, keepdims=True)
                db_ref[:, cols] += jnp.sum(du, axis=0, keepdims=True)

    return pl.pallas_call(
        body, name=name, grid=(lp // tr,),
        in_specs=_up_specs(tr, c2) + [pl.BlockSpec((tr, D_FF), lambda i: (i, 0)),
                                      pl.BlockSpec((3, c2), lambda i: (0, 0)),
                                      pl.BlockSpec((1, c2), lambda i: (0, 0))],
        out_specs=[pl.BlockSpec((tr, c2), lambda i: (i, 0)),
                   pl.BlockSpec((3, c2), lambda i: (0, 0)),
                   pl.BlockSpec((1, c2), lambda i: (0, 0))],
        out_shape=[jax.ShapeDtypeStruct((lp, c2), BF16), jax.ShapeDtypeStruct((3, c2), F32),
                   jax.ShapeDtypeStruct((1, c2), F32)],
        compiler_params=_cparams(("arbitrary",)),
    )(up, up, da, conv_w, conv_b)


def _conv_transpose(du, conv_w, name):
    lp, c2 = du.shape
    tr = TCONV
    per = tr // HALO
    n_halo = lp // HALO
    nt = lp // tr

    def body(cur_ref, nxt_ref, w_ref, o_ref):
        last = pl.program_id(0) == nt - 1
        for c0 in range(0, c2, CW):
            cols = slice(c0, c0 + CW)
            cur = cur_ref[:, cols].astype(F32)
            n0 = jnp.where(last, 0.0, nxt_ref[0:1, cols].astype(F32))
            n1 = jnp.where(last, 0.0, nxt_ref[1:2, cols].astype(F32))
            row = lax.broadcasted_iota(jnp.int32, cur.shape, 0)
            y1 = jnp.where(row == tr - 1, n0, pltpu.roll(cur, tr - 1, 0))
            y2 = jnp.where(row == tr - 1, n1, jnp.where(row == tr - 2, n0, pltpu.roll(cur, tr - 2, 0)))
            o_ref[:, cols] = (w_ref[2:3, cols] * cur + w_ref[1:2, cols] * y1 + w_ref[0:1, cols] * y2).astype(BF16)

    return pl.pallas_call(
        body, name=name, grid=(nt,),
        in_specs=[pl.BlockSpec((tr, c2), lambda i: (i, 0)),
                  pl.BlockSpec((HALO, c2), lambda i: (jnp.minimum((i + 1) * per, n_halo - 1), 0)),
                  pl.BlockSpec((3, c2), lambda i: (0, 0))],
        out_specs=pl.BlockSpec((tr, c2), lambda i: (i, 0)),
        out_shape=jax.ShapeDtypeStruct((lp, c2), BF16),
        compiler_params=_cparams(("parallel",)),
    )(du, du, conv_w)


def _down_loss(a, w_down, h1, g3, target, name):
    lp, dm = h1.shape
    tr = TQ

    def body(a_ref, w_ref, h1_ref, g_ref, t_ref, ffn_ref, dy_ref, ss_ref):
        i = pl.program_id(0)
        ffn = _dot(a_ref[...], w_ref[...])
        ffn_ref[...] = ffn
        h2 = h1_ref[...] + ffn * _rms(ffn) * g_ref[...]
        d = jnp.where(i > 0, h2 - t_ref[...], 0.0)
        dy_ref[...] = d * (1.0 / dm)
        part = jnp.sum(jnp.sum(d * d, axis=0, keepdims=True), axis=1, keepdims=True)

        @pl.when(i == 0)
        def _():
            ss_ref[...] = jnp.zeros_like(ss_ref)

        ss_ref[...] += part

    return pl.pallas_call(
        body, name=name, grid=(lp // tr,),
        in_specs=[pl.BlockSpec((tr, D_FF), lambda i: (i, 0)),
                  pl.BlockSpec(w_down.shape, lambda i: (0, 0)),
                  pl.BlockSpec((tr, dm), lambda i: (i, 0)),
                  pl.BlockSpec((1, dm), lambda i: (0, 0)),
                  pl.BlockSpec((tr, dm), lambda i: (jnp.maximum(i - 1, 0), 0))],
        out_specs=[pl.BlockSpec((tr, dm), lambda i: (i, 0)),
                   pl.BlockSpec((tr, dm), lambda i: (i, 0)),
                   pl.BlockSpec((8, LANES), lambda i: (0, 0))],
        out_shape=[jax.ShapeDtypeStruct((lp, dm), F32), jax.ShapeDtypeStruct((lp, dm), F32),
                   jax.ShapeDtypeStruct((8, LANES), F32)],
        compiler_params=_cparams(("arbitrary",)),
    )(a, w_down, h1, g3, target)


def _down_bwd(dy, ffn, g3, w_down, name):
    lp, dm = dy.shape
    tr = TQ

    def body(dy_ref, f_ref, g_ref, w_ref, dffn_ref, da_ref, dg_ref):
        i = pl.program_id(0)
        dn = dy_ref[...]
        x = f_ref[...]
        r = _rms(x)
        yhat = x * r
        part = jnp.sum(dn * yhat, axis=0, keepdims=True)
        dyh = dn * g_ref[...]
        dffn = (r * (dyh - yhat * jnp.mean(dyh * yhat, axis=-1, keepdims=True))).astype(BF16)
        dffn_ref[...] = dffn
        da_ref[...] = _dot_nt(dffn, w_ref[...]).astype(BF16)

        @pl.when(i == 0)
        def _():
            dg_ref[...] = part

        @pl.when(i > 0)
        def _():
            dg_ref[...] += part

    return pl.pallas_call(
        body, name=name, grid=(lp // tr,),
        in_specs=[pl.BlockSpec((tr, dm), lambda i: (i, 0)),
                  pl.BlockSpec((tr, dm), lambda i: (i, 0)),
                  pl.BlockSpec((1, dm), lambda i: (0, 0)),
                  pl.BlockSpec(w_down.shape, lambda i: (0, 0))],
        out_specs=[pl.BlockSpec((tr, dm), lambda i: (i, 0)),
                   pl.BlockSpec((tr, D_FF), lambda i: (i, 0)),
                   pl.BlockSpec((1, dm), lambda i: (0, 0))],
        out_shape=[jax.ShapeDtypeStruct((lp, dm), BF16), jax.ShapeDtypeStruct((lp, D_FF), BF16),
                   jax.ShapeDtypeStruct((1, dm), F32)],
        compiler_params=_cparams(("arbitrary",)),
    )(dy, ffn, g3, w_down)


def _pair_specs(lp, base):
    return [pl.BlockSpec((TQ, LANES), lambda p, i: (i, base + p)),
            pl.BlockSpec((lp, LANES), lambda p, i: (0, base + N_PAIRS + p)),
            pl.BlockSpec((lp, LANES), lambda p, i: (0, base + 2 * N_PAIRS + p))]


def _col_spec():
    return pl.BlockSpec((None, 2, TQ, 1), lambda p, i: (p, 0, i, 0))


def _rowvec_spec(nb):
    return pl.BlockSpec((None, 2, nb, 1, TQ), lambda p, i: (p, 0, 0, 0, 0))


def _tile_spec():
    return pl.BlockSpec((TQ, LANES), lambda p, i: (i, p))


RC = 64
T2 = 2 * TQ


def _stack_heads(x, scale=None):
    lane = lax.broadcasted_iota(jnp.int32, x.shape, 1)
    zero = jnp.zeros_like(x)
    x2 = jnp.concatenate([jnp.where(lane < HEAD_DIM, x, zero), jnp.where(lane >= HEAD_DIM, x, zero)], axis=0)
    return x2 if scale is None else x2 * scale


def _unstack_heads(x2):
    lane = lax.broadcasted_iota(jnp.int32, (TQ, LANES), 1)
    return jnp.where(lane < HEAD_DIM, x2[:TQ], x2[TQ:])


def _stack_cols(ref):
    return jnp.concatenate([ref[0], ref[1]], axis=0)


def _chunk_valid(i, j, r, strict):
    qpos = i * TQ + (r % TQ) + lax.broadcasted_iota(jnp.int32, (RC, TQ), 0)
    kpos = j * TQ + lax.broadcasted_iota(jnp.int32, (RC, TQ), 1)
    causal = (kpos < qpos) if strict else (kpos <= qpos)
    return causal & (kpos >= PAD)


def _walk_tiles(i, step, reverse):
    first, last = (i, 0) if reverse else (0, i)
    step(first, True)
    _between_in_pairs(i, lambda j, nxt: step(j, False), reverse)

    @pl.when(i > 0)
    def _():
        step(last, True)


def _between_in_pairs(i, step, reverse):
    tile = (lambda t: i - 1 - t) if reverse else (lambda t: t + 1)
    after = (lambda j: j - 1) if reverse else (lambda j: j + 1)
    n = jnp.maximum(i - 1, 0)

    def pair(u, c):
        j = tile(2 * u)
        step(j, after(j))
        step(after(j), after(after(j)))
        return c

    lax.fori_loop(0, n // 2, pair, 0)

    @pl.when(n % 2 == 1)
    def _():
        j = tile(n - 1)
        step(j, after(j))


_HALF = (slice(0, TQ), slice(TQ, T2))


def _walk_tiles_lead(i, lead, step, reverse):
    first, last = (i, 0) if reverse else (0, i)
    lead(first, 0)
    lead(first, 1)
    step(first, jnp.maximum(i - 1, 0) if reverse else jnp.minimum(1, i), True)

    _between_in_pairs(i, lambda j, nxt: step(j, nxt, False), reverse)

    @pl.when(i > 0)
    def _():
        step(last, None, True)


def _krows(j):
    return pl.ds(pl.multiple_of(j * TQ, TQ), TQ)


def _fox_fwd(proj, crow, name):
    lp = proj.shape[0]
    nb = lp // TQ

    def body(q_ref, k_ref, v_ref, cr_ref, o_ref, lse_ref, m_ref, acc_ref, p_ref):
        i = pl.program_id(1)
        q2 = _stack_heads(q_ref[...], 0.125)
        m_ref[...] = jnp.full(m_ref.shape, NEG, F32)
        acc_ref[...] = jnp.zeros_like(acc_ref)
        lane = lax.broadcasted_iota(jnp.int32, (TQ, LANES), 1)

        def step(j, masked):
            rows_j = _krows(j)
            s = _dot_nt(q2, k_ref[rows_j, :])
            v = v_ref[rows_j, :]
            one = jnp.ones_like(v)
            v_heads = (jnp.where(lane < HEAD_DIM, v, one), jnp.where(lane >= HEAD_DIM, v, one))
            for r in range(0, T2, RC):
                rows = slice(r, r + RC)
                s_c = s[rows] - cr_ref[r // TQ, j]
                if masked:
                    s_c = jnp.where(_chunk_valid(i, j, r, False), s_c, NEG)
                s0, s1 = s_c[:, :LANES], s_c[:, LANES:]
                m_old = m_ref[rows]
                m_new = jnp.maximum(m_old, jnp.max(jnp.maximum(s0, s1), axis=-1, keepdims=True))
                m_ref[rows] = m_new
                acc_ref[rows] = jnp.exp(m_old - m_new) * acc_ref[rows]
                p_ref[rows, :LANES] = jnp.exp(s0 - m_new).astype(BF16)
                p_ref[rows, LANES:] = jnp.exp(s1 - m_new).astype(BF16)
            for h in range(2):
                acc_ref[_HALF[h]] += _dot(p_ref[_HALF[h]], v_heads[h])

        _walk_tiles(i, step, reverse=False)
        acc = acc_ref[...]
        m = m_ref[...]
        outs = []
        for h in range(2):
            a_h = acc[_HALF[h]]
            l = a_h[:, HEAD_DIM:HEAD_DIM + 1] if h == 0 else a_h[:, 0:1]
            lse_ref[h] = m[_HALF[h]][:, 0:1] + jnp.log(l)
            outs.append(a_h / l)
        o_ref[...] = jnp.where(lane < HEAD_DIM, outs[0], outs[1]).astype(BF16)

    return pl.pallas_call(
        body, name=name, grid=(N_PAIRS, nb),
        in_specs=_pair_specs(lp, 0) + [_rowvec_spec(nb)],
        out_specs=[_tile_spec(), _col_spec()],
        out_shape=[jax.ShapeDtypeStruct((lp, W_ATT), BF16),
                   jax.ShapeDtypeStruct((N_PAIRS, 2, lp, 1), F32)],
        scratch_shapes=[pltpu.VMEM((T2, LANES), F32), pltpu.VMEM((T2, LANES), F32), pltpu.VMEM((T2, TQ), BF16)],
        compiler_params=_cparams(("parallel", "arbitrary")),
    )(proj, proj, proj, crow)


def _fox_bwd(proj, do, o, lse, crow, name):
    lp = proj.shape[0]
    nb = lp // TQ

    def body(q_ref, k_ref, v_ref, do_ref, o_ref, lse_ref, cr_ref,
             dq_ref, dk_ref, dv_ref, dcs_ref, dct_ref,
             dk_acc, dv_acc, dq_acc, dct_acc, p_ref, ds_ref, s_ref, dp_ref):
        i = pl.program_id(1)

        @pl.when(i == 0)
        def _():
            dk_acc[...] = jnp.zeros_like(dk_acc)
            dv_acc[...] = jnp.zeros_like(dv_acc)
            dcs_ref[...] = jnp.zeros_like(dcs_ref)

        dq_acc[...] = jnp.zeros_like(dq_acc)
        dct_acc[...] = jnp.zeros_like(dct_acc)
        do_ = do_ref[...]
        q2 = _stack_heads(q_ref[...], 0.125)
        do2 = _stack_heads(do_)
        prod = do_.astype(F32) * o_ref[...].astype(F32)
        lane = lax.broadcasted_iota(jnp.int32, prod.shape, 1)
        delta2 = jnp.concatenate(
            [jnp.sum(jnp.where(lane < HEAD_DIM, prod, 0.0), axis=-1, keepdims=True),
             jnp.sum(jnp.where(lane >= HEAD_DIM, prod, 0.0), axis=-1, keepdims=True)], axis=0)
        lse2 = _stack_cols(lse_ref)

        def lead(j, h):
            rows_j = _krows(j)
            s_ref[_HALF[h]] = _dot_nt(q2[_HALF[h]], k_ref[rows_j, :])
            dp_ref[_HALF[h]] = _dot_nt(do2[_HALF[h]], v_ref[rows_j, :])

        def step(j, nxt, masked):
            rows_j = _krows(j)
            k = k_ref[rows_j, :]
            for h in range(2):
                cs = jnp.zeros((1, TQ), F32)
                for r in range(h * TQ, (h + 1) * TQ, RC):
                    rows = slice(r, r + RC)
                    p = jnp.exp(s_ref[rows] - cr_ref[h, j] - lse2[rows])
                    if masked:
                        p = jnp.where(_chunk_valid(i, j, r, False), p, 0.0)
                    ds = p * (dp_ref[rows] - delta2[rows])
                    p_ref[rows] = p.astype(BF16)
                    ds_ref[rows] = ds.astype(BF16)
                    dct_acc[rows] += jnp.sum(ds, axis=-1, keepdims=True)
                    cs = cs + jnp.sum(ds, axis=0, keepdims=True)
                dcs_ref[h, j] -= cs
                if nxt is not None:
                    lead(nxt, h)
                dsb = ds_ref[_HALF[h]]
                dq_acc[_HALF[h]] += _dot(dsb, k)
                dk_acc[rows_j, :] += _dot_tn(dsb, q2[_HALF[h]])
                dv_acc[rows_j, :] += _dot_tn(p_ref[_HALF[h]], do2[_HALF[h]])

        _walk_tiles_lead(i, lead, step, reverse=False)
        dct = dct_acc[...]
        dct_ref[0] = dct[:TQ]
        dct_ref[1] = dct[TQ:]
        dq_ref[...] = (_unstack_heads(dq_acc[...]) * 0.125).astype(BF16)

        @pl.when(i == nb - 1)
        def _():
            dk_ref[...] = dk_acc[...].astype(BF16)
            dv_ref[...] = dv_acc[...].astype(BF16)

    whole = pl.BlockSpec((lp, LANES), lambda p, i: (0, p))
    return pl.pallas_call(
        body, name=name, grid=(N_PAIRS, nb),
        in_specs=_pair_specs(lp, 0) + [_tile_spec(), _tile_spec(), _col_spec(), _rowvec_spec(nb)],
        out_specs=[_tile_spec(), whole, whole, _rowvec_spec(nb), _col_spec()],
        out_shape=[jax.ShapeDtypeStruct((lp, W_ATT), BF16)] * 3
        + [jax.ShapeDtypeStruct((N_PAIRS, 2, nb, 1, TQ), F32), jax.ShapeDtypeStruct((N_PAIRS, 2, lp, 1), F32)],
        scratch_shapes=[pltpu.VMEM((lp, LANES), F32), pltpu.VMEM((lp, LANES), F32),
                        pltpu.VMEM((T2, LANES), F32), pltpu.VMEM((T2, 1), F32),
                        pltpu.VMEM((T2, TQ), BF16), pltpu.VMEM((T2, TQ), BF16),
                        pltpu.VMEM((T2, TQ), F32), pltpu.VMEM((T2, TQ), F32)],
        compiler_params=_cparams(("parallel", "arbitrary")),
    )(proj, proj, proj, do, o, lse, crow)


def _sb_fwd(proj, name):
    lp = proj.shape[0]
    nb = lp // TQ
    tsuf = _tri(TQ, lambda r, c: r > c)

    def body(q_ref, k_ref, v_ref, t_ref, o_ref, lt_ref, run_ref, acc_ref, zl_ref, hl_ref, a_ref, z_ref):
        i = pl.program_id(1)
        q2 = _stack_heads(q_ref[...], 0.125)
        run_ref[...] = jnp.zeros_like(run_ref)
        acc_ref[...] = jnp.zeros_like(acc_ref)

        def lead(j, h):
            z_ref[_HALF[h]] = _dot_nt(q2[_HALF[h]], k_ref[_krows(j), :])

        def step(j, nxt, masked):
            t = t_ref[...]
            v = v_ref[_krows(j), :]
            later = []
            for h in range(2):
                for r in range(h * TQ, (h + 1) * TQ, RC):
                    rows = slice(r, r + RC)
                    z_c = z_ref[rows]
                    lk = _log_keep(z_c)
                    if masked:
                        lk = jnp.where(_chunk_valid(i, j, r, True), lk, 0.0)
                    hi, lo = _split2(lk)
                    hl_ref[rows, :TQ] = hi
                    hl_ref[rows, TQ:] = lo
                    zl_ref[rows] = z_c + lk + run_ref[rows]
                    run_ref[rows] += jnp.sum(lk, axis=-1, keepdims=True)
                if nxt is not None:
                    lead(nxt, h)
                later.append(_dot(hl_ref[_HALF[h]], t))
            for h in range(2):
                for r in range(0, TQ, RC):
                    rows = slice(h * TQ + r, h * TQ + r + RC)
                    a = jnp.exp(zl_ref[rows] + later[h][r:r + RC])
                    if masked:
                        a = jnp.where(_chunk_valid(i, j, h * TQ + r, True), a, 0.0)
                    a_ref[rows] = a.astype(BF16)
                acc_ref[_HALF[h]] += _dot(a_ref[_HALF[h]], v)

        _walk_tiles_lead(i, lead, step, reverse=True)
        run = run_ref[...]
        lt_ref[0] = run[:TQ]
        lt_ref[1] = run[TQ:]
        o_ref[...] = _unstack_heads(acc_ref[...]).astype(BF16)

    base = 3 * N_PAIRS
    return pl.pallas_call(
        body, name=name, grid=(N_PAIRS, nb),
        in_specs=_pair_specs(lp, base) + [pl.BlockSpec((2 * TQ, TQ), lambda p, i: (0, 0))],
        out_specs=[_tile_spec(), _col_spec()],
        out_shape=[jax.ShapeDtypeStruct((lp, W_ATT), BF16),
                   jax.ShapeDtypeStruct((N_PAIRS, 2, lp, 1), F32)],
        scratch_shapes=[pltpu.VMEM((T2, 1), F32), pltpu.VMEM((T2, LANES), F32), pltpu.VMEM((T2, TQ), F32),
                        pltpu.VMEM((T2, 2 * TQ), BF16), pltpu.VMEM((T2, TQ), BF16), pltpu.VMEM((T2, TQ), F32)],
        compiler_params=_cparams(("parallel", "arbitrary")),
    )(proj, proj, proj, jnp.concatenate([tsuf, tsuf], axis=0))


def _sb_bwd(proj, do, ltot, name):
    lp = proj.shape[0]
    nb = lp // TQ
    tincl = _tri(TQ, lambda r, c: r <= c)
    texcl = _tri(TQ, lambda r, c: r < c)

    def body(q_ref, k_ref, v_ref, do_ref, lt_ref, ti_ref, te_ref, dq_ref, dk_ref, dv_ref,
             dk_acc, dv_acc, dq_acc, pc_ref, gc_ref, zl_ref, keep_ref, g_ref, z_ref, da_ref,
             hl_ref, gb_ref, a_ref, dz_ref):
        i = pl.program_id(1)

        @pl.when(i == 0)
        def _():
            dk_acc[...] = jnp.zeros_like(dk_acc)
            dv_acc[...] = jnp.zeros_like(dv_acc)

        dq_acc[...] = jnp.zeros_like(dq_acc)
        gc_ref[...] = jnp.zeros_like(gc_ref)
        pc_ref[...] = _stack_cols(lt_ref)
        q2 = _stack_heads(q_ref[...], 0.125)
        do2 = _stack_heads(do_ref[...])

        def lead(j, h):
            rows_j = _krows(j)
            z_ref[_HALF[h]] = _dot_nt(q2[_HALF[h]], k_ref[rows_j, :])
            da_ref[_HALF[h]] = _dot_nt(do2[_HALF[h]], v_ref[rows_j, :])

        def step(j, nxt, masked):
            rows_j = _krows(j)
            k = k_ref[rows_j, :]
            ti = ti_ref[...]
            te = te_ref[...]
            upto, before = [], []
            for h in range(2):
                for r in range(h * TQ, (h + 1) * TQ, RC):
                    rows = slice(r, r + RC)
                    z_c = z_ref[rows]
                    lk = _log_keep(z_c)
                    if masked:
                        lk = jnp.where(_chunk_valid(i, j, r, True), lk, 0.0)
                    hi, lo = _split2(lk)
                    hl_ref[rows, :TQ] = hi
                    hl_ref[rows, TQ:] = lo
                    keep_ref[rows] = jnp.exp(lk)
                    zl_ref[rows] = z_c + lk + pc_ref[rows]
                    pc_ref[rows] -= jnp.sum(lk, axis=-1, keepdims=True)
                upto.append(_dot(hl_ref[_HALF[h]], ti))
            for h in range(2):
                for r in range(0, TQ, RC):
                    rows = slice(h * TQ + r, h * TQ + r + RC)
                    a = jnp.exp(zl_ref[rows] - upto[h][r:r + RC])
                    if masked:
                        a = jnp.where(_chunk_valid(i, j, h * TQ + r, True), a, 0.0)
                    g = a * da_ref[rows]
                    a_ref[rows] = a.astype(BF16)
                    g_ref[rows] = g
                    gb_ref[rows] = g.astype(BF16)
                if nxt is not None:
                    lead(nxt, h)
                before.append(_dot(gb_ref[_HALF[h]], te))
            for h in range(2):
                for r in range(0, TQ, RC):
                    rows = slice(h * TQ + r, h * TQ + r + RC)
                    g = g_ref[rows]
                    keep = keep_ref[rows]
                    dz = g * keep - (1.0 - keep) * (gc_ref[rows] + before[h][r:r + RC])
                    if masked:
                        dz = jnp.where(_chunk_valid(i, j, h * TQ + r, True), dz, 0.0)
                    dz_ref[rows] = dz.astype(BF16)
                    gc_ref[rows] += jnp.sum(g, axis=-1, keepdims=True)
                dzb = dz_ref[_HALF[h]]
                dq_acc[_HALF[h]] += _dot(dzb, k)
                dk_acc[rows_j, :] += _dot_tn(dzb, q2[_HALF[h]])
                dv_acc[rows_j, :] += _dot_tn(a_ref[_HALF[h]], do2[_HALF[h]])

        _walk_tiles_lead(i, lead, step, reverse=False)
        dq_ref[...] = (_unstack_heads(dq_acc[...]) * 0.125).astype(BF16)

        @pl.when(i == nb - 1)
        def _():
            dk_ref[...] = dk_acc[...].astype(BF16)
            dv_ref[...] = dv_acc[...].astype(BF16)

    base = 3 * N_PAIRS
    whole = pl.BlockSpec((lp, LANES), lambda p, i: (0, p))
    tri = lambda rows: pl.BlockSpec((rows, TQ), lambda p, i: (0, 0))
    wide = lambda dt: pltpu.VMEM((T2, TQ), dt)
    return pl.pallas_call(
        body, name=name, grid=(N_PAIRS, nb),
        in_specs=_pair_specs(lp, base) + [_tile_spec(), _col_spec(), tri(2 * TQ), tri(TQ)],
        out_specs=[_tile_spec(), whole, whole],
        out_shape=[jax.ShapeDtypeStruct((lp, W_ATT), BF16)] * 3,
        scratch_shapes=[pltpu.VMEM((lp, LANES), F32), pltpu.VMEM((lp, LANES), F32),
                        pltpu.VMEM((T2, LANES), F32), pltpu.VMEM((T2, 1), F32), pltpu.VMEM((T2, 1), F32),
                        wide(F32), wide(F32), wide(F32), wide(F32), wide(F32),
                        pltpu.VMEM((T2, 2 * TQ), BF16), wide(BF16), wide(BF16), wide(BF16)],
        compiler_params=_cparams(("parallel", "arbitrary")),
    )(proj, proj, proj, do, ltot, jnp.concatenate([tincl, tincl], axis=0), texcl)


def _local_step(x, target, meta, gains, w_in, b_forget, w_o_fox, w_o_sb, w_out, w_up, conv_w, conv_b, w_down):
    seq, dm = x.shape
    lp = PAD + N_META + seq
    nb = lp // TQ
    s = [W_ATT, W_ATT, W_ATT, 8, W_ATT, W_ATT, W_ATT, dm, dm]
    off = [sum(s[:i]) for i in range(len(s) + 1)]
    cols = lambda i: w_in[:, off[i]:off[i + 1]]
    w1 = jnp.concatenate([cols(0), cols(1), cols(2), cols(4), cols(5), cols(6), cols(7), cols(8)], axis=1)
    wf = jnp.pad(cols(3), ((0, 0), (0, LANES - 8)))
    n1 = w1.shape[1]
    ncat = n1 + 512
    w_cat = jnp.concatenate([w1, wf, jnp.zeros((dm, ncat - n1 - LANES), BF16)], axis=1)
    bf = jnp.pad(b_forget.reshape(1, 8), ((0, 0), (0, LANES - 8)))
    g = [gains[i].reshape(1, dm) for i in range(4)]
    cb = conv_b.reshape(1, -1)

    h0 = jnp.concatenate([jnp.zeros((PAD, dm), F32), meta, x], axis=0)

    proj, xn1 = _rms_mm(h0, g[0], w1, "in_proj")
    logf = _logf(xn1, wf, bf, "log_forget")
    c = _cumsum_rows(logf, "forget_cumsum")
    crow = c[:, :8].T.reshape(N_PAIRS, 2, nb, 1, TQ)
    o_a, lse = _fox_fwd(proj, crow, "fox_fwd")
    o_b, ltot = _sb_fwd(proj, "sb_fwd")
    y_a, y_b, m, mixed, h1 = _merge_fwd(o_a, o_b, proj, h0, w_o_fox, w_o_sb, w_out, g[1], "merge_fwd")
    up, xn3 = _rms_mm(h1, g[2], w_up, "up_proj")
    a = _convgate_fwd(up, conv_w, cb, "convgate_fwd")
    ffn, dy, ss = _down_loss(a, w_down, h1, g[3], target, "down_loss")

    dffn, da, dg3 = _down_bwd(dy, ffn, g[3], w_down, "down_bwd")
    d_w_down = _mm_tn(a, dffn, "dw_down")
    du, d_conv_w, d_conv_b = _convgate_bwd(up, da, conv_w, cb, "convgate_bwd")
    dup = _conv_transpose(du, conv_w, "conv_transpose")
    d_w_up = _mm_tn(xn3, dup, "dw_up")
    dh1, dg2 = _mm_rmsbwd(dup, w_up, h1, g[2], dy, "up_bwd")
    dmx, dya, dyb, dga, dgb, do_a, do_b, dg1 = _merge_bwd(
        dh1, mixed, g[1], w_out, proj, y_a, y_b, w_o_fox, w_o_sb, "merge_bwd")
    d_w_out = _mm_tn(m, dmx, "dw_out")
    d_w_o_fox = _mm_tn(o_a, dya, "dw_o_fox")
    d_w_o_sb = _mm_tn(o_b, dyb, "dw_o_sb")
    dq_a, dk_a, dv_a, dcs, dct = _fox_bwd(proj, do_a, o_a, lse, crow, "fox_bwd")
    dq_b, dk_b, dv_b = _sb_bwd(proj, do_b, ltot, "sb_bwd")
    dc = (dct.reshape(8, lp) + dcs.reshape(8, lp)).T
    df, db = _dlogf(jnp.pad(dc, ((0, 0), (0, LANES - 8))), logf, "forget_bwd")
    dcat = jnp.concatenate([dq_a, dk_a, dv_a, dq_b, dk_b, dv_b, dga, dgb, df.astype(BF16),
                            jnp.zeros((lp, ncat - n1 - LANES), BF16)], axis=1)
    d_w_cat = _mm_tn(xn1, dcat, "dw_in")
    dh0, dg0 = _mm_rmsbwd(dcat, w_cat, h0, g[0], dh1, "in_bwd")

    wc = lambda k: d_w_cat[:, k * W_ATT:(k + 1) * W_ATT]
    d_w_in = jnp.concatenate([wc(0), wc(1), wc(2), d_w_cat[:, n1:n1 + 8], wc(3), wc(4), wc(5),
                              d_w_cat[:, 6 * W_ATT:n1]], axis=1)
    d_gains = jnp.concatenate([dg0, dg1, dg2, dg3], axis=0)
    grads = (dh0[PAD:PAD + N_META], d_gains, d_w_in, db[0, :8], d_w_o_fox, d_w_o_sb, d_w_out,
             d_w_up, d_conv_w, d_conv_b[0], d_w_down)
    return ss[0, 0], dh0[PAD + N_META:], grads


def _rows(a, n_rows):
    flat = a.reshape(-1)
    return jnp.pad(flat, (0, n_rows * D_MODEL - flat.shape[0])).reshape(n_rows, D_MODEL)


_SMALL = (("meta", 4), ("gains", 1), ("conv_w", 5), ("b_forget", 1), ("conv_b", 6))


def _pack(meta, gains, w_in, b_forget, w_o_fox, w_o_sb, w_out, w_up, conv_w, conv_b, w_down):
    big = [a.reshape(-1, D_MODEL) for a in (w_in, w_o_fox, w_o_sb, w_out, w_up, w_down)]
    n_big = sum(a.shape[0] for a in big)
    small = [_rows(a, n) for a, (_, n) in zip((meta, gains, conv_w, b_forget, conv_b), _SMALL)]
    n_small = sum(n for _, n in _SMALL)
    z = lambda n: jnp.zeros((n, D_MODEL), big[0].dtype)
    return jnp.concatenate(big + [z(PK_BIG_ROWS - n_big)] + small
                           + [z(PK_ROWS - PK_BIG_ROWS - n_small)], axis=0)


def _unpack(p):
    def take(r0, shape):
        n = math.prod(shape)
        nr = -(-n // D_MODEL)
        return p[r0:r0 + nr].reshape(-1)[:n].reshape(shape), r0 + nr
    w_in, r = take(0, (1, 1024, 1282))
    w_o_fox, r = take(r, (1, 512, 256))
    w_o_sb, r = take(r, (1, 512, 256))
    w_out, r = take(r, (1, 256, 1024))
    w_up, r = take(r, (1, 1024, 1408))
    w_down, r = take(r, (1, 704, 1024))
    r = PK_BIG_ROWS
    meta, r = take(r, (16, 256))
    gains, r = take(r, (1, 4, 256))
    conv_w, r = take(r, (1, 3, 1408))
    b_forget, r = take(r, (1, 8))
    conv_b, r = take(r, (1, 5632))
    return meta, gains, w_in, b_forget, w_o_fox, w_o_sb, w_out, w_up, conv_w, conv_b, w_down


def _chip_peers():
    x, y, c = lax.axis_index("x"), lax.axis_index("y"), lax.axis_index("c")
    return [(x, 1 - y, c), (1 - x, y, c), (1 - x, 1 - y, c)]


def _all_gather_chips(arrays, name):
    n = len(arrays)

    def body(*refs):
        ins, outs = refs[:n], refs[n:2 * n]
        send_sems, recv_sems, local_sems = refs[2 * n:]
        x, y = lax.axis_index("x"), lax.axis_index("y")
        me = 2 * x + y
        peers = _chip_peers()
        copies = []
        for a in range(n):
            mine = pltpu.make_async_copy(ins[a], outs[a].at[me], local_sems.at[a])
            mine.start()
            copies.append(mine)
        remote = []
        for a in range(n):
            for j, peer in enumerate(peers):
                cp = pltpu.make_async_remote_copy(
                    src_ref=ins[a], dst_ref=outs[a].at[me],
                    send_sem=send_sems.at[3 * a + j], recv_sem=recv_sems.at[3 * a + j],
                    device_id=peer, device_id_type=MESH)
                cp.start()
                remote.append(cp)
        for cp in remote:
            cp.wait()
        for cp in copies:
            cp.wait()

    any_spec = pl.BlockSpec(memory_space=pl.ANY)
    return pl.pallas_call(
        body, name=name,
        in_specs=[any_spec] * n, out_specs=[any_spec] * n,
        out_shape=[jax.ShapeDtypeStruct((4,) + a.shape, a.dtype) for a in arrays],
        scratch_shapes=[pltpu.SemaphoreType.DMA((3 * n,)), pltpu.SemaphoreType.DMA((3 * n,)),
                        pltpu.SemaphoreType.DMA((n,))],
    )(*arrays)


def _scatter_chips(chunks, name):
    _, rows, cols = chunks.shape

    def body(in_ref, out_ref, send_sems, recv_sems):
        x, y = lax.axis_index("x"), lax.axis_index("y")
        targets = [2 * x + (1 - y), 2 * (1 - x) + y, 2 * (1 - x) + (1 - y)]
        remote = []
        for j, peer in enumerate(_chip_peers()):
            cp = pltpu.make_async_remote_copy(
                src_ref=in_ref.at[targets[j]], dst_ref=out_ref.at[j],
                send_sem=send_sems.at[j], recv_sem=recv_sems.at[j],
                device_id=peer, device_id_type=MESH)
            cp.start()
            remote.append(cp)
        for cp in remote:
            cp.wait()

    any_spec = pl.BlockSpec(memory_space=pl.ANY)
    return pl.pallas_call(
        body, name=name, in_specs=[any_spec], out_specs=any_spec,
        out_shape=jax.ShapeDtypeStruct((3, rows, cols), chunks.dtype),
        scratch_shapes=[pltpu.SemaphoreType.DMA((3,)), pltpu.SemaphoreType.DMA((3,))],
    )(chunks)


def _swap_half_rows(chunks, name):
    n, rows, cols = chunks.shape
    r2 = rows // 2

    def body(in_ref, out_ref, send_sem, recv_sem):
        x, y, c = lax.axis_index("x"), lax.axis_index("y"), lax.axis_index("c")
        cp = pltpu.make_async_remote_copy(
            src_ref=in_ref.at[:, pl.ds((1 - c) * r2, r2), :], dst_ref=out_ref,
            send_sem=send_sem, recv_sem=recv_sem, device_id=(x, y, 1 - c), device_id_type=MESH)
        cp.start()
        cp.wait()

    any_spec = pl.BlockSpec(memory_space=pl.ANY)
    return pl.pallas_call(
        body, name=name, in_specs=[any_spec], out_specs=any_spec,
        out_shape=jax.ShapeDtypeStruct((n, r2, cols), chunks.dtype),
        scratch_shapes=[pltpu.SemaphoreType.DMA, pltpu.SemaphoreType.DMA],
    )(chunks)


def _gather_halves(half, name):
    def body(in_ref, out_ref, send_sem, recv_sem):
        x, y, c = lax.axis_index("x"), lax.axis_index("y"), lax.axis_index("c")
        cp = pltpu.make_async_remote_copy(
            src_ref=in_ref, dst_ref=out_ref, send_sem=send_sem, recv_sem=recv_sem,
            device_id=(x, y, 1 - c), device_id_type=MESH)
        cp.start()
        cp.wait()

    any_spec = pl.BlockSpec(memory_space=pl.ANY)
    other = pl.pallas_call(
        body, name=name, in_specs=[any_spec], out_specs=any_spec,
        out_shape=jax.ShapeDtypeStruct(half.shape, half.dtype),
        scratch_shapes=[pltpu.SemaphoreType.DMA, pltpu.SemaphoreType.DMA],
    )(half)
    axis = half.ndim - 2
    return lax.cond(lax.axis_index("c") == 0,
                    lambda: jnp.concatenate([half, other], axis=axis),
                    lambda: jnp.concatenate([other, half], axis=axis))


def _add(a, b, name):
    n, rows, cols = a.shape

    def body(a_ref, b_ref, o_ref):
        o_ref[...] = a_ref[...] + b_ref[...]

    spec = pl.BlockSpec((None, PK_TILE, cols), lambda k, i: (k, i, 0))
    return pl.pallas_call(
        body, name=name, grid=(n, rows // PK_TILE),
        in_specs=[spec, spec], out_specs=spec,
        out_shape=jax.ShapeDtypeStruct(a.shape, F32),
        compiler_params=_cparams(("parallel", "parallel")),
    )(a, b)


def _chip_sum(chunks, recv, name):
    _, rows, cols = chunks.shape

    def body(own_ref, r_ref, o_ref):
        o_ref[...] = (own_ref[...] + r_ref[0]) + (r_ref[1] + r_ref[2])

    me = 2 * lax.axis_index("x") + lax.axis_index("y")
    own = lax.dynamic_index_in_dim(chunks, me, axis=0, keepdims=False)
    return pl.pallas_call(
        body, name=name, grid=(rows // PK_TILE,),
        in_specs=[pl.BlockSpec((PK_TILE, cols), lambda i: (i, 0)),
                  pl.BlockSpec((3, PK_TILE, cols), lambda i: (0, i, 0))],
        out_specs=pl.BlockSpec((PK_TILE, cols), lambda i: (i, 0)),
        out_shape=jax.ShapeDtypeStruct((rows, cols), F32),
        compiler_params=_cparams(("parallel",)),
    )(own, recv)


def _adamw(w, m, v, g, name):
    rows, cols = w.shape
    c1 = 1.0 - ADAM_B1 ** ADAM_STEP
    c2 = 1.0 - ADAM_B2 ** ADAM_STEP

    def body(w_ref, m_ref, v_ref, g_ref, d_ref, nm_ref, nv_ref):
        g = g_ref[...]
        nm = ADAM_B1 * m_ref[...] + (1.0 - ADAM_B1) * g
        nv = ADAM_B2 * v_ref[...] + (1.0 - ADAM_B2) * (g * g)
        nm_ref[...] = nm
        nv_ref[...] = nv
        d_ref[...] = -ADAM_LR * ((nm / c1) / (jnp.sqrt(nv / c2) + ADAM_EPS) + ADAM_WD * w_ref[...])

    spec = pl.BlockSpec((PK_TILE, cols), lambda i: (i, 0))
    return pl.pallas_call(
        body, name=name, grid=(rows // PK_TILE,),
        in_specs=[spec] * 4, out_specs=[spec] * 3,
        out_shape=[jax.ShapeDtypeStruct((rows, cols), F32)] * 3,
        compiler_params=_cparams(("parallel",)),
    )(w, m, v, g)


def _full_weights(big, small):
    def gather(src, r0, shape, axis):
        n = math.prod(shape)
        nr = -(-n // D_MODEL)
        parts = [src[k, r0:r0 + nr].reshape(-1)[:n].reshape(shape) for k in range(4)]
        return jnp.concatenate(parts, axis=axis), r0 + nr
    w_in, r = gather(big, 0, (1024, 1282), 1)
    w_o_fox, r = gather(big, r, (512, 256), 1)
    w_o_sb, r = gather(big, r, (512, 256), 1)
    w_out, r = gather(big, r, (256, 1024), 0)
    w_up, r = gather(big, r, (1024, 1408), 1)
    w_down, r = gather(big, r, (704, 1024), 0)
    meta, r = gather(small, 0, (16, 256), 1)
    gains, r = gather(small, r, (4, 256), 1)
    conv_w, r = gather(small, r, (3, 1408), 1)
    return meta, gains, w_in, w_o_fox, w_o_sb, w_out, w_up, conv_w, w_down


def _chunks_for_chips(grads):
    d_meta, d_gains, d_w_in, d_b, d_w_o_fox, d_w_o_sb, d_w_out, d_w_up, d_conv_w, d_conv_b, d_w_down = grads
    out = []
    for k in range(4):
        col = lambda a, w: a[:, k * w:(k + 1) * w]
        row = lambda a, w: a[k * w:(k + 1) * w]
        out.append(_pack(col(d_meta, 256), col(d_gains, 256), col(d_w_in, 1282), d_b, col(d_w_o_fox, 256),
                         col(d_w_o_sb, 256), row(d_w_out, 256), col(d_w_up, 1408), col(d_conv_w, 1408),
                         d_conv_b, row(d_w_down, 704)))
    return jnp.stack(out, axis=0)


def kernel(x, meta_tokens, norm_gains, w_in, b_forget, w_o_fox, w_o_sb, w_out, w_up, conv_w, conv_b, w_down, loss_target, m_meta_tokens, m_norm_gains, m_w_in, m_b_forget, m_w_o_fox, m_w_o_sb, m_w_out, m_w_up, m_conv_w, m_conv_b, m_w_down, v_meta_tokens, v_norm_gains, v_w_in, v_b_forget, v_w_o_fox, v_w_o_sb, v_w_out, v_w_up, v_conv_w, v_conv_b, v_w_down):
    shard = lambda mt, ng, wi, bf, wof, wos, wo, wu, cw, cb, wd: _pack(
        mt, ng[0], wi[0], bf[0], wof[0], wos[0], wo[0], wu[0], cw[0], cb[0], wd[0])
    wp = shard(meta_tokens, norm_gains, w_in, b_forget, w_o_fox, w_o_sb, w_out, w_up, conv_w, conv_b, w_down)
    mp = shard(m_meta_tokens, m_norm_gains, m_w_in, m_b_forget, m_w_o_fox, m_w_o_sb, m_w_out, m_w_up,
               m_conv_w, m_conv_b, m_w_down)
    vp = shard(v_meta_tokens, v_norm_gains, v_w_in, v_b_forget, v_w_o_fox, v_w_o_sb, v_w_out, v_w_up,
               v_conv_w, v_conv_b, v_w_down)

    hb = PK_BIG_ROWS // 2
    big_half = lax.dynamic_slice_in_dim(wp[:PK_BIG_ROWS].astype(BF16), lax.axis_index("c") * hb, hb, axis=0)
    big_half, small = _all_gather_chips(
        [big_half, wp[PK_BIG_ROWS:PK_BIG_ROWS + PK_SMALL_ROWS]], "gather_weights")
    big = _gather_halves(big_half, "gather_weight_halves")
    meta, gains, f_w_in, f_w_o_fox, f_w_o_sb, f_w_out, f_w_up, f_conv_w, f_w_down = _full_weights(big, small)

    ss, dx, grads = _local_step(x[0], loss_target[0], meta, gains, f_w_in, b_forget[0], f_w_o_fox, f_w_o_sb,
                                f_w_out, f_w_up, f_conv_w, conv_b[0], f_w_down)
    loss = lax.psum(0.5 * ss / D_MODEL, ("x", "y", "c"))

    chunks = _chunks_for_chips(grads)
    r2 = PK_ROWS // 2
    from_sibling = _swap_half_rows(chunks, "swap_halves")
    own = lax.dynamic_slice_in_dim(chunks, lax.axis_index("c") * r2, r2, axis=1)
    core_sum = _add(own, from_sibling, "core_sum")
    recv = _scatter_chips(core_sum, "scatter_grads")
    g = _gather_halves(_chip_sum(core_sum, recv, "chip_sum"), "gather_halves")
    delta, new_m, new_v = _adamw(wp, mp, vp, g, "adamw")
    return (loss, dx[None], *_unpack(g), *_unpack(delta), *_unpack(new_m), *_unpack(new_v))
```

```python
import functools
import math

import jax
import jax.numpy as jnp
from jax import lax
from jax.experimental import pallas as pl
from jax.experimental.pallas import tpu as pltpu

F32 = jnp.float32
BF16 = jnp.bfloat16

D_MODEL = 1024
N_META = 16
HEAD_DIM = 64
N_PAIRS = 4
W_ATT = 512
D_FF = 2816
EPS = 1e-6
NEG = -1e30
TQ = 256
PAD = TQ - N_META
TCONV = 128
HALO = 16
LANES = 128
VMEM_LIMIT = 56 * 1024 * 1024

ADAM_LR = 0.001
ADAM_B1 = 0.9
ADAM_B2 = 0.999
ADAM_EPS = 1e-08
ADAM_WD = 0.01
ADAM_STEP = 10

MESH = pl.DeviceIdType.MESH

PK_BIG_ROWS = 3936
PK_SMALL_ROWS = 24
PK_ROWS = 4096
PK_TILE = 128


def _cparams(sem, **kw):
    return pltpu.CompilerParams(dimension_semantics=sem, vmem_limit_bytes=VMEM_LIMIT, **kw)


def _row_tile(lp):
    return 768 if lp % 768 == 0 else 256


def _wide_tile(n):
    return next(t for t in (1408, 1280, 1024, 512, 256) if n % t == 0)


def _rms(x):
    return lax.rsqrt(jnp.mean(x * x, axis=-1, keepdims=True) + EPS)


def _log_sigmoid(x):
    return jnp.minimum(x, 0.0) - jnp.log(1.0 + jnp.exp(-jnp.abs(x)))


_LOG2E = 1.4426950408889634
_LN2 = 0.6931471805599453


def _log_keep(z):
    t = jnp.exp2(jnp.abs(z) * (-_LOG2E))
    return jnp.log2(1.0 + t) * (-_LN2) - jnp.maximum(z, 0.0)


def _split2(x):
    hi = x.astype(BF16)
    lo = (x - hi.astype(F32)).astype(BF16)
    return hi, lo


def _dot(a, b):
    return jnp.dot(a, b, preferred_element_type=F32)


def _dot_nt(a, b):
    return lax.dot_general(a, b, (((1,), (1,)), ((), ())), preferred_element_type=F32)


def _dot_tn(a, b):
    return lax.dot_general(a, b, (((0,), (0,)), ((), ())), preferred_element_type=F32)


def _rms_mm(h, g, w, name):
    lp, dm = h.shape
    n = w.shape[1]
    tr, tn = _row_tile(lp), _wide_tile(n)

    def body(h_ref, g_ref, w_ref, out_ref, xn_ref):
        @pl.when(pl.program_id(1) == 0)
        def _():
            x = h_ref[...]
            xn_ref[...] = (x * _rms(x) * g_ref[...]).astype(BF16)
        out_ref[...] = _dot(xn_ref[...], w_ref[...]).astype(BF16)

    return pl.pallas_call(
        body, name=name, grid=(lp // tr, n // tn),
        in_specs=[pl.BlockSpec((tr, dm), lambda i, j: (i, 0)),
                  pl.BlockSpec((1, dm), lambda i, j: (0, 0)),
                  pl.BlockSpec((dm, tn), lambda i, j: (0, j))],
        out_specs=[pl.BlockSpec((tr, tn), lambda i, j: (i, j)),
                   pl.BlockSpec((tr, dm), lambda i, j: (i, 0))],
        out_shape=[jax.ShapeDtypeStruct((lp, n), BF16), jax.ShapeDtypeStruct((lp, dm), BF16)],
        compiler_params=_cparams(("parallel", "arbitrary")),
    )(h, g, w)


def _mm_rmsbwd(dy, w, h, g, dh_in, name):
    lp, kd = dy.shape
    dm = w.shape[0]
    tr, tk = 384, _wide_tile(kd)
    nk = kd // tk

    def body(dy_ref, w_ref, h_ref, g_ref, dhin_ref, dh_ref, dg_ref, acc_ref):
        i, k = pl.program_id(0), pl.program_id(1)

        @pl.when(k == 0)
        def _():
            acc_ref[...] = jnp.zeros_like(acc_ref)

        acc_ref[...] += _dot_nt(dy_ref[...], w_ref[...])

        @pl.when(k == nk - 1)
        def _():
            dxn = acc_ref[...]
            x = h_ref[...]
            r = _rms(x)
            yhat = x * r
            part = jnp.sum(dxn * yhat, axis=0, keepdims=True)
            dyh = dxn * g_ref[...]
            dx = r * (dyh - yhat * jnp.mean(dyh * yhat, axis=-1, keepdims=True))
            dh_ref[...] = dhin_ref[...] + dx

            @pl.when(i == 0)
            def _():
                dg_ref[...] = part

            @pl.when(i > 0)
            def _():
                dg_ref[...] += part

    return pl.pallas_call(
        body, name=name, grid=(lp // tr, nk),
        in_specs=[pl.BlockSpec((tr, tk), lambda i, k: (i, k)),
                  pl.BlockSpec((dm, tk), lambda i, k: (0, k)),
                  pl.BlockSpec((tr, dm), lambda i, k: (i, 0)),
                  pl.BlockSpec((1, dm), lambda i, k: (0, 0)),
                  pl.BlockSpec((tr, dm), lambda i, k: (i, 0))],
        out_specs=[pl.BlockSpec((tr, dm), lambda i, k: (i, 0)),
                   pl.BlockSpec((1, dm), lambda i, k: (0, 0))],
        out_shape=[jax.ShapeDtypeStruct((lp, dm), F32), jax.ShapeDtypeStruct((1, dm), F32)],
        scratch_shapes=[pltpu.VMEM((tr, dm), F32)],
        compiler_params=_cparams(("arbitrary", "arbitrary")),
    )(dy, w, h, g, dh_in)


def _mm_tn(x, dy, name):
    lp, kd = x.shape
    n = dy.shape[1]
    tl = _row_tile(lp)
    tk = _wide_tile(kd)
    tn = _wide_tile(n)
    nl = lp // tl

    def body(x_ref, dy_ref, o_ref):
        @pl.when(pl.program_id(2) == 0)
        def _():
            o_ref[...] = jnp.zeros_like(o_ref)
        o_ref[...] += _dot_tn(x_ref[...], dy_ref[...])

    return pl.pallas_call(
        body, name=name, grid=(kd // tk, n // tn, nl),
        in_specs=[pl.BlockSpec((tl, tk), lambda a, b, l: (l, a)),
                  pl.BlockSpec((tl, tn), lambda a, b, l: (l, b))],
        out_specs=pl.BlockSpec((tk, tn), lambda a, b, l: (a, b)),
        out_shape=jax.ShapeDtypeStruct((kd, n), F32),
        compiler_params=_cparams(("parallel", "parallel", "arbitrary")),
    )(x, dy)


def _logf(xn, wf, bf, name):
    lp, dm = xn.shape
    tr = _row_tile(lp)

    def body(xn_ref, wf_ref, b_ref, o_ref):
        f = _dot(xn_ref[...], wf_ref[...]) + b_ref[...]
        row = pl.program_id(0) * tr + lax.broadcasted_iota(jnp.int32, f.shape, 0)
        lane = lax.broadcasted_iota(jnp.int32, f.shape, 1)
        o_ref[...] = jnp.where((row >= PAD) & (lane < 8), _log_sigmoid(f), 0.0)

    return pl.pallas_call(
        body, name=name, grid=(lp // tr,),
        in_specs=[pl.BlockSpec((tr, dm), lambda i: (i, 0)),
                  pl.BlockSpec((dm, LANES), lambda i: (0, 0)),
                  pl.BlockSpec((1, LANES), lambda i: (0, 0))],
        out_specs=pl.BlockSpec((tr, LANES), lambda i: (i, 0)),
        out_shape=jax.ShapeDtypeStruct((lp, LANES), F32),
        compiler_params=_cparams(("parallel",)),
    )(xn, wf, bf)


def _tri(n, rel):
    r = lax.broadcasted_iota(jnp.int32, (n, n), 0)
    c = lax.broadcasted_iota(jnp.int32, (n, n), 1)
    return rel(r, c).astype(BF16)


def _cumsum_rows(x, name):
    lp = x.shape[0]
    nb = lp // TQ
    tl = _tri(TQ, lambda r, c: c <= r)

    def body(x_ref, t_ref, o_ref):
        def step(b, carry):
            rows = pl.ds(pl.multiple_of(b * TQ, TQ), TQ)
            xb = x_ref[rows, :]
            hi = xb.astype(BF16)
            r1 = xb - hi.astype(F32)
            mid = r1.astype(BF16)
            lo = (r1 - mid.astype(F32)).astype(BF16)
            t = t_ref[...]
            o_ref[rows, :] = carry + (_dot(t, hi) + _dot(t, mid) + _dot(t, lo))
            return carry + jnp.sum(xb, axis=0, keepdims=True)
        lax.fori_loop(0, nb, step, jnp.zeros((1, LANES), F32))

    return pl.pallas_call(
        body, name=name,
        in_specs=[pl.BlockSpec(memory_space=pltpu.VMEM)] * 2,
        out_specs=pl.BlockSpec(memory_space=pltpu.VMEM),
        out_shape=jax.ShapeDtypeStruct((lp, LANES), F32),
        compiler_params=pltpu.CompilerParams(vmem_limit_bytes=VMEM_LIMIT),
    )(x, tl)


def _dlogf(dc, logf, name):
    lp = dc.shape[0]
    nb = lp // TQ
    tu = _tri(TQ, lambda r, c: c >= r)

    def body(x_ref, lf_ref, t_ref, df_ref, db_ref):
        def step(bb, carry):
            run, db = carry
            b = nb - 1 - bb
            rows = pl.ds(pl.multiple_of(b * TQ, TQ), TQ)
            xb = x_ref[rows, :]
            hi = xb.astype(BF16)
            r1 = xb - hi.astype(F32)
            mid = r1.astype(BF16)
            lo = (r1 - mid.astype(F32)).astype(BF16)
            t = t_ref[...]
            dlf = run + (_dot(t, hi) + _dot(t, mid) + _dot(t, lo))
            df = dlf * (1.0 - jnp.exp(lf_ref[rows, :]))
            df_ref[rows, :] = df
            return run + jnp.sum(xb, axis=0, keepdims=True), db + jnp.sum(df, axis=0, keepdims=True)
        z = jnp.zeros((1, LANES), F32)
        _, db = lax.fori_loop(0, nb, step, (z, z))
        db_ref[...] = db

    return pl.pallas_call(
        body, name=name,
        in_specs=[pl.BlockSpec(memory_space=pltpu.VMEM)] * 3,
        out_specs=[pl.BlockSpec(memory_space=pltpu.VMEM)] * 2,
        out_shape=[jax.ShapeDtypeStruct((lp, LANES), F32), jax.ShapeDtypeStruct((1, LANES), F32)],
        compiler_params=pltpu.CompilerParams(vmem_limit_bytes=VMEM_LIMIT),
    )(dc, logf, tu)


def _merge_fwd(o_a, o_b, proj, h0, w_oa, w_ob, w_out, g1, name):
    lp, dm = h0.shape
    tr = TQ
    ga_blk = (6 * W_ATT) // dm

    def body(oa_ref, ob_ref, ga_ref, gb_ref, h0_ref, woa_ref, wob_ref, wout_ref, g1_ref,
             ya_ref, yb_ref, m_ref, mixed_ref, h1_ref):
        ya = _dot(oa_ref[...], woa_ref[...])
        yb = _dot(ob_ref[...], wob_ref[...])
        m = jax.nn.sigmoid(ga_ref[...].astype(F32)) * ya + jax.nn.sigmoid(gb_ref[...].astype(F32)) * yb
        mb = m.astype(BF16)
        mixed = _dot(mb, wout_ref[...])
        ya_ref[...] = ya.astype(BF16)
        yb_ref[...] = yb.astype(BF16)
        m_ref[...] = mb
        mixed_ref[...] = mixed
        h1_ref[...] = h0_ref[...] + mixed * _rms(mixed) * g1_ref[...]

    row = lambda w: pl.BlockSpec((tr, w), lambda i: (i, 0))
    full = lambda a: pl.BlockSpec(a.shape, lambda i: (0, 0))
    return pl.pallas_call(
        body, name=name, grid=(lp // tr,),
        in_specs=[row(W_ATT), row(W_ATT),
                  pl.BlockSpec((tr, dm), lambda i: (i, ga_blk)),
                  pl.BlockSpec((tr, dm), lambda i: (i, ga_blk + 1)),
                  row(dm), full(w_oa), full(w_ob), full(w_out), full(g1)],
        out_specs=[row(dm)] * 5,
        out_shape=[jax.ShapeDtypeStruct((lp, dm), BF16)] * 3 + [jax.ShapeDtypeStruct((lp, dm), F32)] * 2,
        compiler_params=_cparams(("parallel",)),
    )(o_a, o_b, proj, proj, h0, w_oa, w_ob, w_out, g1)


def _merge_bwd(dh1, mixed, g1, w_out, proj, y_a, y_b, w_oa, w_ob, name):
    lp, dm = dh1.shape
    tr = TQ
    ga_blk = (6 * W_ATT) // dm

    def body(dh_ref, mx_ref, g1_ref, wout_ref, ga_ref, gb_ref, ya_ref, yb_ref, woa_ref, wob_ref,
             dmx_ref, dya_ref, dyb_ref, dga_ref, dgb_ref, doa_ref, dob_ref, dg1_ref):
        i = pl.program_id(0)
        dn = dh_ref[...]
        x = mx_ref[...]
        r = _rms(x)
        yhat = x * r
        part = jnp.sum(dn * yhat, axis=0, keepdims=True)
        dyh = dn * g1_ref[...]
        dmx = (r * (dyh - yhat * jnp.mean(dyh * yhat, axis=-1, keepdims=True))).astype(BF16)
        dmx_ref[...] = dmx
        dm_ = _dot_nt(dmx, wout_ref[...])
        sa = jax.nn.sigmoid(ga_ref[...].astype(F32))
        sb = jax.nn.sigmoid(gb_ref[...].astype(F32))
        dya = (dm_ * sa).astype(BF16)
        dyb = (dm_ * sb).astype(BF16)
        dya_ref[...] = dya
        dyb_ref[...] = dyb
        dga_ref[...] = (dm_ * ya_ref[...].astype(F32) * sa * (1.0 - sa)).astype(BF16)
        dgb_ref[...] = (dm_ * yb_ref[...].astype(F32) * sb * (1.0 - sb)).astype(BF16)
        doa_ref[...] = _dot_nt(dya, woa_ref[...]).astype(BF16)
        dob_ref[...] = _dot_nt(dyb, wob_ref[...]).astype(BF16)

        @pl.when(i == 0)
        def _():
            dg1_ref[...] = part

        @pl.when(i > 0)
        def _():
            dg1_ref[...] += part

    row = lambda w: pl.BlockSpec((tr, w), lambda i: (i, 0))
    full = lambda a: pl.BlockSpec(a.shape, lambda i: (0, 0))
    return pl.pallas_call(
        body, name=name, grid=(lp // tr,),
        in_specs=[row(dm), row(dm), full(g1), full(w_out),
                  pl.BlockSpec((tr, dm), lambda i: (i, ga_blk)),
                  pl.BlockSpec((tr, dm), lambda i: (i, ga_blk + 1)),
                  row(dm), row(dm), full(w_oa), full(w_ob)],
        out_specs=[row(dm)] * 5 + [row(W_ATT)] * 2 + [pl.BlockSpec((1, dm), lambda i: (0, 0))],
        out_shape=[jax.ShapeDtypeStruct((lp, dm), BF16)] * 5 + [jax.ShapeDtypeStruct((lp, W_ATT), BF16)] * 2
        + [jax.ShapeDtypeStruct((1, dm), F32)],
        compiler_params=_cparams(("arbitrary",)),
    )(dh1, mixed, g1, w_out, proj, proj, y_a, y_b, w_oa, w_ob)


_GELU_C = math.sqrt(2.0 / math.pi)
_GELU_A = 0.044715


def _gelu(x):
    t = jnp.tanh(_GELU_C * (x + _GELU_A * x * x * x))
    return 0.5 * x * (1.0 + t), t


CW = 256


def _taps(cur_ref, prev_ref, first, c0):
    cur = cur_ref[:, c0:c0 + CW].astype(F32)
    p1 = jnp.where(first, 0.0, prev_ref[HALO - 1:HALO, c0:c0 + CW].astype(F32))
    p2 = jnp.where(first, 0.0, prev_ref[HALO - 2:HALO - 1, c0:c0 + CW].astype(F32))
    row = lax.broadcasted_iota(jnp.int32, cur.shape, 0)
    x1 = jnp.where(row == 0, p1, pltpu.roll(cur, 1, 0))
    x2 = jnp.where(row == 0, p2, jnp.where(row == 1, p1, pltpu.roll(cur, 2, 0)))
    return cur, x1, x2


def _conv_at(cur_ref, prev_ref, w_ref, b_ref, first, c0):
    cur, x1, x2 = _taps(cur_ref, prev_ref, first, c0)
    cols = slice(c0, c0 + CW)
    u = b_ref[:, cols] + w_ref[0:1, cols] * x2 + w_ref[1:2, cols] * x1 + w_ref[2:3, cols] * cur
    return u, (x2, x1, cur)


def _up_specs(tr, width):
    per = tr // HALO
    return [pl.BlockSpec((tr, width), lambda i: (i, 0)),
            pl.BlockSpec((HALO, width), lambda i: (jnp.maximum(i * per - 1, 0), 0))]


def _convgate_fwd(up, conv_w, conv_b, name):
    lp, c2 = up.shape
    tr = TCONV

    def body(cur_ref, prev_ref, w_ref, b_ref, a_ref):
        first = pl.program_id(0) == 0
        for c0 in range(0, D_FF, CW):
            ug, _ = _conv_at(cur_ref, prev_ref, w_ref, b_ref, first, c0)
            uv, _ = _conv_at(cur_ref, prev_ref, w_ref, b_ref, first, D_FF + c0)
            gel, _ = _gelu(ug)
            a_ref[:, c0:c0 + CW] = (gel * uv).astype(BF16)

    return pl.pallas_call(
        body, name=name, grid=(lp // tr,),
        in_specs=_up_specs(tr, c2) + [pl.BlockSpec((3, c2), lambda i: (0, 0)),
                                      pl.BlockSpec((1, c2), lambda i: (0, 0))],
        out_specs=pl.BlockSpec((tr, D_FF), lambda i: (i, 0)),
        out_shape=jax.ShapeDtypeStruct((lp, D_FF), BF16),
        compiler_params=_cparams(("parallel",)),
    )(up, up, conv_w, conv_b)


def _convgate_bwd(up, da, conv_w, conv_b, name):
    lp, c2 = up.shape
    tr = TCONV

    def body(cur_ref, prev_ref, da_ref, w_ref, b_ref, du_ref, dw_ref, db_ref):
        i = pl.program_id(0)
        first = i == 0

        @pl.when(first)
        def _():
            dw_ref[...] = jnp.zeros_like(dw_ref)
            db_ref[...] = jnp.zeros_like(db_ref)

        for c0 in range(0, D_FF, CW):
            ug, taps_g = _conv_at(cur_ref, prev_ref, w_ref, b_ref, first, c0)
            uv, taps_v = _conv_at(cur_ref, prev_ref, w_ref, b_ref, first, D_FF + c0)
            gel, t = _gelu(ug)
            dgel = 0.5 * (1.0 + t) + 0.5 * ug * (1.0 - t * t) * _GELU_C * (1.0 + 3.0 * _GELU_A * ug * ug)
            da_ = da_ref[:, c0:c0 + CW].astype(F32)
            for base, du, taps in ((c0, da_ * uv * dgel, taps_g), (D_FF + c0, da_ * gel, taps_v)):
                cols = slice(base, base + CW)
                du_ref[:, cols] = du.astype(BF16)
                for tap in range(3):
                    dw_ref[tap:tap + 1, cols] += jnp.sum(du * taps[tap], axis=0, keepdims=True)
                db_ref[:, cols] += jnp.sum(du, axis=0, keepdims=True)

    return pl.pallas_call(
        body, name=name, grid=(lp // tr,),
        in_specs=_up_specs(tr, c2) + [pl.BlockSpec((tr, D_FF), lambda i: (i, 0)),
                                      pl.BlockSpec((3, c2), lambda i: (0, 0)),
                                      pl.BlockSpec((1, c2), lambda i: (0, 0))],
        out_specs=[pl.BlockSpec((tr, c2), lambda i: (i, 0)),
                   pl.BlockSpec((3, c2), lambda i: (0, 0)),
                   pl.BlockSpec((1, c2), lambda i: (0, 0))],
        out_shape=[jax.ShapeDtypeStruct((lp, c2), BF16), jax.ShapeDtypeStruct((3, c2), F32),
                   jax.ShapeDtypeStruct((1, c2), F32)],
        compiler_params=_cparams(("arbitrary",)),
    )(up, up, da, conv_w, conv_b)


def _conv_transpose(du, conv_w, name):
    lp, c2 = du.shape
    tr = TCONV
    per = tr // HALO
    n_halo = lp // HALO
    nt = lp // tr

    def body(cur_ref, nxt_ref, w_ref, o_ref):
        last = pl.program_id(0) == nt - 1
        for c0 in range(0, c2, CW):
            cols = slice(c0, c0 + CW)
            cur = cur_ref[:, cols].astype(F32)
            n0 = jnp.where(last, 0.0, nxt_ref[0:1, cols].astype(F32))
            n1 = jnp.where(last, 0.0, nxt_ref[1:2, cols].astype(F32))
            row = lax.broadcasted_iota(jnp.int32, cur.shape, 0)
            y1 = jnp.where(row == tr - 1, n0, pltpu.roll(cur, tr - 1, 0))
            y2 = jnp.where(row == tr - 1, n1, jnp.where(row == tr - 2, n0, pltpu.roll(cur, tr - 2, 0)))
            o_ref[:, cols] = (w_ref[2:3, cols] * cur + w_ref[1:2, cols] * y1 + w_ref[0:1, cols] * y2).astype(BF16)

    return pl.pallas_call(
        body, name=name, grid=(nt,),
        in_specs=[pl.BlockSpec((tr, c2), lambda i: (i, 0)),
                  pl.BlockSpec((HALO, c2), lambda i: (jnp.minimum((i + 1) * per, n_halo - 1), 0)),
                  pl.BlockSpec((3, c2), lambda i: (0, 0))],
        out_specs=pl.BlockSpec((tr, c2), lambda i: (i, 0)),
        out_shape=jax.ShapeDtypeStruct((lp, c2), BF16),
        compiler_params=_cparams(("parallel",)),
    )(du, du, conv_w)


def _down_loss(a, w_down, h1, g3, target, name):
    lp, dm = h1.shape
    tr = TQ

    def body(a_ref, w_ref, h1_ref, g_ref, t_ref, ffn_ref, dy_ref, ss_ref):
        i = pl.program_id(0)
        ffn = _dot(a_ref[...], w_ref[...])
        ffn_ref[...] = ffn
        h2 = h1_ref[...] + ffn * _rms(ffn) * g_ref[...]
        d = jnp.where(i > 0, h2 - t_ref[...], 0.0)
        dy_ref[...] = d * (1.0 / dm)
        part = jnp.sum(jnp.sum(d * d, axis=0, keepdims=True), axis=1, keepdims=True)

        @pl.when(i == 0)
        def _():
            ss_ref[...] = jnp.zeros_like(ss_ref)

        ss_ref[...] += part

    return pl.pallas_call(
        body, name=name, grid=(lp // tr,),
        in_specs=[pl.BlockSpec((tr, D_FF), lambda i: (i, 0)),
                  pl.BlockSpec(w_down.shape, lambda i: (0, 0)),
                  pl.BlockSpec((tr, dm), lambda i: (i, 0)),
                  pl.BlockSpec((1, dm), lambda i: (0, 0)),
                  pl.BlockSpec((tr, dm), lambda i: (jnp.maximum(i - 1, 0), 0))],
        out_specs=[pl.BlockSpec((tr, dm), lambda i: (i, 0)),
                   pl.BlockSpec((tr, dm), lambda i: (i, 0)),
                   pl.BlockSpec((8, LANES), lambda i: (0, 0))],
        out_shape=[jax.ShapeDtypeStruct((lp, dm), F32), jax.ShapeDtypeStruct((lp, dm), F32),
                   jax.ShapeDtypeStruct((8, LANES), F32)],
        compiler_params=_cparams(("arbitrary",)),
    )(a, w_down, h1, g3, target)


def _down_bwd(dy, ffn, g3, w_down, name):
    lp, dm = dy.shape
    tr = TQ

    def body(dy_ref, f_ref, g_ref, w_ref, dffn_ref, da_ref, dg_ref):
        i = pl.program_id(0)
        dn = dy_ref[...]
        x = f_ref[...]
        r = _rms(x)
        yhat = x * r
        part = jnp.sum(dn * yhat, axis=0, keepdims=True)
        dyh = dn * g_ref[...]
        dffn = (r * (dyh - yhat * jnp.mean(dyh * yhat, axis=-1, keepdims=True))).astype(BF16)
        dffn_ref[...] = dffn
        da_ref[...] = _dot_nt(dffn, w_ref[...]).astype(BF16)

        @pl.when(i == 0)
        def _():
            dg_ref[...] = part

        @pl.when(i > 0)
        def _():
            dg_ref[...] += part

    return pl.pallas_call(
        body, name=name, grid=(lp // tr,),
        in_specs=[pl.BlockSpec((tr, dm), lambda i: (i, 0)),
                  pl.BlockSpec((tr, dm), lambda i: (i, 0)),
                  pl.BlockSpec((1, dm), lambda i: (0, 0)),
                  pl.BlockSpec(w_down.shape, lambda i: (0, 0))],
        out_specs=[pl.BlockSpec((tr, dm), lambda i: (i, 0)),
                   pl.BlockSpec((tr, D_FF), lambda i: (i, 0)),
                   pl.BlockSpec((1, dm), lambda i: (0, 0))],
        out_shape=[jax.ShapeDtypeStruct((lp, dm), BF16), jax.ShapeDtypeStruct((lp, D_FF), BF16),
                   jax.ShapeDtypeStruct((1, dm), F32)],
        compiler_params=_cparams(("arbitrary",)),
    )(dy, ffn, g3, w_down)


def _pair_specs(lp, base):
    return [pl.BlockSpec((TQ, LANES), lambda p, i: (i, base + p)),
            pl.BlockSpec((lp, LANES), lambda p, i: (0, base + N_PAIRS + p)),
            pl.BlockSpec((lp, LANES), lambda p, i: (0, base + 2 * N_PAIRS + p))]


def _col_spec():
    return pl.BlockSpec((None, 2, TQ, 1), lambda p, i: (p, 0, i, 0))


def _rowvec_spec(nb):
    return pl.BlockSpec((None, 2, nb, 1, TQ), lambda p, i: (p, 0, 0, 0, 0))


def _tile_spec():
    return pl.BlockSpec((TQ, LANES), lambda p, i: (i, p))


RC = 64
T2 = 2 * TQ


def _stack_heads(x, scale=None):
    lane = lax.broadcasted_iota(jnp.int32, x.shape, 1)
    zero = jnp.zeros_like(x)
    x2 = jnp.concatenate([jnp.where(lane < HEAD_DIM, x, zero), jnp.where(lane >= HEAD_DIM, x, zero)], axis=0)
    return x2 if scale is None else x2 * scale


def _unstack_heads(x2):
    lane = lax.broadcasted_iota(jnp.int32, (TQ, LANES), 1)
    return jnp.where(lane < HEAD_DIM, x2[:TQ], x2[TQ:])


def _stack_cols(ref):
    return jnp.concatenate([ref[0], ref[1]], axis=0)


def _chunk_valid(i, j, r, strict):
    qpos = i * TQ + (r % TQ) + lax.broadcasted_iota(jnp.int32, (RC, TQ), 0)
    kpos = j * TQ + lax.broadcasted_iota(jnp.int32, (RC, TQ), 1)
    causal = (kpos < qpos) if strict else (kpos <= qpos)
    return causal & (kpos >= PAD)


def _walk_tiles(i, step, reverse):
    first, last = (i, 0) if reverse else (0, i)
    step(first, True)
    _between_in_pairs(i, lambda j, nxt: step(j, False), reverse)

    @pl.when(i > 0)
    def _():
        step(last, True)


def _between_in_pairs(i, step, reverse):
    tile = (lambda t: i - 1 - t) if reverse else (lambda t: t + 1)
    after = (lambda j: j - 1) if reverse else (lambda j: j + 1)
    n = jnp.maximum(i - 1, 0)

    def pair(u, c):
        j = tile(2 * u)
        step(j, after(j))
        step(after(j), after(after(j)))
        return c

    lax.fori_loop(0, n // 2, pair, 0)

    @pl.when(n % 2 == 1)
    def _():
        j = tile(n - 1)
        step(j, after(j))


_HALF = (slice(0, TQ), slice(TQ, T2))


def _walk_tiles_lead(i, lead, step, reverse):
    first, last = (i, 0) if reverse else (0, i)
    lead(first, 0)
    lead(first, 1)
    step(first, jnp.maximum(i - 1, 0) if reverse else jnp.minimum(1, i), True)

    _between_in_pairs(i, lambda j, nxt: step(j, nxt, False), reverse)

    @pl.when(i > 0)
    def _():
        step(last, None, True)


def _krows(j):
    return pl.ds(pl.multiple_of(j * TQ, TQ), TQ)


def _fox_fwd(proj, crow, name):
    lp = proj.shape[0]
    nb = lp // TQ

    def body(q_ref, k_ref, v_ref, cr_ref, o_ref, lse_ref, m_ref, acc_ref, p_ref):
        i = pl.program_id(1)
        q2 = _stack_heads(q_ref[...], 0.125)
        m_ref[...] = jnp.full(m_ref.shape, NEG, F32)
        acc_ref[...] = jnp.zeros_like(acc_ref)
        lane = lax.broadcasted_iota(jnp.int32, (TQ, LANES), 1)

        def step(j, masked):
            rows_j = _krows(j)
            s = _dot_nt(q2, k_ref[rows_j, :])
            v = v_ref[rows_j, :]
            one = jnp.ones_like(v)
            v_heads = (jnp.where(lane < HEAD_DIM, v, one), jnp.where(lane >= HEAD_DIM, v, one))
            for r in range(0, T2, RC):
                rows = slice(r, r + RC)
                s_c = s[rows] - cr_ref[r // TQ, j]
                if masked:
                    s_c = jnp.where(_chunk_valid(i, j, r, False), s_c, NEG)
                s0, s1 = s_c[:, :LANES], s_c[:, LANES:]
                m_old = m_ref[rows]
                m_new = jnp.maximum(m_old, jnp.max(jnp.maximum(s0, s1), axis=-1, keepdims=True))
                m_ref[rows] = m_new
                acc_ref[rows] = jnp.exp(m_old - m_new) * acc_ref[rows]
                p_ref[rows, :LANES] = jnp.exp(s0 - m_new).astype(BF16)
                p_ref[rows, LANES:] = jnp.exp(s1 - m_new).astype(BF16)
            for h in range(2):
                acc_ref[_HALF[h]] += _dot(p_ref[_HALF[h]], v_heads[h])

        _walk_tiles(i, step, reverse=False)
        acc = acc_ref[...]
        m = m_ref[...]
        outs = []
        for h in range(2):
            a_h = acc[_HALF[h]]
            l = a_h[:, HEAD_DIM:HEAD_DIM + 1] if h == 0 else a_h[:, 0:1]
            lse_ref[h] = m[_HALF[h]][:, 0:1] + jnp.log(l)
            outs.append(a_h / l)
        o_ref[...] = jnp.where(lane < HEAD_DIM, outs[0], outs[1]).astype(BF16)

    return pl.pallas_call(
        body, name=name, grid=(N_PAIRS, nb),
        in_specs=_pair_specs(lp, 0) + [_rowvec_spec(nb)],
        out_specs=[_tile_spec(), _col_spec()],
        out_shape=[jax.ShapeDtypeStruct((lp, W_ATT), BF16),
                   jax.ShapeDtypeStruct((N_PAIRS, 2, lp, 1), F32)],
        scratch_shapes=[pltpu.VMEM((T2, LANES), F32), pltpu.VMEM((T2, LANES), F32), pltpu.VMEM((T2, TQ), BF16)],
        compiler_params=_cparams(("parallel", "arbitrary")),
    )(proj, proj, proj, crow)


def _fox_bwd(proj, do, o, lse, crow, name):
    lp = proj.shape[0]
    nb = lp // TQ

    def body(q_ref, k_ref, v_ref, do_ref, o_ref, lse_ref, cr_ref,
             dq_ref, dk_ref, dv_ref, dcs_ref, dct_ref,
             dk_acc, dv_acc, dq_acc, dct_acc, p_ref, ds_ref, s_ref, dp_ref):
        i = pl.program_id(1)

        @pl.when(i == 0)
        def _():
            dk_acc[...] = jnp.zeros_like(dk_acc)
            dv_acc[...] = jnp.zeros_like(dv_acc)
            dcs_ref[...] = jnp.zeros_like(dcs_ref)

        dq_acc[...] = jnp.zeros_like(dq_acc)
        dct_acc[...] = jnp.zeros_like(dct_acc)
        do_ = do_ref[...]
        q2 = _stack_heads(q_ref[...], 0.125)
        do2 = _stack_heads(do_)
        prod = do_.astype(F32) * o_ref[...].astype(F32)
        lane = lax.broadcasted_iota(jnp.int32, prod.shape, 1)
        delta2 = jnp.concatenate(
            [jnp.sum(jnp.where(lane < HEAD_DIM, prod, 0.0), axis=-1, keepdims=True),
             jnp.sum(jnp.where(lane >= HEAD_DIM, prod, 0.0), axis=-1, keepdims=True)], axis=0)
        lse2 = _stack_cols(lse_ref)

        def lead(j, h):
            rows_j = _krows(j)
            s_ref[_HALF[h]] = _dot_nt(q2[_HALF[h]], k_ref[rows_j, :])
            dp_ref[_HALF[h]] = _dot_nt(do2[_HALF[h]], v_ref[rows_j, :])

        def step(j, nxt, masked):
            rows_j = _krows(j)
            k = k_ref[rows_j, :]
            for h in range(2):
                cs = jnp.zeros((1, TQ), F32)
                for r in range(h * TQ, (h + 1) * TQ, RC):
                    rows = slice(r, r + RC)
                    p = jnp.exp(s_ref[rows] - cr_ref[h, j] - lse2[rows])
                    if masked:
                        p = jnp.where(_chunk_valid(i, j, r, False), p, 0.0)
                    ds = p * (dp_ref[rows] - delta2[rows])
                    p_ref[rows] = p.astype(BF16)
                    ds_ref[rows] = ds.astype(BF16)
                    dct_acc[rows] += jnp.sum(ds, axis=-1, keepdims=True)
                    cs = cs + jnp.sum(ds, axis=0, keepdims=True)
                dcs_ref[h, j] -= cs
                if nxt is not None:
                    lead(nxt, h)
                dsb = ds_ref[_HALF[h]]
                dq_acc[_HALF[h]] += _dot(dsb, k)
                dk_acc[rows_j, :] += _dot_tn(dsb, q2[_HALF[h]])
                dv_acc[rows_j, :] += _dot_tn(p_ref[_HALF[h]], do2[_HALF[h]])

        _walk_tiles_lead(i, lead, step, reverse=False)
        dct = dct_acc[...]
        dct_ref[0] = dct[:TQ]
        dct_ref[1] = dct[TQ:]
        dq_ref[...] = (_unstack_heads(dq_acc[...]) * 0.125).astype(BF16)

        @pl.when(i == nb - 1)
        def _():
            dk_ref[...] = dk_acc[...].astype(BF16)
            dv_ref[...] = dv_acc[...].astype(BF16)

    whole = pl.BlockSpec((lp, LANES), lambda p, i: (0, p))
    return pl.pallas_call(
        body, name=name, grid=(N_PAIRS, nb),
        in_specs=_pair_specs(lp, 0) + [_tile_spec(), _tile_spec(), _col_spec(), _rowvec_spec(nb)],
        out_specs=[_tile_spec(), whole, whole, _rowvec_spec(nb), _col_spec()],
        out_shape=[jax.ShapeDtypeStruct((lp, W_ATT), BF16)] * 3
        + [jax.ShapeDtypeStruct((N_PAIRS, 2, nb, 1, TQ), F32), jax.ShapeDtypeStruct((N_PAIRS, 2, lp, 1), F32)],
        scratch_shapes=[pltpu.VMEM((lp, LANES), F32), pltpu.VMEM((lp, LANES), F32),
                        pltpu.VMEM((T2, LANES), F32), pltpu.VMEM((T2, 1), F32),
                        pltpu.VMEM((T2, TQ), BF16), pltpu.VMEM((T2, TQ), BF16),
                        pltpu.VMEM((T2, TQ), F32), pltpu.VMEM((T2, TQ), F32)],
        compiler_params=_cparams(("parallel", "arbitrary")),
    )(proj, proj, proj, do, o, lse, crow)


def _sb_fwd(proj, name):
    lp = proj.shape[0]
    nb = lp // TQ
    tsuf = _tri(TQ, lambda r, c: r > c)

    def body(q_ref, k_ref, v_ref, t_ref, o_ref, lt_ref, run_ref, acc_ref, zl_ref, hl_ref, a_ref, z_ref):
        i = pl.program_id(1)
        q2 = _stack_heads(q_ref[...], 0.125)
        run_ref[...] = jnp.zeros_like(run_ref)
        acc_ref[...] = jnp.zeros_like(acc_ref)

        def lead(j, h):
            z_ref[_HALF[h]] = _dot_nt(q2[_HALF[h]], k_ref[_krows(j), :])

        def step(j, nxt, masked):
            t = t_ref[...]
            v = v_ref[_krows(j), :]
            later = []
            for h in range(2):
                for r in range(h * TQ, (h + 1) * TQ, RC):
                    rows = slice(r, r + RC)
                    z_c = z_ref[rows]
                    lk = _log_keep(z_c)
                    if masked:
                        lk = jnp.where(_chunk_valid(i, j, r, True), lk, 0.0)
                    hi, lo = _split2(lk)
                    hl_ref[rows, :TQ] = hi
                    hl_ref[rows, TQ:] = lo
                    zl_ref[rows] = z_c + lk + run_ref[rows]
                    run_ref[rows] += jnp.sum(lk, axis=-1, keepdims=True)
                if nxt is not None:
                    lead(nxt, h)
                later.append(_dot(hl_ref[_HALF[h]], t))
            for h in range(2):
                for r in range(0, TQ, RC):
                    rows = slice(h * TQ + r, h * TQ + r + RC)
                    a = jnp.exp(zl_ref[rows] + later[h][r:r + RC])
                    if masked:
                        a = jnp.where(_chunk_valid(i, j, h * TQ + r, True), a, 0.0)
                    a_ref[rows] = a.astype(BF16)
                acc_ref[_HALF[h]] += _dot(a_ref[_HALF[h]], v)

        _walk_tiles_lead(i, lead, step, reverse=True)
        run = run_ref[...]
        lt_ref[0] = run[:TQ]
        lt_ref[1] = run[TQ:]
        o_ref[...] = _unstack_heads(acc_ref[...]).astype(BF16)

    base = 3 * N_PAIRS
    return pl.pallas_call(
        body, name=name, grid=(N_PAIRS, nb),
        in_specs=_pair_specs(lp, base) + [pl.BlockSpec((2 * TQ, TQ), lambda p, i: (0, 0))],
        out_specs=[_tile_spec(), _col_spec()],
        out_shape=[jax.ShapeDtypeStruct((lp, W_ATT), BF16),
                   jax.ShapeDtypeStruct((N_PAIRS, 2, lp, 1), F32)],
        scratch_shapes=[pltpu.VMEM((T2, 1), F32), pltpu.VMEM((T2, LANES), F32), pltpu.VMEM((T2, TQ), F32),
                        pltpu.VMEM((T2, 2 * TQ), BF16), pltpu.VMEM((T2, TQ), BF16), pltpu.VMEM((T2, TQ), F32)],
        compiler_params=_cparams(("parallel", "arbitrary")),
    )(proj, proj, proj, jnp.concatenate([tsuf, tsuf], axis=0))


def _sb_bwd(proj, do, ltot, name):
    lp = proj.shape[0]
    nb = lp // TQ
    tincl = _tri(TQ, lambda r, c: r <= c)
    texcl = _tri(TQ, lambda r, c: r < c)

    def body(q_ref, k_ref, v_ref, do_ref, lt_ref, ti_ref, te_ref, dq_ref, dk_ref, dv_ref,
             dk_acc, dv_acc, dq_acc, pc_ref, gc_ref, zl_ref, keep_ref, g_ref, z_ref, da_ref,
             hl_ref, gb_ref, a_ref, dz_ref):
        i = pl.program_id(1)

        @pl.when(i == 0)
        def _():
            dk_acc[...] = jnp.zeros_like(dk_acc)
            dv_acc[...] = jnp.zeros_like(dv_acc)

        dq_acc[...] = jnp.zeros_like(dq_acc)
        gc_ref[...] = jnp.zeros_like(gc_ref)
        pc_ref[...] = _stack_cols(lt_ref)
        q2 = _stack_heads(q_ref[...], 0.125)
        do2 = _stack_heads(do_ref[...])

        def lead(j, h):
            rows_j = _krows(j)
            z_ref[_HALF[h]] = _dot_nt(q2[_HALF[h]], k_ref[rows_j, :])
            da_ref[_HALF[h]] = _dot_nt(do2[_HALF[h]], v_ref[rows_j, :])

        def step(j, nxt, masked):
            rows_j = _krows(j)
            k = k_ref[rows_j, :]
            ti = ti_ref[...]
            te = te_ref[...]
            upto, before = [], []
            for h in range(2):
                for r in range(h * TQ, (h + 1) * TQ, RC):
                    rows = slice(r, r + RC)
                    z_c = z_ref[rows]
                    lk = _log_keep(z_c)
                    if masked:
                        lk = jnp.where(_chunk_valid(i, j, r, True), lk, 0.0)
                    hi, lo = _split2(lk)
                    hl_ref[rows, :TQ] = hi
                    hl_ref[rows, TQ:] = lo
                    keep_ref[rows] = jnp.exp(lk)
                    zl_ref[rows] = z_c + lk + pc_ref[rows]
                    pc_ref[rows] -= jnp.sum(lk, axis=-1, keepdims=True)
                upto.append(_dot(hl_ref[_HALF[h]], ti))
            for h in range(2):
                for r in range(0, TQ, RC):
                    rows = slice(h * TQ + r, h * TQ + r + RC)
                    a = jnp.exp(zl_ref[rows] - upto[h][r:r + RC])
                    if masked:
                        a = jnp.where(_chunk_valid(i, j, h * TQ + r, True), a, 0.0)
                    g = a * da_ref[rows]
                    a_ref[rows] = a.astype(BF16)
                    g_ref[rows] = g
                    gb_ref[rows] = g.astype(BF16)
                if nxt is not None:
                    lead(nxt, h)
                before.append(_dot(gb_ref[_HALF[h]], te))
            for h in range(2):
                for r in range(0, TQ, RC):
                    rows = slice(h * TQ + r, h * TQ + r + RC)
                    g = g_ref[rows]
                    keep = keep_ref[rows]
                    dz = g * keep - (1.0 - keep) * (gc_ref[rows] + before[h][r:r + RC])
                    if masked:
                        dz = jnp.where(_chunk_valid(i, j, h * TQ + r, True), dz, 0.0)
                    dz_ref[rows] = dz.astype(BF16)
                    gc_ref[rows] += jnp.sum(g, axis=-1, keepdims=True)
                dzb = dz_ref[_HALF[h]]
                dq_acc[_HALF[h]] += _dot(dzb, k)
                dk_acc[rows_j, :] += _dot_tn(dzb, q2[_HALF[h]])
                dv_acc[rows_j, :] += _dot_tn(a_ref[_HALF[h]], do2[_HALF[h]])

        _walk_tiles_lead(i, lead, step, reverse=False)
        dq_ref[...] = (_unstack_heads(dq_acc[...]) * 0.125).astype(BF16)

        @pl.when(i == nb - 1)
        def _():
            dk_ref[...] = dk_acc[...].astype(BF16)
            dv_ref[...] = dv_acc[...].astype(BF16)

    base = 3 * N_PAIRS
    whole = pl.BlockSpec((lp, LANES), lambda p, i: (0, p))
    tri = lambda rows: pl.BlockSpec((rows, TQ), lambda p, i: (0, 0))
    wide = lambda dt: pltpu.VMEM((T2, TQ), dt)
    return pl.pallas_call(
        body, name=name, grid=(N_PAIRS, nb),
        in_specs=_pair_specs(lp, base) + [_tile_spec(), _col_spec(), tri(2 * TQ), tri(TQ)],
        out_specs=[_tile_spec(), whole, whole],
        out_shape=[jax.ShapeDtypeStruct((lp, W_ATT), BF16)] * 3,
        scratch_shapes=[pltpu.VMEM((lp, LANES), F32), pltpu.VMEM((lp, LANES), F32),
                        pltpu.VMEM((T2, LANES), F32), pltpu.VMEM((T2, 1), F32), pltpu.VMEM((T2, 1), F32),
                        wide(F32), wide(F32), wide(F32), wide(F32), wide(F32),
                        pltpu.VMEM((T2, 2 * TQ), BF16), wide(BF16), wide(BF16), wide(BF16)],
        compiler_params=_cparams(("parallel", "arbitrary")),
    )(proj, proj, proj, do, ltot, jnp.concatenate([tincl, tincl], axis=0), texcl)


def _local_step(x, target, meta, gains, w_in, b_forget, w_o_fox, w_o_sb, w_out, w_up, conv_w, conv_b, w_down):
    seq, dm = x.shape
    lp = PAD + N_META + seq
    nb = lp // TQ
    s = [W_ATT, W_ATT, W_ATT, 8, W_ATT, W_ATT, W_ATT, dm, dm]
    off = [sum(s[:i]) for i in range(len(s) + 1)]
    cols = lambda i: w_in[:, off[i]:off[i + 1]]
    w1 = jnp.concatenate([cols(0), cols(1), cols(2), cols(4), cols(5), cols(6), cols(7), cols(8)], axis=1)
    wf = jnp.pad(cols(3), ((0, 0), (0, LANES - 8)))
    n1 = w1.shape[1]
    ncat = n1 + 512
    w_cat = jnp.concatenate([w1, wf, jnp.zeros((dm, ncat - n1 - LANES), BF16)], axis=1)
    bf = jnp.pad(b_forget.reshape(1, 8), ((0, 0), (0, LANES - 8)))
    g = [gains[i].reshape(1, dm) for i in range(4)]
    cb = conv_b.reshape(1, -1)

    h0 = jnp.concatenate([jnp.zeros((PAD, dm), F32), meta, x], axis=0)

    proj, xn1 = _rms_mm(h0, g[0], w1, "in_proj")
    logf = _logf(xn1, wf, bf, "log_forget")
    c = _cumsum_rows(logf, "forget_cumsum")
    crow = c[:, :8].T.reshape(N_PAIRS, 2, nb, 1, TQ)
    o_a, lse = _fox_fwd(proj, crow, "fox_fwd")
    o_b, ltot = _sb_fwd(proj, "sb_fwd")
    y_a, y_b, m, mixed, h1 = _merge_fwd(o_a, o_b, proj, h0, w_o_fox, w_o_sb, w_out, g[1], "merge_fwd")
    up, xn3 = _rms_mm(h1, g[2], w_up, "up_proj")
    a = _convgate_fwd(up, conv_w, cb, "convgate_fwd")
    ffn, dy, ss = _down_loss(a, w_down, h1, g[3], target, "down_loss")

    dffn, da, dg3 = _down_bwd(dy, ffn, g[3], w_down, "down_bwd")
    d_w_down = _mm_tn(a, dffn, "dw_down")
    du, d_conv_w, d_conv_b = _convgate_bwd(up, da, conv_w, cb, "convgate_bwd")
    dup = _conv_transpose(du, conv_w, "conv_transpose")
    d_w_up = _mm_tn(xn3, dup, "dw_up")
    dh1, dg2 = _mm_rmsbwd(dup, w_up, h1, g[2], dy, "up_bwd")
    dmx, dya, dyb, dga, dgb, do_a, do_b, dg1 = _merge_bwd(
        dh1, mixed, g[1], w_out, proj, y_a, y_b, w_o_fox, w_o_sb, "merge_bwd")
    d_w_out = _mm_tn(m, dmx, "dw_out")
    d_w_o_fox = _mm_tn(o_a, dya, "dw_o_fox")
    d_w_o_sb = _mm_tn(o_b, dyb, "dw_o_sb")
    dq_a, dk_a, dv_a, dcs, dct = _fox_bwd(proj, do_a, o_a, lse, crow, "fox_bwd")
    dq_b, dk_b, dv_b = _sb_bwd(proj, do_b, ltot, "sb_bwd")
    dc = (dct.reshape(8, lp) + dcs.reshape(8, lp)).T
    df, db = _dlogf(jnp.pad(dc, ((0, 0), (0, LANES - 8))), logf, "forget_bwd")
    dcat = jnp.concatenate([dq_a, dk_a, dv_a, dq_b, dk_b, dv_b, dga, dgb, df.astype(BF16),
                            jnp.zeros((lp, ncat - n1 - LANES), BF16)], axis=1)
    d_w_cat = _mm_tn(xn1, dcat, "dw_in")
    dh0, dg0 = _mm_rmsbwd(dcat, w_cat, h0, g[0], dh1, "in_bwd")

    wc = lambda k: d_w_cat[:, k * W_ATT:(k + 1) * W_ATT]
    d_w_in = jnp.concatenate([wc(0), wc(1), wc(2), d_w_cat[:, n1:n1 + 8], wc(3), wc(4), wc(5),
                              d_w_cat[:, 6 * W_ATT:n1]], axis=1)
    d_gains = jnp.concatenate([dg0, dg1, dg2, dg3], axis=0)
    grads = (dh0[PAD:PAD + N_META], d_gains, d_w_in, db[0, :8], d_w_o_fox, d_w_o_sb, d_w_out,
             d_w_up, d_conv_w, d_conv_b[0], d_w_down)
    return ss[0, 0], dh0[PAD + N_META:], grads


def _rows(a, n_rows):
    flat = a.reshape(-1)
    return jnp.pad(flat, (0, n_rows * D_MODEL - flat.shape[0])).reshape(n_rows, D_MODEL)


_SMALL = (("meta", 4), ("gains", 1), ("conv_w", 5), ("b_forget", 1), ("conv_b", 6))


def _pack(meta, gains, w_in, b_forget, w_o_fox, w_o_sb, w_out, w_up, conv_w, conv_b, w_down):
    big = [a.reshape(-1, D_MODEL) for a in (w_in, w_o_fox, w_o_sb, w_out, w_up, w_down)]
    n_big = sum(a.shape[0] for a in big)
    small = [_rows(a, n) for a, (_, n) in zip((meta, gains, conv_w, b_forget, conv_b), _SMALL)]
    n_small = sum(n for _, n in _SMALL)
    z = lambda n: jnp.zeros((n, D_MODEL), big[0].dtype)
    return jnp.concatenate(big + [z(PK_BIG_ROWS - n_big)] + small
                           + [z(PK_ROWS - PK_BIG_ROWS - n_small)], axis=0)


def _unpack(p):
    def take(r0, shape):
        n = math.prod(shape)
        nr = -(-n // D_MODEL)
        return p[r0:r0 + nr].reshape(-1)[:n].reshape(shape), r0 + nr
    w_in, r = take(0, (1, 1024, 1282))
    w_o_fox, r = take(r, (1, 512, 256))
    w_o_sb, r = take(r, (1, 512, 256))
    w_out, r = take(r, (1, 256, 1024))
    w_up, r = take(r, (1, 1024, 1408))
    w_down, r = take(r, (1, 704, 1024))
    r = PK_BIG_ROWS
    meta, r = take(r, (16, 256))
    gains, r = take(r, (1, 4, 256))
    conv_w, r = take(r, (1, 3, 1408))
    b_forget, r = take(r, (1, 8))
    conv_b, r = take(r, (1, 5632))
    return meta, gains, w_in, b_forget, w_o_fox, w_o_sb, w_out, w_up, conv_w, conv_b, w_down


def _chip_peers():
    x, y, c = lax.axis_index("x"), lax.axis_index("y"), lax.axis_index("c")
    return [(x, 1 - y, c), (1 - x, y, c), (1 - x, 1 - y, c)]


def _all_gather_chips(arrays, name):
    n = len(arrays)

    def body(*refs):
        ins, outs = refs[:n], refs[n:2 * n]
        send_sems, recv_sems, local_sems = refs[2 * n:]
        x, y = lax.axis_index("x"), lax.axis_index("y")
        me = 2 * x + y
        peers = _chip_peers()
        copies = []
        for a in range(n):
            mine = pltpu.make_async_copy(ins[a], outs[a].at[me], local_sems.at[a])
            mine.start()
            copies.append(mine)
        remote = []
        for a in range(n):
            for j, peer in enumerate(peers):
                cp = pltpu.make_async_remote_copy(
                    src_ref=ins[a], dst_ref=outs[a].at[me],
                    send_sem=send_sems.at[3 * a + j], recv_sem=recv_sems.at[3 * a + j],
                    device_id=peer, device_id_type=MESH)
                cp.start()
                remote.append(cp)
        for cp in remote:
            cp.wait()
        for cp in copies:
            cp.wait()

    any_spec = pl.BlockSpec(memory_space=pl.ANY)
    return pl.pallas_call(
        body, name=name,
        in_specs=[any_spec] * n, out_specs=[any_spec] * n,
        out_shape=[jax.ShapeDtypeStruct((4,) + a.shape, a.dtype) for a in arrays],
        scratch_shapes=[pltpu.SemaphoreType.DMA((3 * n,)), pltpu.SemaphoreType.DMA((3 * n,)),
                        pltpu.SemaphoreType.DMA((n,))],
    )(*arrays)


def _scatter_chips(chunks, name):
    _, rows, cols = chunks.shape

    def body(in_ref, out_ref, send_sems, recv_sems):
        x, y = lax.axis_index("x"), lax.axis_index("y")
        targets = [2 * x + (1 - y), 2 * (1 - x) + y, 2 * (1 - x) + (1 - y)]
        remote = []
        for j, peer in enumerate(_chip_peers()):
            cp = pltpu.make_async_remote_copy(
                src_ref=in_ref.at[targets[j]], dst_ref=out_ref.at[j],
                send_sem=send_sems.at[j], recv_sem=recv_sems.at[j],
                device_id=peer, device_id_type=MESH)
            cp.start()
            remote.append(cp)
        for cp in remote:
            cp.wait()

    any_spec = pl.BlockSpec(memory_space=pl.ANY)
    return pl.pallas_call(
        body, name=name, in_specs=[any_spec], out_specs=any_spec,
        out_shape=jax.ShapeDtypeStruct((3, rows, cols), chunks.dtype),
        scratch_shapes=[pltpu.SemaphoreType.DMA((3,)), pltpu.SemaphoreType.DMA((3,))],
    )(chunks)


def _swap_half_rows(chunks, name):
    n, rows, cols = chunks.shape
    r2 = rows // 2

    def body(in_ref, out_ref, send_sem, recv_sem):
        x, y, c = lax.axis_index("x"), lax.axis_index("y"), lax.axis_index("c")
        cp = pltpu.make_async_remote_copy(
            src_ref=in_ref.at[:, pl.ds((1 - c) * r2, r2), :], dst_ref=out_ref,
            send_sem=send_sem, recv_sem=recv_sem, device_id=(x, y, 1 - c), device_id_type=MESH)
        cp.start()
        cp.wait()

    any_spec = pl.BlockSpec(memory_space=pl.ANY)
    return pl.pallas_call(
        body, name=name, in_specs=[any_spec], out_specs=any_spec,
        out_shape=jax.ShapeDtypeStruct((n, r2, cols), chunks.dtype),
        scratch_shapes=[pltpu.SemaphoreType.DMA, pltpu.SemaphoreType.DMA],
    )(chunks)


def _gather_halves(half, name):
    def body(in_ref, out_ref, send_sem, recv_sem):
        x, y, c = lax.axis_index("x"), lax.axis_index("y"), lax.axis_index("c")
        cp = pltpu.make_async_remote_copy(
            src_ref=in_ref, dst_ref=out_ref, send_sem=send_sem, recv_sem=recv_sem,
            device_id=(x, y, 1 - c), device_id_type=MESH)
        cp.start()
        cp.wait()

    any_spec = pl.BlockSpec(memory_space=pl.ANY)
    other = pl.pallas_call(
        body, name=name, in_specs=[any_spec], out_specs=any_spec,
        out_shape=jax.ShapeDtypeStruct(half.shape, half.dtype),
        scratch_shapes=[pltpu.SemaphoreType.DMA, pltpu.SemaphoreType.DMA],
    )(half)
    axis = half.ndim - 2
    return lax.cond(lax.axis_index("c") == 0,
                    lambda: jnp.concatenate([half, other], axis=axis),
                    lambda: jnp.concatenate([other, half], axis=axis))


def _add(a, b, name):
    n, rows, cols = a.shape

    def body(a_ref, b_ref, o_ref):
        o_ref[...] = a_ref[...] + b_ref[...]

    spec = pl.BlockSpec((None, PK_TILE, cols), lambda k, i: (k, i, 0))
    return pl.pallas_call(
        body, name=name, grid=(n, rows // PK_TILE),
        in_specs=[spec, spec], out_specs=spec,
        out_shape=jax.ShapeDtypeStruct(a.shape, F32),
        compiler_params=_cparams(("parallel", "parallel")),
    )(a, b)


def _chip_sum(chunks, recv, name):
    _, rows, cols = chunks.shape

    def body(own_ref, r_ref, o_ref):
        o_ref[...] = (own_ref[...] + r_ref[0]) + (r_ref[1] + r_ref[2])

    me = 2 * lax.axis_index("x") + lax.axis_index("y")
    own = lax.dynamic_index_in_dim(chunks, me, axis=0, keepdims=False)
    return pl.pallas_call(
        body, name=name, grid=(rows // PK_TILE,),
        in_specs=[pl.BlockSpec((PK_TILE, cols), lambda i: (i, 0)),
                  pl.BlockSpec((3, PK_TILE, cols), lambda i: (0, i, 0))],
        out_specs=pl.BlockSpec((PK_TILE, cols), lambda i: (i, 0)),
        out_shape=jax.ShapeDtypeStruct((rows, cols), F32),
        compiler_params=_cparams(("parallel",)),
    )(own, recv)


def _adamw(w, m, v, g, name):
    shape = w.shape
    rows, cols = math.prod(shape[:-1]), shape[-1]
    w, m, v, g = (a.reshape(rows, cols) for a in (w, m, v, g))
    tile = next((t for t in (256, 128, 64, 32, 16, 8) if rows % t == 0), rows)
    c1 = 1.0 - ADAM_B1 ** ADAM_STEP
    c2 = 1.0 - ADAM_B2 ** ADAM_STEP

    def body(w_ref, m_ref, v_ref, g_ref, d_ref, nm_ref, nv_ref):
        g = g_ref[...]
        nm = ADAM_B1 * m_ref[...] + (1.0 - ADAM_B1) * g
        nv = ADAM_B2 * v_ref[...] + (1.0 - ADAM_B2) * (g * g)
        nm_ref[...] = nm
        nv_ref[...] = nv
        d_ref[...] = -ADAM_LR * ((nm / c1) / (jnp.sqrt(nv / c2) + ADAM_EPS) + ADAM_WD * w_ref[...])

    spec = pl.BlockSpec((tile, cols), lambda i: (i, 0))
    outs = pl.pallas_call(
        body, name=name, grid=(rows // tile,),
        in_specs=[spec] * 4, out_specs=[spec] * 3,
        out_shape=[jax.ShapeDtypeStruct((rows, cols), F32)] * 3,
        compiler_params=_cparams(("parallel",)),
    )(w, m, v, g)
    return [o.reshape(shape) for o in outs]


def _full_weights(big, small):
    def gather(src, r0, shape, axis):
        n = math.prod(shape)
        nr = -(-n // D_MODEL)
        parts = [src[k, r0:r0 + nr].reshape(-1)[:n].reshape(shape) for k in range(4)]
        return jnp.concatenate(parts, axis=axis), r0 + nr
    w_in, r = gather(big, 0, (1024, 1282), 1)
    w_o_fox, r = gather(big, r, (512, 256), 1)
    w_o_sb, r = gather(big, r, (512, 256), 1)
    w_out, r = gather(big, r, (256, 1024), 0)
    w_up, r = gather(big, r, (1024, 1408), 1)
    w_down, r = gather(big, r, (704, 1024), 0)
    meta, r = gather(small, 0, (16, 256), 1)
    gains, r = gather(small, r, (4, 256), 1)
    conv_w, r = gather(small, r, (3, 1408), 1)
    return meta, gains, w_in, w_o_fox, w_o_sb, w_out, w_up, conv_w, w_down


def _chunks_for_chips(grads):
    d_meta, d_gains, d_w_in, d_b, d_w_o_fox, d_w_o_sb, d_w_out, d_w_up, d_conv_w, d_conv_b, d_w_down = grads
    out = []
    for k in range(4):
        col = lambda a, w: a[:, k * w:(k + 1) * w]
        row = lambda a, w: a[k * w:(k + 1) * w]
        out.append(_pack(col(d_meta, 256), col(d_gains, 256), col(d_w_in, 1282), d_b, col(d_w_o_fox, 256),
                         col(d_w_o_sb, 256), row(d_w_out, 256), col(d_w_up, 1408), col(d_conv_w, 1408),
                         d_conv_b, row(d_w_down, 704)))
    return jnp.stack(out, axis=0)


def kernel(x, meta_tokens, norm_gains, w_in, b_forget, w_o_fox, w_o_sb, w_out, w_up, conv_w, conv_b, w_down, loss_target, m_meta_tokens, m_norm_gains, m_w_in, m_b_forget, m_w_o_fox, m_w_o_sb, m_w_out, m_w_up, m_conv_w, m_conv_b, m_w_down, v_meta_tokens, v_norm_gains, v_w_in, v_b_forget, v_w_o_fox, v_w_o_sb, v_w_out, v_w_up, v_conv_w, v_conv_b, v_w_down):
    names = ("meta_tokens", "norm_gains", "w_in", "b_forget", "w_o_fox", "w_o_sb", "w_out", "w_up", "conv_w",
             "conv_b", "w_down")
    ws = (meta_tokens, norm_gains, w_in, b_forget, w_o_fox, w_o_sb, w_out, w_up, conv_w, conv_b, w_down)
    ms = (m_meta_tokens, m_norm_gains, m_w_in, m_b_forget, m_w_o_fox, m_w_o_sb, m_w_out, m_w_up, m_conv_w,
          m_conv_b, m_w_down)
    vs = (v_meta_tokens, v_norm_gains, v_w_in, v_b_forget, v_w_o_fox, v_w_o_sb, v_w_out, v_w_up, v_conv_w,
          v_conv_b, v_w_down)
    wp = _pack(meta_tokens, norm_gains[0], w_in[0], b_forget[0], w_o_fox[0], w_o_sb[0], w_out[0], w_up[0],
               conv_w[0], conv_b[0], w_down[0])

    hb = PK_BIG_ROWS // 2
    big_half = lax.dynamic_slice_in_dim(wp[:PK_BIG_ROWS].astype(BF16), lax.axis_index("c") * hb, hb, axis=0)
    big_half, small = _all_gather_chips(
        [big_half, wp[PK_BIG_ROWS:PK_BIG_ROWS + PK_SMALL_ROWS]], "gather_weights")
    big = _gather_halves(big_half, "gather_weight_halves")
    meta, gains, f_w_in, f_w_o_fox, f_w_o_sb, f_w_out, f_w_up, f_conv_w, f_w_down = _full_weights(big, small)

    ss, dx, grads = _local_step(x[0], loss_target[0], meta, gains, f_w_in, b_forget[0], f_w_o_fox, f_w_o_sb,
                                f_w_out, f_w_up, f_conv_w, conv_b[0], f_w_down)
    loss = lax.psum(0.5 * ss / D_MODEL, ("x", "y", "c"))

    chunks = _chunks_for_chips(grads)
    r2 = PK_ROWS // 2
    from_sibling = _swap_half_rows(chunks, "swap_halves")
    own = lax.dynamic_slice_in_dim(chunks, lax.axis_index("c") * r2, r2, axis=1)
    core_sum = _add(own, from_sibling, "core_sum")
    recv = _scatter_chips(core_sum, "scatter_grads")
    g = _gather_halves(_chip_sum(core_sum, recv, "chip_sum"), "gather_halves")
    gs = _unpack(g)
    steps = [_adamw(w, m, v, gp, "adamw_" + n) for n, w, m, v, gp in zip(names, ws, ms, vs, gs)]
    return (loss, dx[None], *gs, *[s[0] for s in steps], *[s[1] for s in steps], *[s[2] for s in steps])
```

```python
import functools
import math

import jax
import jax.numpy as jnp
from jax import lax
from jax.experimental import pallas as pl
from jax.experimental.pallas import tpu as pltpu

F32 = jnp.float32
BF16 = jnp.bfloat16

D_MODEL = 1024
N_META = 16
HEAD_DIM = 64
N_PAIRS = 4
W_ATT = 512
D_FF = 2816
EPS = 1e-6
NEG = -1e30
TQ = 256
PAD = TQ - N_META
TCONV = 128
HALO = 16
LANES = 128
VMEM_LIMIT = 56 * 1024 * 1024

ADAM_LR = 0.001
ADAM_B1 = 0.9
ADAM_B2 = 0.999
ADAM_EPS = 1e-08
ADAM_WD = 0.01
ADAM_STEP = 10

MESH = pl.DeviceIdType.MESH

PK_BIG_ROWS = 3936
PK_SMALL_ROWS = 24
PK_ROWS = 4096
PK_TILE = 128


def _cparams(sem, **kw):
    return pltpu.CompilerParams(dimension_semantics=sem, vmem_limit_bytes=VMEM_LIMIT, **kw)


def _row_tile(lp):
    return 768 if lp % 768 == 0 else 256


def _wide_tile(n):
    return next(t for t in (1408, 1280, 1024, 512, 256) if n % t == 0)


def _rms(x):
    return lax.rsqrt(jnp.mean(x * x, axis=-1, keepdims=True) + EPS)


def _log_sigmoid(x):
    return jnp.minimum(x, 0.0) - jnp.log(1.0 + jnp.exp(-jnp.abs(x)))


_LOG2E = 1.4426950408889634
_LN2 = 0.6931471805599453


def _log_keep(z):
    t = jnp.exp2(jnp.abs(z) * (-_LOG2E))
    return jnp.log2(1.0 + t) * (-_LN2) - jnp.maximum(z, 0.0)


def _split2(x):
    hi = x.astype(BF16)
    lo = (x - hi.astype(F32)).astype(BF16)
    return hi, lo


def _dot(a, b):
    return jnp.dot(a, b, preferred_element_type=F32)


def _dot_nt(a, b):
    return lax.dot_general(a, b, (((1,), (1,)), ((), ())), preferred_element_type=F32)


def _dot_tn(a, b):
    return lax.dot_general(a, b, (((0,), (0,)), ((), ())), preferred_element_type=F32)


def _rms_mm(h, g, w, name):
    lp, dm = h.shape
    n = w.shape[1]
    tr, tn = _row_tile(lp), _wide_tile(n)

    def body(h_ref, g_ref, w_ref, out_ref, xn_ref):
        @pl.when(pl.program_id(1) == 0)
        def _():
            x = h_ref[...]
            xn_ref[...] = (x * _rms(x) * g_ref[...]).astype(BF16)
        out_ref[...] = _dot(xn_ref[...], w_ref[...]).astype(BF16)

    return pl.pallas_call(
        body, name=name, grid=(lp // tr, n // tn),
        in_specs=[pl.BlockSpec((tr, dm), lambda i, j: (i, 0)),
                  pl.BlockSpec((1, dm), lambda i, j: (0, 0)),
                  pl.BlockSpec((dm, tn), lambda i, j: (0, j))],
        out_specs=[pl.BlockSpec((tr, tn), lambda i, j: (i, j)),
                   pl.BlockSpec((tr, dm), lambda i, j: (i, 0))],
        out_shape=[jax.ShapeDtypeStruct((lp, n), BF16), jax.ShapeDtypeStruct((lp, dm), BF16)],
        compiler_params=_cparams(("parallel", "arbitrary")),
    )(h, g, w)


def _mm_rmsbwd(dy, w, h, g, dh_in, name):
    lp, kd = dy.shape
    dm = w.shape[0]
    tr, tk = 384, _wide_tile(kd)
    nk = kd // tk

    def body(dy_ref, w_ref, h_ref, g_ref, dhin_ref, dh_ref, dg_ref, acc_ref):
        i, k = pl.program_id(0), pl.program_id(1)

        @pl.when(k == 0)
        def _():
            acc_ref[...] = jnp.zeros_like(acc_ref)

        acc_ref[...] += _dot_nt(dy_ref[...], w_ref[...])

        @pl.when(k == nk - 1)
        def _():
            dxn = acc_ref[...]
            x = h_ref[...]
            r = _rms(x)
            yhat = x * r
            part = jnp.sum(dxn * yhat, axis=0, keepdims=True)
            dyh = dxn * g_ref[...]
            dx = r * (dyh - yhat * jnp.mean(dyh * yhat, axis=-1, keepdims=True))
            dh_ref[...] = dhin_ref[...] + dx

            @pl.when(i == 0)
            def _():
                dg_ref[...] = part

            @pl.when(i > 0)
            def _():
                dg_ref[...] += part

    return pl.pallas_call(
        body, name=name, grid=(lp // tr, nk),
        in_specs=[pl.BlockSpec((tr, tk), lambda i, k: (i, k)),
                  pl.BlockSpec((dm, tk), lambda i, k: (0, k)),
                  pl.BlockSpec((tr, dm), lambda i, k: (i, 0)),
                  pl.BlockSpec((1, dm), lambda i, k: (0, 0)),
                  pl.BlockSpec((tr, dm), lambda i, k: (i, 0))],
        out_specs=[pl.BlockSpec((tr, dm), lambda i, k: (i, 0)),
                   pl.BlockSpec((1, dm), lambda i, k: (0, 0))],
        out_shape=[jax.ShapeDtypeStruct((lp, dm), F32), jax.ShapeDtypeStruct((1, dm), F32)],
        scratch_shapes=[pltpu.VMEM((tr, dm), F32)],
        compiler_params=_cparams(("arbitrary", "arbitrary")),
    )(dy, w, h, g, dh_in)


def _mm_tn(x, dy, name):
    lp, kd = x.shape
    n = dy.shape[1]
    tl = _row_tile(lp)
    tk = _wide_tile(kd)
    tn = _wide_tile(n)
    nl = lp // tl

    def body(x_ref, dy_ref, o_ref):
        @pl.when(pl.program_id(2) == 0)
        def _():
            o_ref[...] = jnp.zeros_like(o_ref)
        o_ref[...] += _dot_tn(x_ref[...], dy_ref[...])

    return pl.pallas_call(
        body, name=name, grid=(kd // tk, n // tn, nl),
        in_specs=[pl.BlockSpec((tl, tk), lambda a, b, l: (l, a)),
                  pl.BlockSpec((tl, tn), lambda a, b, l: (l, b))],
        out_specs=pl.BlockSpec((tk, tn), lambda a, b, l: (a, b)),
        out_shape=jax.ShapeDtypeStruct((kd, n), F32),
        compiler_params=_cparams(("parallel", "parallel", "arbitrary")),
    )(x, dy)


def _logf(xn, wf, bf, name):
    lp, dm = xn.shape
    tr = _row_tile(lp)

    def body(xn_ref, wf_ref, b_ref, o_ref):
        f = _dot(xn_ref[...], wf_ref[...]) + b_ref[...]
        row = pl.program_id(0) * tr + lax.broadcasted_iota(jnp.int32, f.shape, 0)
        lane = lax.broadcasted_iota(jnp.int32, f.shape, 1)
        o_ref[...] = jnp.where((row >= PAD) & (lane < 8), _log_sigmoid(f), 0.0)

    return pl.pallas_call(
        body, name=name, grid=(lp // tr,),
        in_specs=[pl.BlockSpec((tr, dm), lambda i: (i, 0)),
                  pl.BlockSpec((dm, LANES), lambda i: (0, 0)),
                  pl.BlockSpec((1, LANES), lambda i: (0, 0))],
        out_specs=pl.BlockSpec((tr, LANES), lambda i: (i, 0)),
        out_shape=jax.ShapeDtypeStruct((lp, LANES), F32),
        compiler_params=_cparams(("parallel",)),
    )(xn, wf, bf)


def _tri(n, rel):
    r = lax.broadcasted_iota(jnp.int32, (n, n), 0)
    c = lax.broadcasted_iota(jnp.int32, (n, n), 1)
    return rel(r, c).astype(BF16)


def _cumsum_rows(x, name):
    lp = x.shape[0]
    nb = lp // TQ
    tl = _tri(TQ, lambda r, c: c <= r)

    def body(x_ref, t_ref, o_ref):
        def step(b, carry):
            rows = pl.ds(pl.multiple_of(b * TQ, TQ), TQ)
            xb = x_ref[rows, :]
            hi = xb.astype(BF16)
            r1 = xb - hi.astype(F32)
            mid = r1.astype(BF16)
            lo = (r1 - mid.astype(F32)).astype(BF16)
            t = t_ref[...]
            o_ref[rows, :] = carry + (_dot(t, hi) + _dot(t, mid) + _dot(t, lo))
            return carry + jnp.sum(xb, axis=0, keepdims=True)
        lax.fori_loop(0, nb, step, jnp.zeros((1, LANES), F32))

    return pl.pallas_call(
        body, name=name,
        in_specs=[pl.BlockSpec(memory_space=pltpu.VMEM)] * 2,
        out_specs=pl.BlockSpec(memory_space=pltpu.VMEM),
        out_shape=jax.ShapeDtypeStruct((lp, LANES), F32),
        compiler_params=pltpu.CompilerParams(vmem_limit_bytes=VMEM_LIMIT),
    )(x, tl)


def _dlogf(dc, logf, name):
    lp = dc.shape[0]
    nb = lp // TQ
    tu = _tri(TQ, lambda r, c: c >= r)

    def body(x_ref, lf_ref, t_ref, df_ref, db_ref):
        def step(bb, carry):
            run, db = carry
            b = nb - 1 - bb
            rows = pl.ds(pl.multiple_of(b * TQ, TQ), TQ)
            xb = x_ref[rows, :]
            hi = xb.astype(BF16)
            r1 = xb - hi.astype(F32)
            mid = r1.astype(BF16)
            lo = (r1 - mid.astype(F32)).astype(BF16)
            t = t_ref[...]
            dlf = run + (_dot(t, hi) + _dot(t, mid) + _dot(t, lo))
            df = dlf * (1.0 - jnp.exp(lf_ref[rows, :]))
            df_ref[rows, :] = df
            return run + jnp.sum(xb, axis=0, keepdims=True), db + jnp.sum(df, axis=0, keepdims=True)
        z = jnp.zeros((1, LANES), F32)
        _, db = lax.fori_loop(0, nb, step, (z, z))
        db_ref[...] = db

    return pl.pallas_call(
        body, name=name,
        in_specs=[pl.BlockSpec(memory_space=pltpu.VMEM)] * 3,
        out_specs=[pl.BlockSpec(memory_space=pltpu.VMEM)] * 2,
        out_shape=[jax.ShapeDtypeStruct((lp, LANES), F32), jax.ShapeDtypeStruct((1, LANES), F32)],
        compiler_params=pltpu.CompilerParams(vmem_limit_bytes=VMEM_LIMIT),
    )(dc, logf, tu)


def _merge_fwd(o_a, o_b, proj, h0, w_oa, w_ob, w_out, g1, name):
    lp, dm = h0.shape
    tr = TQ
    ga_blk = (6 * W_ATT) // dm

    def body(oa_ref, ob_ref, ga_ref, gb_ref, h0_ref, woa_ref, wob_ref, wout_ref, g1_ref,
             ya_ref, yb_ref, m_ref, mixed_ref, h1_ref):
        ya = _dot(oa_ref[...], woa_ref[...])
        yb = _dot(ob_ref[...], wob_ref[...])
        m = jax.nn.sigmoid(ga_ref[...].astype(F32)) * ya + jax.nn.sigmoid(gb_ref[...].astype(F32)) * yb
        mb = m.astype(BF16)
        mixed = _dot(mb, wout_ref[...])
        ya_ref[...] = ya.astype(BF16)
        yb_ref[...] = yb.astype(BF16)
        m_ref[...] = mb
        mixed_ref[...] = mixed
        h1_ref[...] = h0_ref[...] + mixed * _rms(mixed) * g1_ref[...]

    row = lambda w: pl.BlockSpec((tr, w), lambda i: (i, 0))
    full = lambda a: pl.BlockSpec(a.shape, lambda i: (0, 0))
    return pl.pallas_call(
        body, name=name, grid=(lp // tr,),
        in_specs=[row(W_ATT), row(W_ATT),
                  pl.BlockSpec((tr, dm), lambda i: (i, ga_blk)),
                  pl.BlockSpec((tr, dm), lambda i: (i, ga_blk + 1)),
                  row(dm), full(w_oa), full(w_ob), full(w_out), full(g1)],
        out_specs=[row(dm)] * 5,
        out_shape=[jax.ShapeDtypeStruct((lp, dm), BF16)] * 3 + [jax.ShapeDtypeStruct((lp, dm), F32)] * 2,
        compiler_params=_cparams(("parallel",)),
    )(o_a, o_b, proj, proj, h0, w_oa, w_ob, w_out, g1)


def _merge_bwd(dh1, mixed, g1, w_out, proj, y_a, y_b, w_oa, w_ob, name):
    lp, dm = dh1.shape
    tr = TQ
    ga_blk = (6 * W_ATT) // dm

    def body(dh_ref, mx_ref, g1_ref, wout_ref, ga_ref, gb_ref, ya_ref, yb_ref, woa_ref, wob_ref,
             dmx_ref, dya_ref, dyb_ref, dga_ref, dgb_ref, doa_ref, dob_ref, dg1_ref):
        i = pl.program_id(0)
        dn = dh_ref[...]
        x = mx_ref[...]
        r = _rms(x)
        yhat = x * r
        part = jnp.sum(dn * yhat, axis=0, keepdims=True)
        dyh = dn * g1_ref[...]
        dmx = (r * (dyh - yhat * jnp.mean(dyh * yhat, axis=-1, keepdims=True))).astype(BF16)
        dmx_ref[...] = dmx
        dm_ = _dot_nt(dmx, wout_ref[...])
        sa = jax.nn.sigmoid(ga_ref[...].astype(F32))
        sb = jax.nn.sigmoid(gb_ref[...].astype(F32))
        dya = (dm_ * sa).astype(BF16)
        dyb = (dm_ * sb).astype(BF16)
        dya_ref[...] = dya
        dyb_ref[...] = dyb
        dga_ref[...] = (dm_ * ya_ref[...].astype(F32) * sa * (1.0 - sa)).astype(BF16)
        dgb_ref[...] = (dm_ * yb_ref[...].astype(F32) * sb * (1.0 - sb)).astype(BF16)
        doa_ref[...] = _dot_nt(dya, woa_ref[...]).astype(BF16)
        dob_ref[...] = _dot_nt(dyb, wob_ref[...]).astype(BF16)

        @pl.when(i == 0)
        def _():
            dg1_ref[...] = part

        @pl.when(i > 0)
        def _():
            dg1_ref[...] += part

    row = lambda w: pl.BlockSpec((tr, w), lambda i: (i, 0))
    full = lambda a: pl.BlockSpec(a.shape, lambda i: (0, 0))
    return pl.pallas_call(
        body, name=name, grid=(lp // tr,),
        in_specs=[row(dm), row(dm), full(g1), full(w_out),
                  pl.BlockSpec((tr, dm), lambda i: (i, ga_blk)),
                  pl.BlockSpec((tr, dm), lambda i: (i, ga_blk + 1)),
                  row(dm), row(dm), full(w_oa), full(w_ob)],
        out_specs=[row(dm)] * 5 + [row(W_ATT)] * 2 + [pl.BlockSpec((1, dm), lambda i: (0, 0))],
        out_shape=[jax.ShapeDtypeStruct((lp, dm), BF16)] * 5 + [jax.ShapeDtypeStruct((lp, W_ATT), BF16)] * 2
        + [jax.ShapeDtypeStruct((1, dm), F32)],
        compiler_params=_cparams(("arbitrary",)),
    )(dh1, mixed, g1, w_out, proj, proj, y_a, y_b, w_oa, w_ob)


_GELU_C = math.sqrt(2.0 / math.pi)
_GELU_A = 0.044715


def _gelu(x):
    t = jnp.tanh(_GELU_C * (x + _GELU_A * x * x * x))
    return 0.5 * x * (1.0 + t), t


CW = 256


def _taps(cur_ref, prev_ref, first, c0):
    cur = cur_ref[:, c0:c0 + CW].astype(F32)
    p1 = jnp.where(first, 0.0, prev_ref[HALO - 1:HALO, c0:c0 + CW].astype(F32))
    p2 = jnp.where(first, 0.0, prev_ref[HALO - 2:HALO - 1, c0:c0 + CW].astype(F32))
    row = lax.broadcasted_iota(jnp.int32, cur.shape, 0)
    x1 = jnp.where(row == 0, p1, pltpu.roll(cur, 1, 0))
    x2 = jnp.where(row == 0, p2, jnp.where(row == 1, p1, pltpu.roll(cur, 2, 0)))
    return cur, x1, x2


def _conv_at(cur_ref, prev_ref, w_ref, b_ref, first, c0):
    cur, x1, x2 = _taps(cur_ref, prev_ref, first, c0)
    cols = slice(c0, c0 + CW)
    u = b_ref[:, cols] + w_ref[0:1, cols] * x2 + w_ref[1:2, cols] * x1 + w_ref[2:3, cols] * cur
    return u, (x2, x1, cur)


def _up_specs(tr, width):
    per = tr // HALO
    return [pl.BlockSpec((tr, width), lambda i: (i, 0)),
            pl.BlockSpec((HALO, width), lambda i: (jnp.maximum(i * per - 1, 0), 0))]


def _convgate_fwd(up, conv_w, conv_b, name):
    lp, c2 = up.shape
    tr = TCONV

    def body(cur_ref, prev_ref, w_ref, b_ref, a_ref):
        first = pl.program_id(0) == 0
        for c0 in range(0, D_FF, CW):
            ug, _ = _conv_at(cur_ref, prev_ref, w_ref, b_ref, first, c0)
            uv, _ = _conv_at(cur_ref, prev_ref, w_ref, b_ref, first, D_FF + c0)
            gel, _ = _gelu(ug)
            a_ref[:, c0:c0 + CW] = (gel * uv).astype(BF16)

    return pl.pallas_call(
        body, name=name, grid=(lp // tr,),
        in_specs=_up_specs(tr, c2) + [pl.BlockSpec((3, c2), lambda i: (0, 0)),
                                      pl.BlockSpec((1, c2), lambda i: (0, 0))],
        out_specs=pl.BlockSpec((tr, D_FF), lambda i: (i, 0)),
        out_shape=jax.ShapeDtypeStruct((lp, D_FF), BF16),
        compiler_params=_cparams(("parallel",)),
    )(up, up, conv_w, conv_b)


def _convgate_bwd(up, da, conv_w, conv_b, name):
    lp, c2 = up.shape
    tr = TCONV

    def body(cur_ref, prev_ref, da_ref, w_ref, b_ref, du_ref, dw_ref, db_ref):
        i = pl.program_id(0)
        first = i == 0

        @pl.when(first)
        def _():
            dw_ref[...] = jnp.zeros_like(dw_ref)
            db_ref[...] = jnp.zeros_like(db_ref)

        for c0 in range(0, D_FF, CW):
            ug, taps_g = _conv_at(cur_ref, prev_ref, w_ref, b_ref, first, c0)
            uv, taps_v = _conv_at(cur_ref, prev_ref, w_ref, b_ref, first, D_FF + c0)
            gel, t = _gelu(ug)
            dgel = 0.5 * (1.0 + t) + 0.5 * ug * (1.0 - t * t) * _GELU_C * (1.0 + 3.0 * _GELU_A * ug * ug)
            da_ = da_ref[:, c0:c0 + CW].astype(F32)
            for base, du, taps in ((c0, da_ * uv * dgel, taps_g), (D_FF + c0, da_ * gel, taps_v)):
                cols = slice(base, base + CW)
                du_ref[:, cols] = du.astype(BF16)
                for tap in range(3):
                    dw_ref[tap:tap + 1, cols] += jnp.sum(du * taps[tap], axis=0, keepdims=True)
                db_ref[:, cols] += jnp.sum(du, axis=0, keepdims=True)

    return pl.pallas_call(
        body, name=name, grid=(lp // tr,),
        in_specs=_up_specs(tr, c2) + [pl.BlockSpec((tr, D_FF), lambda i: (i, 0)),
                                      pl.BlockSpec((3, c2), lambda i: (0, 0)),
                                      pl.BlockSpec((1, c2), lambda i: (0, 0))],
        out_specs=[pl.BlockSpec((tr, c2), lambda i: (i, 0)),
                   pl.BlockSpec((3, c2), lambda i: (0, 0)),
                   pl.BlockSpec((1, c2), lambda i: (0, 0))],
        out_shape=[jax.ShapeDtypeStruct((lp, c2), BF16), jax.ShapeDtypeStruct((3, c2), F32),
                   jax.ShapeDtypeStruct((1, c2), F32)],
        compiler_params=_cparams(("arbitrary",)),
    )(up, up, da, conv_w, conv_b)


def _conv_transpose(du, conv_w, name):
    lp, c2 = du.shape
    tr = TCONV
    per = tr // HALO
    n_halo = lp // HALO
    nt = lp // tr

    def body(cur_ref, nxt_ref, w_ref, o_ref):
        last = pl.program_id(0) == nt - 1
        for c0 in range(0, c2, CW):
            cols = slice(c0, c0 + CW)
            cur = cur_ref[:, cols].astype(F32)
            n0 = jnp.where(last, 0.0, nxt_ref[0:1, cols].astype(F32))
            n1 = jnp.where(last, 0.0, nxt_ref[1:2, cols].astype(F32))
            row = lax.broadcasted_iota(jnp.int32, cur.shape, 0)
            y1 = jnp.where(row == tr - 1, n0, pltpu.roll(cur, tr - 1, 0))
            y2 = jnp.where(row == tr - 1, n1, jnp.where(row == tr - 2, n0, pltpu.roll(cur, tr - 2, 0)))
            o_ref[:, cols] = (w_ref[2:3, cols] * cur + w_ref[1:2, cols] * y1 + w_ref[0:1, cols] * y2).astype(BF16)

    return pl.pallas_call(
        body, name=name, grid=(nt,),
        in_specs=[pl.BlockSpec((tr, c2), lambda i: (i, 0)),
                  pl.BlockSpec((HALO, c2), lambda i: (jnp.minimum((i + 1) * per, n_halo - 1), 0)),
                  pl.BlockSpec((3, c2), lambda i: (0, 0))],
        out_specs=pl.BlockSpec((tr, c2), lambda i: (i, 0)),
        out_shape=jax.ShapeDtypeStruct((lp, c2), BF16),
        compiler_params=_cparams(("parallel",)),
    )(du, du, conv_w)


def _down_loss(a, w_down, h1, g3, target, name):
    lp, dm = h1.shape
    tr = TQ

    def body(a_ref, w_ref, h1_ref, g_ref, t_ref, ffn_ref, dy_ref, ss_ref):
        i = pl.program_id(0)
        ffn = _dot(a_ref[...], w_ref[...])
        ffn_ref[...] = ffn
        h2 = h1_ref[...] + ffn * _rms(ffn) * g_ref[...]
        d = jnp.where(i > 0, h2 - t_ref[...], 0.0)
        dy_ref[...] = d * (1.0 / dm)
        part = jnp.sum(jnp.sum(d * d, axis=0, keepdims=True), axis=1, keepdims=True)

        @pl.when(i == 0)
        def _():
            ss_ref[...] = jnp.zeros_like(ss_ref)

        ss_ref[...] += part

    return pl.pallas_call(
        body, name=name, grid=(lp // tr,),
        in_specs=[pl.BlockSpec((tr, D_FF), lambda i: (i, 0)),
                  pl.BlockSpec(w_down.shape, lambda i: (0, 0)),
                  pl.BlockSpec((tr, dm), lambda i: (i, 0)),
                  pl.BlockSpec((1, dm), lambda i: (0, 0)),
                  pl.BlockSpec((tr, dm), lambda i: (jnp.maximum(i - 1, 0), 0))],
        out_specs=[pl.BlockSpec((tr, dm), lambda i: (i, 0)),
                   pl.BlockSpec((tr, dm), lambda i: (i, 0)),
                   pl.BlockSpec((8, LANES), lambda i: (0, 0))],
        out_shape=[jax.ShapeDtypeStruct((lp, dm), F32), jax.ShapeDtypeStruct((lp, dm), F32),
                   jax.ShapeDtypeStruct((8, LANES), F32)],
        compiler_params=_cparams(("arbitrary",)),
    )(a, w_down, h1, g3, target)


def _down_bwd(dy, ffn, g3, w_down, name):
    lp, dm = dy.shape
    tr = TQ

    def body(dy_ref, f_ref, g_ref, w_ref, dffn_ref, da_ref, dg_ref):
        i = pl.program_id(0)
        dn = dy_ref[...]
        x = f_ref[...]
        r = _rms(x)
        yhat = x * r
        part = jnp.sum(dn * yhat, axis=0, keepdims=True)
        dyh = dn * g_ref[...]
        dffn = (r * (dyh - yhat * jnp.mean(dyh * yhat, axis=-1, keepdims=True))).astype(BF16)
        dffn_ref[...] = dffn
        da_ref[...] = _dot_nt(dffn, w_ref[...]).astype(BF16)

        @pl.when(i == 0)
        def _():
            dg_ref[...] = part

        @pl.when(i > 0)
        def _():
            dg_ref[...] += part

    return pl.pallas_call(
        body, name=name, grid=(lp // tr,),
        in_specs=[pl.BlockSpec((tr, dm), lambda i: (i, 0)),
                  pl.BlockSpec((tr, dm), lambda i: (i, 0)),
                  pl.BlockSpec((1, dm), lambda i: (0, 0)),
                  pl.BlockSpec(w_down.shape, lambda i: (0, 0))],
        out_specs=[pl.BlockSpec((tr, dm), lambda i: (i, 0)),
                   pl.BlockSpec((tr, D_FF), lambda i: (i, 0)),
                   pl.BlockSpec((1, dm), lambda i: (0, 0))],
        out_shape=[jax.ShapeDtypeStruct((lp, dm), BF16), jax.ShapeDtypeStruct((lp, D_FF), BF16),
                   jax.ShapeDtypeStruct((1, dm), F32)],
        compiler_params=_cparams(("arbitrary",)),
    )(dy, ffn, g3, w_down)


def _pair_specs(lp, base):
    return [pl.BlockSpec((TQ, LANES), lambda p, i: (i, base + p)),
            pl.BlockSpec((lp, LANES), lambda p, i: (0, base + N_PAIRS + p)),
            pl.BlockSpec((lp, LANES), lambda p, i: (0, base + 2 * N_PAIRS + p))]


def _col_spec():
    return pl.BlockSpec((None, 2, TQ, 1), lambda p, i: (p, 0, i, 0))


def _rowvec_spec(nb):
    return pl.BlockSpec((None, 2, nb, 1, TQ), lambda p, i: (p, 0, 0, 0, 0))


def _tile_spec():
    return pl.BlockSpec((TQ, LANES), lambda p, i: (i, p))


RC = 64
T2 = 2 * TQ


def _stack_heads(x, scale=None):
    lane = lax.broadcasted_iota(jnp.int32, x.shape, 1)
    zero = jnp.zeros_like(x)
    x2 = jnp.concatenate([jnp.where(lane < HEAD_DIM, x, zero), jnp.where(lane >= HEAD_DIM, x, zero)], axis=0)
    return x2 if scale is None else x2 * scale


def _unstack_heads(x2):
    lane = lax.broadcasted_iota(jnp.int32, (TQ, LANES), 1)
    return jnp.where(lane < HEAD_DIM, x2[:TQ], x2[TQ:])


def _stack_cols(ref):
    return jnp.concatenate([ref[0], ref[1]], axis=0)


def _chunk_valid(i, j, r, strict):
    qpos = i * TQ + (r % TQ) + lax.broadcasted_iota(jnp.int32, (RC, TQ), 0)
    kpos = j * TQ + lax.broadcasted_iota(jnp.int32, (RC, TQ), 1)
    causal = (kpos < qpos) if strict else (kpos <= qpos)
    return causal & (kpos >= PAD)


def _walk_tiles(i, step, reverse):
    first, last = (i, 0) if reverse else (0, i)
    step(first, True)
    _between_unrolled(i, lambda j, nxt: step(j, False), reverse)

    @pl.when(i > 0)
    def _():
        step(last, True)


UNROLL = 4


def _between_unrolled(i, step, reverse):
    tile = (lambda t: i - 1 - t) if reverse else (lambda t: t + 1)
    after = (lambda j: j - 1) if reverse else (lambda j: j + 1)
    n = jnp.maximum(i - 1, 0)

    def group(u, c):
        j = tile(UNROLL * u)
        for _ in range(UNROLL):
            step(j, after(j))
            j = after(j)
        return c

    lax.fori_loop(0, n // UNROLL, group, 0)
    for k in range(UNROLL - 1):
        @pl.when(n % UNROLL > k)
        def _(k=k):
            j = tile(n - n % UNROLL + k)
            step(j, after(j))


_HALF = (slice(0, TQ), slice(TQ, T2))


def _walk_tiles_lead(i, lead, step, reverse):
    first, last = (i, 0) if reverse else (0, i)
    lead(first, 0)
    lead(first, 1)
    step(first, jnp.maximum(i - 1, 0) if reverse else jnp.minimum(1, i), True)

    _between_unrolled(i, lambda j, nxt: step(j, nxt, False), reverse)

    @pl.when(i > 0)
    def _():
        step(last, None, True)


def _krows(j):
    return pl.ds(pl.multiple_of(j * TQ, TQ), TQ)


def _fox_fwd(proj, crow, name):
    lp = proj.shape[0]
    nb = lp // TQ

    def body(q_ref, k_ref, v_ref, cr_ref, o_ref, lse_ref, m_ref, acc_ref, p_ref):
        i = pl.program_id(1)
        q2 = _stack_heads(q_ref[...], 0.125)
        m_ref[...] = jnp.full(m_ref.shape, NEG, F32)
        acc_ref[...] = jnp.zeros_like(acc_ref)
        lane = lax.broadcasted_iota(jnp.int32, (TQ, LANES), 1)

        def step(j, masked):
            rows_j = _krows(j)
            s = _dot_nt(q2, k_ref[rows_j, :])
            v = v_ref[rows_j, :]
            one = jnp.ones_like(v)
            v_heads = (jnp.where(lane < HEAD_DIM, v, one), jnp.where(lane >= HEAD_DIM, v, one))
            for r in range(0, T2, RC):
                rows = slice(r, r + RC)
                s_c = s[rows] - cr_ref[r // TQ, j]
                if masked:
                    s_c = jnp.where(_chunk_valid(i, j, r, False), s_c, NEG)
                s0, s1 = s_c[:, :LANES], s_c[:, LANES:]
                m_old = m_ref[rows]
                m_new = jnp.maximum(m_old, jnp.max(jnp.maximum(s0, s1), axis=-1, keepdims=True))
                m_ref[rows] = m_new
                acc_ref[rows] = jnp.exp(m_old - m_new) * acc_ref[rows]
                p_ref[rows, :LANES] = jnp.exp(s0 - m_new).astype(BF16)
                p_ref[rows, LANES:] = jnp.exp(s1 - m_new).astype(BF16)
            for h in range(2):
                acc_ref[_HALF[h]] += _dot(p_ref[_HALF[h]], v_heads[h])

        _walk_tiles(i, step, reverse=False)
        acc = acc_ref[...]
        m = m_ref[...]
        outs = []
        for h in range(2):
            a_h = acc[_HALF[h]]
            l = a_h[:, HEAD_DIM:HEAD_DIM + 1] if h == 0 else a_h[:, 0:1]
            lse_ref[h] = m[_HALF[h]][:, 0:1] + jnp.log(l)
            outs.append(a_h / l)
        o_ref[...] = jnp.where(lane < HEAD_DIM, outs[0], outs[1]).astype(BF16)

    return pl.pallas_call(
        body, name=name, grid=(N_PAIRS, nb),
        in_specs=_pair_specs(lp, 0) + [_rowvec_spec(nb)],
        out_specs=[_tile_spec(), _col_spec()],
        out_shape=[jax.ShapeDtypeStruct((lp, W_ATT), BF16),
                   jax.ShapeDtypeStruct((N_PAIRS, 2, lp, 1), F32)],
        scratch_shapes=[pltpu.VMEM((T2, LANES), F32), pltpu.VMEM((T2, LANES), F32), pltpu.VMEM((T2, TQ), BF16)],
        compiler_params=_cparams(("parallel", "arbitrary")),
    )(proj, proj, proj, crow)


def _fox_bwd(proj, do, o, lse, crow, name):
    lp = proj.shape[0]
    nb = lp // TQ

    def body(q_ref, k_ref, v_ref, do_ref, o_ref, lse_ref, cr_ref,
             dq_ref, dk_ref, dv_ref, dcs_ref, dct_ref,
             dk_acc, dv_acc, dq_acc, dct_acc, p_ref, ds_ref, s_ref, dp_ref):
        i = pl.program_id(1)

        @pl.when(i == 0)
        def _():
            dk_acc[...] = jnp.zeros_like(dk_acc)
            dv_acc[...] = jnp.zeros_like(dv_acc)
            dcs_ref[...] = jnp.zeros_like(dcs_ref)

        dq_acc[...] = jnp.zeros_like(dq_acc)
        dct_acc[...] = jnp.zeros_like(dct_acc)
        do_ = do_ref[...]
        q2 = _stack_heads(q_ref[...], 0.125)
        do2 = _stack_heads(do_)
        prod = do_.astype(F32) * o_ref[...].astype(F32)
        lane = lax.broadcasted_iota(jnp.int32, prod.shape, 1)
        delta2 = jnp.concatenate(
            [jnp.sum(jnp.where(lane < HEAD_DIM, prod, 0.0), axis=-1, keepdims=True),
             jnp.sum(jnp.where(lane >= HEAD_DIM, prod, 0.0), axis=-1, keepdims=True)], axis=0)
        lse2 = _stack_cols(lse_ref)

        def lead(j, h):
            rows_j = _krows(j)
            s_ref[_HALF[h]] = _dot_nt(q2[_HALF[h]], k_ref[rows_j, :])
            dp_ref[_HALF[h]] = _dot_nt(do2[_HALF[h]], v_ref[rows_j, :])

        def step(j, nxt, masked):
            rows_j = _krows(j)
            k = k_ref[rows_j, :]
            for h in range(2):
                cs = jnp.zeros((1, TQ), F32)
                for r in range(h * TQ, (h + 1) * TQ, RC):
                    rows = slice(r, r + RC)
                    p = jnp.exp(s_ref[rows] - cr_ref[h, j] - lse2[rows])
                    if masked:
                        p = jnp.where(_chunk_valid(i, j, r, False), p, 0.0)
                    ds = p * (dp_ref[rows] - delta2[rows])
                    p_ref[rows] = p.astype(BF16)
                    ds_ref[rows] = ds.astype(BF16)
                    dct_acc[rows] += jnp.sum(ds, axis=-1, keepdims=True)
                    cs = cs + jnp.sum(ds, axis=0, keepdims=True)
                dcs_ref[h, j] -= cs
                if nxt is not None:
                    lead(nxt, h)
                dsb = ds_ref[_HALF[h]]
                dq_acc[_HALF[h]] += _dot(dsb, k)
                dk_acc[rows_j, :] += _dot_tn(dsb, q2[_HALF[h]])
                dv_acc[rows_j, :] += _dot_tn(p_ref[_HALF[h]], do2[_HALF[h]])

        _walk_tiles_lead(i, lead, step, reverse=False)
        dct = dct_acc[...]
        dct_ref[0] = dct[:TQ]
        dct_ref[1] = dct[TQ:]
        dq_ref[...] = (_unstack_heads(dq_acc[...]) * 0.125).astype(BF16)

        @pl.when(i == nb - 1)
        def _():
            dk_ref[...] = dk_acc[...].astype(BF16)
            dv_ref[...] = dv_acc[...].astype(BF16)

    whole = pl.BlockSpec((lp, LANES), lambda p, i: (0, p))
    return pl.pallas_call(
        body, name=name, grid=(N_PAIRS, nb),
        in_specs=_pair_specs(lp, 0) + [_tile_spec(), _tile_spec(), _col_spec(), _rowvec_spec(nb)],
        out_specs=[_tile_spec(), whole, whole, _rowvec_spec(nb), _col_spec()],
        out_shape=[jax.ShapeDtypeStruct((lp, W_ATT), BF16)] * 3
        + [jax.ShapeDtypeStruct((N_PAIRS, 2, nb, 1, TQ), F32), jax.ShapeDtypeStruct((N_PAIRS, 2, lp, 1), F32)],
        scratch_shapes=[pltpu.VMEM((lp, LANES), F32), pltpu.VMEM((lp, LANES), F32),
                        pltpu.VMEM((T2, LANES), F32), pltpu.VMEM((T2, 1), F32),
                        pltpu.VMEM((T2, TQ), BF16), pltpu.VMEM((T2, TQ), BF16),
                        pltpu.VMEM((T2, TQ), F32), pltpu.VMEM((T2, TQ), F32)],
        compiler_params=_cparams(("parallel", "arbitrary")),
    )(proj, proj, proj, do, o, lse, crow)


def _sb_fwd(proj, name):
    lp = proj.shape[0]
    nb = lp // TQ
    tsuf = _tri(TQ, lambda r, c: r > c)

    def body(q_ref, k_ref, v_ref, t_ref, o_ref, lt_ref, run_ref, acc_ref, zl_ref, hl_ref, a_ref, z_ref):
        i = pl.program_id(1)
        q2 = _stack_heads(q_ref[...], 0.125)
        run_ref[...] = jnp.zeros_like(run_ref)
        acc_ref[...] = jnp.zeros_like(acc_ref)

        def lead(j, h):
            z_ref[_HALF[h]] = _dot_nt(q2[_HALF[h]], k_ref[_krows(j), :])

        def step(j, nxt, masked):
            t = t_ref[...]
            v = v_ref[_krows(j), :]
            later = []
            for h in range(2):
                for r in range(h * TQ, (h + 1) * TQ, RC):
                    rows = slice(r, r + RC)
                    z_c = z_ref[rows]
                    lk = _log_keep(z_c)
                    if masked:
                        lk = jnp.where(_chunk_valid(i, j, r, True), lk, 0.0)
                    hi, lo = _split2(lk)
                    hl_ref[rows, :TQ] = hi
                    hl_ref[rows, TQ:] = lo
                    zl_ref[rows] = z_c + lk + run_ref[rows]
                    run_ref[rows] += jnp.sum(lk, axis=-1, keepdims=True)
                if nxt is not None:
                    lead(nxt, h)
                later.append(_dot(hl_ref[_HALF[h]], t))
            for h in range(2):
                for r in range(0, TQ, RC):
                    rows = slice(h * TQ + r, h * TQ + r + RC)
                    a = jnp.exp(zl_ref[rows] + later[h][r:r + RC])
                    if masked:
                        a = jnp.where(_chunk_valid(i, j, h * TQ + r, True), a, 0.0)
                    a_ref[rows] = a.astype(BF16)
                acc_ref[_HALF[h]] += _dot(a_ref[_HALF[h]], v)

        _walk_tiles_lead(i, lead, step, reverse=True)
        run = run_ref[...]
        lt_ref[0] = run[:TQ]
        lt_ref[1] = run[TQ:]
        o_ref[...] = _unstack_heads(acc_ref[...]).astype(BF16)

    base = 3 * N_PAIRS
    return pl.pallas_call(
        body, name=name, grid=(N_PAIRS, nb),
        in_specs=_pair_specs(lp, base) + [pl.BlockSpec((2 * TQ, TQ), lambda p, i: (0, 0))],
        out_specs=[_tile_spec(), _col_spec()],
        out_shape=[jax.ShapeDtypeStruct((lp, W_ATT), BF16),
                   jax.ShapeDtypeStruct((N_PAIRS, 2, lp, 1), F32)],
        scratch_shapes=[pltpu.VMEM((T2, 1), F32), pltpu.VMEM((T2, LANES), F32), pltpu.VMEM((T2, TQ), F32),
                        pltpu.VMEM((T2, 2 * TQ), BF16), pltpu.VMEM((T2, TQ), BF16), pltpu.VMEM((T2, TQ), F32)],
        compiler_params=_cparams(("parallel", "arbitrary")),
    )(proj, proj, proj, jnp.concatenate([tsuf, tsuf], axis=0))


def _sb_bwd(proj, do, ltot, name):
    lp = proj.shape[0]
    nb = lp // TQ
    tincl = _tri(TQ, lambda r, c: r <= c)
    texcl = _tri(TQ, lambda r, c: r < c)

    def body(q_ref, k_ref, v_ref, do_ref, lt_ref, ti_ref, te_ref, dq_ref, dk_ref, dv_ref,
             dk_acc, dv_acc, dq_acc, pc_ref, gc_ref, zl_ref, keep_ref, g_ref, z_ref, da_ref,
             hl_ref, gb_ref, a_ref, dz_ref):
        i = pl.program_id(1)

        @pl.when(i == 0)
        def _():
            dk_acc[...] = jnp.zeros_like(dk_acc)
            dv_acc[...] = jnp.zeros_like(dv_acc)

        dq_acc[...] = jnp.zeros_like(dq_acc)
        gc_ref[...] = jnp.zeros_like(gc_ref)
        pc_ref[...] = _stack_cols(lt_ref)
        q2 = _stack_heads(q_ref[...], 0.125)
        do2 = _stack_heads(do_ref[...])

        def lead(j, h):
            rows_j = _krows(j)
            z_ref[_HALF[h]] = _dot_nt(q2[_HALF[h]], k_ref[rows_j, :])
            da_ref[_HALF[h]] = _dot_nt(do2[_HALF[h]], v_ref[rows_j, :])

        def step(j, nxt, masked):
            rows_j = _krows(j)
            k = k_ref[rows_j, :]
            ti = ti_ref[...]
            te = te_ref[...]
            upto, before = [], []
            for h in range(2):
                for r in range(h * TQ, (h + 1) * TQ, RC):
                    rows = slice(r, r + RC)
                    z_c = z_ref[rows]
                    lk = _log_keep(z_c)
                    if masked:
                        lk = jnp.where(_chunk_valid(i, j, r, True), lk, 0.0)
                    hi, lo = _split2(lk)
                    hl_ref[rows, :TQ] = hi
                    hl_ref[rows, TQ:] = lo
                    keep_ref[rows] = jnp.exp(lk)
                    zl_ref[rows] = z_c + lk + pc_ref[rows]
                    pc_ref[rows] -= jnp.sum(lk, axis=-1, keepdims=True)
                upto.append(_dot(hl_ref[_HALF[h]], ti))
            for h in range(2):
                for r in range(0, TQ, RC):
                    rows = slice(h * TQ + r, h * TQ + r + RC)
                    a = jnp.exp(zl_ref[rows] - upto[h][r:r + RC])
                    if masked:
                        a = jnp.where(_chunk_valid(i, j, h * TQ + r, True), a, 0.0)
                    g = a * da_ref[rows]
                    a_ref[rows] = a.astype(BF16)
                    g_ref[rows] = g
                    gb_ref[rows] = g.astype(BF16)
                if nxt is not None:
                    lead(nxt, h)
                before.append(_dot(gb_ref[_HALF[h]], te))
            for h in range(2):
                for r in range(0, TQ, RC):
                    rows = slice(h * TQ + r, h * TQ + r + RC)
                    g = g_ref[rows]
                    keep = keep_ref[rows]
                    dz = g * keep - (1.0 - keep) * (gc_ref[rows] + before[h][r:r + RC])
                    if masked:
                        dz = jnp.where(_chunk_valid(i, j, h * TQ + r, True), dz, 0.0)
                    dz_ref[rows] = dz.astype(BF16)
                    gc_ref[rows] += jnp.sum(g, axis=-1, keepdims=True)
                dzb = dz_ref[_HALF[h]]
                dq_acc[_HALF[h]] += _dot(dzb, k)
                dk_acc[rows_j, :] += _dot_tn(dzb, q2[_HALF[h]])
                dv_acc[rows_j, :] += _dot_tn(a_ref[_HALF[h]], do2[_HALF[h]])

        _walk_tiles_lead(i, lead, step, reverse=False)
        dq_ref[...] = (_unstack_heads(dq_acc[...]) * 0.125).astype(BF16)

        @pl.when(i == nb - 1)
        def _():
            dk_ref[...] = dk_acc[...].astype(BF16)
            dv_ref[...] = dv_acc[...].astype(BF16)

    base = 3 * N_PAIRS
    whole = pl.BlockSpec((lp, LANES), lambda p, i: (0, p))
    tri = lambda rows: pl.BlockSpec((rows, TQ), lambda p, i: (0, 0))
    wide = lambda dt: pltpu.VMEM((T2, TQ), dt)
    return pl.pallas_call(
        body, name=name, grid=(N_PAIRS, nb),
        in_specs=_pair_specs(lp, base) + [_tile_spec(), _col_spec(), tri(2 * TQ), tri(TQ)],
        out_specs=[_tile_spec(), whole, whole],
        out_shape=[jax.ShapeDtypeStruct((lp, W_ATT), BF16)] * 3,
        scratch_shapes=[pltpu.VMEM((lp, LANES), F32), pltpu.VMEM((lp, LANES), F32),
                        pltpu.VMEM((T2, LANES), F32), pltpu.VMEM((T2, 1), F32), pltpu.VMEM((T2, 1), F32),
                        wide(F32), wide(F32), wide(F32), wide(F32), wide(F32),
                        pltpu.VMEM((T2, 2 * TQ), BF16), wide(BF16), wide(BF16), wide(BF16)],
        compiler_params=_cparams(("parallel", "arbitrary")),
    )(proj, proj, proj, do, ltot, jnp.concatenate([tincl, tincl], axis=0), texcl)


def _local_step(x, target, meta, gains, w_in, b_forget, w_o_fox, w_o_sb, w_out, w_up, conv_w, conv_b, w_down):
    seq, dm = x.shape
    lp = PAD + N_META + seq
    nb = lp // TQ
    s = [W_ATT, W_ATT, W_ATT, 8, W_ATT, W_ATT, W_ATT, dm, dm]
    off = [sum(s[:i]) for i in range(len(s) + 1)]
    cols = lambda i: w_in[:, off[i]:off[i + 1]]
    w1 = jnp.concatenate([cols(0), cols(1), cols(2), cols(4), cols(5), cols(6), cols(7), cols(8)], axis=1)
    wf = jnp.pad(cols(3), ((0, 0), (0, LANES - 8)))
    n1 = w1.shape[1]
    ncat = n1 + 512
    w_cat = jnp.concatenate([w1, wf, jnp.zeros((dm, ncat - n1 - LANES), BF16)], axis=1)
    bf = jnp.pad(b_forget.reshape(1, 8), ((0, 0), (0, LANES - 8)))
    g = [gains[i].reshape(1, dm) for i in range(4)]
    cb = conv_b.reshape(1, -1)

    h0 = jnp.concatenate([jnp.zeros((PAD, dm), F32), meta, x], axis=0)

    proj, xn1 = _rms_mm(h0, g[0], w1, "in_proj")
    logf = _logf(xn1, wf, bf, "log_forget")
    c = _cumsum_rows(logf, "forget_cumsum")
    crow = c[:, :8].T.reshape(N_PAIRS, 2, nb, 1, TQ)
    o_a, lse = _fox_fwd(proj, crow, "fox_fwd")
    o_b, ltot = _sb_fwd(proj, "sb_fwd")
    y_a, y_b, m, mixed, h1 = _merge_fwd(o_a, o_b, proj, h0, w_o_fox, w_o_sb, w_out, g[1], "merge_fwd")
    up, xn3 = _rms_mm(h1, g[2], w_up, "up_proj")
    a = _convgate_fwd(up, conv_w, cb, "convgate_fwd")
    ffn, dy, ss = _down_loss(a, w_down, h1, g[3], target, "down_loss")

    dffn, da, dg3 = _down_bwd(dy, ffn, g[3], w_down, "down_bwd")
    d_w_down = _mm_tn(a, dffn, "dw_down")
    du, d_conv_w, d_conv_b = _convgate_bwd(up, da, conv_w, cb, "convgate_bwd")
    dup = _conv_transpose(du, conv_w, "conv_transpose")
    d_w_up = _mm_tn(xn3, dup, "dw_up")
    dh1, dg2 = _mm_rmsbwd(dup, w_up, h1, g[2], dy, "up_bwd")
    dmx, dya, dyb, dga, dgb, do_a, do_b, dg1 = _merge_bwd(
        dh1, mixed, g[1], w_out, proj, y_a, y_b, w_o_fox, w_o_sb, "merge_bwd")
    d_w_out = _mm_tn(m, dmx, "dw_out")
    d_w_o_fox = _mm_tn(o_a, dya, "dw_o_fox")
    d_w_o_sb = _mm_tn(o_b, dyb, "dw_o_sb")
    dq_a, dk_a, dv_a, dcs, dct = _fox_bwd(proj, do_a, o_a, lse, crow, "fox_bwd")
    dq_b, dk_b, dv_b = _sb_bwd(proj, do_b, ltot, "sb_bwd")
    dc = (dct.reshape(8, lp) + dcs.reshape(8, lp)).T
    df, db = _dlogf(jnp.pad(dc, ((0, 0), (0, LANES - 8))), logf, "forget_bwd")
    dcat = jnp.concatenate([dq_a, dk_a, dv_a, dq_b, dk_b, dv_b, dga, dgb, df.astype(BF16),
                            jnp.zeros((lp, ncat - n1 - LANES), BF16)], axis=1)
    d_w_cat = _mm_tn(xn1, dcat, "dw_in")
    dh0, dg0 = _mm_rmsbwd(dcat, w_cat, h0, g[0], dh1, "in_bwd")

    wc = lambda k: d_w_cat[:, k * W_ATT:(k + 1) * W_ATT]
    d_w_in = jnp.concatenate([wc(0), wc(1), wc(2), d_w_cat[:, n1:n1 + 8], wc(3), wc(4), wc(5),
                              d_w_cat[:, 6 * W_ATT:n1]], axis=1)
    d_gains = jnp.concatenate([dg0, dg1, dg2, dg3], axis=0)
    grads = (dh0[PAD:PAD + N_META], d_gains, d_w_in, db[0, :8], d_w_o_fox, d_w_o_sb, d_w_out,
             d_w_up, d_conv_w, d_conv_b[0], d_w_down)
    return ss[0, 0], dh0[PAD + N_META:], grads


def _rows(a, n_rows):
    flat = a.reshape(-1)
    return jnp.pad(flat, (0, n_rows * D_MODEL - flat.shape[0])).reshape(n_rows, D_MODEL)


_SMALL = (("meta", 4), ("gains", 1), ("conv_w", 5), ("b_forget", 1), ("conv_b", 6))


def _pack(meta, gains, w_in, b_forget, w_o_fox, w_o_sb, w_out, w_up, conv_w, conv_b, w_down):
    big = [a.reshape(-1, D_MODEL) for a in (w_in, w_o_fox, w_o_sb, w_out, w_up, w_down)]
    n_big = sum(a.shape[0] for a in big)
    small = [_rows(a, n) for a, (_, n) in zip((meta, gains, conv_w, b_forget, conv_b), _SMALL)]
    n_small = sum(n for _, n in _SMALL)
    z = lambda n: jnp.zeros((n, D_MODEL), big[0].dtype)
    return jnp.concatenate(big + [z(PK_BIG_ROWS - n_big)] + small
                           + [z(PK_ROWS - PK_BIG_ROWS - n_small)], axis=0)


def _unpack(p):
    def take(r0, shape):
        n = math.prod(shape)
        nr = -(-n // D_MODEL)
        return p[r0:r0 + nr].reshape(-1)[:n].reshape(shape), r0 + nr
    w_in, r = take(0, (1, 1024, 1282))
    w_o_fox, r = take(r, (1, 512, 256))
    w_o_sb, r = take(r, (1, 512, 256))
    w_out, r = take(r, (1, 256, 1024))
    w_up, r = take(r, (1, 1024, 1408))
    w_down, r = take(r, (1, 704, 1024))
    r = PK_BIG_ROWS
    meta, r = take(r, (16, 256))
    gains, r = take(r, (1, 4, 256))
    conv_w, r = take(r, (1, 3, 1408))
    b_forget, r = take(r, (1, 8))
    conv_b, r = take(r, (1, 5632))
    return meta, gains, w_in, b_forget, w_o_fox, w_o_sb, w_out, w_up, conv_w, conv_b, w_down


def _chip_peers():
    x, y, c = lax.axis_index("x"), lax.axis_index("y"), lax.axis_index("c")
    return [(x, 1 - y, c), (1 - x, y, c), (1 - x, 1 - y, c)]


def _all_gather_chips(arrays, name):
    n = len(arrays)

    def body(*refs):
        ins, outs = refs[:n], refs[n:2 * n]
        send_sems, recv_sems, local_sems = refs[2 * n:]
        x, y = lax.axis_index("x"), lax.axis_index("y")
        me = 2 * x + y
        peers = _chip_peers()
        copies = []
        for a in range(n):
            mine = pltpu.make_async_copy(ins[a], outs[a].at[me], local_sems.at[a])
            mine.start()
            copies.append(mine)
        remote = []
        for a in range(n):
            for j, peer in enumerate(peers):
                cp = pltpu.make_async_remote_copy(
                    src_ref=ins[a], dst_ref=outs[a].at[me],
                    send_sem=send_sems.at[3 * a + j], recv_sem=recv_sems.at[3 * a + j],
                    device_id=peer, device_id_type=MESH)
                cp.start()
                remote.append(cp)
        for cp in remote:
            cp.wait()
        for cp in copies:
            cp.wait()

    any_spec = pl.BlockSpec(memory_space=pl.ANY)
    return pl.pallas_call(
        body, name=name,
        in_specs=[any_spec] * n, out_specs=[any_spec] * n,
        out_shape=[jax.ShapeDtypeStruct((4,) + a.shape, a.dtype) for a in arrays],
        scratch_shapes=[pltpu.SemaphoreType.DMA((3 * n,)), pltpu.SemaphoreType.DMA((3 * n,)),
                        pltpu.SemaphoreType.DMA((n,))],
    )(*arrays)


def _scatter_chips(chunks, name):
    _, rows, cols = chunks.shape

    def body(in_ref, out_ref, send_sems, recv_sems):
        x, y = lax.axis_index("x"), lax.axis_index("y")
        targets = [2 * x + (1 - y), 2 * (1 - x) + y, 2 * (1 - x) + (1 - y)]
        remote = []
        for j, peer in enumerate(_chip_peers()):
            cp = pltpu.make_async_remote_copy(
                src_ref=in_ref.at[targets[j]], dst_ref=out_ref.at[j],
                send_sem=send_sems.at[j], recv_sem=recv_sems.at[j],
                device_id=peer, device_id_type=MESH)
            cp.start()
            remote.append(cp)
        for cp in remote:
            cp.wait()

    any_spec = pl.BlockSpec(memory_space=pl.ANY)
    return pl.pallas_call(
        body, name=name, in_specs=[any_spec], out_specs=any_spec,
        out_shape=jax.ShapeDtypeStruct((3, rows, cols), chunks.dtype),
        scratch_shapes=[pltpu.SemaphoreType.DMA((3,)), pltpu.SemaphoreType.DMA((3,))],
    )(chunks)


def _swap_half_rows(chunks, name):
    n, rows, cols = chunks.shape
    r2 = rows // 2

    def body(in_ref, out_ref, send_sem, recv_sem):
        x, y, c = lax.axis_index("x"), lax.axis_index("y"), lax.axis_index("c")
        cp = pltpu.make_async_remote_copy(
            src_ref=in_ref.at[:, pl.ds((1 - c) * r2, r2), :], dst_ref=out_ref,
            send_sem=send_sem, recv_sem=recv_sem, device_id=(x, y, 1 - c), device_id_type=MESH)
        cp.start()
        cp.wait()

    any_spec = pl.BlockSpec(memory_space=pl.ANY)
    return pl.pallas_call(
        body, name=name, in_specs=[any_spec], out_specs=any_spec,
        out_shape=jax.ShapeDtypeStruct((n, r2, cols), chunks.dtype),
        scratch_shapes=[pltpu.SemaphoreType.DMA, pltpu.SemaphoreType.DMA],
    )(chunks)


def _gather_halves(half, name):
    def body(in_ref, out_ref, send_sem, recv_sem):
        x, y, c = lax.axis_index("x"), lax.axis_index("y"), lax.axis_index("c")
        cp = pltpu.make_async_remote_copy(
            src_ref=in_ref, dst_ref=out_ref, send_sem=send_sem, recv_sem=recv_sem,
            device_id=(x, y, 1 - c), device_id_type=MESH)
        cp.start()
        cp.wait()

    any_spec = pl.BlockSpec(memory_space=pl.ANY)
    other = pl.pallas_call(
        body, name=name, in_specs=[any_spec], out_specs=any_spec,
        out_shape=jax.ShapeDtypeStruct(half.shape, half.dtype),
        scratch_shapes=[pltpu.SemaphoreType.DMA, pltpu.SemaphoreType.DMA],
    )(half)
    axis = half.ndim - 2
    return lax.cond(lax.axis_index("c") == 0,
                    lambda: jnp.concatenate([half, other], axis=axis),
                    lambda: jnp.concatenate([other, half], axis=axis))


def _add(a, b, name):
    n, rows, cols = a.shape

    def body(a_ref, b_ref, o_ref):
        o_ref[...] = a_ref[...] + b_ref[...]

    spec = pl.BlockSpec((None, PK_TILE, cols), lambda k, i: (k, i, 0))
    return pl.pallas_call(
        body, name=name, grid=(n, rows // PK_TILE),
        in_specs=[spec, spec], out_specs=spec,
        out_shape=jax.ShapeDtypeStruct(a.shape, F32),
        compiler_params=_cparams(("parallel", "parallel")),
    )(a, b)


def _chip_sum(chunks, recv, name):
    _, rows, cols = chunks.shape

    def body(own_ref, r_ref, o_ref):
        o_ref[...] = (own_ref[...] + r_ref[0]) + (r_ref[1] + r_ref[2])

    me = 2 * lax.axis_index("x") + lax.axis_index("y")
    own = lax.dynamic_index_in_dim(chunks, me, axis=0, keepdims=False)
    return pl.pallas_call(
        body, name=name, grid=(rows // PK_TILE,),
        in_specs=[pl.BlockSpec((PK_TILE, cols), lambda i: (i, 0)),
                  pl.BlockSpec((3, PK_TILE, cols), lambda i: (0, i, 0))],
        out_specs=pl.BlockSpec((PK_TILE, cols), lambda i: (i, 0)),
        out_shape=jax.ShapeDtypeStruct((rows, cols), F32),
        compiler_params=_cparams(("parallel",)),
    )(own, recv)


def _adamw(w, m, v, g, name):
    shape = w.shape
    rows, cols = math.prod(shape[:-1]), shape[-1]
    w, m, v, g = (a.reshape(rows, cols) for a in (w, m, v, g))
    tile = next((t for t in (256, 128, 64, 32, 16, 8) if rows % t == 0), rows)
    c1 = 1.0 - ADAM_B1 ** ADAM_STEP
    c2 = 1.0 - ADAM_B2 ** ADAM_STEP

    def body(w_ref, m_ref, v_ref, g_ref, d_ref, nm_ref, nv_ref):
        g = g_ref[...]
        nm = ADAM_B1 * m_ref[...] + (1.0 - ADAM_B1) * g
        nv = ADAM_B2 * v_ref[...] + (1.0 - ADAM_B2) * (g * g)
        nm_ref[...] = nm
        nv_ref[...] = nv
        d_ref[...] = -ADAM_LR * ((nm / c1) / (jnp.sqrt(nv / c2) + ADAM_EPS) + ADAM_WD * w_ref[...])

    spec = pl.BlockSpec((tile, cols), lambda i: (i, 0))
    outs = pl.pallas_call(
        body, name=name, grid=(rows // tile,),
        in_specs=[spec] * 4, out_specs=[spec] * 3,
        out_shape=[jax.ShapeDtypeStruct((rows, cols), F32)] * 3,
        compiler_params=_cparams(("parallel",)),
    )(w, m, v, g)
    return [o.reshape(shape) for o in outs]


def _full_weights(big, small):
    def gather(src, r0, shape, axis):
        n = math.prod(shape)
        nr = -(-n // D_MODEL)
        parts = [src[k, r0:r0 + nr].reshape(-1)[:n].reshape(shape) for k in range(4)]
        return jnp.concatenate(parts, axis=axis), r0 + nr
    w_in, r = gather(big, 0, (1024, 1282), 1)
    w_o_fox, r = gather(big, r, (512, 256), 1)
    w_o_sb, r = gather(big, r, (512, 256), 1)
    w_out, r = gather(big, r, (256, 1024), 0)
    w_up, r = gather(big, r, (1024, 1408), 1)
    w_down, r = gather(big, r, (704, 1024), 0)
    meta, r = gather(small, 0, (16, 256), 1)
    gains, r = gather(small, r, (4, 256), 1)
    conv_w, r = gather(small, r, (3, 1408), 1)
    return meta, gains, w_in, w_o_fox, w_o_sb, w_out, w_up, conv_w, w_down


def _chunks_for_chips(grads):
    d_meta, d_gains, d_w_in, d_b, d_w_o_fox, d_w_o_sb, d_w_out, d_w_up, d_conv_w, d_conv_b, d_w_down = grads
    out = []
    for k in range(4):
        col = lambda a, w: a[:, k * w:(k + 1) * w]
        row = lambda a, w: a[k * w:(k + 1) * w]
        out.append(_pack(col(d_meta, 256), col(d_gains, 256), col(d_w_in, 1282), d_b, col(d_w_o_fox, 256),
                         col(d_w_o_sb, 256), row(d_w_out, 256), col(d_w_up, 1408), col(d_conv_w, 1408),
                         d_conv_b, row(d_w_down, 704)))
    return jnp.stack(out, axis=0)


def kernel(x, meta_tokens, norm_gains, w_in, b_forget, w_o_fox, w_o_sb, w_out, w_up, conv_w, conv_b, w_down, loss_target, m_meta_tokens, m_norm_gains, m_w_in, m_b_forget, m_w_o_fox, m_w_o_sb, m_w_out, m_w_up, m_conv_w, m_conv_b, m_w_down, v_meta_tokens, v_norm_gains, v_w_in, v_b_forget, v_w_o_fox, v_w_o_sb, v_w_out, v_w_up, v_conv_w, v_conv_b, v_w_down):
    names = ("meta_tokens", "norm_gains", "w_in", "b_forget", "w_o_fox", "w_o_sb", "w_out", "w_up", "conv_w",
             "conv_b", "w_down")
    ws = (meta_tokens, norm_gains, w_in, b_forget, w_o_fox, w_o_sb, w_out, w_up, conv_w, conv_b, w_down)
    ms = (m_meta_tokens, m_norm_gains, m_w_in, m_b_forget, m_w_o_fox, m_w_o_sb, m_w_out, m_w_up, m_conv_w,
          m_conv_b, m_w_down)
    vs = (v_meta_tokens, v_norm_gains, v_w_in, v_b_forget, v_w_o_fox, v_w_o_sb, v_w_out, v_w_up, v_conv_w,
          v_conv_b, v_w_down)
    wp = _pack(meta_tokens, norm_gains[0], w_in[0], b_forget[0], w_o_fox[0], w_o_sb[0], w_out[0], w_up[0],
               conv_w[0], conv_b[0], w_down[0])

    hb = PK_BIG_ROWS // 2
    big_half = lax.dynamic_slice_in_dim(wp[:PK_BIG_ROWS].astype(BF16), lax.axis_index("c") * hb, hb, axis=0)
    big_half, small = _all_gather_chips(
        [big_half, wp[PK_BIG_ROWS:PK_BIG_ROWS + PK_SMALL_ROWS]], "gather_weights")
    big = _gather_halves(big_half, "gather_weight_halves")
    meta, gains, f_w_in, f_w_o_fox, f_w_o_sb, f_w_out, f_w_up, f_conv_w, f_w_down = _full_weights(big, small)

    ss, dx, grads = _local_step(x[0], loss_target[0], meta, gains, f_w_in, b_forget[0], f_w_o_fox, f_w_o_sb,
                                f_w_out, f_w_up, f_conv_w, conv_b[0], f_w_down)
    loss = lax.psum(0.5 * ss / D_MODEL, ("x", "y", "c"))

    chunks = _chunks_for_chips(grads)
    r2 = PK_ROWS // 2
    from_sibling = _swap_half_rows(chunks, "swap_halves")
    own = lax.dynamic_slice_in_dim(chunks, lax.axis_index("c") * r2, r2, axis=1)
    core_sum = _add(own, from_sibling, "core_sum")
    recv = _scatter_chips(core_sum, "scatter_grads")
    g = _gather_halves(_chip_sum(core_sum, recv, "chip_sum"), "gather_halves")
    gs = _unpack(g)
    steps = [_adamw(w, m, v, gp, "adamw_" + n) for n, w, m, v, gp in zip(names, ws, ms, vs, gs)]
    return (loss, dx[None], *gs, *[s[0] for s in steps], *[s[1] for s in steps], *[s[2] for s in steps])
```

```python
import functools
import math

import jax
import jax.numpy as jnp
from jax import lax
from jax.experimental import pallas as pl
from jax.experimental.pallas import tpu as pltpu

F32 = jnp.float32
BF16 = jnp.bfloat16

D_MODEL = 1024
N_META = 16
HEAD_DIM = 64
N_PAIRS = 4
W_ATT = 512
D_FF = 2816
EPS = 1e-6
NEG = -1e30
TQ = 256
PAD = TQ - N_META
TCONV = 128
HALO = 16
LANES = 128
VMEM_LIMIT = 56 * 1024 * 1024

ADAM_LR = 0.001
ADAM_B1 = 0.9
ADAM_B2 = 0.999
ADAM_EPS = 1e-08
ADAM_WD = 0.01
ADAM_STEP = 10

MESH = pl.DeviceIdType.MESH

PK_BIG_ROWS = 3936
PK_SMALL_ROWS = 24
PK_ROWS = 4096
PK_TILE = 128


def _cparams(sem, **kw):
    return pltpu.CompilerParams(dimension_semantics=sem, vmem_limit_bytes=VMEM_LIMIT, **kw)


def _row_tile(lp):
    return 768 if lp % 768 == 0 else 256


def _wide_tile(n):
    return next(t for t in (1408, 1280, 1024, 512, 256) if n % t == 0)


def _rms(x):
    return lax.rsqrt(jnp.mean(x * x, axis=-1, keepdims=True) + EPS)


def _log_sigmoid(x):
    return jnp.minimum(x, 0.0) - jnp.log(1.0 + jnp.exp(-jnp.abs(x)))


_LOG2E = 1.4426950408889634
_LN2 = 0.6931471805599453


def _log_keep(z):
    t = jnp.exp2(jnp.abs(z) * (-_LOG2E))
    return jnp.log2(1.0 + t) * (-_LN2) - jnp.maximum(z, 0.0)


def _split2(x):
    hi = x.astype(BF16)
    lo = (x - hi.astype(F32)).astype(BF16)
    return hi, lo


def _dot(a, b):
    return jnp.dot(a, b, preferred_element_type=F32)


def _dot_nt(a, b):
    return lax.dot_general(a, b, (((1,), (1,)), ((), ())), preferred_element_type=F32)


def _dot_tn(a, b):
    return lax.dot_general(a, b, (((0,), (0,)), ((), ())), preferred_element_type=F32)


def _rms_mm(h, g, w, name):
    lp, dm = h.shape
    n = w.shape[1]
    tr, tn = _row_tile(lp), _wide_tile(n)

    def body(h_ref, g_ref, w_ref, out_ref, xn_ref):
        @pl.when(pl.program_id(1) == 0)
        def _():
            x = h_ref[...]
            xn_ref[...] = (x * _rms(x) * g_ref[...]).astype(BF16)
        out_ref[...] = _dot(xn_ref[...], w_ref[...]).astype(BF16)

    return pl.pallas_call(
        body, name=name, grid=(lp // tr, n // tn),
        in_specs=[pl.BlockSpec((tr, dm), lambda i, j: (i, 0)),
                  pl.BlockSpec((1, dm), lambda i, j: (0, 0)),
                  pl.BlockSpec((dm, tn), lambda i, j: (0, j))],
        out_specs=[pl.BlockSpec((tr, tn), lambda i, j: (i, j)),
                   pl.BlockSpec((tr, dm), lambda i, j: (i, 0))],
        out_shape=[jax.ShapeDtypeStruct((lp, n), BF16), jax.ShapeDtypeStruct((lp, dm), BF16)],
        compiler_params=_cparams(("parallel", "arbitrary")),
    )(h, g, w)


def _mm_rmsbwd(dy, w, h, g, dh_in, name):
    lp, kd = dy.shape
    dm = w.shape[0]
    tr, tk = 384, _wide_tile(kd)
    nk = kd // tk

    def body(dy_ref, w_ref, h_ref, g_ref, dhin_ref, dh_ref, dg_ref, acc_ref):
        i, k = pl.program_id(0), pl.program_id(1)

        @pl.when(k == 0)
        def _():
            acc_ref[...] = jnp.zeros_like(acc_ref)

        acc_ref[...] += _dot_nt(dy_ref[...], w_ref[...])

        @pl.when(k == nk - 1)
        def _():
            dxn = acc_ref[...]
            x = h_ref[...]
            r = _rms(x)
            yhat = x * r
            part = jnp.sum(dxn * yhat, axis=0, keepdims=True)
            dyh = dxn * g_ref[...]
            dx = r * (dyh - yhat * jnp.mean(dyh * yhat, axis=-1, keepdims=True))
            dh_ref[...] = dhin_ref[...] + dx

            @pl.when(i == 0)
            def _():
                dg_ref[...] = part

            @pl.when(i > 0)
            def _():
                dg_ref[...] += part

    return pl.pallas_call(
        body, name=name, grid=(lp // tr, nk),
        in_specs=[pl.BlockSpec((tr, tk), lambda i, k: (i, k)),
                  pl.BlockSpec((dm, tk), lambda i, k: (0, k)),
                  pl.BlockSpec((tr, dm), lambda i, k: (i, 0)),
                  pl.BlockSpec((1, dm), lambda i, k: (0, 0)),
                  pl.BlockSpec((tr, dm), lambda i, k: (i, 0))],
        out_specs=[pl.BlockSpec((tr, dm), lambda i, k: (i, 0)),
                   pl.BlockSpec((1, dm), lambda i, k: (0, 0))],
        out_shape=[jax.ShapeDtypeStruct((lp, dm), F32), jax.ShapeDtypeStruct((1, dm), F32)],
        scratch_shapes=[pltpu.VMEM((tr, dm), F32)],
        compiler_params=_cparams(("arbitrary", "arbitrary")),
    )(dy, w, h, g, dh_in)


def _mm_tn(x, dy, name):
    lp, kd = x.shape
    n = dy.shape[1]
    tl = _row_tile(lp)
    tk = _wide_tile(kd)
    tn = _wide_tile(n)
    nl = lp // tl

    def body(x_ref, dy_ref, o_ref):
        @pl.when(pl.program_id(2) == 0)
        def _():
            o_ref[...] = jnp.zeros_like(o_ref)
        o_ref[...] += _dot_tn(x_ref[...], dy_ref[...])

    return pl.pallas_call(
        body, name=name, grid=(kd // tk, n // tn, nl),
        in_specs=[pl.BlockSpec((tl, tk), lambda a, b, l: (l, a)),
                  pl.BlockSpec((tl, tn), lambda a, b, l: (l, b))],
        out_specs=pl.BlockSpec((tk, tn), lambda a, b, l: (a, b)),
        out_shape=jax.ShapeDtypeStruct((kd, n), F32),
        compiler_params=_cparams(("parallel", "parallel", "arbitrary")),
    )(x, dy)


def _logf(xn, wf, bf, name):
    lp, dm = xn.shape
    tr = _row_tile(lp)

    def body(xn_ref, wf_ref, b_ref, o_ref):
        f = _dot(xn_ref[...], wf_ref[...]) + b_ref[...]
        row = pl.program_id(0) * tr + lax.broadcasted_iota(jnp.int32, f.shape, 0)
        lane = lax.broadcasted_iota(jnp.int32, f.shape, 1)
        o_ref[...] = jnp.where((row >= PAD) & (lane < 8), _log_sigmoid(f), 0.0)

    return pl.pallas_call(
        body, name=name, grid=(lp // tr,),
        in_specs=[pl.BlockSpec((tr, dm), lambda i: (i, 0)),
                  pl.BlockSpec((dm, LANES), lambda i: (0, 0)),
                  pl.BlockSpec((1, LANES), lambda i: (0, 0))],
        out_specs=pl.BlockSpec((tr, LANES), lambda i: (i, 0)),
        out_shape=jax.ShapeDtypeStruct((lp, LANES), F32),
        compiler_params=_cparams(("parallel",)),
    )(xn, wf, bf)


def _tri(n, rel):
    r = lax.broadcasted_iota(jnp.int32, (n, n), 0)
    c = lax.broadcasted_iota(jnp.int32, (n, n), 1)
    return rel(r, c).astype(BF16)


def _cumsum_rows(x, name):
    lp = x.shape[0]
    nb = lp // TQ
    tl = _tri(TQ, lambda r, c: c <= r)

    def body(x_ref, t_ref, o_ref):
        def step(b, carry):
            rows = pl.ds(pl.multiple_of(b * TQ, TQ), TQ)
            xb = x_ref[rows, :]
            hi = xb.astype(BF16)
            r1 = xb - hi.astype(F32)
            mid = r1.astype(BF16)
            lo = (r1 - mid.astype(F32)).astype(BF16)
            t = t_ref[...]
            o_ref[rows, :] = carry + (_dot(t, hi) + _dot(t, mid) + _dot(t, lo))
            return carry + jnp.sum(xb, axis=0, keepdims=True)
        lax.fori_loop(0, nb, step, jnp.zeros((1, LANES), F32))

    return pl.pallas_call(
        body, name=name,
        in_specs=[pl.BlockSpec(memory_space=pltpu.VMEM)] * 2,
        out_specs=pl.BlockSpec(memory_space=pltpu.VMEM),
        out_shape=jax.ShapeDtypeStruct((lp, LANES), F32),
        compiler_params=pltpu.CompilerParams(vmem_limit_bytes=VMEM_LIMIT),
    )(x, tl)


def _dlogf(dc, logf, name):
    lp = dc.shape[0]
    nb = lp // TQ
    tu = _tri(TQ, lambda r, c: c >= r)

    def body(x_ref, lf_ref, t_ref, df_ref, db_ref):
        def step(bb, carry):
            run, db = carry
            b = nb - 1 - bb
            rows = pl.ds(pl.multiple_of(b * TQ, TQ), TQ)
            xb = x_ref[rows, :]
            hi = xb.astype(BF16)
            r1 = xb - hi.astype(F32)
            mid = r1.astype(BF16)
            lo = (r1 - mid.astype(F32)).astype(BF16)
            t = t_ref[...]
            dlf = run + (_dot(t, hi) + _dot(t, mid) + _dot(t, lo))
            df = dlf * (1.0 - jnp.exp(lf_ref[rows, :]))
            df_ref[rows, :] = df
            return run + jnp.sum(xb, axis=0, keepdims=True), db + jnp.sum(df, axis=0, keepdims=True)
        z = jnp.zeros((1, LANES), F32)
        _, db = lax.fori_loop(0, nb, step, (z, z))
        db_ref[...] = db

    return pl.pallas_call(
        body, name=name,
        in_specs=[pl.BlockSpec(memory_space=pltpu.VMEM)] * 3,
        out_specs=[pl.BlockSpec(memory_space=pltpu.VMEM)] * 2,
        out_shape=[jax.ShapeDtypeStruct((lp, LANES), F32), jax.ShapeDtypeStruct((1, LANES), F32)],
        compiler_params=pltpu.CompilerParams(vmem_limit_bytes=VMEM_LIMIT),
    )(dc, logf, tu)


def _merge_fwd(o_a, o_b, proj, h0, w_oa, w_ob, w_out, g1, name):
    lp, dm = h0.shape
    tr = TQ
    ga_blk = (6 * W_ATT) // dm

    def body(oa_ref, ob_ref, ga_ref, gb_ref, h0_ref, woa_ref, wob_ref, wout_ref, g1_ref,
             ya_ref, yb_ref, m_ref, mixed_ref, h1_ref):
        ya = _dot(oa_ref[...], woa_ref[...])
        yb = _dot(ob_ref[...], wob_ref[...])
        m = jax.nn.sigmoid(ga_ref[...].astype(F32)) * ya + jax.nn.sigmoid(gb_ref[...].astype(F32)) * yb
        mb = m.astype(BF16)
        mixed = _dot(mb, wout_ref[...])
        ya_ref[...] = ya.astype(BF16)
        yb_ref[...] = yb.astype(BF16)
        m_ref[...] = mb
        mixed_ref[...] = mixed
        h1_ref[...] = h0_ref[...] + mixed * _rms(mixed) * g1_ref[...]

    row = lambda w: pl.BlockSpec((tr, w), lambda i: (i, 0))
    full = lambda a: pl.BlockSpec(a.shape, lambda i: (0, 0))
    return pl.pallas_call(
        body, name=name, grid=(lp // tr,),
        in_specs=[row(W_ATT), row(W_ATT),
                  pl.BlockSpec((tr, dm), lambda i: (i, ga_blk)),
                  pl.BlockSpec((tr, dm), lambda i: (i, ga_blk + 1)),
                  row(dm), full(w_oa), full(w_ob), full(w_out), full(g1)],
        out_specs=[row(dm)] * 5,
        out_shape=[jax.ShapeDtypeStruct((lp, dm), BF16)] * 3 + [jax.ShapeDtypeStruct((lp, dm), F32)] * 2,
        compiler_params=_cparams(("parallel",)),
    )(o_a, o_b, proj, proj, h0, w_oa, w_ob, w_out, g1)


def _merge_bwd(dh1, mixed, g1, w_out, proj, y_a, y_b, w_oa, w_ob, name):
    lp, dm = dh1.shape
    tr = TQ
    ga_blk = (6 * W_ATT) // dm

    def body(dh_ref, mx_ref, g1_ref, wout_ref, ga_ref, gb_ref, ya_ref, yb_ref, woa_ref, wob_ref,
             dmx_ref, dya_ref, dyb_ref, dga_ref, dgb_ref, doa_ref, dob_ref, dg1_ref):
        i = pl.program_id(0)
        dn = dh_ref[...]
        x = mx_ref[...]
        r = _rms(x)
        yhat = x * r
        part = jnp.sum(dn * yhat, axis=0, keepdims=True)
        dyh = dn * g1_ref[...]
        dmx = (r * (dyh - yhat * jnp.mean(dyh * yhat, axis=-1, keepdims=True))).astype(BF16)
        dmx_ref[...] = dmx
        dm_ = _dot_nt(dmx, wout_ref[...])
        sa = jax.nn.sigmoid(ga_ref[...].astype(F32))
        sb = jax.nn.sigmoid(gb_ref[...].astype(F32))
        dya = (dm_ * sa).astype(BF16)
        dyb = (dm_ * sb).astype(BF16)
        dya_ref[...] = dya
        dyb_ref[...] = dyb
        dga_ref[...] = (dm_ * ya_ref[...].astype(F32) * sa * (1.0 - sa)).astype(BF16)
        dgb_ref[...] = (dm_ * yb_ref[...].astype(F32) * sb * (1.0 - sb)).astype(BF16)
        doa_ref[...] = _dot_nt(dya, woa_ref[...]).astype(BF16)
        dob_ref[...] = _dot_nt(dyb, wob_ref[...]).astype(BF16)

        @pl.when(i == 0)
        def _():
            dg1_ref[...] = part

        @pl.when(i > 0)
        def _():
            dg1_ref[...] += part

    row = lambda w: pl.BlockSpec((tr, w), lambda i: (i, 0))
    full = lambda a: pl.BlockSpec(a.shape, lambda i: (0, 0))
    return pl.pallas_call(
        body, name=name, grid=(lp // tr,),
        in_specs=[row(dm), row(dm), full(g1), full(w_out),
                  pl.BlockSpec((tr, dm), lambda i: (i, ga_blk)),
                  pl.BlockSpec((tr, dm), lambda i: (i, ga_blk + 1)),
                  row(dm), row(dm), full(w_oa), full(w_ob)],
        out_specs=[row(dm)] * 5 + [row(W_ATT)] * 2 + [pl.BlockSpec((1, dm), lambda i: (0, 0))],
        out_shape=[jax.ShapeDtypeStruct((lp, dm), BF16)] * 5 + [jax.ShapeDtypeStruct((lp, W_ATT), BF16)] * 2
        + [jax.ShapeDtypeStruct((1, dm), F32)],
        compiler_params=_cparams(("arbitrary",)),
    )(dh1, mixed, g1, w_out, proj, proj, y_a, y_b, w_oa, w_ob)


_GELU_C = math.sqrt(2.0 / math.pi)
_GELU_A = 0.044715


def _gelu(x):
    t = jnp.tanh(_GELU_C * (x + _GELU_A * x * x * x))
    return 0.5 * x * (1.0 + t), t


CW = 256


def _taps(cur_ref, prev_ref, first, c0):
    cur = cur_ref[:, c0:c0 + CW].astype(F32)
    p1 = jnp.where(first, 0.0, prev_ref[HALO - 1:HALO, c0:c0 + CW].astype(F32))
    p2 = jnp.where(first, 0.0, prev_ref[HALO - 2:HALO - 1, c0:c0 + CW].astype(F32))
    row = lax.broadcasted_iota(jnp.int32, cur.shape, 0)
    x1 = jnp.where(row == 0, p1, pltpu.roll(cur, 1, 0))
    x2 = jnp.where(row == 0, p2, jnp.where(row == 1, p1, pltpu.roll(cur, 2, 0)))
    return cur, x1, x2


def _conv_at(cur_ref, prev_ref, w_ref, b_ref, first, c0):
    cur, x1, x2 = _taps(cur_ref, prev_ref, first, c0)
    cols = slice(c0, c0 + CW)
    u = b_ref[:, cols] + w_ref[0:1, cols] * x2 + w_ref[1:2, cols] * x1 + w_ref[2:3, cols] * cur
    return u, (x2, x1, cur)


def _up_specs(tr, width):
    per = tr // HALO
    return [pl.BlockSpec((tr, width), lambda i: (i, 0)),
            pl.BlockSpec((HALO, width), lambda i: (jnp.maximum(i * per - 1, 0), 0))]


def _convgate_fwd(up, conv_w, conv_b, name):
    lp, c2 = up.shape
    tr = TCONV

    def body(cur_ref, prev_ref, w_ref, b_ref, a_ref):
        first = pl.program_id(0) == 0
        for c0 in range(0, D_FF, CW):
            ug, _ = _conv_at(cur_ref, prev_ref, w_ref, b_ref, first, c0)
            uv, _ = _conv_at(cur_ref, prev_ref, w_ref, b_ref, first, D_FF + c0)
            gel, _ = _gelu(ug)
            a_ref[:, c0:c0 + CW] = (gel * uv).astype(BF16)

    return pl.pallas_call(
        body, name=name, grid=(lp // tr,),
        in_specs=_up_specs(tr, c2) + [pl.BlockSpec((3, c2), lambda i: (0, 0)),
                                      pl.BlockSpec((1, c2), lambda i: (0, 0))],
        out_specs=pl.BlockSpec((tr, D_FF), lambda i: (i, 0)),
        out_shape=jax.ShapeDtypeStruct((lp, D_FF), BF16),
        compiler_params=_cparams(("parallel",)),
    )(up, up, conv_w, conv_b)


def _convgate_bwd(up, da, conv_w, conv_b, name):
    lp, c2 = up.shape
    tr = TCONV

    def body(cur_ref, prev_ref, da_ref, w_ref, b_ref, du_ref, dw_ref, db_ref):
        i = pl.program_id(0)
        first = i == 0

        @pl.when(first)
        def _():
            dw_ref[...] = jnp.zeros_like(dw_ref)
            db_ref[...] = jnp.zeros_like(db_ref)

        for c0 in range(0, D_FF, CW):
            ug, taps_g = _conv_at(cur_ref, prev_ref, w_ref, b_ref, first, c0)
            uv, taps_v = _conv_at(cur_ref, prev_ref, w_ref, b_ref, first, D_FF + c0)
            gel, t = _gelu(ug)
            dgel = 0.5 * (1.0 + t) + 0.5 * ug * (1.0 - t * t) * _GELU_C * (1.0 + 3.0 * _GELU_A * ug * ug)
            da_ = da_ref[:, c0:c0 + CW].astype(F32)
            for base, du, taps in ((c0, da_ * uv * dgel, taps_g), (D_FF + c0, da_ * gel, taps_v)):
                cols = slice(base, base + CW)
                du_ref[:, cols] = du.astype(BF16)
                for tap in range(3):
                    dw_ref[tap:tap + 1, cols] += jnp.sum(du * taps[tap], axis=0, keepdims=True)
                db_ref[:, cols] += jnp.sum(du, axis=0, keepdims=True)

    return pl.pallas_call(
        body, name=name, grid=(lp // tr,),
        in_specs=_up_specs(tr, c2) + [pl.BlockSpec((tr, D_FF), lambda i: (i, 0)),
                                      pl.BlockSpec((3, c2), lambda i: (0, 0)),
                                      pl.BlockSpec((1, c2), lambda i: (0, 0))],
        out_specs=[pl.BlockSpec((tr, c2), lambda i: (i, 0)),
                   pl.BlockSpec((3, c2), lambda i: (0, 0)),
                   pl.BlockSpec((1, c2), lambda i: (0, 0))],
        out_shape=[jax.ShapeDtypeStruct((lp, c2), BF16), jax.ShapeDtypeStruct((3, c2), F32),
                   jax.ShapeDtypeStruct((1, c2), F32)],
        compiler_params=_cparams(("arbitrary",)),
    )(up, up, da, conv_w, conv_b)


def _conv_transpose(du, conv_w, name):
    lp, c2 = du.shape
    tr = TCONV
    per = tr // HALO
    n_halo = lp // HALO
    nt = lp // tr

    def body(cur_ref, nxt_ref, w_ref, o_ref):
        last = pl.program_id(0) == nt - 1
        for c0 in range(0, c2, CW):
            cols = slice(c0, c0 + CW)
            cur = cur_ref[:, cols].astype(F32)
            n0 = jnp.where(last, 0.0, nxt_ref[0:1, cols].astype(F32))
            n1 = jnp.where(last, 0.0, nxt_ref[1:2, cols].astype(F32))
            row = lax.broadcasted_iota(jnp.int32, cur.shape, 0)
            y1 = jnp.where(row == tr - 1, n0, pltpu.roll(cur, tr - 1, 0))
            y2 = jnp.where(row == tr - 1, n1, jnp.where(row == tr - 2, n0, pltpu.roll(cur, tr - 2, 0)))
            o_ref[:, cols] = (w_ref[2:3, cols] * cur + w_ref[1:2, cols] * y1 + w_ref[0:1, cols] * y2).astype(BF16)

    return pl.pallas_call(
        body, name=name, grid=(nt,),
        in_specs=[pl.BlockSpec((tr, c2), lambda i: (i, 0)),
                  pl.BlockSpec((HALO, c2), lambda i: (jnp.minimum((i + 1) * per, n_halo - 1), 0)),
                  pl.BlockSpec((3, c2), lambda i: (0, 0))],
        out_specs=pl.BlockSpec((tr, c2), lambda i: (i, 0)),
        out_shape=jax.ShapeDtypeStruct((lp, c2), BF16),
        compiler_params=_cparams(("parallel",)),
    )(du, du, conv_w)


def _down_loss(a, w_down, h1, g3, target, name):
    lp, dm = h1.shape
    tr = TQ

    def body(a_ref, w_ref, h1_ref, g_ref, t_ref, ffn_ref, dy_ref, ss_ref):
        i = pl.program_id(0)
        ffn = _dot(a_ref[...], w_ref[...])
        ffn_ref[...] = ffn
        h2 = h1_ref[...] + ffn * _rms(ffn) * g_ref[...]
        d = jnp.where(i > 0, h2 - t_ref[...], 0.0)
        dy_ref[...] = d * (1.0 / dm)
        part = jnp.sum(jnp.sum(d * d, axis=0, keepdims=True), axis=1, keepdims=True)

        @pl.when(i == 0)
        def _():
            ss_ref[...] = jnp.zeros_like(ss_ref)

        ss_ref[...] += part

    return pl.pallas_call(
        body, name=name, grid=(lp // tr,),
        in_specs=[pl.BlockSpec((tr, D_FF), lambda i: (i, 0)),
                  pl.BlockSpec(w_down.shape, lambda i: (0, 0)),
                  pl.BlockSpec((tr, dm), lambda i: (i, 0)),
                  pl.BlockSpec((1, dm), lambda i: (0, 0)),
                  pl.BlockSpec((tr, dm), lambda i: (jnp.maximum(i - 1, 0), 0))],
        out_specs=[pl.BlockSpec((tr, dm), lambda i: (i, 0)),
                   pl.BlockSpec((tr, dm), lambda i: (i, 0)),
                   pl.BlockSpec((8, LANES), lambda i: (0, 0))],
        out_shape=[jax.ShapeDtypeStruct((lp, dm), F32), jax.ShapeDtypeStruct((lp, dm), F32),
                   jax.ShapeDtypeStruct((8, LANES), F32)],
        compiler_params=_cparams(("arbitrary",)),
    )(a, w_down, h1, g3, target)


def _down_bwd(dy, ffn, g3, w_down, name):
    lp, dm = dy.shape
    tr = TQ

    def body(dy_ref, f_ref, g_ref, w_ref, dffn_ref, da_ref, dg_ref):
        i = pl.program_id(0)
        dn = dy_ref[...]
        x = f_ref[...]
        r = _rms(x)
        yhat = x * r
        part = jnp.sum(dn * yhat, axis=0, keepdims=True)
        dyh = dn * g_ref[...]
        dffn = (r * (dyh - yhat * jnp.mean(dyh * yhat, axis=-1, keepdims=True))).astype(BF16)
        dffn_ref[...] = dffn
        da_ref[...] = _dot_nt(dffn, w_ref[...]).astype(BF16)

        @pl.when(i == 0)
        def _():
            dg_ref[...] = part

        @pl.when(i > 0)
        def _():
            dg_ref[...] += part

    return pl.pallas_call(
        body, name=name, grid=(lp // tr,),
        in_specs=[pl.BlockSpec((tr, dm), lambda i: (i, 0)),
                  pl.BlockSpec((tr, dm), lambda i: (i, 0)),
                  pl.BlockSpec((1, dm), lambda i: (0, 0)),
                  pl.BlockSpec(w_down.shape, lambda i: (0, 0))],
        out_specs=[pl.BlockSpec((tr, dm), lambda i: (i, 0)),
                   pl.BlockSpec((tr, D_FF), lambda i: (i, 0)),
                   pl.BlockSpec((1, dm), lambda i: (0, 0))],
        out_shape=[jax.ShapeDtypeStruct((lp, dm), BF16), jax.ShapeDtypeStruct((lp, D_FF), BF16),
                   jax.ShapeDtypeStruct((1, dm), F32)],
        compiler_params=_cparams(("arbitrary",)),
    )(dy, ffn, g3, w_down)


def _pair_specs(lp, base):
    return [pl.BlockSpec((TQ, LANES), lambda p, i: (i, base + p)),
            pl.BlockSpec((lp, LANES), lambda p, i: (0, base + N_PAIRS + p)),
            pl.BlockSpec((lp, LANES), lambda p, i: (0, base + 2 * N_PAIRS + p))]


def _col_spec():
    return pl.BlockSpec((None, 2, TQ, 1), lambda p, i: (p, 0, i, 0))


def _rowvec_spec(nb):
    return pl.BlockSpec((None, 2, nb, 1, TQ), lambda p, i: (p, 0, 0, 0, 0))


def _tile_spec():
    return pl.BlockSpec((TQ, LANES), lambda p, i: (i, p))


RC = 64
T2 = 2 * TQ


def _stack_heads(x, scale=None):
    lane = lax.broadcasted_iota(jnp.int32, x.shape, 1)
    zero = jnp.zeros_like(x)
    x2 = jnp.concatenate([jnp.where(lane < HEAD_DIM, x, zero), jnp.where(lane >= HEAD_DIM, x, zero)], axis=0)
    return x2 if scale is None else x2 * scale


def _unstack_heads(x2):
    lane = lax.broadcasted_iota(jnp.int32, (TQ, LANES), 1)
    return jnp.where(lane < HEAD_DIM, x2[:TQ], x2[TQ:])


def _stack_cols(ref):
    return jnp.concatenate([ref[0], ref[1]], axis=0)


def _chunk_valid(i, j, r, strict):
    qpos = i * TQ + (r % TQ) + lax.broadcasted_iota(jnp.int32, (RC, TQ), 0)
    kpos = j * TQ + lax.broadcasted_iota(jnp.int32, (RC, TQ), 1)
    causal = (kpos < qpos) if strict else (kpos <= qpos)
    return causal & (kpos >= PAD)


def _walk_tiles(i, step, reverse):
    first, last = (i, 0) if reverse else (0, i)
    step(first, True)
    _between_unrolled(i, lambda j, nxt: step(j, False), reverse)

    @pl.when(i > 0)
    def _():
        step(last, True)


UNROLL = 4


def _between_unrolled(i, step, reverse):
    tile = (lambda t: i - 1 - t) if reverse else (lambda t: t + 1)
    after = (lambda j: j - 1) if reverse else (lambda j: j + 1)
    n = jnp.maximum(i - 1, 0)

    def group(u, c):
        j = tile(UNROLL * u)
        for _ in range(UNROLL):
            step(j, after(j))
            j = after(j)
        return c

    lax.fori_loop(0, n // UNROLL, group, 0)
    for k in range(UNROLL - 1):
        @pl.when(n % UNROLL > k)
        def _(k=k):
            j = tile(n - n % UNROLL + k)
            step(j, after(j))


_HALF = (slice(0, TQ), slice(TQ, T2))


def _walk_tiles_lead(i, lead, step, reverse):
    first, last = (i, 0) if reverse else (0, i)
    lead(first, 0)
    lead(first, 1)
    step(first, jnp.maximum(i - 1, 0) if reverse else jnp.minimum(1, i), True)

    _between_unrolled(i, lambda j, nxt: step(j, nxt, False), reverse)

    @pl.when(i > 0)
    def _():
        step(last, None, True)


def _krows(j):
    return pl.ds(pl.multiple_of(j * TQ, TQ), TQ)


def _fox_fwd(proj, crow, name):
    lp = proj.shape[0]
    nb = lp // TQ

    def body(q_ref, k_ref, v_ref, cr_ref, o_ref, lse_ref, m_ref, acc_ref, p_ref):
        i = pl.program_id(1)
        q2 = _stack_heads(q_ref[...], 0.125)
        m_ref[...] = jnp.full(m_ref.shape, NEG, F32)
        acc_ref[...] = jnp.zeros_like(acc_ref)
        lane = lax.broadcasted_iota(jnp.int32, (TQ, LANES), 1)

        def step(j, masked):
            rows_j = _krows(j)
            s = _dot_nt(q2, k_ref[rows_j, :])
            v = v_ref[rows_j, :]
            one = jnp.ones_like(v)
            v_heads = (jnp.where(lane < HEAD_DIM, v, one), jnp.where(lane >= HEAD_DIM, v, one))
            for r in range(0, T2, RC):
                rows = slice(r, r + RC)
                s_c = s[rows] - cr_ref[r // TQ, j]
                if masked:
                    s_c = jnp.where(_chunk_valid(i, j, r, False), s_c, NEG)
                s0, s1 = s_c[:, :LANES], s_c[:, LANES:]
                m_old = m_ref[rows]
                m_new = jnp.maximum(m_old, jnp.max(jnp.maximum(s0, s1), axis=-1, keepdims=True))
                m_ref[rows] = m_new
                acc_ref[rows] = jnp.exp(m_old - m_new) * acc_ref[rows]
                p_ref[rows, :LANES] = jnp.exp(s0 - m_new).astype(BF16)
                p_ref[rows, LANES:] = jnp.exp(s1 - m_new).astype(BF16)
            for h in range(2):
                acc_ref[_HALF[h]] += _dot(p_ref[_HALF[h]], v_heads[h])

        _walk_tiles(i, step, reverse=False)
        acc = acc_ref[...]
        m = m_ref[...]
        outs = []
        for h in range(2):
            a_h = acc[_HALF[h]]
            l = a_h[:, HEAD_DIM:HEAD_DIM + 1] if h == 0 else a_h[:, 0:1]
            lse_ref[h] = m[_HALF[h]][:, 0:1] + jnp.log(l)
            outs.append(a_h / l)
        o_ref[...] = jnp.where(lane < HEAD_DIM, outs[0], outs[1]).astype(BF16)

    return pl.pallas_call(
        body, name=name, grid=(N_PAIRS, nb),
        in_specs=_pair_specs(lp, 0) + [_rowvec_spec(nb)],
        out_specs=[_tile_spec(), _col_spec()],
        out_shape=[jax.ShapeDtypeStruct((lp, W_ATT), BF16),
                   jax.ShapeDtypeStruct((N_PAIRS, 2, lp, 1), F32)],
        scratch_shapes=[pltpu.VMEM((T2, LANES), F32), pltpu.VMEM((T2, LANES), F32), pltpu.VMEM((T2, TQ), BF16)],
        compiler_params=_cparams(("parallel", "arbitrary")),
    )(proj, proj, proj, crow)


def _fox_bwd(proj, do, o, lse, crow, name):
    lp = proj.shape[0]
    nb = lp // TQ

    def body(q_ref, k_ref, v_ref, do_ref, o_ref, lse_ref, cr_ref,
             dq_ref, dk_ref, dv_ref, dcs_ref, dct_ref,
             dk_acc, dv_acc, dq_acc, dct_acc, p_ref, ds_ref, s_ref, dp_ref):
        i = pl.program_id(1)

        @pl.when(i == 0)
        def _():
            dk_acc[...] = jnp.zeros_like(dk_acc)
            dv_acc[...] = jnp.zeros_like(dv_acc)
            dcs_ref[...] = jnp.zeros_like(dcs_ref)

        dq_acc[...] = jnp.zeros_like(dq_acc)
        dct_acc[...] = jnp.zeros_like(dct_acc)
        do_ = do_ref[...]
        q2 = _stack_heads(q_ref[...], 0.125)
        do2 = _stack_heads(do_)
        prod = do_.astype(F32) * o_ref[...].astype(F32)
        lane = lax.broadcasted_iota(jnp.int32, prod.shape, 1)
        delta2 = jnp.concatenate(
            [jnp.sum(jnp.where(lane < HEAD_DIM, prod, 0.0), axis=-1, keepdims=True),
             jnp.sum(jnp.where(lane >= HEAD_DIM, prod, 0.0), axis=-1, keepdims=True)], axis=0)
        lse2 = _stack_cols(lse_ref)

        def lead(j, h):
            rows_j = _krows(j)
            s_ref[_HALF[h]] = _dot_nt(q2[_HALF[h]], k_ref[rows_j, :])
            dp_ref[_HALF[h]] = _dot_nt(do2[_HALF[h]], v_ref[rows_j, :])

        def step(j, nxt, masked):
            rows_j = _krows(j)
            k = k_ref[rows_j, :]
            for h in range(2):
                cs = jnp.zeros((1, TQ), F32)
                for r in range(h * TQ, (h + 1) * TQ, RC):
                    rows = slice(r, r + RC)
                    p = jnp.exp(s_ref[rows] - cr_ref[h, j] - lse2[rows])
                    if masked:
                        p = jnp.where(_chunk_valid(i, j, r, False), p, 0.0)
                    ds = p * (dp_ref[rows] - delta2[rows])
                    p_ref[rows] = p.astype(BF16)
                    ds_ref[rows] = ds.astype(BF16)
                    dct_acc[rows] += jnp.sum(ds, axis=-1, keepdims=True)
                    cs = cs + jnp.sum(ds, axis=0, keepdims=True)
                dcs_ref[h, j] -= cs
                if nxt is not None:
                    lead(nxt, h)
                dsb = ds_ref[_HALF[h]]
                dq_acc[_HALF[h]] += _dot(dsb, k)
                dk_acc[rows_j, :] += _dot_tn(dsb, q2[_HALF[h]])
                dv_acc[rows_j, :] += _dot_tn(p_ref[_HALF[h]], do2[_HALF[h]])

        _walk_tiles_lead(i, lead, step, reverse=False)
        dct = dct_acc[...]
        dct_ref[0] = dct[:TQ]
        dct_ref[1] = dct[TQ:]
        dq_ref[...] = (_unstack_heads(dq_acc[...]) * 0.125).astype(BF16)

        @pl.when(i == nb - 1)
        def _():
            dk_ref[...] = dk_acc[...].astype(BF16)
            dv_ref[...] = dv_acc[...].astype(BF16)

    whole = pl.BlockSpec((lp, LANES), lambda p, i: (0, p))
    return pl.pallas_call(
        body, name=name, grid=(N_PAIRS, nb),
        in_specs=_pair_specs(lp, 0) + [_tile_spec(), _tile_spec(), _col_spec(), _rowvec_spec(nb)],
        out_specs=[_tile_spec(), whole, whole, _rowvec_spec(nb), _col_spec()],
        out_shape=[jax.ShapeDtypeStruct((lp, W_ATT), BF16)] * 3
        + [jax.ShapeDtypeStruct((N_PAIRS, 2, nb, 1, TQ), F32), jax.ShapeDtypeStruct((N_PAIRS, 2, lp, 1), F32)],
        scratch_shapes=[pltpu.VMEM((lp, LANES), F32), pltpu.VMEM((lp, LANES), F32),
                        pltpu.VMEM((T2, LANES), F32), pltpu.VMEM((T2, 1), F32),
                        pltpu.VMEM((T2, TQ), BF16), pltpu.VMEM((T2, TQ), BF16),
                        pltpu.VMEM((T2, TQ), F32), pltpu.VMEM((T2, TQ), F32)],
        compiler_params=_cparams(("parallel", "arbitrary")),
    )(proj, proj, proj, do, o, lse, crow)


def _sb_fwd(proj, name):
    lp = proj.shape[0]
    nb = lp // TQ
    tsuf = _tri(TQ, lambda r, c: r > c)

    def body(q_ref, k_ref, v_ref, t_ref, o_ref, lt_ref, run_ref, acc_ref, zl_ref, hl_ref, a_ref, z_ref):
        i = pl.program_id(1)
        q2 = _stack_heads(q_ref[...], 0.125)
        run_ref[...] = jnp.zeros_like(run_ref)
        acc_ref[...] = jnp.zeros_like(acc_ref)

        def lead(j, h):
            z_ref[_HALF[h]] = _dot_nt(q2[_HALF[h]], k_ref[_krows(j), :])

        def step(j, nxt, masked):
            t = t_ref[...]
            v = v_ref[_krows(j), :]
            later = []
            for h in range(2):
                for r in range(h * TQ, (h + 1) * TQ, RC):
                    rows = slice(r, r + RC)
                    z_c = z_ref[rows]
                    lk = _log_keep(z_c)
                    if masked:
                        lk = jnp.where(_chunk_valid(i, j, r, True), lk, 0.0)
                    hi, lo = _split2(lk)
                    hl_ref[rows, :TQ] = hi
                    hl_ref[rows, TQ:] = lo
                    zl_ref[rows] = z_c + lk + run_ref[rows]
                    run_ref[rows] += jnp.sum(lk, axis=-1, keepdims=True)
                if nxt is not None:
                    lead(nxt, h)
                later.append(_dot(hl_ref[_HALF[h]], t))
            for h in range(2):
                for r in range(0, TQ, RC):
                    rows = slice(h * TQ + r, h * TQ + r + RC)
                    a = jnp.exp(zl_ref[rows] + later[h][r:r + RC])
                    if masked:
                        a = jnp.where(_chunk_valid(i, j, h * TQ + r, True), a, 0.0)
                    a_ref[rows] = a.astype(BF16)
                acc_ref[_HALF[h]] += _dot(a_ref[_HALF[h]], v)

        _walk_tiles_lead(i, lead, step, reverse=True)
        run = run_ref[...]
        lt_ref[0] = run[:TQ]
        lt_ref[1] = run[TQ:]
        o_ref[...] = _unstack_heads(acc_ref[...]).astype(BF16)

    base = 3 * N_PAIRS
    return pl.pallas_call(
        body, name=name, grid=(N_PAIRS, nb),
        in_specs=_pair_specs(lp, base) + [pl.BlockSpec((2 * TQ, TQ), lambda p, i: (0, 0))],
        out_specs=[_tile_spec(), _col_spec()],
        out_shape=[jax.ShapeDtypeStruct((lp, W_ATT), BF16),
                   jax.ShapeDtypeStruct((N_PAIRS, 2, lp, 1), F32)],
        scratch_shapes=[pltpu.VMEM((T2, 1), F32), pltpu.VMEM((T2, LANES), F32), pltpu.VMEM((T2, TQ), F32),
                        pltpu.VMEM((T2, 2 * TQ), BF16), pltpu.VMEM((T2, TQ), BF16), pltpu.VMEM((T2, TQ), F32)],
        compiler_params=_cparams(("parallel", "arbitrary")),
    )(proj, proj, proj, jnp.concatenate([tsuf, tsuf], axis=0))


def _sb_bwd(proj, do, ltot, name):
    lp = proj.shape[0]
    nb = lp // TQ
    tincl = _tri(TQ, lambda r, c: r <= c)
    texcl = _tri(TQ, lambda r, c: r < c)

    def body(q_ref, k_ref, v_ref, do_ref, lt_ref, ti_ref, te_ref, dq_ref, dk_ref, dv_ref,
             dk_acc, dv_acc, dq_acc, pc_ref, gc_ref, zl_ref, keep_ref, g_ref, z_ref, da_ref,
             hl_ref, gb_ref, a_ref, dz_ref):
        i = pl.program_id(1)

        @pl.when(i == 0)
        def _():
            dk_acc[...] = jnp.zeros_like(dk_acc)
            dv_acc[...] = jnp.zeros_like(dv_acc)

        dq_acc[...] = jnp.zeros_like(dq_acc)
        gc_ref[...] = jnp.zeros_like(gc_ref)
        pc_ref[...] = _stack_cols(lt_ref)
        q2 = _stack_heads(q_ref[...], 0.125)
        do2 = _stack_heads(do_ref[...])

        def lead(j, h):
            rows_j = _krows(j)
            z_ref[_HALF[h]] = _dot_nt(q2[_HALF[h]], k_ref[rows_j, :])
            da_ref[_HALF[h]] = _dot_nt(do2[_HALF[h]], v_ref[rows_j, :])

        def step(j, nxt, masked):
            rows_j = _krows(j)
            k = k_ref[rows_j, :]
            ti = ti_ref[...]
            te = te_ref[...]
            upto, before = [], []
            for h in range(2):
                for r in range(h * TQ, (h + 1) * TQ, RC):
                    rows = slice(r, r + RC)
                    z_c = z_ref[rows]
                    lk = _log_keep(z_c)
                    if masked:
                        lk = jnp.where(_chunk_valid(i, j, r, True), lk, 0.0)
                    hi, lo = _split2(lk)
                    hl_ref[rows, :TQ] = hi
                    hl_ref[rows, TQ:] = lo
                    keep_ref[rows] = jnp.exp(lk)
                    zl_ref[rows] = z_c + lk + pc_ref[rows]
                    pc_ref[rows] -= jnp.sum(lk, axis=-1, keepdims=True)
                upto.append(_dot(hl_ref[_HALF[h]], ti))
            for h in range(2):
                for r in range(0, TQ, RC):
                    rows = slice(h * TQ + r, h * TQ + r + RC)
                    a = jnp.exp(zl_ref[rows] - upto[h][r:r + RC])
                    if masked:
                        a = jnp.where(_chunk_valid(i, j, h * TQ + r, True), a, 0.0)
                    g = a * da_ref[rows]
                    a_ref[rows] = a.astype(BF16)
                    g_ref[rows] = g
                    gb_ref[rows] = g.astype(BF16)
                if nxt is not None:
                    lead(nxt, h)
                before.append(_dot(gb_ref[_HALF[h]], te))
            for h in range(2):
                for r in range(0, TQ, RC):
                    rows = slice(h * TQ + r, h * TQ + r + RC)
                    g = g_ref[rows]
                    keep = keep_ref[rows]
                    dz = g * keep - (1.0 - keep) * (gc_ref[rows] + before[h][r:r + RC])
                    if masked:
                        dz = jnp.where(_chunk_valid(i, j, h * TQ + r, True), dz, 0.0)
                    dz_ref[rows] = dz.astype(BF16)
                    gc_ref[rows] += jnp.sum(g, axis=-1, keepdims=True)
                dzb = dz_ref[_HALF[h]]
                dq_acc[_HALF[h]] += _dot(dzb, k)
                dk_acc[rows_j, :] += _dot_tn(dzb, q2[_HALF[h]])
                dv_acc[rows_j, :] += _dot_tn(a_ref[_HALF[h]], do2[_HALF[h]])

        _walk_tiles_lead(i, lead, step, reverse=False)
        dq_ref[...] = (_unstack_heads(dq_acc[...]) * 0.125).astype(BF16)

        @pl.when(i == nb - 1)
        def _():
            dk_ref[...] = dk_acc[...].astype(BF16)
            dv_ref[...] = dv_acc[...].astype(BF16)

    base = 3 * N_PAIRS
    whole = pl.BlockSpec((lp, LANES), lambda p, i: (0, p))
    tri = lambda rows: pl.BlockSpec((rows, TQ), lambda p, i: (0, 0))
    wide = lambda dt: pltpu.VMEM((T2, TQ), dt)
    return pl.pallas_call(
        body, name=name, grid=(N_PAIRS, nb),
        in_specs=_pair_specs(lp, base) + [_tile_spec(), _col_spec(), tri(2 * TQ), tri(TQ)],
        out_specs=[_tile_spec(), whole, whole],
        out_shape=[jax.ShapeDtypeStruct((lp, W_ATT), BF16)] * 3,
        scratch_shapes=[pltpu.VMEM((lp, LANES), F32), pltpu.VMEM((lp, LANES), F32),
                        pltpu.VMEM((T2, LANES), F32), pltpu.VMEM((T2, 1), F32), pltpu.VMEM((T2, 1), F32),
                        wide(F32), wide(F32), wide(F32), wide(F32), wide(F32),
                        pltpu.VMEM((T2, 2 * TQ), BF16), wide(BF16), wide(BF16), wide(BF16)],
        compiler_params=_cparams(("parallel", "arbitrary")),
    )(proj, proj, proj, do, ltot, jnp.concatenate([tincl, tincl], axis=0), texcl)


def _local_step(x, target, meta, gains, w_in, b_forget, w_o_fox, w_o_sb, w_out, w_up, conv_w, conv_b, w_down):
    seq, dm = x.shape
    lp = PAD + N_META + seq
    nb = lp // TQ
    s = [W_ATT, W_ATT, W_ATT, 8, W_ATT, W_ATT, W_ATT, dm, dm]
    off = [sum(s[:i]) for i in range(len(s) + 1)]
    cols = lambda i: w_in[:, off[i]:off[i + 1]]
    w1 = jnp.concatenate([cols(0), cols(1), cols(2), cols(4), cols(5), cols(6), cols(7), cols(8)], axis=1)
    wf = jnp.pad(cols(3), ((0, 0), (0, LANES - 8)))
    n1 = w1.shape[1]
    ncat = n1 + 512
    w_cat = jnp.concatenate([w1, wf, jnp.zeros((dm, ncat - n1 - LANES), BF16)], axis=1)
    bf = jnp.pad(b_forget.reshape(1, 8), ((0, 0), (0, LANES - 8)))
    g = [gains[i].reshape(1, dm) for i in range(4)]
    cb = conv_b.reshape(1, -1)

    h0 = jnp.concatenate([jnp.zeros((PAD, dm), F32), meta, x], axis=0)

    proj, xn1 = _rms_mm(h0, g[0], w1, "in_proj")
    logf = _logf(xn1, wf, bf, "log_forget")
    c = _cumsum_rows(logf, "forget_cumsum")
    crow = c[:, :8].T.reshape(N_PAIRS, 2, nb, 1, TQ)
    o_a, lse = _fox_fwd(proj, crow, "fox_fwd")
    o_b, ltot = _sb_fwd(proj, "sb_fwd")
    y_a, y_b, m, mixed, h1 = _merge_fwd(o_a, o_b, proj, h0, w_o_fox, w_o_sb, w_out, g[1], "merge_fwd")
    up, xn3 = _rms_mm(h1, g[2], w_up, "up_proj")
    a = _convgate_fwd(up, conv_w, cb, "convgate_fwd")
    ffn, dy, ss = _down_loss(a, w_down, h1, g[3], target, "down_loss")

    dffn, da, dg3 = _down_bwd(dy, ffn, g[3], w_down, "down_bwd")
    d_w_down = _mm_tn(a, dffn, "dw_down")
    du, d_conv_w, d_conv_b = _convgate_bwd(up, da, conv_w, cb, "convgate_bwd")
    dup = _conv_transpose(du, conv_w, "conv_transpose")
    d_w_up = _mm_tn(xn3, dup, "dw_up")
    dh1, dg2 = _mm_rmsbwd(dup, w_up, h1, g[2], dy, "up_bwd")
    dmx, dya, dyb, dga, dgb, do_a, do_b, dg1 = _merge_bwd(
        dh1, mixed, g[1], w_out, proj, y_a, y_b, w_o_fox, w_o_sb, "merge_bwd")
    d_w_out = _mm_tn(m, dmx, "dw_out")
    d_w_o_fox = _mm_tn(o_a, dya, "dw_o_fox")
    d_w_o_sb = _mm_tn(o_b, dyb, "dw_o_sb")
    dq_a, dk_a, dv_a, dcs, dct = _fox_bwd(proj, do_a, o_a, lse, crow, "fox_bwd")
    dq_b, dk_b, dv_b = _sb_bwd(proj, do_b, ltot, "sb_bwd")
    dc = (dct.reshape(8, lp) + dcs.reshape(8, lp)).T
    df, db = _dlogf(jnp.pad(dc, ((0, 0), (0, LANES - 8))), logf, "forget_bwd")
    dcat = jnp.concatenate([dq_a, dk_a, dv_a, dq_b, dk_b, dv_b, dga, dgb, df.astype(BF16),
                            jnp.zeros((lp, ncat - n1 - LANES), BF16)], axis=1)
    d_w_cat = _mm_tn(xn1, dcat, "dw_in")
    dh0, dg0 = _mm_rmsbwd(dcat, w_cat, h0, g[0], dh1, "in_bwd")

    wc = lambda k: d_w_cat[:, k * W_ATT:(k + 1) * W_ATT]
    d_w_in = jnp.concatenate([wc(0), wc(1), wc(2), d_w_cat[:, n1:n1 + 8], wc(3), wc(4), wc(5),
                              d_w_cat[:, 6 * W_ATT:n1]], axis=1)
    d_gains = jnp.concatenate([dg0, dg1, dg2, dg3], axis=0)
    grads = (dh0[PAD:PAD + N_META], d_gains, d_w_in, db[0, :8], d_w_o_fox, d_w_o_sb, d_w_out,
             d_w_up, d_conv_w, d_conv_b[0], d_w_down)
    return ss[0, 0], dh0[PAD + N_META:], grads


def _rows(a, n_rows):
    flat = a.reshape(-1)
    return jnp.pad(flat, (0, n_rows * D_MODEL - flat.shape[0])).reshape(n_rows, D_MODEL)


_SMALL = (("meta", 4), ("gains", 1), ("conv_w", 5), ("b_forget", 1), ("conv_b", 6))


def _pack(meta, gains, w_in, b_forget, w_o_fox, w_o_sb, w_out, w_up, conv_w, conv_b, w_down):
    big = [a.reshape(-1, D_MODEL) for a in (w_in, w_o_fox, w_o_sb, w_out, w_up, w_down)]
    n_big = sum(a.shape[0] for a in big)
    small = [_rows(a, n) for a, (_, n) in zip((meta, gains, conv_w, b_forget, conv_b), _SMALL)]
    n_small = sum(n for _, n in _SMALL)
    z = lambda n: jnp.zeros((n, D_MODEL), big[0].dtype)
    return jnp.concatenate(big + [z(PK_BIG_ROWS - n_big)] + small
                           + [z(PK_ROWS - PK_BIG_ROWS - n_small)], axis=0)


def _unpack(p):
    def take(r0, shape):
        n = math.prod(shape)
        nr = -(-n // D_MODEL)
        return p[r0:r0 + nr].reshape(-1)[:n].reshape(shape), r0 + nr
    w_in, r = take(0, (1, 1024, 1282))
    w_o_fox, r = take(r, (1, 512, 256))
    w_o_sb, r = take(r, (1, 512, 256))
    w_out, r = take(r, (1, 256, 1024))
    w_up, r = take(r, (1, 1024, 1408))
    w_down, r = take(r, (1, 704, 1024))
    r = PK_BIG_ROWS
    meta, r = take(r, (16, 256))
    gains, r = take(r, (1, 4, 256))
    conv_w, r = take(r, (1, 3, 1408))
    b_forget, r = take(r, (1, 8))
    conv_b, r = take(r, (1, 5632))
    return meta, gains, w_in, b_forget, w_o_fox, w_o_sb, w_out, w_up, conv_w, conv_b, w_down


def _chip_peers():
    x, y, c = lax.axis_index("x"), lax.axis_index("y"), lax.axis_index("c")
    return [(x, 1 - y, c), (1 - x, y, c), (1 - x, 1 - y, c)]


def _all_gather_chips(arrays, name):
    n = len(arrays)

    def body(*refs):
        ins, outs = refs[:n], refs[n:2 * n]
        send_sems, recv_sems, local_sems = refs[2 * n:]
        x, y = lax.axis_index("x"), lax.axis_index("y")
        me = 2 * x + y
        peers = _chip_peers()
        copies = []
        for a in range(n):
            mine = pltpu.make_async_copy(ins[a], outs[a].at[me], local_sems.at[a])
            mine.start()
            copies.append(mine)
        remote = []
        for a in range(n):
            for j, peer in enumerate(peers):
                cp = pltpu.make_async_remote_copy(
                    src_ref=ins[a], dst_ref=outs[a].at[me],
                    send_sem=send_sems.at[3 * a + j], recv_sem=recv_sems.at[3 * a + j],
                    device_id=peer, device_id_type=MESH)
                cp.start()
                remote.append(cp)
        for cp in remote:
            cp.wait()
        for cp in copies:
            cp.wait()

    any_spec = pl.BlockSpec(memory_space=pl.ANY)
    return pl.pallas_call(
        body, name=name,
        in_specs=[any_spec] * n, out_specs=[any_spec] * n,
        out_shape=[jax.ShapeDtypeStruct((4,) + a.shape, a.dtype) for a in arrays],
        scratch_shapes=[pltpu.SemaphoreType.DMA((3 * n,)), pltpu.SemaphoreType.DMA((3 * n,)),
                        pltpu.SemaphoreType.DMA((n,))],
    )(*arrays)


def _scatter_chips(chunks, name):
    _, rows, cols = chunks.shape

    def body(in_ref, out_ref, send_sems, recv_sems):
        x, y = lax.axis_index("x"), lax.axis_index("y")
        targets = [2 * x + (1 - y), 2 * (1 - x) + y, 2 * (1 - x) + (1 - y)]
        remote = []
        for j, peer in enumerate(_chip_peers()):
            cp = pltpu.make_async_remote_copy(
                src_ref=in_ref.at[targets[j]], dst_ref=out_ref.at[j],
                send_sem=send_sems.at[j], recv_sem=recv_sems.at[j],
                device_id=peer, device_id_type=MESH)
            cp.start()
            remote.append(cp)
        for cp in remote:
            cp.wait()

    any_spec = pl.BlockSpec(memory_space=pl.ANY)
    return pl.pallas_call(
        body, name=name, in_specs=[any_spec], out_specs=any_spec,
        out_shape=jax.ShapeDtypeStruct((3, rows, cols), chunks.dtype),
        scratch_shapes=[pltpu.SemaphoreType.DMA((3,)), pltpu.SemaphoreType.DMA((3,))],
    )(chunks)


def _swap_half_rows(chunks, name):
    n, rows, cols = chunks.shape
    r2 = rows // 2

    def body(in_ref, out_ref, send_sem, recv_sem):
        x, y, c = lax.axis_index("x"), lax.axis_index("y"), lax.axis_index("c")
        cp = pltpu.make_async_remote_copy(
            src_ref=in_ref.at[:, pl.ds((1 - c) * r2, r2), :], dst_ref=out_ref,
            send_sem=send_sem, recv_sem=recv_sem, device_id=(x, y, 1 - c), device_id_type=MESH)
        cp.start()
        cp.wait()

    any_spec = pl.BlockSpec(memory_space=pl.ANY)
    return pl.pallas_call(
        body, name=name, in_specs=[any_spec], out_specs=any_spec,
        out_shape=jax.ShapeDtypeStruct((n, r2, cols), chunks.dtype),
        scratch_shapes=[pltpu.SemaphoreType.DMA, pltpu.SemaphoreType.DMA],
    )(chunks)


def _gather_halves(half, name):
    def body(in_ref, out_ref, send_sem, recv_sem):
        x, y, c = lax.axis_index("x"), lax.axis_index("y"), lax.axis_index("c")
        cp = pltpu.make_async_remote_copy(
            src_ref=in_ref, dst_ref=out_ref, send_sem=send_sem, recv_sem=recv_sem,
            device_id=(x, y, 1 - c), device_id_type=MESH)
        cp.start()
        cp.wait()

    any_spec = pl.BlockSpec(memory_space=pl.ANY)
    other = pl.pallas_call(
        body, name=name, in_specs=[any_spec], out_specs=any_spec,
        out_shape=jax.ShapeDtypeStruct(half.shape, half.dtype),
        scratch_shapes=[pltpu.SemaphoreType.DMA, pltpu.SemaphoreType.DMA],
    )(half)
    axis = half.ndim - 2
    return lax.cond(lax.axis_index("c") == 0,
                    lambda: jnp.concatenate([half, other], axis=axis),
                    lambda: jnp.concatenate([other, half], axis=axis))


def _add_bf16(a, b, name):
    n, rows, cols = a.shape

    def body(a_ref, b_ref, o_ref):
        o_ref[...] = (a_ref[...] + b_ref[...]).astype(BF16)

    spec = pl.BlockSpec((None, PK_TILE, cols), lambda k, i: (k, i, 0))
    return pl.pallas_call(
        body, name=name, grid=(n, rows // PK_TILE),
        in_specs=[spec, spec], out_specs=spec,
        out_shape=jax.ShapeDtypeStruct(a.shape, BF16),
        compiler_params=_cparams(("parallel", "parallel")),
    )(a, b)


def _chip_sum(chunks, recv, name):
    _, rows, cols = chunks.shape

    def body(own_ref, r_ref, o_ref):
        f = lambda a: a.astype(F32)
        o_ref[...] = (f(own_ref[...]) + f(r_ref[0])) + (f(r_ref[1]) + f(r_ref[2]))

    me = 2 * lax.axis_index("x") + lax.axis_index("y")
    own = lax.dynamic_index_in_dim(chunks, me, axis=0, keepdims=False)
    return pl.pallas_call(
        body, name=name, grid=(rows // PK_TILE,),
        in_specs=[pl.BlockSpec((PK_TILE, cols), lambda i: (i, 0)),
                  pl.BlockSpec((3, PK_TILE, cols), lambda i: (0, i, 0))],
        out_specs=pl.BlockSpec((PK_TILE, cols), lambda i: (i, 0)),
        out_shape=jax.ShapeDtypeStruct((rows, cols), F32),
        compiler_params=_cparams(("parallel",)),
    )(own, recv)


def _adamw(w, m, v, g, name):
    shape = w.shape
    rows, cols = math.prod(shape[:-1]), shape[-1]
    w, m, v, g = (a.reshape(rows, cols) for a in (w, m, v, g))
    tile = next((t for t in (256, 128, 64, 32, 16, 8) if rows % t == 0), rows)
    c1 = 1.0 - ADAM_B1 ** ADAM_STEP
    c2 = 1.0 - ADAM_B2 ** ADAM_STEP

    def body(w_ref, m_ref, v_ref, g_ref, d_ref, nm_ref, nv_ref):
        g = g_ref[...]
        nm = ADAM_B1 * m_ref[...] + (1.0 - ADAM_B1) * g
        nv = ADAM_B2 * v_ref[...] + (1.0 - ADAM_B2) * (g * g)
        nm_ref[...] = nm
        nv_ref[...] = nv
        d_ref[...] = -ADAM_LR * ((nm / c1) / (jnp.sqrt(nv / c2) + ADAM_EPS) + ADAM_WD * w_ref[...])

    spec = pl.BlockSpec((tile, cols), lambda i: (i, 0))
    outs = pl.pallas_call(
        body, name=name, grid=(rows // tile,),
        in_specs=[spec] * 4, out_specs=[spec] * 3,
        out_shape=[jax.ShapeDtypeStruct((rows, cols), F32)] * 3,
        compiler_params=_cparams(("parallel",)),
    )(w, m, v, g)
    return [o.reshape(shape) for o in outs]


def _full_weights(big, small):
    def gather(src, r0, shape, axis):
        n = math.prod(shape)
        nr = -(-n // D_MODEL)
        parts = [src[k, r0:r0 + nr].reshape(-1)[:n].reshape(shape) for k in range(4)]
        return jnp.concatenate(parts, axis=axis), r0 + nr
    w_in, r = gather(big, 0, (1024, 1282), 1)
    w_o_fox, r = gather(big, r, (512, 256), 1)
    w_o_sb, r = gather(big, r, (512, 256), 1)
    w_out, r = gather(big, r, (256, 1024), 0)
    w_up, r = gather(big, r, (1024, 1408), 1)
    w_down, r = gather(big, r, (704, 1024), 0)
    meta, r = gather(small, 0, (16, 256), 1)
    gains, r = gather(small, r, (4, 256), 1)
    conv_w, r = gather(small, r, (3, 1408), 1)
    return meta, gains, w_in, w_o_fox, w_o_sb, w_out, w_up, conv_w, w_down


def _chunks_for_chips(grads):
    d_meta, d_gains, d_w_in, d_b, d_w_o_fox, d_w_o_sb, d_w_out, d_w_up, d_conv_w, d_conv_b, d_w_down = grads
    out = []
    for k in range(4):
        col = lambda a, w: a[:, k * w:(k + 1) * w]
        row = lambda a, w: a[k * w:(k + 1) * w]
        out.append(_pack(col(d_meta, 256), col(d_gains, 256), col(d_w_in, 1282), d_b, col(d_w_o_fox, 256),
                         col(d_w_o_sb, 256), row(d_w_out, 256), col(d_w_up, 1408), col(d_conv_w, 1408),
                         d_conv_b, row(d_w_down, 704)))
    return jnp.stack(out, axis=0)


def kernel(x, meta_tokens, norm_gains, w_in, b_forget, w_o_fox, w_o_sb, w_out, w_up, conv_w, conv_b, w_down, loss_target, m_meta_tokens, m_norm_gains, m_w_in, m_b_forget, m_w_o_fox, m_w_o_sb, m_w_out, m_w_up, m_conv_w, m_conv_b, m_w_down, v_meta_tokens, v_norm_gains, v_w_in, v_b_forget, v_w_o_fox, v_w_o_sb, v_w_out, v_w_up, v_conv_w, v_conv_b, v_w_down):
    names = ("meta_tokens", "norm_gains", "w_in", "b_forget", "w_o_fox", "w_o_sb", "w_out", "w_up", "conv_w",
             "conv_b", "w_down")
    ws = (meta_tokens, norm_gains, w_in, b_forget, w_o_fox, w_o_sb, w_out, w_up, conv_w, conv_b, w_down)
    ms = (m_meta_tokens, m_norm_gains, m_w_in, m_b_forget, m_w_o_fox, m_w_o_sb, m_w_out, m_w_up, m_conv_w,
          m_conv_b, m_w_down)
    vs = (v_meta_tokens, v_norm_gains, v_w_in, v_b_forget, v_w_o_fox, v_w_o_sb, v_w_out, v_w_up, v_conv_w,
          v_conv_b, v_w_down)
    wp = _pack(meta_tokens, norm_gains[0], w_in[0], b_forget[0], w_o_fox[0], w_o_sb[0], w_out[0], w_up[0],
               conv_w[0], conv_b[0], w_down[0])

    hb = PK_BIG_ROWS // 2
    big_half = lax.dynamic_slice_in_dim(wp[:PK_BIG_ROWS].astype(BF16), lax.axis_index("c") * hb, hb, axis=0)
    big_half, small = _all_gather_chips(
        [big_half, wp[PK_BIG_ROWS:PK_BIG_ROWS + PK_SMALL_ROWS]], "gather_weights")
    big = _gather_halves(big_half, "gather_weight_halves")
    meta, gains, f_w_in, f_w_o_fox, f_w_o_sb, f_w_out, f_w_up, f_conv_w, f_w_down = _full_weights(big, small)

    ss, dx, grads = _local_step(x[0], loss_target[0], meta, gains, f_w_in, b_forget[0], f_w_o_fox, f_w_o_sb,
                                f_w_out, f_w_up, f_conv_w, conv_b[0], f_w_down)
    loss = lax.psum(0.5 * ss / D_MODEL, ("x", "y", "c"))

    chunks = _chunks_for_chips(grads)
    r2 = PK_ROWS // 2
    from_sibling = _swap_half_rows(chunks, "swap_halves")
    own = lax.dynamic_slice_in_dim(chunks, lax.axis_index("c") * r2, r2, axis=1)
    core_sum = _add_bf16(own, from_sibling, "core_sum")
    recv = _scatter_chips(core_sum, "scatter_grads")
    g = _gather_halves(_chip_sum(core_sum, recv, "chip_sum"), "gather_halves")
    gs = _unpack(g)
    steps = [_adamw(w, m, v, gp, "adamw_" + n) for n, w, m, v, gp in zip(names, ws, ms, vs, gs)]
    return (loss, dx[None], *gs, *[s[0] for s in steps], *[s[1] for s in steps], *[s[2] for s in steps])
```

```python
import functools
import math

import jax
import jax.numpy as jnp
from jax import lax
from jax.experimental import pallas as pl
from jax.experimental.pallas import tpu as pltpu

F32 = jnp.float32
BF16 = jnp.bfloat16

D_MODEL = 1024
N_META = 16
HEAD_DIM = 64
N_PAIRS = 4
W_ATT = 512
D_FF = 2816
EPS = 1e-6
NEG = -1e30
TQ = 256
PAD = TQ - N_META
TCONV = 128
HALO = 16
LANES = 128
VMEM_LIMIT = 56 * 1024 * 1024

ADAM_LR = 0.001
ADAM_B1 = 0.9
ADAM_B2 = 0.999
ADAM_EPS = 1e-08
ADAM_WD = 0.01
ADAM_STEP = 10

MESH = pl.DeviceIdType.MESH

PK_BIG_ROWS = 3936
PK_SMALL_ROWS = 24
PK_ROWS = 4096
PK_TILE = 128


def _cparams(sem, **kw):
    return pltpu.CompilerParams(dimension_semantics=sem, vmem_limit_bytes=VMEM_LIMIT, **kw)


def _row_tile(lp):
    return 768 if lp % 768 == 0 else 256


def _wide_tile(n):
    return next(t for t in (1408, 1280, 1024, 512, 256) if n % t == 0)


def _rms(x):
    return lax.rsqrt(jnp.mean(x * x, axis=-1, keepdims=True) + EPS)


def _log_sigmoid(x):
    return jnp.minimum(x, 0.0) - jnp.log(1.0 + jnp.exp(-jnp.abs(x)))


_LOG2E = 1.4426950408889634
_LN2 = 0.6931471805599453


def _log_keep(z):
    t = jnp.exp2(jnp.abs(z) * (-_LOG2E))
    return jnp.log2(1.0 + t) * (-_LN2) - jnp.maximum(z, 0.0)


def _dot(a, b):
    return jnp.dot(a, b, preferred_element_type=F32)


def _dot_nt(a, b):
    return lax.dot_general(a, b, (((1,), (1,)), ((), ())), preferred_element_type=F32)


def _dot_tn(a, b):
    return lax.dot_general(a, b, (((0,), (0,)), ((), ())), preferred_element_type=F32)


def _rms_mm(h, g, w, name):
    lp, dm = h.shape
    n = w.shape[1]
    tr, tn = _row_tile(lp), _wide_tile(n)

    def body(h_ref, g_ref, w_ref, out_ref, xn_ref):
        @pl.when(pl.program_id(1) == 0)
        def _():
            x = h_ref[...]
            xn_ref[...] = (x * _rms(x) * g_ref[...]).astype(BF16)
        out_ref[...] = _dot(xn_ref[...], w_ref[...]).astype(BF16)

    return pl.pallas_call(
        body, name=name, grid=(lp // tr, n // tn),
        in_specs=[pl.BlockSpec((tr, dm), lambda i, j: (i, 0)),
                  pl.BlockSpec((1, dm), lambda i, j: (0, 0)),
                  pl.BlockSpec((dm, tn), lambda i, j: (0, j))],
        out_specs=[pl.BlockSpec((tr, tn), lambda i, j: (i, j)),
                   pl.BlockSpec((tr, dm), lambda i, j: (i, 0))],
        out_shape=[jax.ShapeDtypeStruct((lp, n), BF16), jax.ShapeDtypeStruct((lp, dm), BF16)],
        compiler_params=_cparams(("parallel", "arbitrary")),
    )(h, g, w)


def _mm_rmsbwd(dy, w, h, g, dh_in, name):
    lp, kd = dy.shape
    dm = w.shape[0]
    tr, tk = 384, _wide_tile(kd)
    nk = kd // tk

    def body(dy_ref, w_ref, h_ref, g_ref, dhin_ref, dh_ref, dg_ref, acc_ref):
        i, k = pl.program_id(0), pl.program_id(1)

        @pl.when(k == 0)
        def _():
            acc_ref[...] = jnp.zeros_like(acc_ref)

        acc_ref[...] += _dot_nt(dy_ref[...], w_ref[...])

        @pl.when(k == nk - 1)
        def _():
            dxn = acc_ref[...]
            x = h_ref[...]
            r = _rms(x)
            yhat = x * r
            part = jnp.sum(dxn * yhat, axis=0, keepdims=True)
            dyh = dxn * g_ref[...]
            dx = r * (dyh - yhat * jnp.mean(dyh * yhat, axis=-1, keepdims=True))
            dh_ref[...] = dhin_ref[...] + dx

            @pl.when(i == 0)
            def _():
                dg_ref[...] = part

            @pl.when(i > 0)
            def _():
                dg_ref[...] += part

    return pl.pallas_call(
        body, name=name, grid=(lp // tr, nk),
        in_specs=[pl.BlockSpec((tr, tk), lambda i, k: (i, k)),
                  pl.BlockSpec((dm, tk), lambda i, k: (0, k)),
                  pl.BlockSpec((tr, dm), lambda i, k: (i, 0)),
                  pl.BlockSpec((1, dm), lambda i, k: (0, 0)),
                  pl.BlockSpec((tr, dm), lambda i, k: (i, 0))],
        out_specs=[pl.BlockSpec((tr, dm), lambda i, k: (i, 0)),
                   pl.BlockSpec((1, dm), lambda i, k: (0, 0))],
        out_shape=[jax.ShapeDtypeStruct((lp, dm), F32), jax.ShapeDtypeStruct((1, dm), F32)],
        scratch_shapes=[pltpu.VMEM((tr, dm), F32)],
        compiler_params=_cparams(("arbitrary", "arbitrary")),
    )(dy, w, h, g, dh_in)


def _mm_tn(x, dy, name):
    lp, kd = x.shape
    n = dy.shape[1]
    tl = _row_tile(lp)
    tk = _wide_tile(kd)
    tn = _wide_tile(n)
    nl = lp // tl

    def body(x_ref, dy_ref, o_ref):
        @pl.when(pl.program_id(2) == 0)
        def _():
            o_ref[...] = jnp.zeros_like(o_ref)
        o_ref[...] += _dot_tn(x_ref[...], dy_ref[...])

    return pl.pallas_call(
        body, name=name, grid=(kd // tk, n // tn, nl),
        in_specs=[pl.BlockSpec((tl, tk), lambda a, b, l: (l, a)),
                  pl.BlockSpec((tl, tn), lambda a, b, l: (l, b))],
        out_specs=pl.BlockSpec((tk, tn), lambda a, b, l: (a, b)),
        out_shape=jax.ShapeDtypeStruct((kd, n), F32),
        compiler_params=_cparams(("parallel", "parallel", "arbitrary")),
    )(x, dy)


def _logf(xn, wf, bf, name):
    lp, dm = xn.shape
    tr = _row_tile(lp)

    def body(xn_ref, wf_ref, b_ref, o_ref):
        f = _dot(xn_ref[...], wf_ref[...]) + b_ref[...]
        row = pl.program_id(0) * tr + lax.broadcasted_iota(jnp.int32, f.shape, 0)
        lane = lax.broadcasted_iota(jnp.int32, f.shape, 1)
        o_ref[...] = jnp.where((row >= PAD) & (lane < 8), _log_sigmoid(f), 0.0)

    return pl.pallas_call(
        body, name=name, grid=(lp // tr,),
        in_specs=[pl.BlockSpec((tr, dm), lambda i: (i, 0)),
                  pl.BlockSpec((dm, LANES), lambda i: (0, 0)),
                  pl.BlockSpec((1, LANES), lambda i: (0, 0))],
        out_specs=pl.BlockSpec((tr, LANES), lambda i: (i, 0)),
        out_shape=jax.ShapeDtypeStruct((lp, LANES), F32),
        compiler_params=_cparams(("parallel",)),
    )(xn, wf, bf)


def _tri(n, rel):
    r = lax.broadcasted_iota(jnp.int32, (n, n), 0)
    c = lax.broadcasted_iota(jnp.int32, (n, n), 1)
    return rel(r, c).astype(BF16)


def _cumsum_rows(x, name):
    lp = x.shape[0]
    nb = lp // TQ
    tl = _tri(TQ, lambda r, c: c <= r)

    def body(x_ref, t_ref, o_ref):
        def step(b, carry):
            rows = pl.ds(pl.multiple_of(b * TQ, TQ), TQ)
            xb = x_ref[rows, :]
            hi = xb.astype(BF16)
            r1 = xb - hi.astype(F32)
            mid = r1.astype(BF16)
            lo = (r1 - mid.astype(F32)).astype(BF16)
            t = t_ref[...]
            o_ref[rows, :] = carry + (_dot(t, hi) + _dot(t, mid) + _dot(t, lo))
            return carry + jnp.sum(xb, axis=0, keepdims=True)
        lax.fori_loop(0, nb, step, jnp.zeros((1, LANES), F32))

    return pl.pallas_call(
        body, name=name,
        in_specs=[pl.BlockSpec(memory_space=pltpu.VMEM)] * 2,
        out_specs=pl.BlockSpec(memory_space=pltpu.VMEM),
        out_shape=jax.ShapeDtypeStruct((lp, LANES), F32),
        compiler_params=pltpu.CompilerParams(vmem_limit_bytes=VMEM_LIMIT),
    )(x, tl)


def _dlogf(dc, logf, name):
    lp = dc.shape[0]
    nb = lp // TQ
    tu = _tri(TQ, lambda r, c: c >= r)

    def body(x_ref, lf_ref, t_ref, df_ref, db_ref):
        def step(bb, carry):
            run, db = carry
            b = nb - 1 - bb
            rows = pl.ds(pl.multiple_of(b * TQ, TQ), TQ)
            xb = x_ref[rows, :]
            hi = xb.astype(BF16)
            r1 = xb - hi.astype(F32)
            mid = r1.astype(BF16)
            lo = (r1 - mid.astype(F32)).astype(BF16)
            t = t_ref[...]
            dlf = run + (_dot(t, hi) + _dot(t, mid) + _dot(t, lo))
            df = dlf * (1.0 - jnp.exp(lf_ref[rows, :]))
            df_ref[rows, :] = df
            return run + jnp.sum(xb, axis=0, keepdims=True), db + jnp.sum(df, axis=0, keepdims=True)
        z = jnp.zeros((1, LANES), F32)
        _, db = lax.fori_loop(0, nb, step, (z, z))
        db_ref[...] = db

    return pl.pallas_call(
        body, name=name,
        in_specs=[pl.BlockSpec(memory_space=pltpu.VMEM)] * 3,
        out_specs=[pl.BlockSpec(memory_space=pltpu.VMEM)] * 2,
        out_shape=[jax.ShapeDtypeStruct((lp, LANES), F32), jax.ShapeDtypeStruct((1, LANES), F32)],
        compiler_params=pltpu.CompilerParams(vmem_limit_bytes=VMEM_LIMIT),
    )(dc, logf, tu)


def _merge_fwd(o_a, o_b, proj, h0, w_oa, w_ob, w_out, g1, name):
    lp, dm = h0.shape
    tr = TQ
    ga_blk = (6 * W_ATT) // dm

    def body(oa_ref, ob_ref, ga_ref, gb_ref, h0_ref, woa_ref, wob_ref, wout_ref, g1_ref,
             ya_ref, yb_ref, m_ref, mixed_ref, h1_ref):
        ya = _dot(oa_ref[...], woa_ref[...])
        yb = _dot(ob_ref[...], wob_ref[...])
        m = jax.nn.sigmoid(ga_ref[...].astype(F32)) * ya + jax.nn.sigmoid(gb_ref[...].astype(F32)) * yb
        mb = m.astype(BF16)
        mixed = _dot(mb, wout_ref[...])
        ya_ref[...] = ya.astype(BF16)
        yb_ref[...] = yb.astype(BF16)
        m_ref[...] = mb
        mixed_ref[...] = mixed
        h1_ref[...] = h0_ref[...] + mixed * _rms(mixed) * g1_ref[...]

    row = lambda w: pl.BlockSpec((tr, w), lambda i: (i, 0))
    full = lambda a: pl.BlockSpec(a.shape, lambda i: (0, 0))
    return pl.pallas_call(
        body, name=name, grid=(lp // tr,),
        in_specs=[row(W_ATT), row(W_ATT),
                  pl.BlockSpec((tr, dm), lambda i: (i, ga_blk)),
                  pl.BlockSpec((tr, dm), lambda i: (i, ga_blk + 1)),
                  row(dm), full(w_oa), full(w_ob), full(w_out), full(g1)],
        out_specs=[row(dm)] * 5,
        out_shape=[jax.ShapeDtypeStruct((lp, dm), BF16)] * 3 + [jax.ShapeDtypeStruct((lp, dm), F32)] * 2,
        compiler_params=_cparams(("parallel",)),
    )(o_a, o_b, proj, proj, h0, w_oa, w_ob, w_out, g1)


def _merge_bwd(dh1, mixed, g1, w_out, proj, y_a, y_b, w_oa, w_ob, name):
    lp, dm = dh1.shape
    tr = TQ
    ga_blk = (6 * W_ATT) // dm

    def body(dh_ref, mx_ref, g1_ref, wout_ref, ga_ref, gb_ref, ya_ref, yb_ref, woa_ref, wob_ref,
             dmx_ref, dya_ref, dyb_ref, dga_ref, dgb_ref, doa_ref, dob_ref, dg1_ref):
        i = pl.program_id(0)
        dn = dh_ref[...]
        x = mx_ref[...]
        r = _rms(x)
        yhat = x * r
        part = jnp.sum(dn * yhat, axis=0, keepdims=True)
        dyh = dn * g1_ref[...]
        dmx = (r * (dyh - yhat * jnp.mean(dyh * yhat, axis=-1, keepdims=True))).astype(BF16)
        dmx_ref[...] = dmx
        dm_ = _dot_nt(dmx, wout_ref[...])
        sa = jax.nn.sigmoid(ga_ref[...].astype(F32))
        sb = jax.nn.sigmoid(gb_ref[...].astype(F32))
        dya = (dm_ * sa).astype(BF16)
        dyb = (dm_ * sb).astype(BF16)
        dya_ref[...] = dya
        dyb_ref[...] = dyb
        dga_ref[...] = (dm_ * ya_ref[...].astype(F32) * sa * (1.0 - sa)).astype(BF16)
        dgb_ref[...] = (dm_ * yb_ref[...].astype(F32) * sb * (1.0 - sb)).astype(BF16)
        doa_ref[...] = _dot_nt(dya, woa_ref[...]).astype(BF16)
        dob_ref[...] = _dot_nt(dyb, wob_ref[...]).astype(BF16)

        @pl.when(i == 0)
        def _():
            dg1_ref[...] = part

        @pl.when(i > 0)
        def _():
            dg1_ref[...] += part

    row = lambda w: pl.BlockSpec((tr, w), lambda i: (i, 0))
    full = lambda a: pl.BlockSpec(a.shape, lambda i: (0, 0))
    return pl.pallas_call(
        body, name=name, grid=(lp // tr,),
        in_specs=[row(dm), row(dm), full(g1), full(w_out),
                  pl.BlockSpec((tr, dm), lambda i: (i, ga_blk)),
                  pl.BlockSpec((tr, dm), lambda i: (i, ga_blk + 1)),
                  row(dm), row(dm), full(w_oa), full(w_ob)],
        out_specs=[row(dm)] * 5 + [row(W_ATT)] * 2 + [pl.BlockSpec((1, dm), lambda i: (0, 0))],
        out_shape=[jax.ShapeDtypeStruct((lp, dm), BF16)] * 5 + [jax.ShapeDtypeStruct((lp, W_ATT), BF16)] * 2
        + [jax.ShapeDtypeStruct((1, dm), F32)],
        compiler_params=_cparams(("arbitrary",)),
    )(dh1, mixed, g1, w_out, proj, proj, y_a, y_b, w_oa, w_ob)


_GELU_C = math.sqrt(2.0 / math.pi)
_GELU_A = 0.044715


def _gelu(x):
    t = jnp.tanh(_GELU_C * (x + _GELU_A * x * x * x))
    return 0.5 * x * (1.0 + t), t


CW = 256


def _taps(cur_ref, prev_ref, first, c0):
    cur = cur_ref[:, c0:c0 + CW].astype(F32)
    p1 = jnp.where(first, 0.0, prev_ref[HALO - 1:HALO, c0:c0 + CW].astype(F32))
    p2 = jnp.where(first, 0.0, prev_ref[HALO - 2:HALO - 1, c0:c0 + CW].astype(F32))
    row = lax.broadcasted_iota(jnp.int32, cur.shape, 0)
    x1 = jnp.where(row == 0, p1, pltpu.roll(cur, 1, 0))
    x2 = jnp.where(row == 0, p2, jnp.where(row == 1, p1, pltpu.roll(cur, 2, 0)))
    return cur, x1, x2


def _conv_at(cur_ref, prev_ref, w_ref, b_ref, first, c0):
    cur, x1, x2 = _taps(cur_ref, prev_ref, first, c0)
    cols = slice(c0, c0 + CW)
    u = b_ref[:, cols] + w_ref[0:1, cols] * x2 + w_ref[1:2, cols] * x1 + w_ref[2:3, cols] * cur
    return u, (x2, x1, cur)


def _up_specs(tr, width):
    per = tr // HALO
    return [pl.BlockSpec((tr, width), lambda i: (i, 0)),
            pl.BlockSpec((HALO, width), lambda i: (jnp.maximum(i * per - 1, 0), 0))]


def _convgate_fwd(up, conv_w, conv_b, name):
    lp, c2 = up.shape
    tr = TCONV

    def body(cur_ref, prev_ref, w_ref, b_ref, a_ref):
        first = pl.program_id(0) == 0
        for c0 in range(0, D_FF, CW):
            ug, _ = _conv_at(cur_ref, prev_ref, w_ref, b_ref, first, c0)
            uv, _ = _conv_at(cur_ref, prev_ref, w_ref, b_ref, first, D_FF + c0)
            gel, _ = _gelu(ug)
            a_ref[:, c0:c0 + CW] = (gel * uv).astype(BF16)

    return pl.pallas_call(
        body, name=name, grid=(lp // tr,),
        in_specs=_up_specs(tr, c2) + [pl.BlockSpec((3, c2), lambda i: (0, 0)),
                                      pl.BlockSpec((1, c2), lambda i: (0, 0))],
        out_specs=pl.BlockSpec((tr, D_FF), lambda i: (i, 0)),
        out_shape=jax.ShapeDtypeStruct((lp, D_FF), BF16),
        compiler_params=_cparams(("parallel",)),
    )(up, up, conv_w, conv_b)


def _convgate_bwd(up, da, conv_w, conv_b, name):
    lp, c2 = up.shape
    tr = TCONV

    def body(cur_ref, prev_ref, da_ref, w_ref, b_ref, du_ref, dw_ref, db_ref):
        i = pl.program_id(0)
        first = i == 0

        @pl.when(first)
        def _():
            dw_ref[...] = jnp.zeros_like(dw_ref)
            db_ref[...] = jnp.zeros_like(db_ref)

        for c0 in range(0, D_FF, CW):
            ug, taps_g = _conv_at(cur_ref, prev_ref, w_ref, b_ref, first, c0)
            uv, taps_v = _conv_at(cur_ref, prev_ref, w_ref, b_ref, first, D_FF + c0)
            gel, t = _gelu(ug)
            dgel = 0.5 * (1.0 + t) + 0.5 * ug * (1.0 - t * t) * _GELU_C * (1.0 + 3.0 * _GELU_A * ug * ug)
            da_ = da_ref[:, c0:c0 + CW].astype(F32)
            for base, du, taps in ((c0, da_ * uv * dgel, taps_g), (D_FF + c0, da_ * gel, taps_v)):
                cols = slice(base, base + CW)
                du_ref[:, cols] = du.astype(BF16)
                for tap in range(3):
                    dw_ref[tap:tap + 1, cols] += jnp.sum(du * taps[tap], axis=0, keepdims=True)
                db_ref[:, cols] += jnp.sum(du, axis=0, keepdims=True)

    return pl.pallas_call(
        body, name=name, grid=(lp // tr,),
        in_specs=_up_specs(tr, c2) + [pl.BlockSpec((tr, D_FF), lambda i: (i, 0)),
                                      pl.BlockSpec((3, c2), lambda i: (0, 0)),
                                      pl.BlockSpec((1, c2), lambda i: (0, 0))],
        out_specs=[pl.BlockSpec((tr, c2), lambda i: (i, 0)),
                   pl.BlockSpec((3, c2), lambda i: (0, 0)),
                   pl.BlockSpec((1, c2), lambda i: (0, 0))],
        out_shape=[jax.ShapeDtypeStruct((lp, c2), BF16), jax.ShapeDtypeStruct((3, c2), F32),
                   jax.ShapeDtypeStruct((1, c2), F32)],
        compiler_params=_cparams(("arbitrary",)),
    )(up, up, da, conv_w, conv_b)


def _conv_transpose(du, conv_w, name):
    lp, c2 = du.shape
    tr = TCONV
    per = tr // HALO
    n_halo = lp // HALO
    nt = lp // tr

    def body(cur_ref, nxt_ref, w_ref, o_ref):
        last = pl.program_id(0) == nt - 1
        for c0 in range(0, c2, CW):
            cols = slice(c0, c0 + CW)
            cur = cur_ref[:, cols].astype(F32)
            n0 = jnp.where(last, 0.0, nxt_ref[0:1, cols].astype(F32))
            n1 = jnp.where(last, 0.0, nxt_ref[1:2, cols].astype(F32))
            row = lax.broadcasted_iota(jnp.int32, cur.shape, 0)
            y1 = jnp.where(row == tr - 1, n0, pltpu.roll(cur, tr - 1, 0))
            y2 = jnp.where(row == tr - 1, n1, jnp.where(row == tr - 2, n0, pltpu.roll(cur, tr - 2, 0)))
            o_ref[:, cols] = (w_ref[2:3, cols] * cur + w_ref[1:2, cols] * y1 + w_ref[0:1, cols] * y2).astype(BF16)

    return pl.pallas_call(
        body, name=name, grid=(nt,),
        in_specs=[pl.BlockSpec((tr, c2), lambda i: (i, 0)),
                  pl.BlockSpec((HALO, c2), lambda i: (jnp.minimum((i + 1) * per, n_halo - 1), 0)),
                  pl.BlockSpec((3, c2), lambda i: (0, 0))],
        out_specs=pl.BlockSpec((tr, c2), lambda i: (i, 0)),
        out_shape=jax.ShapeDtypeStruct((lp, c2), BF16),
        compiler_params=_cparams(("parallel",)),
    )(du, du, conv_w)


def _down_loss(a, w_down, h1, g3, target, name):
    lp, dm = h1.shape
    tr = TQ

    def body(a_ref, w_ref, h1_ref, g_ref, t_ref, ffn_ref, dy_ref, ss_ref):
        i = pl.program_id(0)
        ffn = _dot(a_ref[...], w_ref[...])
        ffn_ref[...] = ffn
        h2 = h1_ref[...] + ffn * _rms(ffn) * g_ref[...]
        d = jnp.where(i > 0, h2 - t_ref[...], 0.0)
        dy_ref[...] = d * (1.0 / dm)
        part = jnp.sum(jnp.sum(d * d, axis=0, keepdims=True), axis=1, keepdims=True)

        @pl.when(i == 0)
        def _():
            ss_ref[...] = jnp.zeros_like(ss_ref)

        ss_ref[...] += part

    return pl.pallas_call(
        body, name=name, grid=(lp // tr,),
        in_specs=[pl.BlockSpec((tr, D_FF), lambda i: (i, 0)),
                  pl.BlockSpec(w_down.shape, lambda i: (0, 0)),
                  pl.BlockSpec((tr, dm), lambda i: (i, 0)),
                  pl.BlockSpec((1, dm), lambda i: (0, 0)),
                  pl.BlockSpec((tr, dm), lambda i: (jnp.maximum(i - 1, 0), 0))],
        out_specs=[pl.BlockSpec((tr, dm), lambda i: (i, 0)),
                   pl.BlockSpec((tr, dm), lambda i: (i, 0)),
                   pl.BlockSpec((8, LANES), lambda i: (0, 0))],
        out_shape=[jax.ShapeDtypeStruct((lp, dm), F32), jax.ShapeDtypeStruct((lp, dm), F32),
                   jax.ShapeDtypeStruct((8, LANES), F32)],
        compiler_params=_cparams(("arbitrary",)),
    )(a, w_down, h1, g3, target)


def _down_bwd(dy, ffn, g3, w_down, name):
    lp, dm = dy.shape
    tr = TQ

    def body(dy_ref, f_ref, g_ref, w_ref, dffn_ref, da_ref, dg_ref):
        i = pl.program_id(0)
        dn = dy_ref[...]
        x = f_ref[...]
        r = _rms(x)
        yhat = x * r
        part = jnp.sum(dn * yhat, axis=0, keepdims=True)
        dyh = dn * g_ref[...]
        dffn = (r * (dyh - yhat * jnp.mean(dyh * yhat, axis=-1, keepdims=True))).astype(BF16)
        dffn_ref[...] = dffn
        da_ref[...] = _dot_nt(dffn, w_ref[...]).astype(BF16)

        @pl.when(i == 0)
        def _():
            dg_ref[...] = part

        @pl.when(i > 0)
        def _():
            dg_ref[...] += part

    return pl.pallas_call(
        body, name=name, grid=(lp // tr,),
        in_specs=[pl.BlockSpec((tr, dm), lambda i: (i, 0)),
                  pl.BlockSpec((tr, dm), lambda i: (i, 0)),
                  pl.BlockSpec((1, dm), lambda i: (0, 0)),
                  pl.BlockSpec(w_down.shape, lambda i: (0, 0))],
        out_specs=[pl.BlockSpec((tr, dm), lambda i: (i, 0)),
                   pl.BlockSpec((tr, D_FF), lambda i: (i, 0)),
                   pl.BlockSpec((1, dm), lambda i: (0, 0))],
        out_shape=[jax.ShapeDtypeStruct((lp, dm), BF16), jax.ShapeDtypeStruct((lp, D_FF), BF16),
                   jax.ShapeDtypeStruct((1, dm), F32)],
        compiler_params=_cparams(("arbitrary",)),
    )(dy, ffn, g3, w_down)


def _pair_specs(lp, base):
    return [pl.BlockSpec((TQ, LANES), lambda p, i: (i, base + p)),
            pl.BlockSpec((lp, LANES), lambda p, i: (0, base + N_PAIRS + p)),
            pl.BlockSpec((lp, LANES), lambda p, i: (0, base + 2 * N_PAIRS + p))]


def _col_spec():
    return pl.BlockSpec((None, 2, TQ, 1), lambda p, i: (p, 0, i, 0))


def _rowvec_spec(nb):
    return pl.BlockSpec((None, 2, nb, 1, TQ), lambda p, i: (p, 0, 0, 0, 0))


def _tile_spec():
    return pl.BlockSpec((TQ, LANES), lambda p, i: (i, p))


RC = 64
T2 = 2 * TQ


def _stack_heads(x, scale=None):
    lane = lax.broadcasted_iota(jnp.int32, x.shape, 1)
    zero = jnp.zeros_like(x)
    x2 = jnp.concatenate([jnp.where(lane < HEAD_DIM, x, zero), jnp.where(lane >= HEAD_DIM, x, zero)], axis=0)
    return x2 if scale is None else x2 * scale


def _unstack_heads(x2):
    lane = lax.broadcasted_iota(jnp.int32, (TQ, LANES), 1)
    return jnp.where(lane < HEAD_DIM, x2[:TQ], x2[TQ:])


def _stack_cols(ref):
    return jnp.concatenate([ref[0], ref[1]], axis=0)


def _chunk_valid(i, j, r, strict):
    qpos = i * TQ + (r % TQ) + lax.broadcasted_iota(jnp.int32, (RC, TQ), 0)
    kpos = j * TQ + lax.broadcasted_iota(jnp.int32, (RC, TQ), 1)
    causal = (kpos < qpos) if strict else (kpos <= qpos)
    return causal & (kpos >= PAD)


def _walk_tiles(i, step, reverse):
    first, last = (i, 0) if reverse else (0, i)
    step(first, True)
    _between_unrolled(i, lambda j, nxt: step(j, False), reverse)

    @pl.when(i > 0)
    def _():
        step(last, True)


UNROLL = 4


def _between_unrolled(i, step, reverse):
    tile = (lambda t: i - 1 - t) if reverse else (lambda t: t + 1)
    after = (lambda j: j - 1) if reverse else (lambda j: j + 1)
    n = jnp.maximum(i - 1, 0)

    def group(u, c):
        j = tile(UNROLL * u)
        for _ in range(UNROLL):
            step(j, after(j))
            j = after(j)
        return c

    lax.fori_loop(0, n // UNROLL, group, 0)
    for k in range(UNROLL - 1):
        @pl.when(n % UNROLL > k)
        def _(k=k):
            j = tile(n - n % UNROLL + k)
            step(j, after(j))


_HALF = (slice(0, TQ), slice(TQ, T2))


def _walk_tiles_lead(i, lead, step, reverse):
    first, last = (i, 0) if reverse else (0, i)
    lead(first, 0)
    lead(first, 1)
    step(first, jnp.maximum(i - 1, 0) if reverse else jnp.minimum(1, i), True)

    _between_unrolled(i, lambda j, nxt: step(j, nxt, False), reverse)

    @pl.when(i > 0)
    def _():
        step(last, None, True)


def _krows(j):
    return pl.ds(pl.multiple_of(j * TQ, TQ), TQ)


def _fox_fwd(proj, crow, name):
    lp = proj.shape[0]
    nb = lp // TQ

    def body(q_ref, k_ref, v_ref, cr_ref, o_ref, lse_ref, m_ref, acc_ref, p_ref):
        i = pl.program_id(1)
        q2 = _stack_heads(q_ref[...], 0.125)
        m_ref[...] = jnp.full(m_ref.shape, NEG, F32)
        acc_ref[...] = jnp.zeros_like(acc_ref)
        lane = lax.broadcasted_iota(jnp.int32, (TQ, LANES), 1)

        def step(j, masked):
            rows_j = _krows(j)
            s = _dot_nt(q2, k_ref[rows_j, :])
            v = v_ref[rows_j, :]
            one = jnp.ones_like(v)
            v_heads = (jnp.where(lane < HEAD_DIM, v, one), jnp.where(lane >= HEAD_DIM, v, one))
            for r in range(0, T2, RC):
                rows = slice(r, r + RC)
                s_c = s[rows] - cr_ref[r // TQ, j]
                if masked:
                    s_c = jnp.where(_chunk_valid(i, j, r, False), s_c, NEG)
                s0, s1 = s_c[:, :LANES], s_c[:, LANES:]
                m_old = m_ref[rows]
                m_new = jnp.maximum(m_old, jnp.max(jnp.maximum(s0, s1), axis=-1, keepdims=True))
                m_ref[rows] = m_new
                acc_ref[rows] = jnp.exp(m_old - m_new) * acc_ref[rows]
                p_ref[rows, :LANES] = jnp.exp(s0 - m_new).astype(BF16)
                p_ref[rows, LANES:] = jnp.exp(s1 - m_new).astype(BF16)
            for h in range(2):
                acc_ref[_HALF[h]] += _dot(p_ref[_HALF[h]], v_heads[h])

        _walk_tiles(i, step, reverse=False)
        acc = acc_ref[...]
        m = m_ref[...]
        outs = []
        for h in range(2):
            a_h = acc[_HALF[h]]
            l = a_h[:, HEAD_DIM:HEAD_DIM + 1] if h == 0 else a_h[:, 0:1]
            lse_ref[h] = m[_HALF[h]][:, 0:1] + jnp.log(l)
            outs.append(a_h / l)
        o_ref[...] = jnp.where(lane < HEAD_DIM, outs[0], outs[1]).astype(BF16)

    return pl.pallas_call(
        body, name=name, grid=(N_PAIRS, nb),
        in_specs=_pair_specs(lp, 0) + [_rowvec_spec(nb)],
        out_specs=[_tile_spec(), _col_spec()],
        out_shape=[jax.ShapeDtypeStruct((lp, W_ATT), BF16),
                   jax.ShapeDtypeStruct((N_PAIRS, 2, lp, 1), F32)],
        scratch_shapes=[pltpu.VMEM((T2, LANES), F32), pltpu.VMEM((T2, LANES), F32), pltpu.VMEM((T2, TQ), BF16)],
        compiler_params=_cparams(("parallel", "arbitrary")),
    )(proj, proj, proj, crow)


def _fox_bwd(proj, do, o, lse, crow, name):
    lp = proj.shape[0]
    nb = lp // TQ

    def body(q_ref, k_ref, v_ref, do_ref, o_ref, lse_ref, cr_ref,
             dq_ref, dk_ref, dv_ref, dcs_ref, dct_ref,
             dk_acc, dv_acc, dq_acc, dct_acc, p_ref, ds_ref, s_ref, dp_ref):
        i = pl.program_id(1)

        @pl.when(i == 0)
        def _():
            dk_acc[...] = jnp.zeros_like(dk_acc)
            dv_acc[...] = jnp.zeros_like(dv_acc)
            dcs_ref[...] = jnp.zeros_like(dcs_ref)

        dq_acc[...] = jnp.zeros_like(dq_acc)
        dct_acc[...] = jnp.zeros_like(dct_acc)
        do_ = do_ref[...]
        q2 = _stack_heads(q_ref[...], 0.125)
        do2 = _stack_heads(do_)
        prod = do_.astype(F32) * o_ref[...].astype(F32)
        lane = lax.broadcasted_iota(jnp.int32, prod.shape, 1)
        delta2 = jnp.concatenate(
            [jnp.sum(jnp.where(lane < HEAD_DIM, prod, 0.0), axis=-1, keepdims=True),
             jnp.sum(jnp.where(lane >= HEAD_DIM, prod, 0.0), axis=-1, keepdims=True)], axis=0)
        lse2 = _stack_cols(lse_ref)

        def lead(j, h):
            rows_j = _krows(j)
            s_ref[_HALF[h]] = _dot_nt(q2[_HALF[h]], k_ref[rows_j, :])
            dp_ref[_HALF[h]] = _dot_nt(do2[_HALF[h]], v_ref[rows_j, :])

        def step(j, nxt, masked):
            rows_j = _krows(j)
            k = k_ref[rows_j, :]
            for h in range(2):
                cs = jnp.zeros((1, TQ), F32)
                for r in range(h * TQ, (h + 1) * TQ, RC):
                    rows = slice(r, r + RC)
                    p = jnp.exp(s_ref[rows] - cr_ref[h, j] - lse2[rows])
                    if masked:
                        p = jnp.where(_chunk_valid(i, j, r, False), p, 0.0)
                    ds = p * (dp_ref[rows] - delta2[rows])
                    p_ref[rows] = p.astype(BF16)
                    ds_ref[rows] = ds.astype(BF16)
                    dct_acc[rows] += jnp.sum(ds, axis=-1, keepdims=True)
                    cs = cs + jnp.sum(ds, axis=0, keepdims=True)
                dcs_ref[h, j] -= cs
                if nxt is not None:
                    lead(nxt, h)
                dsb = ds_ref[_HALF[h]]
                dq_acc[_HALF[h]] += _dot(dsb, k)
                dk_acc[rows_j, :] += _dot_tn(dsb, q2[_HALF[h]])
                dv_acc[rows_j, :] += _dot_tn(p_ref[_HALF[h]], do2[_HALF[h]])

        _walk_tiles_lead(i, lead, step, reverse=False)
        dct = dct_acc[...]
        dct_ref[0] = dct[:TQ]
        dct_ref[1] = dct[TQ:]
        dq_ref[...] = (_unstack_heads(dq_acc[...]) * 0.125).astype(BF16)

        @pl.when(i == nb - 1)
        def _():
            dk_ref[...] = dk_acc[...].astype(BF16)
            dv_ref[...] = dv_acc[...].astype(BF16)

    whole = pl.BlockSpec((lp, LANES), lambda p, i: (0, p))
    return pl.pallas_call(
        body, name=name, grid=(N_PAIRS, nb),
        in_specs=_pair_specs(lp, 0) + [_tile_spec(), _tile_spec(), _col_spec(), _rowvec_spec(nb)],
        out_specs=[_tile_spec(), whole, whole, _rowvec_spec(nb), _col_spec()],
        out_shape=[jax.ShapeDtypeStruct((lp, W_ATT), BF16)] * 3
        + [jax.ShapeDtypeStruct((N_PAIRS, 2, nb, 1, TQ), F32), jax.ShapeDtypeStruct((N_PAIRS, 2, lp, 1), F32)],
        scratch_shapes=[pltpu.VMEM((lp, LANES), F32), pltpu.VMEM((lp, LANES), F32),
                        pltpu.VMEM((T2, LANES), F32), pltpu.VMEM((T2, 1), F32),
                        pltpu.VMEM((T2, TQ), BF16), pltpu.VMEM((T2, TQ), BF16),
                        pltpu.VMEM((T2, TQ), F32), pltpu.VMEM((T2, TQ), F32)],
        compiler_params=_cparams(("parallel", "arbitrary")),
    )(proj, proj, proj, do, o, lse, crow)


def _sb_fwd(proj, name):
    lp = proj.shape[0]
    nb = lp // TQ
    tsuf = _tri(TQ, lambda r, c: r > c)

    def body(q_ref, k_ref, v_ref, t_ref, o_ref, lt_ref, run_ref, acc_ref, zl_ref, hl_ref, a_ref, z_ref):
        i = pl.program_id(1)
        q2 = _stack_heads(q_ref[...], 0.125)
        run_ref[...] = jnp.zeros_like(run_ref)
        acc_ref[...] = jnp.zeros_like(acc_ref)

        def lead(j, h):
            z_ref[_HALF[h]] = _dot_nt(q2[_HALF[h]], k_ref[_krows(j), :])

        def step(j, nxt, masked):
            t = t_ref[...]
            v = v_ref[_krows(j), :]
            later = []
            for h in range(2):
                for r in range(h * TQ, (h + 1) * TQ, RC):
                    rows = slice(r, r + RC)
                    z_c = z_ref[rows]
                    lk = _log_keep(z_c)
                    if masked:
                        lk = jnp.where(_chunk_valid(i, j, r, True), lk, 0.0)
                    hl_ref[rows] = lk.astype(BF16)
                    zl_ref[rows] = z_c + lk + run_ref[rows]
                    run_ref[rows] += jnp.sum(lk, axis=-1, keepdims=True)
                if nxt is not None:
                    lead(nxt, h)
                later.append(_dot(hl_ref[_HALF[h]], t))
            for h in range(2):
                for r in range(0, TQ, RC):
                    rows = slice(h * TQ + r, h * TQ + r + RC)
                    a = jnp.exp(zl_ref[rows] + later[h][r:r + RC])
                    if masked:
                        a = jnp.where(_chunk_valid(i, j, h * TQ + r, True), a, 0.0)
                    a_ref[rows] = a.astype(BF16)
                acc_ref[_HALF[h]] += _dot(a_ref[_HALF[h]], v)

        _walk_tiles_lead(i, lead, step, reverse=True)
        run = run_ref[...]
        lt_ref[0] = run[:TQ]
        lt_ref[1] = run[TQ:]
        o_ref[...] = _unstack_heads(acc_ref[...]).astype(BF16)

    base = 3 * N_PAIRS
    return pl.pallas_call(
        body, name=name, grid=(N_PAIRS, nb),
        in_specs=_pair_specs(lp, base) + [pl.BlockSpec((TQ, TQ), lambda p, i: (0, 0))],
        out_specs=[_tile_spec(), _col_spec()],
        out_shape=[jax.ShapeDtypeStruct((lp, W_ATT), BF16),
                   jax.ShapeDtypeStruct((N_PAIRS, 2, lp, 1), F32)],
        scratch_shapes=[pltpu.VMEM((T2, 1), F32), pltpu.VMEM((T2, LANES), F32), pltpu.VMEM((T2, TQ), F32),
                        pltpu.VMEM((T2, TQ), BF16), pltpu.VMEM((T2, TQ), BF16), pltpu.VMEM((T2, TQ), F32)],
        compiler_params=_cparams(("parallel", "arbitrary")),
    )(proj, proj, proj, tsuf)


def _sb_bwd(proj, do, ltot, name):
    lp = proj.shape[0]
    nb = lp // TQ
    tincl = _tri(TQ, lambda r, c: r <= c)
    texcl = _tri(TQ, lambda r, c: r < c)

    def body(q_ref, k_ref, v_ref, do_ref, lt_ref, ti_ref, te_ref, dq_ref, dk_ref, dv_ref,
             dk_acc, dv_acc, dq_acc, pc_ref, gc_ref, zl_ref, keep_ref, g_ref, z_ref, da_ref,
             hl_ref, gb_ref, a_ref, dz_ref):
        i = pl.program_id(1)

        @pl.when(i == 0)
        def _():
            dk_acc[...] = jnp.zeros_like(dk_acc)
            dv_acc[...] = jnp.zeros_like(dv_acc)

        dq_acc[...] = jnp.zeros_like(dq_acc)
        gc_ref[...] = jnp.zeros_like(gc_ref)
        pc_ref[...] = _stack_cols(lt_ref)
        q2 = _stack_heads(q_ref[...], 0.125)
        do2 = _stack_heads(do_ref[...])

        def lead(j, h):
            rows_j = _krows(j)
            z_ref[_HALF[h]] = _dot_nt(q2[_HALF[h]], k_ref[rows_j, :])
            da_ref[_HALF[h]] = _dot_nt(do2[_HALF[h]], v_ref[rows_j, :])

        def step(j, nxt, masked):
            rows_j = _krows(j)
            k = k_ref[rows_j, :]
            ti = ti_ref[...]
            te = te_ref[...]
            upto, before = [], []
            for h in range(2):
                for r in range(h * TQ, (h + 1) * TQ, RC):
                    rows = slice(r, r + RC)
                    z_c = z_ref[rows]
                    lk = _log_keep(z_c)
                    if masked:
                        lk = jnp.where(_chunk_valid(i, j, r, True), lk, 0.0)
                    hl_ref[rows] = lk.astype(BF16)
                    keep_ref[rows] = jnp.exp(lk)
                    zl_ref[rows] = z_c + lk + pc_ref[rows]
                    pc_ref[rows] -= jnp.sum(lk, axis=-1, keepdims=True)
                upto.append(_dot(hl_ref[_HALF[h]], ti))
            for h in range(2):
                for r in range(0, TQ, RC):
                    rows = slice(h * TQ + r, h * TQ + r + RC)
                    a = jnp.exp(zl_ref[rows] - upto[h][r:r + RC])
                    if masked:
                        a = jnp.where(_chunk_valid(i, j, h * TQ + r, True), a, 0.0)
                    g = a * da_ref[rows]
                    a_ref[rows] = a.astype(BF16)
                    g_ref[rows] = g
                    gb_ref[rows] = g.astype(BF16)
                if nxt is not None:
                    lead(nxt, h)
                before.append(_dot(gb_ref[_HALF[h]], te))
            for h in range(2):
                for r in range(0, TQ, RC):
                    rows = slice(h * TQ + r, h * TQ + r + RC)
                    g = g_ref[rows]
                    keep = keep_ref[rows]
                    dz = g * keep - (1.0 - keep) * (gc_ref[rows] + before[h][r:r + RC])
                    if masked:
                        dz = jnp.where(_chunk_valid(i, j, h * TQ + r, True), dz, 0.0)
                    dz_ref[rows] = dz.astype(BF16)
                    gc_ref[rows] += jnp.sum(g, axis=-1, keepdims=True)
                dzb = dz_ref[_HALF[h]]
                dq_acc[_HALF[h]] += _dot(dzb, k)
                dk_acc[rows_j, :] += _dot_tn(dzb, q2[_HALF[h]])
                dv_acc[rows_j, :] += _dot_tn(a_ref[_HALF[h]], do2[_HALF[h]])

        _walk_tiles_lead(i, lead, step, reverse=False)
        dq_ref[...] = (_unstack_heads(dq_acc[...]) * 0.125).astype(BF16)

        @pl.when(i == nb - 1)
        def _():
            dk_ref[...] = dk_acc[...].astype(BF16)
            dv_ref[...] = dv_acc[...].astype(BF16)

    base = 3 * N_PAIRS
    whole = pl.BlockSpec((lp, LANES), lambda p, i: (0, p))
    tri = lambda rows: pl.BlockSpec((rows, TQ), lambda p, i: (0, 0))
    wide = lambda dt: pltpu.VMEM((T2, TQ), dt)
    return pl.pallas_call(
        body, name=name, grid=(N_PAIRS, nb),
        in_specs=_pair_specs(lp, base) + [_tile_spec(), _col_spec(), tri(TQ), tri(TQ)],
        out_specs=[_tile_spec(), whole, whole],
        out_shape=[jax.ShapeDtypeStruct((lp, W_ATT), BF16)] * 3,
        scratch_shapes=[pltpu.VMEM((lp, LANES), F32), pltpu.VMEM((lp, LANES), F32),
                        pltpu.VMEM((T2, LANES), F32), pltpu.VMEM((T2, 1), F32), pltpu.VMEM((T2, 1), F32),
                        wide(F32), wide(F32), wide(F32), wide(F32), wide(F32),
                        wide(BF16), wide(BF16), wide(BF16), wide(BF16)],
        compiler_params=_cparams(("parallel", "arbitrary")),
    )(proj, proj, proj, do, ltot, tincl, texcl)


def _local_step(x, target, meta, gains, w_in, b_forget, w_o_fox, w_o_sb, w_out, w_up, conv_w, conv_b, w_down):
    seq, dm = x.shape
    lp = PAD + N_META + seq
    nb = lp // TQ
    s = [W_ATT, W_ATT, W_ATT, 8, W_ATT, W_ATT, W_ATT, dm, dm]
    off = [sum(s[:i]) for i in range(len(s) + 1)]
    cols = lambda i: w_in[:, off[i]:off[i + 1]]
    w1 = jnp.concatenate([cols(0), cols(1), cols(2), cols(4), cols(5), cols(6), cols(7), cols(8)], axis=1)
    wf = jnp.pad(cols(3), ((0, 0), (0, LANES - 8)))
    n1 = w1.shape[1]
    ncat = n1 + 512
    w_cat = jnp.concatenate([w1, wf, jnp.zeros((dm, ncat - n1 - LANES), BF16)], axis=1)
    bf = jnp.pad(b_forget.reshape(1, 8), ((0, 0), (0, LANES - 8)))
    g = [gains[i].reshape(1, dm) for i in range(4)]
    cb = conv_b.reshape(1, -1)

    h0 = jnp.concatenate([jnp.zeros((PAD, dm), F32), meta, x], axis=0)

    proj, xn1 = _rms_mm(h0, g[0], w1, "in_proj")
    logf = _logf(xn1, wf, bf, "log_forget")
    c = _cumsum_rows(logf, "forget_cumsum")
    crow = c[:, :8].T.reshape(N_PAIRS, 2, nb, 1, TQ)
    o_a, lse = _fox_fwd(proj, crow, "fox_fwd")
    o_b, ltot = _sb_fwd(proj, "sb_fwd")
    y_a, y_b, m, mixed, h1 = _merge_fwd(o_a, o_b, proj, h0, w_o_fox, w_o_sb, w_out, g[1], "merge_fwd")
    up, xn3 = _rms_mm(h1, g[2], w_up, "up_proj")
    a = _convgate_fwd(up, conv_w, cb, "convgate_fwd")
    ffn, dy, ss = _down_loss(a, w_down, h1, g[3], target, "down_loss")

    dffn, da, dg3 = _down_bwd(dy, ffn, g[3], w_down, "down_bwd")
    d_w_down = _mm_tn(a, dffn, "dw_down")
    du, d_conv_w, d_conv_b = _convgate_bwd(up, da, conv_w, cb, "convgate_bwd")
    dup = _conv_transpose(du, conv_w, "conv_transpose")
    d_w_up = _mm_tn(xn3, dup, "dw_up")
    dh1, dg2 = _mm_rmsbwd(dup, w_up, h1, g[2], dy, "up_bwd")
    dmx, dya, dyb, dga, dgb, do_a, do_b, dg1 = _merge_bwd(
        dh1, mixed, g[1], w_out, proj, y_a, y_b, w_o_fox, w_o_sb, "merge_bwd")
    d_w_out = _mm_tn(m, dmx, "dw_out")
    d_w_o_fox = _mm_tn(o_a, dya, "dw_o_fox")
    d_w_o_sb = _mm_tn(o_b, dyb, "dw_o_sb")
    dq_a, dk_a, dv_a, dcs, dct = _fox_bwd(proj, do_a, o_a, lse, crow, "fox_bwd")
    dq_b, dk_b, dv_b = _sb_bwd(proj, do_b, ltot, "sb_bwd")
    dc = (dct.reshape(8, lp) + dcs.reshape(8, lp)).T
    df, db = _dlogf(jnp.pad(dc, ((0, 0), (0, LANES - 8))), logf, "forget_bwd")
    dcat = jnp.concatenate([dq_a, dk_a, dv_a, dq_b, dk_b, dv_b, dga, dgb, df.astype(BF16),
                            jnp.zeros((lp, ncat - n1 - LANES), BF16)], axis=1)
    d_w_cat = _mm_tn(xn1, dcat, "dw_in")
    dh0, dg0 = _mm_rmsbwd(dcat, w_cat, h0, g[0], dh1, "in_bwd")

    wc = lambda k: d_w_cat[:, k * W_ATT:(k + 1) * W_ATT]
    d_w_in = jnp.concatenate([wc(0), wc(1), wc(2), d_w_cat[:, n1:n1 + 8], wc(3), wc(4), wc(5),
                              d_w_cat[:, 6 * W_ATT:n1]], axis=1)
    d_gains = jnp.concatenate([dg0, dg1, dg2, dg3], axis=0)
    grads = (dh0[PAD:PAD + N_META], d_gains, d_w_in, db[0, :8], d_w_o_fox, d_w_o_sb, d_w_out,
             d_w_up, d_conv_w, d_conv_b[0], d_w_down)
    return ss[0, 0], dh0[PAD + N_META:], grads


def _rows(a, n_rows):
    flat = a.reshape(-1)
    return jnp.pad(flat, (0, n_rows * D_MODEL - flat.shape[0])).reshape(n_rows, D_MODEL)


_SMALL = (("meta", 4), ("gains", 1), ("conv_w", 5), ("b_forget", 1), ("conv_b", 6))


def _pack(meta, gains, w_in, b_forget, w_o_fox, w_o_sb, w_out, w_up, conv_w, conv_b, w_down):
    big = [a.reshape(-1, D_MODEL) for a in (w_in, w_o_fox, w_o_sb, w_out, w_up, w_down)]
    n_big = sum(a.shape[0] for a in big)
    small = [_rows(a, n) for a, (_, n) in zip((meta, gains, conv_w, b_forget, conv_b), _SMALL)]
    n_small = sum(n for _, n in _SMALL)
    z = lambda n: jnp.zeros((n, D_MODEL), big[0].dtype)
    return jnp.concatenate(big + [z(PK_BIG_ROWS - n_big)] + small
                           + [z(PK_ROWS - PK_BIG_ROWS - n_small)], axis=0)


def _unpack(p):
    def take(r0, shape):
        n = math.prod(shape)
        nr = -(-n // D_MODEL)
        return p[r0:r0 + nr].reshape(-1)[:n].reshape(shape), r0 + nr
    w_in, r = take(0, (1, 1024, 1282))
    w_o_fox, r = take(r, (1, 512, 256))
    w_o_sb, r = take(r, (1, 512, 256))
    w_out, r = take(r, (1, 256, 1024))
    w_up, r = take(r, (1, 1024, 1408))
    w_down, r = take(r, (1, 704, 1024))
    r = PK_BIG_ROWS
    meta, r = take(r, (16, 256))
    gains, r = take(r, (1, 4, 256))
    conv_w, r = take(r, (1, 3, 1408))
    b_forget, r = take(r, (1, 8))
    conv_b, r = take(r, (1, 5632))
    return meta, gains, w_in, b_forget, w_o_fox, w_o_sb, w_out, w_up, conv_w, conv_b, w_down


def _chip_peers():
    x, y, c = lax.axis_index("x"), lax.axis_index("y"), lax.axis_index("c")
    return [(x, 1 - y, c), (1 - x, y, c), (1 - x, 1 - y, c)]


def _all_gather_chips(arrays, name):
    n = len(arrays)

    def body(*refs):
        ins, outs = refs[:n], refs[n:2 * n]
        send_sems, recv_sems, local_sems = refs[2 * n:]
        x, y = lax.axis_index("x"), lax.axis_index("y")
        me = 2 * x + y
        peers = _chip_peers()
        copies = []
        for a in range(n):
            mine = pltpu.make_async_copy(ins[a], outs[a].at[me], local_sems.at[a])
            mine.start()
            copies.append(mine)
        remote = []
        for a in range(n):
            for j, peer in enumerate(peers):
                cp = pltpu.make_async_remote_copy(
                    src_ref=ins[a], dst_ref=outs[a].at[me],
                    send_sem=send_sems.at[3 * a + j], recv_sem=recv_sems.at[3 * a + j],
                    device_id=peer, device_id_type=MESH)
                cp.start()
                remote.append(cp)
        for cp in remote:
            cp.wait()
        for cp in copies:
            cp.wait()

    any_spec = pl.BlockSpec(memory_space=pl.ANY)
    return pl.pallas_call(
        body, name=name,
        in_specs=[any_spec] * n, out_specs=[any_spec] * n,
        out_shape=[jax.ShapeDtypeStruct((4,) + a.shape, a.dtype) for a in arrays],
        scratch_shapes=[pltpu.SemaphoreType.DMA((3 * n,)), pltpu.SemaphoreType.DMA((3 * n,)),
                        pltpu.SemaphoreType.DMA((n,))],
    )(*arrays)


def _scatter_chips(chunks, name):
    _, rows, cols = chunks.shape

    def body(in_ref, out_ref, send_sems, recv_sems):
        x, y = lax.axis_index("x"), lax.axis_index("y")
        targets = [2 * x + (1 - y), 2 * (1 - x) + y, 2 * (1 - x) + (1 - y)]
        remote = []
        for j, peer in enumerate(_chip_peers()):
            cp = pltpu.make_async_remote_copy(
                src_ref=in_ref.at[targets[j]], dst_ref=out_ref.at[j],
                send_sem=send_sems.at[j], recv_sem=recv_sems.at[j],
                device_id=peer, device_id_type=MESH)
            cp.start()
            remote.append(cp)
        for cp in remote:
            cp.wait()

    any_spec = pl.BlockSpec(memory_space=pl.ANY)
    return pl.pallas_call(
        body, name=name, in_specs=[any_spec], out_specs=any_spec,
        out_shape=jax.ShapeDtypeStruct((3, rows, cols), chunks.dtype),
        scratch_shapes=[pltpu.SemaphoreType.DMA((3,)), pltpu.SemaphoreType.DMA((3,))],
    )(chunks)


def _swap_half_rows(chunks, name):
    n, rows, cols = chunks.shape
    r2 = rows // 2

    def body(in_ref, out_ref, send_sem, recv_sem):
        x, y, c = lax.axis_index("x"), lax.axis_index("y"), lax.axis_index("c")
        cp = pltpu.make_async_remote_copy(
            src_ref=in_ref.at[:, pl.ds((1 - c) * r2, r2), :], dst_ref=out_ref,
            send_sem=send_sem, recv_sem=recv_sem, device_id=(x, y, 1 - c), device_id_type=MESH)
        cp.start()
        cp.wait()

    any_spec = pl.BlockSpec(memory_space=pl.ANY)
    return pl.pallas_call(
        body, name=name, in_specs=[any_spec], out_specs=any_spec,
        out_shape=jax.ShapeDtypeStruct((n, r2, cols), chunks.dtype),
        scratch_shapes=[pltpu.SemaphoreType.DMA, pltpu.SemaphoreType.DMA],
    )(chunks)


def _gather_halves(half, name):
    def body(in_ref, out_ref, send_sem, recv_sem):
        x, y, c = lax.axis_index("x"), lax.axis_index("y"), lax.axis_index("c")
        cp = pltpu.make_async_remote_copy(
            src_ref=in_ref, dst_ref=out_ref, send_sem=send_sem, recv_sem=recv_sem,
            device_id=(x, y, 1 - c), device_id_type=MESH)
        cp.start()
        cp.wait()

    any_spec = pl.BlockSpec(memory_space=pl.ANY)
    other = pl.pallas_call(
        body, name=name, in_specs=[any_spec], out_specs=any_spec,
        out_shape=jax.ShapeDtypeStruct(half.shape, half.dtype),
        scratch_shapes=[pltpu.SemaphoreType.DMA, pltpu.SemaphoreType.DMA],
    )(half)
    axis = half.ndim - 2
    return lax.cond(lax.axis_index("c") == 0,
                    lambda: jnp.concatenate([half, other], axis=axis),
                    lambda: jnp.concatenate([other, half], axis=axis))


def _add_bf16(a, b, name):
    n, rows, cols = a.shape

    def body(a_ref, b_ref, o_ref):
        o_ref[...] = (a_ref[...].astype(F32) + b_ref[...].astype(F32)).astype(BF16)

    spec = pl.BlockSpec((None, PK_TILE, cols), lambda k, i: (k, i, 0))
    return pl.pallas_call(
        body, name=name, grid=(n, rows // PK_TILE),
        in_specs=[spec, spec], out_specs=spec,
        out_shape=jax.ShapeDtypeStruct(a.shape, BF16),
        compiler_params=_cparams(("parallel", "parallel")),
    )(a, b)


def _chip_sum(chunks, recv, name):
    _, rows, cols = chunks.shape

    def body(own_ref, r_ref, o_ref):
        f = lambda a: a.astype(F32)
        o_ref[...] = (f(own_ref[...]) + f(r_ref[0])) + (f(r_ref[1]) + f(r_ref[2]))

    me = 2 * lax.axis_index("x") + lax.axis_index("y")
    own = lax.dynamic_index_in_dim(chunks, me, axis=0, keepdims=False)
    return pl.pallas_call(
        body, name=name, grid=(rows // PK_TILE,),
        in_specs=[pl.BlockSpec((PK_TILE, cols), lambda i: (i, 0)),
                  pl.BlockSpec((3, PK_TILE, cols), lambda i: (0, i, 0))],
        out_specs=pl.BlockSpec((PK_TILE, cols), lambda i: (i, 0)),
        out_shape=jax.ShapeDtypeStruct((rows, cols), F32),
        compiler_params=_cparams(("parallel",)),
    )(own, recv)


def _adamw(w, m, v, g, name):
    shape = w.shape
    rows, cols = math.prod(shape[:-1]), shape[-1]
    w, m, v, g = (a.reshape(rows, cols) for a in (w, m, v, g))
    tile = next((t for t in (256, 128, 64, 32, 16, 8) if rows % t == 0), rows)
    c1 = 1.0 - ADAM_B1 ** ADAM_STEP
    c2 = 1.0 - ADAM_B2 ** ADAM_STEP

    def body(w_ref, m_ref, v_ref, g_ref, d_ref, nm_ref, nv_ref):
        g = g_ref[...]
        nm = ADAM_B1 * m_ref[...] + (1.0 - ADAM_B1) * g
        nv = ADAM_B2 * v_ref[...] + (1.0 - ADAM_B2) * (g * g)
        nm_ref[...] = nm
        nv_ref[...] = nv
        d_ref[...] = -ADAM_LR * ((nm / c1) / (jnp.sqrt(nv / c2) + ADAM_EPS) + ADAM_WD * w_ref[...])

    spec = pl.BlockSpec((tile, cols), lambda i: (i, 0))
    outs = pl.pallas_call(
        body, name=name, grid=(rows // tile,),
        in_specs=[spec] * 4, out_specs=[spec] * 3,
        out_shape=[jax.ShapeDtypeStruct((rows, cols), F32)] * 3,
        compiler_params=_cparams(("parallel",)),
    )(w, m, v, g)
    return [o.reshape(shape) for o in outs]


def _full_weights(big, small):
    def gather(src, r0, shape, axis):
        n = math.prod(shape)
        nr = -(-n // D_MODEL)
        parts = [src[k, r0:r0 + nr].reshape(-1)[:n].reshape(shape) for k in range(4)]
        return jnp.concatenate(parts, axis=axis), r0 + nr
    w_in, r = gather(big, 0, (1024, 1282), 1)
    w_o_fox, r = gather(big, r, (512, 256), 1)
    w_o_sb, r = gather(big, r, (512, 256), 1)
    w_out, r = gather(big, r, (256, 1024), 0)
    w_up, r = gather(big, r, (1024, 1408), 1)
    w_down, r = gather(big, r, (704, 1024), 0)
    meta, r = gather(small, 0, (16, 256), 1)
    gains, r = gather(small, r, (4, 256), 1)
    conv_w, r = gather(small, r, (3, 1408), 1)
    return meta, gains, w_in, w_o_fox, w_o_sb, w_out, w_up, conv_w, w_down


def _chunks_for_chips(grads):
    d_meta, d_gains, d_w_in, d_b, d_w_o_fox, d_w_o_sb, d_w_out, d_w_up, d_conv_w, d_conv_b, d_w_down = (
        g.astype(BF16) for g in grads)
    out = []
    for k in range(4):
        col = lambda a, w: a[:, k * w:(k + 1) * w]
        row = lambda a, w: a[k * w:(k + 1) * w]
        out.append(_pack(col(d_meta, 256), col(d_gains, 256), col(d_w_in, 1282), d_b, col(d_w_o_fox, 256),
                         col(d_w_o_sb, 256), row(d_w_out, 256), col(d_w_up, 1408), col(d_conv_w, 1408),
                         d_conv_b, row(d_w_down, 704)))
    return jnp.stack(out, axis=0)


def kernel(x, meta_tokens, norm_gains, w_in, b_forget, w_o_fox, w_o_sb, w_out, w_up, conv_w, conv_b, w_down, loss_target, m_meta_tokens, m_norm_gains, m_w_in, m_b_forget, m_w_o_fox, m_w_o_sb, m_w_out, m_w_up, m_conv_w, m_conv_b, m_w_down, v_meta_tokens, v_norm_gains, v_w_in, v_b_forget, v_w_o_fox, v_w_o_sb, v_w_out, v_w_up, v_conv_w, v_conv_b, v_w_down):
    names = ("meta_tokens", "norm_gains", "w_in", "b_forget", "w_o_fox", "w_o_sb", "w_out", "w_up", "conv_w",
             "conv_b", "w_down")
    ws = (meta_tokens, norm_gains, w_in, b_forget, w_o_fox, w_o_sb, w_out, w_up, conv_w, conv_b, w_down)
    ms = (m_meta_tokens, m_norm_gains, m_w_in, m_b_forget, m_w_o_fox, m_w_o_sb, m_w_out, m_w_up, m_conv_w,
          m_conv_b, m_w_down)
    vs = (v_meta_tokens, v_norm_gains, v_w_in, v_b_forget, v_w_o_fox, v_w_o_sb, v_w_out, v_w_up, v_conv_w,
          v_conv_b, v_w_down)
    wp = _pack(meta_tokens, norm_gains[0], w_in[0], b_forget[0], w_o_fox[0], w_o_sb[0], w_out[0], w_up[0],
               conv_w[0], conv_b[0], w_down[0])

    hb = PK_BIG_ROWS // 2
    big_half = lax.dynamic_slice_in_dim(wp[:PK_BIG_ROWS].astype(BF16), lax.axis_index("c") * hb, hb, axis=0)
    big_half, small = _all_gather_chips(
        [big_half, wp[PK_BIG_ROWS:PK_BIG_ROWS + PK_SMALL_ROWS]], "gather_weights")
    big = _gather_halves(big_half, "gather_weight_halves")
    meta, gains, f_w_in, f_w_o_fox, f_w_o_sb, f_w_out, f_w_up, f_conv_w, f_w_down = _full_weights(big, small)

    ss, dx, grads = _local_step(x[0], loss_target[0], meta, gains, f_w_in, b_forget[0], f_w_o_fox, f_w_o_sb,
                                f_w_out, f_w_up, f_conv_w, conv_b[0], f_w_down)
    loss = lax.psum(0.5 * ss / D_MODEL, ("x", "y", "c"))

    chunks = _chunks_for_chips(grads)
    r2 = PK_ROWS // 2
    from_sibling = _swap_half_rows(chunks, "swap_halves")
    own = lax.dynamic_slice_in_dim(chunks, lax.axis_index("c") * r2, r2, axis=1)
    core_sum = _add_bf16(own, from_sibling, "core_sum")
    recv = _scatter_chips(core_sum, "scatter_grads")
    g = _gather_halves(_chip_sum(core_sum, recv, "chip_sum"), "gather_halves")
    gs = _unpack(g)
    steps = [_adamw(w, m, v, gp, "adamw_" + n) for n, w, m, v, gp in zip(names, ws, ms, vs, gs)]
    return (loss, dx[None], *gs, *[s[0] for s in steps], *[s[1] for s in steps], *[s[2] for s in steps])
```

```python
import functools
import math

import jax
import jax.numpy as jnp
from jax import lax
from jax.experimental import pallas as pl
from jax.experimental.pallas import tpu as pltpu

F32 = jnp.float32
BF16 = jnp.bfloat16

D_MODEL = 1024
N_META = 16
HEAD_DIM = 64
N_PAIRS = 4
W_ATT = 512
D_FF = 2816
EPS = 1e-6
NEG = -1e30
TQ = 256
PAD = TQ - N_META
TCONV = 128
HALO = 16
LANES = 128
VMEM_LIMIT = 56 * 1024 * 1024

ADAM_LR = 0.001
ADAM_B1 = 0.9
ADAM_B2 = 0.999
ADAM_EPS = 1e-08
ADAM_WD = 0.01
ADAM_STEP = 10

MESH = pl.DeviceIdType.MESH

PK_BIG_ROWS = 3936
PK_SMALL_ROWS = 24
PK_ROWS = 4096
PK_TILE = 128


def _cparams(sem, **kw):
    return pltpu.CompilerParams(dimension_semantics=sem, vmem_limit_bytes=VMEM_LIMIT, **kw)


def _row_tile(lp):
    return 768 if lp % 768 == 0 else 256


def _wide_tile(n):
    return next(t for t in (1408, 1280, 1024, 512, 256) if n % t == 0)


def _rms(x):
    return lax.rsqrt(jnp.mean(x * x, axis=-1, keepdims=True) + EPS)


def _log_sigmoid(x):
    return jnp.minimum(x, 0.0) - jnp.log(1.0 + jnp.exp(-jnp.abs(x)))


_LOG2E = 1.4426950408889634
_LN2 = 0.6931471805599453


def _log_keep(z):
    t = jnp.exp2(jnp.abs(z) * (-_LOG2E))
    return jnp.log2(1.0 + t) * (-_LN2) - jnp.maximum(z, 0.0)


def _dot(a, b):
    return jnp.dot(a, b, preferred_element_type=F32)


def _dot_nt(a, b):
    return lax.dot_general(a, b, (((1,), (1,)), ((), ())), preferred_element_type=F32)


def _dot_tn(a, b):
    return lax.dot_general(a, b, (((0,), (0,)), ((), ())), preferred_element_type=F32)


def _rms_mm(h, g, w, name):
    lp, dm = h.shape
    n = w.shape[1]
    tr, tn = _row_tile(lp), _wide_tile(n)

    def body(h_ref, g_ref, w_ref, out_ref, xn_ref):
        @pl.when(pl.program_id(1) == 0)
        def _():
            x = h_ref[...]
            xn_ref[...] = (x * _rms(x) * g_ref[...]).astype(BF16)
        out_ref[...] = _dot(xn_ref[...], w_ref[...]).astype(BF16)

    return pl.pallas_call(
        body, name=name, grid=(lp // tr, n // tn),
        in_specs=[pl.BlockSpec((tr, dm), lambda i, j: (i, 0)),
                  pl.BlockSpec((1, dm), lambda i, j: (0, 0)),
                  pl.BlockSpec((dm, tn), lambda i, j: (0, j))],
        out_specs=[pl.BlockSpec((tr, tn), lambda i, j: (i, j)),
                   pl.BlockSpec((tr, dm), lambda i, j: (i, 0))],
        out_shape=[jax.ShapeDtypeStruct((lp, n), BF16), jax.ShapeDtypeStruct((lp, dm), BF16)],
        compiler_params=_cparams(("parallel", "arbitrary")),
    )(h, g, w)


def _mm_rmsbwd(dy, w, h, g, dh_in, name):
    lp, kd = dy.shape
    dm = w.shape[0]
    tr, tk = _row_tile(lp), _wide_tile(kd)
    nk = kd // tk

    def body(dy_ref, w_ref, h_ref, g_ref, dhin_ref, dh_ref, dg_ref, acc_ref):
        i, k = pl.program_id(0), pl.program_id(1)

        @pl.when(k == 0)
        def _():
            acc_ref[...] = jnp.zeros_like(acc_ref)

        acc_ref[...] += _dot_nt(dy_ref[...], w_ref[...])

        @pl.when(k == nk - 1)
        def _():
            dxn = acc_ref[...]
            x = h_ref[...]
            r = _rms(x)
            yhat = x * r
            part = jnp.sum(dxn * yhat, axis=0, keepdims=True)
            dyh = dxn * g_ref[...]
            dx = r * (dyh - yhat * jnp.mean(dyh * yhat, axis=-1, keepdims=True))
            dh_ref[...] = dhin_ref[...] + dx

            @pl.when(i == 0)
            def _():
                dg_ref[...] = part

            @pl.when(i > 0)
            def _():
                dg_ref[...] += part

    return pl.pallas_call(
        body, name=name, grid=(lp // tr, nk),
        in_specs=[pl.BlockSpec((tr, tk), lambda i, k: (i, k)),
                  pl.BlockSpec((dm, tk), lambda i, k: (0, k)),
                  pl.BlockSpec((tr, dm), lambda i, k: (i, 0)),
                  pl.BlockSpec((1, dm), lambda i, k: (0, 0)),
                  pl.BlockSpec((tr, dm), lambda i, k: (i, 0))],
        out_specs=[pl.BlockSpec((tr, dm), lambda i, k: (i, 0)),
                   pl.BlockSpec((1, dm), lambda i, k: (0, 0))],
        out_shape=[jax.ShapeDtypeStruct((lp, dm), F32), jax.ShapeDtypeStruct((1, dm), F32)],
        scratch_shapes=[pltpu.VMEM((tr, dm), F32)],
        compiler_params=_cparams(("arbitrary", "arbitrary")),
    )(dy, w, h, g, dh_in)


def _mm_tn(x, dy, name):
    lp, kd = x.shape
    n = dy.shape[1]
    tl = _row_tile(lp)
    tk = _wide_tile(kd)
    tn = _wide_tile(n)
    nl = lp // tl

    def body(x_ref, dy_ref, o_ref):
        @pl.when(pl.program_id(2) == 0)
        def _():
            o_ref[...] = jnp.zeros_like(o_ref)
        o_ref[...] += _dot_tn(x_ref[...], dy_ref[...])

    return pl.pallas_call(
        body, name=name, grid=(kd // tk, n // tn, nl),
        in_specs=[pl.BlockSpec((tl, tk), lambda a, b, l: (l, a)),
                  pl.BlockSpec((tl, tn), lambda a, b, l: (l, b))],
        out_specs=pl.BlockSpec((tk, tn), lambda a, b, l: (a, b)),
        out_shape=jax.ShapeDtypeStruct((kd, n), F32),
        compiler_params=_cparams(("parallel", "parallel", "arbitrary")),
    )(x, dy)


def _logf(xn, wf, bf, name):
    lp, dm = xn.shape
    tr = _row_tile(lp)

    def body(xn_ref, wf_ref, b_ref, o_ref):
        f = _dot(xn_ref[...], wf_ref[...]) + b_ref[...]
        row = pl.program_id(0) * tr + lax.broadcasted_iota(jnp.int32, f.shape, 0)
        lane = lax.broadcasted_iota(jnp.int32, f.shape, 1)
        o_ref[...] = jnp.where((row >= PAD) & (lane < 8), _log_sigmoid(f), 0.0)

    return pl.pallas_call(
        body, name=name, grid=(lp // tr,),
        in_specs=[pl.BlockSpec((tr, dm), lambda i: (i, 0)),
                  pl.BlockSpec((dm, LANES), lambda i: (0, 0)),
                  pl.BlockSpec((1, LANES), lambda i: (0, 0))],
        out_specs=pl.BlockSpec((tr, LANES), lambda i: (i, 0)),
        out_shape=jax.ShapeDtypeStruct((lp, LANES), F32),
        compiler_params=_cparams(("parallel",)),
    )(xn, wf, bf)


def _tri(n, rel):
    r = lax.broadcasted_iota(jnp.int32, (n, n), 0)
    c = lax.broadcasted_iota(jnp.int32, (n, n), 1)
    return rel(r, c).astype(BF16)


def _cumsum_rows(x, name):
    lp = x.shape[0]
    nb = lp // TQ
    tl = _tri(TQ, lambda r, c: c <= r)

    def body(x_ref, t_ref, o_ref):
        def step(b, carry):
            rows = pl.ds(pl.multiple_of(b * TQ, TQ), TQ)
            xb = x_ref[rows, :]
            hi = xb.astype(BF16)
            r1 = xb - hi.astype(F32)
            mid = r1.astype(BF16)
            lo = (r1 - mid.astype(F32)).astype(BF16)
            t = t_ref[...]
            o_ref[rows, :] = carry + (_dot(t, hi) + _dot(t, mid) + _dot(t, lo))
            return carry + jnp.sum(xb, axis=0, keepdims=True)
        lax.fori_loop(0, nb, step, jnp.zeros((1, LANES), F32))

    return pl.pallas_call(
        body, name=name,
        in_specs=[pl.BlockSpec(memory_space=pltpu.VMEM)] * 2,
        out_specs=pl.BlockSpec(memory_space=pltpu.VMEM),
        out_shape=jax.ShapeDtypeStruct((lp, LANES), F32),
        compiler_params=pltpu.CompilerParams(vmem_limit_bytes=VMEM_LIMIT),
    )(x, tl)


def _dlogf(dc, logf, name):
    lp = dc.shape[0]
    nb = lp // TQ
    tu = _tri(TQ, lambda r, c: c >= r)

    def body(x_ref, lf_ref, t_ref, df_ref, db_ref):
        def step(bb, carry):
            run, db = carry
            b = nb - 1 - bb
            rows = pl.ds(pl.multiple_of(b * TQ, TQ), TQ)
            xb = x_ref[rows, :]
            hi = xb.astype(BF16)
            r1 = xb - hi.astype(F32)
            mid = r1.astype(BF16)
            lo = (r1 - mid.astype(F32)).astype(BF16)
            t = t_ref[...]
            dlf = run + (_dot(t, hi) + _dot(t, mid) + _dot(t, lo))
            df = dlf * (1.0 - jnp.exp(lf_ref[rows, :]))
            df_ref[rows, :] = df
            return run + jnp.sum(xb, axis=0, keepdims=True), db + jnp.sum(df, axis=0, keepdims=True)
        z = jnp.zeros((1, LANES), F32)
        _, db = lax.fori_loop(0, nb, step, (z, z))
        db_ref[...] = db

    return pl.pallas_call(
        body, name=name,
        in_specs=[pl.BlockSpec(memory_space=pltpu.VMEM)] * 3,
        out_specs=[pl.BlockSpec(memory_space=pltpu.VMEM)] * 2,
        out_shape=[jax.ShapeDtypeStruct((lp, LANES), F32), jax.ShapeDtypeStruct((1, LANES), F32)],
        compiler_params=pltpu.CompilerParams(vmem_limit_bytes=VMEM_LIMIT),
    )(dc, logf, tu)


def _merge_fwd(o_a, o_b, proj, h0, w_oa, w_ob, w_out, g1, name):
    lp, dm = h0.shape
    tr = TQ
    ga_blk = (6 * W_ATT) // dm

    def body(oa_ref, ob_ref, ga_ref, gb_ref, h0_ref, woa_ref, wob_ref, wout_ref, g1_ref,
             ya_ref, yb_ref, m_ref, mixed_ref, h1_ref):
        ya = _dot(oa_ref[...], woa_ref[...])
        yb = _dot(ob_ref[...], wob_ref[...])
        m = jax.nn.sigmoid(ga_ref[...].astype(F32)) * ya + jax.nn.sigmoid(gb_ref[...].astype(F32)) * yb
        mb = m.astype(BF16)
        mixed = _dot(mb, wout_ref[...])
        ya_ref[...] = ya.astype(BF16)
        yb_ref[...] = yb.astype(BF16)
        m_ref[...] = mb
        mixed_ref[...] = mixed
        h1_ref[...] = h0_ref[...] + mixed * _rms(mixed) * g1_ref[...]

    row = lambda w: pl.BlockSpec((tr, w), lambda i: (i, 0))
    full = lambda a: pl.BlockSpec(a.shape, lambda i: (0, 0))
    return pl.pallas_call(
        body, name=name, grid=(lp // tr,),
        in_specs=[row(W_ATT), row(W_ATT),
                  pl.BlockSpec((tr, dm), lambda i: (i, ga_blk)),
                  pl.BlockSpec((tr, dm), lambda i: (i, ga_blk + 1)),
                  row(dm), full(w_oa), full(w_ob), full(w_out), full(g1)],
        out_specs=[row(dm)] * 5,
        out_shape=[jax.ShapeDtypeStruct((lp, dm), BF16)] * 3 + [jax.ShapeDtypeStruct((lp, dm), F32)] * 2,
        compiler_params=_cparams(("parallel",)),
    )(o_a, o_b, proj, proj, h0, w_oa, w_ob, w_out, g1)


def _merge_bwd(dh1, mixed, g1, w_out, proj, y_a, y_b, w_oa, w_ob, name):
    lp, dm = dh1.shape
    tr = TQ
    ga_blk = (6 * W_ATT) // dm

    def body(dh_ref, mx_ref, g1_ref, wout_ref, ga_ref, gb_ref, ya_ref, yb_ref, woa_ref, wob_ref,
             dmx_ref, dya_ref, dyb_ref, dga_ref, dgb_ref, doa_ref, dob_ref, dg1_ref):
        i = pl.program_id(0)
        dn = dh_ref[...]
        x = mx_ref[...]
        r = _rms(x)
        yhat = x * r
        part = jnp.sum(dn * yhat, axis=0, keepdims=True)
        dyh = dn * g1_ref[...]
        dmx = (r * (dyh - yhat * jnp.mean(dyh * yhat, axis=-1, keepdims=True))).astype(BF16)
        dmx_ref[...] = dmx
        dm_ = _dot_nt(dmx, wout_ref[...])
        sa = jax.nn.sigmoid(ga_ref[...].astype(F32))
        sb = jax.nn.sigmoid(gb_ref[...].astype(F32))
        dya = (dm_ * sa).astype(BF16)
        dyb = (dm_ * sb).astype(BF16)
        dya_ref[...] = dya
        dyb_ref[...] = dyb
        dga_ref[...] = (dm_ * ya_ref[...].astype(F32) * sa * (1.0 - sa)).astype(BF16)
        dgb_ref[...] = (dm_ * yb_ref[...].astype(F32) * sb * (1.0 - sb)).astype(BF16)
        doa_ref[...] = _dot_nt(dya, woa_ref[...]).astype(BF16)
        dob_ref[...] = _dot_nt(dyb, wob_ref[...]).astype(BF16)

        @pl.when(i == 0)
        def _():
            dg1_ref[...] = part

        @pl.when(i > 0)
        def _():
            dg1_ref[...] += part

    row = lambda w: pl.BlockSpec((tr, w), lambda i: (i, 0))
    full = lambda a: pl.BlockSpec(a.shape, lambda i: (0, 0))
    return pl.pallas_call(
        body, name=name, grid=(lp // tr,),
        in_specs=[row(dm), row(dm), full(g1), full(w_out),
                  pl.BlockSpec((tr, dm), lambda i: (i, ga_blk)),
                  pl.BlockSpec((tr, dm), lambda i: (i, ga_blk + 1)),
                  row(dm), row(dm), full(w_oa), full(w_ob)],
        out_specs=[row(dm)] * 5 + [row(W_ATT)] * 2 + [pl.BlockSpec((1, dm), lambda i: (0, 0))],
        out_shape=[jax.ShapeDtypeStruct((lp, dm), BF16)] * 5 + [jax.ShapeDtypeStruct((lp, W_ATT), BF16)] * 2
        + [jax.ShapeDtypeStruct((1, dm), F32)],
        compiler_params=_cparams(("arbitrary",)),
    )(dh1, mixed, g1, w_out, proj, proj, y_a, y_b, w_oa, w_ob)


_GELU_C = math.sqrt(2.0 / math.pi)
_GELU_A = 0.044715


def _gelu(x):
    t = jnp.tanh(_GELU_C * (x + _GELU_A * x * x * x))
    return 0.5 * x * (1.0 + t), t


CW = 256


def _taps(cur_ref, prev_ref, first, c0):
    cur = cur_ref[:, c0:c0 + CW].astype(F32)
    p1 = jnp.where(first, 0.0, prev_ref[HALO - 1:HALO, c0:c0 + CW].astype(F32))
    p2 = jnp.where(first, 0.0, prev_ref[HALO - 2:HALO - 1, c0:c0 + CW].astype(F32))
    row = lax.broadcasted_iota(jnp.int32, cur.shape, 0)
    x1 = jnp.where(row == 0, p1, pltpu.roll(cur, 1, 0))
    x2 = jnp.where(row == 0, p2, jnp.where(row == 1, p1, pltpu.roll(cur, 2, 0)))
    return cur, x1, x2


def _conv_at(cur_ref, prev_ref, w_ref, b_ref, first, c0):
    cur, x1, x2 = _taps(cur_ref, prev_ref, first, c0)
    cols = slice(c0, c0 + CW)
    u = b_ref[:, cols] + w_ref[0:1, cols] * x2 + w_ref[1:2, cols] * x1 + w_ref[2:3, cols] * cur
    return u, (x2, x1, cur)


def _up_specs(tr, width):
    per = tr // HALO
    return [pl.BlockSpec((tr, width), lambda i: (i, 0)),
            pl.BlockSpec((HALO, width), lambda i: (jnp.maximum(i * per - 1, 0), 0))]


def _convgate_fwd(up, conv_w, conv_b, name):
    lp, c2 = up.shape
    tr = TCONV

    def body(cur_ref, prev_ref, w_ref, b_ref, a_ref):
        first = pl.program_id(0) == 0
        for c0 in range(0, D_FF, CW):
            ug, _ = _conv_at(cur_ref, prev_ref, w_ref, b_ref, first, c0)
            uv, _ = _conv_at(cur_ref, prev_ref, w_ref, b_ref, first, D_FF + c0)
            gel, _ = _gelu(ug)
            a_ref[:, c0:c0 + CW] = (gel * uv).astype(BF16)

    return pl.pallas_call(
        body, name=name, grid=(lp // tr,),
        in_specs=_up_specs(tr, c2) + [pl.BlockSpec((3, c2), lambda i: (0, 0)),
                                      pl.BlockSpec((1, c2), lambda i: (0, 0))],
        out_specs=pl.BlockSpec((tr, D_FF), lambda i: (i, 0)),
        out_shape=jax.ShapeDtypeStruct((lp, D_FF), BF16),
        compiler_params=_cparams(("parallel",)),
    )(up, up, conv_w, conv_b)


def _convgate_bwd(up, da, conv_w, conv_b, name):
    lp, c2 = up.shape
    tr = TCONV

    def body(cur_ref, prev_ref, da_ref, w_ref, b_ref, du_ref, dw_ref, db_ref):
        i = pl.program_id(0)
        first = i == 0

        @pl.when(first)
        def _():
            dw_ref[...] = jnp.zeros_like(dw_ref)
            db_ref[...] = jnp.zeros_like(db_ref)

        for c0 in range(0, D_FF, CW):
            ug, taps_g = _conv_at(cur_ref, prev_ref, w_ref, b_ref, first, c0)
            uv, taps_v = _conv_at(cur_ref, prev_ref, w_ref, b_ref, first, D_FF + c0)
            gel, t = _gelu(ug)
            dgel = 0.5 * (1.0 + t) + 0.5 * ug * (1.0 - t * t) * _GELU_C * (1.0 + 3.0 * _GELU_A * ug * ug)
            da_ = da_ref[:, c0:c0 + CW].astype(F32)
            for base, du, taps in ((c0, da_ * uv * dgel, taps_g), (D_FF + c0, da_ * gel, taps_v)):
                cols = slice(base, base + CW)
                du_ref[:, cols] = du.astype(BF16)
                for tap in range(3):
                    dw_ref[tap:tap + 1, cols] += jnp.sum(du * taps[tap], axis=0, keepdims=True)
                db_ref[:, cols] += jnp.sum(du, axis=0, keepdims=True)

    return pl.pallas_call(
        body, name=name, grid=(lp // tr,),
        in_specs=_up_specs(tr, c2) + [pl.BlockSpec((tr, D_FF), lambda i: (i, 0)),
                                      pl.BlockSpec((3, c2), lambda i: (0, 0)),
                                      pl.BlockSpec((1, c2), lambda i: (0, 0))],
        out_specs=[pl.BlockSpec((tr, c2), lambda i: (i, 0)),
                   pl.BlockSpec((3, c2), lambda i: (0, 0)),
                   pl.BlockSpec((1, c2), lambda i: (0, 0))],
        out_shape=[jax.ShapeDtypeStruct((lp, c2), BF16), jax.ShapeDtypeStruct((3, c2), F32),
                   jax.ShapeDtypeStruct((1, c2), F32)],
        compiler_params=_cparams(("arbitrary",)),
    )(up, up, da, conv_w, conv_b)


def _conv_transpose(du, conv_w, name):
    lp, c2 = du.shape
    tr = TCONV
    per = tr // HALO
    n_halo = lp // HALO
    nt = lp // tr

    def body(cur_ref, nxt_ref, w_ref, o_ref):
        last = pl.program_id(0) == nt - 1
        for c0 in range(0, c2, CW):
            cols = slice(c0, c0 + CW)
            cur = cur_ref[:, cols].astype(F32)
            n0 = jnp.where(last, 0.0, nxt_ref[0:1, cols].astype(F32))
            n1 = jnp.where(last, 0.0, nxt_ref[1:2, cols].astype(F32))
            row = lax.broadcasted_iota(jnp.int32, cur.shape, 0)
            y1 = jnp.where(row == tr - 1, n0, pltpu.roll(cur, tr - 1, 0))
            y2 = jnp.where(row == tr - 1, n1, jnp.where(row == tr - 2, n0, pltpu.roll(cur, tr - 2, 0)))
            o_ref[:, cols] = (w_ref[2:3, cols] * cur + w_ref[1:2, cols] * y1 + w_ref[0:1, cols] * y2).astype(BF16)

    return pl.pallas_call(
        body, name=name, grid=(nt,),
        in_specs=[pl.BlockSpec((tr, c2), lambda i: (i, 0)),
                  pl.BlockSpec((HALO, c2), lambda i: (jnp.minimum((i + 1) * per, n_halo - 1), 0)),
                  pl.BlockSpec((3, c2), lambda i: (0, 0))],
        out_specs=pl.BlockSpec((tr, c2), lambda i: (i, 0)),
        out_shape=jax.ShapeDtypeStruct((lp, c2), BF16),
        compiler_params=_cparams(("parallel",)),
    )(du, du, conv_w)


def _down_loss(a, w_down, h1, g3, target, name):
    lp, dm = h1.shape
    tr = TQ

    def body(a_ref, w_ref, h1_ref, g_ref, t_ref, ffn_ref, dy_ref, ss_ref):
        i = pl.program_id(0)
        ffn = _dot(a_ref[...], w_ref[...])
        ffn_ref[...] = ffn
        h2 = h1_ref[...] + ffn * _rms(ffn) * g_ref[...]
        d = jnp.where(i > 0, h2 - t_ref[...], 0.0)
        dy_ref[...] = d * (1.0 / dm)
        part = jnp.sum(jnp.sum(d * d, axis=0, keepdims=True), axis=1, keepdims=True)

        @pl.when(i == 0)
        def _():
            ss_ref[...] = jnp.zeros_like(ss_ref)

        ss_ref[...] += part

    return pl.pallas_call(
        body, name=name, grid=(lp // tr,),
        in_specs=[pl.BlockSpec((tr, D_FF), lambda i: (i, 0)),
                  pl.BlockSpec(w_down.shape, lambda i: (0, 0)),
                  pl.BlockSpec((tr, dm), lambda i: (i, 0)),
                  pl.BlockSpec((1, dm), lambda i: (0, 0)),
                  pl.BlockSpec((tr, dm), lambda i: (jnp.maximum(i - 1, 0), 0))],
        out_specs=[pl.BlockSpec((tr, dm), lambda i: (i, 0)),
                   pl.BlockSpec((tr, dm), lambda i: (i, 0)),
                   pl.BlockSpec((8, LANES), lambda i: (0, 0))],
        out_shape=[jax.ShapeDtypeStruct((lp, dm), F32), jax.ShapeDtypeStruct((lp, dm), F32),
                   jax.ShapeDtypeStruct((8, LANES), F32)],
        compiler_params=_cparams(("arbitrary",)),
    )(a, w_down, h1, g3, target)


def _down_bwd(dy, ffn, g3, w_down, name):
    lp, dm = dy.shape
    tr = TQ

    def body(dy_ref, f_ref, g_ref, w_ref, dffn_ref, da_ref, dg_ref):
        i = pl.program_id(0)
        dn = dy_ref[...]
        x = f_ref[...]
        r = _rms(x)
        yhat = x * r
        part = jnp.sum(dn * yhat, axis=0, keepdims=True)
        dyh = dn * g_ref[...]
        dffn = (r * (dyh - yhat * jnp.mean(dyh * yhat, axis=-1, keepdims=True))).astype(BF16)
        dffn_ref[...] = dffn
        da_ref[...] = _dot_nt(dffn, w_ref[...]).astype(BF16)

        @pl.when(i == 0)
        def _():
            dg_ref[...] = part

        @pl.when(i > 0)
        def _():
            dg_ref[...] += part

    return pl.pallas_call(
        body, name=name, grid=(lp // tr,),
        in_specs=[pl.BlockSpec((tr, dm), lambda i: (i, 0)),
                  pl.BlockSpec((tr, dm), lambda i: (i, 0)),
                  pl.BlockSpec((1, dm), lambda i: (0, 0)),
                  pl.BlockSpec(w_down.shape, lambda i: (0, 0))],
        out_specs=[pl.BlockSpec((tr, dm), lambda i: (i, 0)),
                   pl.BlockSpec((tr, D_FF), lambda i: (i, 0)),
                   pl.BlockSpec((1, dm), lambda i: (0, 0))],
        out_shape=[jax.ShapeDtypeStruct((lp, dm), BF16), jax.ShapeDtypeStruct((lp, D_FF), BF16),
                   jax.ShapeDtypeStruct((1, dm), F32)],
        compiler_params=_cparams(("arbitrary",)),
    )(dy, ffn, g3, w_down)


def _pair_specs(lp, base):
    return [pl.BlockSpec((TQ, LANES), lambda p, i: (i, base + p)),
            pl.BlockSpec((lp, LANES), lambda p, i: (0, base + N_PAIRS + p)),
            pl.BlockSpec((lp, LANES), lambda p, i: (0, base + 2 * N_PAIRS + p))]


def _col_spec():
    return pl.BlockSpec((None, 2, TQ, 1), lambda p, i: (p, 0, i, 0))


def _rowvec_spec(nb):
    return pl.BlockSpec((None, 2, nb, 1, TQ), lambda p, i: (p, 0, 0, 0, 0))


def _tile_spec():
    return pl.BlockSpec((TQ, LANES), lambda p, i: (i, p))


RC = 64
T2 = 2 * TQ


def _stack_heads(x, scale=None):
    lane = lax.broadcasted_iota(jnp.int32, x.shape, 1)
    zero = jnp.zeros_like(x)
    x2 = jnp.concatenate([jnp.where(lane < HEAD_DIM, x, zero), jnp.where(lane >= HEAD_DIM, x, zero)], axis=0)
    return x2 if scale is None else x2 * scale


def _unstack_heads(x2):
    lane = lax.broadcasted_iota(jnp.int32, (TQ, LANES), 1)
    return jnp.where(lane < HEAD_DIM, x2[:TQ], x2[TQ:])


def _stack_cols(ref):
    return jnp.concatenate([ref[0], ref[1]], axis=0)


def _chunk_valid(i, j, r, strict):
    qpos = i * TQ + (r % TQ) + lax.broadcasted_iota(jnp.int32, (RC, TQ), 0)
    kpos = j * TQ + lax.broadcasted_iota(jnp.int32, (RC, TQ), 1)
    causal = (kpos < qpos) if strict else (kpos <= qpos)
    return causal & (kpos >= PAD)


def _walk_tiles(i, step, reverse):
    first, last = (i, 0) if reverse else (0, i)
    step(first, True)
    _between_unrolled(i, lambda j, nxt: step(j, False), reverse)

    @pl.when(i > 0)
    def _():
        step(last, True)


UNROLL = 4


def _between_unrolled(i, step, reverse):
    tile = (lambda t: i - 1 - t) if reverse else (lambda t: t + 1)
    after = (lambda j: j - 1) if reverse else (lambda j: j + 1)
    n = jnp.maximum(i - 1, 0)

    def group(u, c):
        j = tile(UNROLL * u)
        for _ in range(UNROLL):
            step(j, after(j))
            j = after(j)
        return c

    lax.fori_loop(0, n // UNROLL, group, 0)
    for k in range(UNROLL - 1):
        @pl.when(n % UNROLL > k)
        def _(k=k):
            j = tile(n - n % UNROLL + k)
            step(j, after(j))


_HALF = (slice(0, TQ), slice(TQ, T2))


def _walk_tiles_lead(i, lead, step, reverse):
    first, last = (i, 0) if reverse else (0, i)
    lead(first, 0)
    lead(first, 1)
    step(first, jnp.maximum(i - 1, 0) if reverse else jnp.minimum(1, i), True)

    _between_unrolled(i, lambda j, nxt: step(j, nxt, False), reverse)

    @pl.when(i > 0)
    def _():
        step(last, None, True)


def _krows(j):
    return pl.ds(pl.multiple_of(j * TQ, TQ), TQ)


def _fox_fwd(proj, crow, name):
    lp = proj.shape[0]
    nb = lp // TQ

    def body(q_ref, k_ref, v_ref, cr_ref, o_ref, lse_ref, m_ref, acc_ref, p_ref):
        i = pl.program_id(1)
        q2 = _stack_heads(q_ref[...], 0.125)
        m_ref[...] = jnp.full(m_ref.shape, NEG, F32)
        acc_ref[...] = jnp.zeros_like(acc_ref)
        lane = lax.broadcasted_iota(jnp.int32, (TQ, LANES), 1)

        def step(j, masked):
            rows_j = _krows(j)
            s = _dot_nt(q2, k_ref[rows_j, :])
            v = v_ref[rows_j, :]
            one = jnp.ones_like(v)
            v_heads = (jnp.where(lane < HEAD_DIM, v, one), jnp.where(lane >= HEAD_DIM, v, one))
            for r in range(0, T2, RC):
                rows = slice(r, r + RC)
                s_c = s[rows] - cr_ref[r // TQ, j]
                if masked:
                    s_c = jnp.where(_chunk_valid(i, j, r, False), s_c, NEG)
                s0, s1 = s_c[:, :LANES], s_c[:, LANES:]
                m_old = m_ref[rows]
                m_new = jnp.maximum(m_old, jnp.max(jnp.maximum(s0, s1), axis=-1, keepdims=True))
                m_ref[rows] = m_new
                acc_ref[rows] = jnp.exp(m_old - m_new) * acc_ref[rows]
                p_ref[rows, :LANES] = jnp.exp(s0 - m_new).astype(BF16)
                p_ref[rows, LANES:] = jnp.exp(s1 - m_new).astype(BF16)
            for h in range(2):
                acc_ref[_HALF[h]] += _dot(p_ref[_HALF[h]], v_heads[h])

        _walk_tiles(i, step, reverse=False)
        acc = acc_ref[...]
        m = m_ref[...]
        outs = []
        for h in range(2):
            a_h = acc[_HALF[h]]
            l = a_h[:, HEAD_DIM:HEAD_DIM + 1] if h == 0 else a_h[:, 0:1]
            lse_ref[h] = m[_HALF[h]][:, 0:1] + jnp.log(l)
            outs.append(a_h / l)
        o_ref[...] = jnp.where(lane < HEAD_DIM, outs[0], outs[1]).astype(BF16)

    return pl.pallas_call(
        body, name=name, grid=(N_PAIRS, nb),
        in_specs=_pair_specs(lp, 0) + [_rowvec_spec(nb)],
        out_specs=[_tile_spec(), _col_spec()],
        out_shape=[jax.ShapeDtypeStruct((lp, W_ATT), BF16),
                   jax.ShapeDtypeStruct((N_PAIRS, 2, lp, 1), F32)],
        scratch_shapes=[pltpu.VMEM((T2, LANES), F32), pltpu.VMEM((T2, LANES), F32), pltpu.VMEM((T2, TQ), BF16)],
        compiler_params=_cparams(("parallel", "arbitrary")),
    )(proj, proj, proj, crow)


def _fox_bwd(proj, do, o, lse, crow, name):
    lp = proj.shape[0]
    nb = lp // TQ

    def body(q_ref, k_ref, v_ref, do_ref, o_ref, lse_ref, cr_ref,
             dq_ref, dk_ref, dv_ref, dcs_ref, dct_ref,
             dk_acc, dv_acc, dq_acc, dct_acc, p_ref, ds_ref, s_ref, dp_ref):
        i = pl.program_id(1)

        @pl.when(i == 0)
        def _():
            dk_acc[...] = jnp.zeros_like(dk_acc)
            dv_acc[...] = jnp.zeros_like(dv_acc)
            dcs_ref[...] = jnp.zeros_like(dcs_ref)

        dq_acc[...] = jnp.zeros_like(dq_acc)
        dct_acc[...] = jnp.zeros_like(dct_acc)
        do_ = do_ref[...]
        q2 = _stack_heads(q_ref[...], 0.125)
        do2 = _stack_heads(do_)
        prod = do_.astype(F32) * o_ref[...].astype(F32)
        lane = lax.broadcasted_iota(jnp.int32, prod.shape, 1)
        delta2 = jnp.concatenate(
            [jnp.sum(jnp.where(lane < HEAD_DIM, prod, 0.0), axis=-1, keepdims=True),
             jnp.sum(jnp.where(lane >= HEAD_DIM, prod, 0.0), axis=-1, keepdims=True)], axis=0)
        lse2 = _stack_cols(lse_ref)

        def lead(j, h):
            rows_j = _krows(j)
            s_ref[_HALF[h]] = _dot_nt(q2[_HALF[h]], k_ref[rows_j, :])
            dp_ref[_HALF[h]] = _dot_nt(do2[_HALF[h]], v_ref[rows_j, :])

        def step(j, nxt, masked):
            rows_j = _krows(j)
            k = k_ref[rows_j, :]
            for h in range(2):
                cs = jnp.zeros((1, TQ), F32)
                for r in range(h * TQ, (h + 1) * TQ, RC):
                    rows = slice(r, r + RC)
                    p = jnp.exp(s_ref[rows] - cr_ref[h, j] - lse2[rows])
                    if masked:
                        p = jnp.where(_chunk_valid(i, j, r, False), p, 0.0)
                    ds = p * (dp_ref[rows] - delta2[rows])
                    p_ref[rows] = p.astype(BF16)
                    ds_ref[rows] = ds.astype(BF16)
                    dct_acc[rows] += jnp.sum(ds, axis=-1, keepdims=True)
                    cs = cs + jnp.sum(ds, axis=0, keepdims=True)
                dcs_ref[h, j] -= cs
                if nxt is not None:
                    lead(nxt, h)
                dsb = ds_ref[_HALF[h]]
                dq_acc[_HALF[h]] += _dot(dsb, k)
                dk_acc[rows_j, :] += _dot_tn(dsb, q2[_HALF[h]])
                dv_acc[rows_j, :] += _dot_tn(p_ref[_HALF[h]], do2[_HALF[h]])

        _walk_tiles_lead(i, lead, step, reverse=False)
        dct = dct_acc[...]
        dct_ref[0] = dct[:TQ]
        dct_ref[1] = dct[TQ:]
        dq_ref[...] = (_unstack_heads(dq_acc[...]) * 0.125).astype(BF16)

        @pl.when(i == nb - 1)
        def _():
            dk_ref[...] = dk_acc[...].astype(BF16)
            dv_ref[...] = dv_acc[...].astype(BF16)

    whole = pl.BlockSpec((lp, LANES), lambda p, i: (0, p))
    return pl.pallas_call(
        body, name=name, grid=(N_PAIRS, nb),
        in_specs=_pair_specs(lp, 0) + [_tile_spec(), _tile_spec(), _col_spec(), _rowvec_spec(nb)],
        out_specs=[_tile_spec(), whole, whole, _rowvec_spec(nb), _col_spec()],
        out_shape=[jax.ShapeDtypeStruct((lp, W_ATT), BF16)] * 3
        + [jax.ShapeDtypeStruct((N_PAIRS, 2, nb, 1, TQ), F32), jax.ShapeDtypeStruct((N_PAIRS, 2, lp, 1), F32)],
        scratch_shapes=[pltpu.VMEM((lp, LANES), F32), pltpu.VMEM((lp, LANES), F32),
                        pltpu.VMEM((T2, LANES), F32), pltpu.VMEM((T2, 1), F32),
                        pltpu.VMEM((T2, TQ), BF16), pltpu.VMEM((T2, TQ), BF16),
                        pltpu.VMEM((T2, TQ), F32), pltpu.VMEM((T2, TQ), F32)],
        compiler_params=_cparams(("parallel", "arbitrary")),
    )(proj, proj, proj, do, o, lse, crow)


def _sb_fwd(proj, name):
    lp = proj.shape[0]
    nb = lp // TQ
    tsuf = _tri(TQ, lambda r, c: r > c)

    def body(q_ref, k_ref, v_ref, t_ref, o_ref, lt_ref, run_ref, acc_ref, zl_ref, hl_ref, a_ref, z_ref):
        i = pl.program_id(1)
        q2 = _stack_heads(q_ref[...], 0.125)
        run_ref[...] = jnp.zeros_like(run_ref)
        acc_ref[...] = jnp.zeros_like(acc_ref)

        def lead(j, h):
            z_ref[_HALF[h]] = _dot_nt(q2[_HALF[h]], k_ref[_krows(j), :])

        def step(j, nxt, masked):
            t = t_ref[...]
            v = v_ref[_krows(j), :]
            later = []
            for h in range(2):
                for r in range(h * TQ, (h + 1) * TQ, RC):
                    rows = slice(r, r + RC)
                    z_c = z_ref[rows]
                    lk = _log_keep(z_c)
                    if masked:
                        lk = jnp.where(_chunk_valid(i, j, r, True), lk, 0.0)
                    hl_ref[rows] = lk.astype(BF16)
                    zl_ref[rows] = z_c + lk + run_ref[rows]
                    run_ref[rows] += jnp.sum(lk, axis=-1, keepdims=True)
                if nxt is not None:
                    lead(nxt, h)
                later.append(_dot(hl_ref[_HALF[h]], t))
            for h in range(2):
                for r in range(0, TQ, RC):
                    rows = slice(h * TQ + r, h * TQ + r + RC)
                    a = jnp.exp(zl_ref[rows] + later[h][r:r + RC])
                    if masked:
                        a = jnp.where(_chunk_valid(i, j, h * TQ + r, True), a, 0.0)
                    a_ref[rows] = a.astype(BF16)
                acc_ref[_HALF[h]] += _dot(a_ref[_HALF[h]], v)

        _walk_tiles_lead(i, lead, step, reverse=True)
        run = run_ref[...]
        lt_ref[0] = run[:TQ]
        lt_ref[1] = run[TQ:]
        o_ref[...] = _unstack_heads(acc_ref[...]).astype(BF16)

    base = 3 * N_PAIRS
    return pl.pallas_call(
        body, name=name, grid=(N_PAIRS, nb),
        in_specs=_pair_specs(lp, base) + [pl.BlockSpec((TQ, TQ), lambda p, i: (0, 0))],
        out_specs=[_tile_spec(), _col_spec()],
        out_shape=[jax.ShapeDtypeStruct((lp, W_ATT), BF16),
                   jax.ShapeDtypeStruct((N_PAIRS, 2, lp, 1), F32)],
        scratch_shapes=[pltpu.VMEM((T2, 1), F32), pltpu.VMEM((T2, LANES), F32), pltpu.VMEM((T2, TQ), F32),
                        pltpu.VMEM((T2, TQ), BF16), pltpu.VMEM((T2, TQ), BF16), pltpu.VMEM((T2, TQ), F32)],
        compiler_params=_cparams(("parallel", "arbitrary")),
    )(proj, proj, proj, tsuf)


def _sb_bwd(proj, do, ltot, name):
    lp = proj.shape[0]
    nb = lp // TQ
    tincl = _tri(TQ, lambda r, c: r <= c)
    texcl = _tri(TQ, lambda r, c: r < c)

    def body(q_ref, k_ref, v_ref, do_ref, lt_ref, ti_ref, te_ref, dq_ref, dk_ref, dv_ref,
             dk_acc, dv_acc, dq_acc, pc_ref, gc_ref, zl_ref, keep_ref, g_ref, z_ref, da_ref,
             hl_ref, gb_ref, a_ref, dz_ref):
        i = pl.program_id(1)

        @pl.when(i == 0)
        def _():
            dk_acc[...] = jnp.zeros_like(dk_acc)
            dv_acc[...] = jnp.zeros_like(dv_acc)

        dq_acc[...] = jnp.zeros_like(dq_acc)
        gc_ref[...] = jnp.zeros_like(gc_ref)
        pc_ref[...] = _stack_cols(lt_ref)
        q2 = _stack_heads(q_ref[...], 0.125)
        do2 = _stack_heads(do_ref[...])

        def lead(j, h):
            rows_j = _krows(j)
            z_ref[_HALF[h]] = _dot_nt(q2[_HALF[h]], k_ref[rows_j, :])
            da_ref[_HALF[h]] = _dot_nt(do2[_HALF[h]], v_ref[rows_j, :])

        def step(j, nxt, masked):
            rows_j = _krows(j)
            k = k_ref[rows_j, :]
            ti = ti_ref[...]
            te = te_ref[...]
            upto, before = [], []
            for h in range(2):
                for r in range(h * TQ, (h + 1) * TQ, RC):
                    rows = slice(r, r + RC)
                    z_c = z_ref[rows]
                    lk = _log_keep(z_c)
                    if masked:
                        lk = jnp.where(_chunk_valid(i, j, r, True), lk, 0.0)
                    hl_ref[rows] = lk.astype(BF16)
                    keep_ref[rows] = jnp.exp(lk)
                    zl_ref[rows] = z_c + lk + pc_ref[rows]
                    pc_ref[rows] -= jnp.sum(lk, axis=-1, keepdims=True)
                upto.append(_dot(hl_ref[_HALF[h]], ti))
            for h in range(2):
                for r in range(0, TQ, RC):
                    rows = slice(h * TQ + r, h * TQ + r + RC)
                    a = jnp.exp(zl_ref[rows] - upto[h][r:r + RC])
                    if masked:
                        a = jnp.where(_chunk_valid(i, j, h * TQ + r, True), a, 0.0)
                    g = a * da_ref[rows]
                    a_ref[rows] = a.astype(BF16)
                    g_ref[rows] = g
                    gb_ref[rows] = g.astype(BF16)
                if nxt is not None:
                    lead(nxt, h)
                before.append(_dot(gb_ref[_HALF[h]], te))
            for h in range(2):
                for r in range(0, TQ, RC):
                    rows = slice(h * TQ + r, h * TQ + r + RC)
                    g = g_ref[rows]
                    keep = keep_ref[rows]
                    dz = g * keep - (1.0 - keep) * (gc_ref[rows] + before[h][r:r + RC])
                    if masked:
                        dz = jnp.where(_chunk_valid(i, j, h * TQ + r, True), dz, 0.0)
                    dz_ref[rows] = dz.astype(BF16)
                    gc_ref[rows] += jnp.sum(g, axis=-1, keepdims=True)
                dzb = dz_ref[_HALF[h]]
                dq_acc[_HALF[h]] += _dot(dzb, k)
                dk_acc[rows_j, :] += _dot_tn(dzb, q2[_HALF[h]])
                dv_acc[rows_j, :] += _dot_tn(a_ref[_HALF[h]], do2[_HALF[h]])

        _walk_tiles_lead(i, lead, step, reverse=False)
        dq_ref[...] = (_unstack_heads(dq_acc[...]) * 0.125).astype(BF16)

        @pl.when(i == nb - 1)
        def _():
            dk_ref[...] = dk_acc[...].astype(BF16)
            dv_ref[...] = dv_acc[...].astype(BF16)

    base = 3 * N_PAIRS
    whole = pl.BlockSpec((lp, LANES), lambda p, i: (0, p))
    tri = lambda rows: pl.BlockSpec((rows, TQ), lambda p, i: (0, 0))
    wide = lambda dt: pltpu.VMEM((T2, TQ), dt)
    return pl.pallas_call(
        body, name=name, grid=(N_PAIRS, nb),
        in_specs=_pair_specs(lp, base) + [_tile_spec(), _col_spec(), tri(TQ), tri(TQ)],
        out_specs=[_tile_spec(), whole, whole],
        out_shape=[jax.ShapeDtypeStruct((lp, W_ATT), BF16)] * 3,
        scratch_shapes=[pltpu.VMEM((lp, LANES), F32), pltpu.VMEM((lp, LANES), F32),
                        pltpu.VMEM((T2, LANES), F32), pltpu.VMEM((T2, 1), F32), pltpu.VMEM((T2, 1), F32),
                        wide(F32), wide(F32), wide(F32), wide(F32), wide(F32),
                        wide(BF16), wide(BF16), wide(BF16), wide(BF16)],
        compiler_params=_cparams(("parallel", "arbitrary")),
    )(proj, proj, proj, do, ltot, tincl, texcl)


def _local_step(x, target, meta, gains, w_in, b_forget, w_o_fox, w_o_sb, w_out, w_up, conv_w, conv_b, w_down):
    seq, dm = x.shape
    lp = PAD + N_META + seq
    nb = lp // TQ
    s = [W_ATT, W_ATT, W_ATT, 8, W_ATT, W_ATT, W_ATT, dm, dm]
    off = [sum(s[:i]) for i in range(len(s) + 1)]
    cols = lambda i: w_in[:, off[i]:off[i + 1]]
    w1 = jnp.concatenate([cols(0), cols(1), cols(2), cols(4), cols(5), cols(6), cols(7), cols(8)], axis=1)
    wf = jnp.pad(cols(3), ((0, 0), (0, LANES - 8)))
    n1 = w1.shape[1]
    ncat = n1 + 512
    w_cat = jnp.concatenate([w1, wf, jnp.zeros((dm, ncat - n1 - LANES), BF16)], axis=1)
    bf = jnp.pad(b_forget.reshape(1, 8), ((0, 0), (0, LANES - 8)))
    g = [gains[i].reshape(1, dm) for i in range(4)]
    cb = conv_b.reshape(1, -1)

    h0 = jnp.concatenate([jnp.zeros((PAD, dm), F32), meta, x], axis=0)

    proj, xn1 = _rms_mm(h0, g[0], w1, "in_proj")
    logf = _logf(xn1, wf, bf, "log_forget")
    c = _cumsum_rows(logf, "forget_cumsum")
    crow = c[:, :8].T.reshape(N_PAIRS, 2, nb, 1, TQ)
    o_a, lse = _fox_fwd(proj, crow, "fox_fwd")
    o_b, ltot = _sb_fwd(proj, "sb_fwd")
    y_a, y_b, m, mixed, h1 = _merge_fwd(o_a, o_b, proj, h0, w_o_fox, w_o_sb, w_out, g[1], "merge_fwd")
    up, xn3 = _rms_mm(h1, g[2], w_up, "up_proj")
    a = _convgate_fwd(up, conv_w, cb, "convgate_fwd")
    ffn, dy, ss = _down_loss(a, w_down, h1, g[3], target, "down_loss")

    dffn, da, dg3 = _down_bwd(dy, ffn, g[3], w_down, "down_bwd")
    d_w_down = _mm_tn(a, dffn, "dw_down")
    du, d_conv_w, d_conv_b = _convgate_bwd(up, da, conv_w, cb, "convgate_bwd")
    dup = _conv_transpose(du, conv_w, "conv_transpose")
    d_w_up = _mm_tn(xn3, dup, "dw_up")
    dh1, dg2 = _mm_rmsbwd(dup, w_up, h1, g[2], dy, "up_bwd")
    dmx, dya, dyb, dga, dgb, do_a, do_b, dg1 = _merge_bwd(
        dh1, mixed, g[1], w_out, proj, y_a, y_b, w_o_fox, w_o_sb, "merge_bwd")
    d_w_out = _mm_tn(m, dmx, "dw_out")
    d_w_o_fox = _mm_tn(o_a, dya, "dw_o_fox")
    d_w_o_sb = _mm_tn(o_b, dyb, "dw_o_sb")
    dq_a, dk_a, dv_a, dcs, dct = _fox_bwd(proj, do_a, o_a, lse, crow, "fox_bwd")
    dq_b, dk_b, dv_b = _sb_bwd(proj, do_b, ltot, "sb_bwd")
    dc = (dct.reshape(8, lp) + dcs.reshape(8, lp)).T
    df, db = _dlogf(jnp.pad(dc, ((0, 0), (0, LANES - 8))), logf, "forget_bwd")
    dcat = jnp.concatenate([dq_a, dk_a, dv_a, dq_b, dk_b, dv_b, dga, dgb, df.astype(BF16),
                            jnp.zeros((lp, ncat - n1 - LANES), BF16)], axis=1)
    d_w_cat = _mm_tn(xn1, dcat, "dw_in")
    dh0, dg0 = _mm_rmsbwd(dcat, w_cat, h0, g[0], dh1, "in_bwd")

    wc = lambda k: d_w_cat[:, k * W_ATT:(k + 1) * W_ATT]
    d_w_in = jnp.concatenate([wc(0), wc(1), wc(2), d_w_cat[:, n1:n1 + 8], wc(3), wc(4), wc(5),
                              d_w_cat[:, 6 * W_ATT:n1]], axis=1)
    d_gains = jnp.concatenate([dg0, dg1, dg2, dg3], axis=0)
    grads = (dh0[PAD:PAD + N_META], d_gains, d_w_in, db[0, :8], d_w_o_fox, d_w_o_sb, d_w_out,
             d_w_up, d_conv_w, d_conv_b[0], d_w_down)
    return ss[0, 0], dh0[PAD + N_META:], grads


def _rows(a, n_rows):
    flat = a.reshape(-1)
    return jnp.pad(flat, (0, n_rows * D_MODEL - flat.shape[0])).reshape(n_rows, D_MODEL)


_SMALL = (("meta", 4), ("gains", 1), ("conv_w", 5), ("b_forget", 1), ("conv_b", 6))


def _pack(meta, gains, w_in, b_forget, w_o_fox, w_o_sb, w_out, w_up, conv_w, conv_b, w_down):
    big = [a.reshape(-1, D_MODEL) for a in (w_in, w_o_fox, w_o_sb, w_out, w_up, w_down)]
    n_big = sum(a.shape[0] for a in big)
    small = [_rows(a, n) for a, (_, n) in zip((meta, gains, conv_w, b_forget, conv_b), _SMALL)]
    n_small = sum(n for _, n in _SMALL)
    z = lambda n: jnp.zeros((n, D_MODEL), big[0].dtype)
    return jnp.concatenate(big + [z(PK_BIG_ROWS - n_big)] + small
                           + [z(PK_ROWS - PK_BIG_ROWS - n_small)], axis=0)


def _unpack(p):
    def take(r0, shape):
        n = math.prod(shape)
        nr = -(-n // D_MODEL)
        return p[r0:r0 + nr].reshape(-1)[:n].reshape(shape), r0 + nr
    w_in, r = take(0, (1, 1024, 1282))
    w_o_fox, r = take(r, (1, 512, 256))
    w_o_sb, r = take(r, (1, 512, 256))
    w_out, r = take(r, (1, 256, 1024))
    w_up, r = take(r, (1, 1024, 1408))
    w_down, r = take(r, (1, 704, 1024))
    r = PK_BIG_ROWS
    meta, r = take(r, (16, 256))
    gains, r = take(r, (1, 4, 256))
    conv_w, r = take(r, (1, 3, 1408))
    b_forget, r = take(r, (1, 8))
    conv_b, r = take(r, (1, 5632))
    return meta, gains, w_in, b_forget, w_o_fox, w_o_sb, w_out, w_up, conv_w, conv_b, w_down


def _chip_peers():
    x, y, c = lax.axis_index("x"), lax.axis_index("y"), lax.axis_index("c")
    return [(x, 1 - y, c), (1 - x, y, c), (1 - x, 1 - y, c)]


def _all_gather_chips(arrays, name):
    n = len(arrays)

    def body(*refs):
        ins, outs = refs[:n], refs[n:2 * n]
        send_sems, recv_sems, local_sems = refs[2 * n:]
        x, y = lax.axis_index("x"), lax.axis_index("y")
        me = 2 * x + y
        peers = _chip_peers()
        copies = []
        for a in range(n):
            mine = pltpu.make_async_copy(ins[a], outs[a].at[me], local_sems.at[a])
            mine.start()
            copies.append(mine)
        remote = []
        for a in range(n):
            for j, peer in enumerate(peers):
                cp = pltpu.make_async_remote_copy(
                    src_ref=ins[a], dst_ref=outs[a].at[me],
                    send_sem=send_sems.at[3 * a + j], recv_sem=recv_sems.at[3 * a + j],
                    device_id=peer, device_id_type=MESH)
                cp.start()
                remote.append(cp)
        for cp in remote:
            cp.wait()
        for cp in copies:
            cp.wait()

    any_spec = pl.BlockSpec(memory_space=pl.ANY)
    return pl.pallas_call(
        body, name=name,
        in_specs=[any_spec] * n, out_specs=[any_spec] * n,
        out_shape=[jax.ShapeDtypeStruct((4,) + a.shape, a.dtype) for a in arrays],
        scratch_shapes=[pltpu.SemaphoreType.DMA((3 * n,)), pltpu.SemaphoreType.DMA((3 * n,)),
                        pltpu.SemaphoreType.DMA((n,))],
    )(*arrays)


def _scatter_chips(chunks, name):
    _, rows, cols = chunks.shape

    def body(in_ref, out_ref, send_sems, recv_sems):
        x, y = lax.axis_index("x"), lax.axis_index("y")
        targets = [2 * x + (1 - y), 2 * (1 - x) + y, 2 * (1 - x) + (1 - y)]
        remote = []
        for j, peer in enumerate(_chip_peers()):
            cp = pltpu.make_async_remote_copy(
                src_ref=in_ref.at[targets[j]], dst_ref=out_ref.at[j],
                send_sem=send_sems.at[j], recv_sem=recv_sems.at[j],
                device_id=peer, device_id_type=MESH)
            cp.start()
            remote.append(cp)
        for cp in remote:
            cp.wait()

    any_spec = pl.BlockSpec(memory_space=pl.ANY)
    return pl.pallas_call(
        body, name=name, in_specs=[any_spec], out_specs=any_spec,
        out_shape=jax.ShapeDtypeStruct((3, rows, cols), chunks.dtype),
        scratch_shapes=[pltpu.SemaphoreType.DMA((3,)), pltpu.SemaphoreType.DMA((3,))],
    )(chunks)


def _swap_half_rows(chunks, name):
    n, rows, cols = chunks.shape
    r2 = rows // 2

    def body(in_ref, out_ref, send_sem, recv_sem):
        x, y, c = lax.axis_index("x"), lax.axis_index("y"), lax.axis_index("c")
        cp = pltpu.make_async_remote_copy(
            src_ref=in_ref.at[:, pl.ds((1 - c) * r2, r2), :], dst_ref=out_ref,
            send_sem=send_sem, recv_sem=recv_sem, device_id=(x, y, 1 - c), device_id_type=MESH)
        cp.start()
        cp.wait()

    any_spec = pl.BlockSpec(memory_space=pl.ANY)
    return pl.pallas_call(
        body, name=name, in_specs=[any_spec], out_specs=any_spec,
        out_shape=jax.ShapeDtypeStruct((n, r2, cols), chunks.dtype),
        scratch_shapes=[pltpu.SemaphoreType.DMA, pltpu.SemaphoreType.DMA],
    )(chunks)


def _gather_halves(half, name):
    def body(in_ref, out_ref, send_sem, recv_sem):
        x, y, c = lax.axis_index("x"), lax.axis_index("y"), lax.axis_index("c")
        cp = pltpu.make_async_remote_copy(
            src_ref=in_ref, dst_ref=out_ref, send_sem=send_sem, recv_sem=recv_sem,
            device_id=(x, y, 1 - c), device_id_type=MESH)
        cp.start()
        cp.wait()

    any_spec = pl.BlockSpec(memory_space=pl.ANY)
    other = pl.pallas_call(
        body, name=name, in_specs=[any_spec], out_specs=any_spec,
        out_shape=jax.ShapeDtypeStruct(half.shape, half.dtype),
        scratch_shapes=[pltpu.SemaphoreType.DMA, pltpu.SemaphoreType.DMA],
    )(half)
    axis = half.ndim - 2
    return lax.cond(lax.axis_index("c") == 0,
                    lambda: jnp.concatenate([half, other], axis=axis),
                    lambda: jnp.concatenate([other, half], axis=axis))


def _add_bf16(a, b, name):
    n, rows, cols = a.shape

    def body(a_ref, b_ref, o_ref):
        o_ref[...] = (a_ref[...].astype(F32) + b_ref[...].astype(F32)).astype(BF16)

    spec = pl.BlockSpec((None, PK_TILE, cols), lambda k, i: (k, i, 0))
    return pl.pallas_call(
        body, name=name, grid=(n, rows // PK_TILE),
        in_specs=[spec, spec], out_specs=spec,
        out_shape=jax.ShapeDtypeStruct(a.shape, BF16),
        compiler_params=_cparams(("parallel", "parallel")),
    )(a, b)


def _chip_sum(chunks, recv, name):
    _, rows, cols = chunks.shape

    def body(own_ref, r_ref, o_ref):
        f = lambda a: a.astype(F32)
        o_ref[...] = (f(own_ref[...]) + f(r_ref[0])) + (f(r_ref[1]) + f(r_ref[2]))

    me = 2 * lax.axis_index("x") + lax.axis_index("y")
    own = lax.dynamic_index_in_dim(chunks, me, axis=0, keepdims=False)
    return pl.pallas_call(
        body, name=name, grid=(rows // PK_TILE,),
        in_specs=[pl.BlockSpec((PK_TILE, cols), lambda i: (i, 0)),
                  pl.BlockSpec((3, PK_TILE, cols), lambda i: (0, i, 0))],
        out_specs=pl.BlockSpec((PK_TILE, cols), lambda i: (i, 0)),
        out_shape=jax.ShapeDtypeStruct((rows, cols), F32),
        compiler_params=_cparams(("parallel",)),
    )(own, recv)


def _adamw(w, m, v, g, name):
    shape = w.shape
    rows, cols = math.prod(shape[:-1]), shape[-1]
    w, m, v, g = (a.reshape(rows, cols) for a in (w, m, v, g))
    tile = next((t for t in (256, 128, 64, 32, 16, 8) if rows % t == 0), rows)
    c1 = 1.0 - ADAM_B1 ** ADAM_STEP
    c2 = 1.0 - ADAM_B2 ** ADAM_STEP

    def body(w_ref, m_ref, v_ref, g_ref, d_ref, nm_ref, nv_ref):
        g = g_ref[...]
        nm = ADAM_B1 * m_ref[...] + (1.0 - ADAM_B1) * g
        nv = ADAM_B2 * v_ref[...] + (1.0 - ADAM_B2) * (g * g)
        nm_ref[...] = nm
        nv_ref[...] = nv
        d_ref[...] = -ADAM_LR * ((nm / c1) / (jnp.sqrt(nv / c2) + ADAM_EPS) + ADAM_WD * w_ref[...])

    spec = pl.BlockSpec((tile, cols), lambda i: (i, 0))
    outs = pl.pallas_call(
        body, name=name, grid=(rows // tile,),
        in_specs=[spec] * 4, out_specs=[spec] * 3,
        out_shape=[jax.ShapeDtypeStruct((rows, cols), F32)] * 3,
        compiler_params=_cparams(("parallel",)),
    )(w, m, v, g)
    return [o.reshape(shape) for o in outs]


def _full_weights(big, small):
    def gather(src, r0, shape, axis):
        n = math.prod(shape)
        nr = -(-n // D_MODEL)
        parts = [src[k, r0:r0 + nr].reshape(-1)[:n].reshape(shape) for k in range(4)]
        return jnp.concatenate(parts, axis=axis), r0 + nr
    w_in, r = gather(big, 0, (1024, 1282), 1)
    w_o_fox, r = gather(big, r, (512, 256), 1)
    w_o_sb, r = gather(big, r, (512, 256), 1)
    w_out, r = gather(big, r, (256, 1024), 0)
    w_up, r = gather(big, r, (1024, 1408), 1)
    w_down, r = gather(big, r, (704, 1024), 0)
    meta, r = gather(small, 0, (16, 256), 1)
    gains, r = gather(small, r, (4, 256), 1)
    conv_w, r = gather(small, r, (3, 1408), 1)
    return meta, gains, w_in, w_o_fox, w_o_sb, w_out, w_up, conv_w, w_down


def _chunks_for_chips(grads):
    d_meta, d_gains, d_w_in, d_b, d_w_o_fox, d_w_o_sb, d_w_out, d_w_up, d_conv_w, d_conv_b, d_w_down = (
        g.astype(BF16) for g in grads)
    out = []
    for k in range(4):
        col = lambda a, w: a[:, k * w:(k + 1) * w]
        row = lambda a, w: a[k * w:(k + 1) * w]
        out.append(_pack(col(d_meta, 256), col(d_gains, 256), col(d_w_in, 1282), d_b, col(d_w_o_fox, 256),
                         col(d_w_o_sb, 256), row(d_w_out, 256), col(d_w_up, 1408), col(d_conv_w, 1408),
                         d_conv_b, row(d_w_down, 704)))
    return jnp.stack(out, axis=0)


def kernel(x, meta_tokens, norm_gains, w_in, b_forget, w_o_fox, w_o_sb, w_out, w_up, conv_w, conv_b, w_down, loss_target, m_meta_tokens, m_norm_gains, m_w_in, m_b_forget, m_w_o_fox, m_w_o_sb, m_w_out, m_w_up, m_conv_w, m_conv_b, m_w_down, v_meta_tokens, v_norm_gains, v_w_in, v_b_forget, v_w_o_fox, v_w_o_sb, v_w_out, v_w_up, v_conv_w, v_conv_b, v_w_down):
    names = ("meta_tokens", "norm_gains", "w_in", "b_forget", "w_o_fox", "w_o_sb", "w_out", "w_up", "conv_w",
             "conv_b", "w_down")
    ws = (meta_tokens, norm_gains, w_in, b_forget, w_o_fox, w_o_sb, w_out, w_up, conv_w, conv_b, w_down)
    ms = (m_meta_tokens, m_norm_gains, m_w_in, m_b_forget, m_w_o_fox, m_w_o_sb, m_w_out, m_w_up, m_conv_w,
          m_conv_b, m_w_down)
    vs = (v_meta_tokens, v_norm_gains, v_w_in, v_b_forget, v_w_o_fox, v_w_o_sb, v_w_out, v_w_up, v_conv_w,
          v_conv_b, v_w_down)
    wp = _pack(meta_tokens, norm_gains[0], w_in[0], b_forget[0], w_o_fox[0], w_o_sb[0], w_out[0], w_up[0],
               conv_w[0], conv_b[0], w_down[0])

    hb = PK_BIG_ROWS // 2
    big_half = lax.dynamic_slice_in_dim(wp[:PK_BIG_ROWS].astype(BF16), lax.axis_index("c") * hb, hb, axis=0)
    big_half, small = _all_gather_chips(
        [big_half, wp[PK_BIG_ROWS:PK_BIG_ROWS + PK_SMALL_ROWS]], "gather_weights")
    big = _gather_halves(big_half, "gather_weight_halves")
    meta, gains, f_w_in, f_w_o_fox, f_w_o_sb, f_w_out, f_w_up, f_conv_w, f_w_down = _full_weights(big, small)

    ss, dx, grads = _local_step(x[0], loss_target[0], meta, gains, f_w_in, b_forget[0], f_w_o_fox, f_w_o_sb,
                                f_w_out, f_w_up, f_conv_w, conv_b[0], f_w_down)
    loss = lax.psum(0.5 * ss / D_MODEL, ("x", "y", "c"))

    chunks = _chunks_for_chips(grads)
    r2 = PK_ROWS // 2
    from_sibling = _swap_half_rows(chunks, "swap_halves")
    own = lax.dynamic_slice_in_dim(chunks, lax.axis_index("c") * r2, r2, axis=1)
    core_sum = _add_bf16(own, from_sibling, "core_sum")
    recv = _scatter_chips(core_sum, "scatter_grads")
    g = _gather_halves(_chip_sum(core_sum, recv, "chip_sum"), "gather_halves")
    gs = _unpack(g)
    steps = [_adamw(w, m, v, gp, "adamw_" + n) for n, w, m, v, gp in zip(names, ws, ms, vs, gs)]
    return (loss, dx[None], *gs, *[s[0] for s in steps], *[s[1] for s in steps], *[s[2] for s in steps])
```
